```python
import math
import jax, jax.numpy as jnp
from jax import lax
import numpy as np

D_MODEL = 2048
BATCH = 2
SEQ = 4096
DEPTH = 1

CHUNK = 64
N_MEM = 256
EPS = 1e-6
NEG = -1e30

MLA_HEADS = 8
MLA_NOPE = 128
MLA_ROPE = 64
MLA_V = 128
MLA_QLORA = 512
MLA_KVLORA = 256
MLA_WIDTH = MLA_HEADS * MLA_V
ROPE_THETA = 10000.0
QUERY_BLOCK = 128

SWA_HEADS = 8
SWA_KV_HEADS = 2
SWA_HEAD_DIM = 64
SWA_WIDTH = SWA_HEADS * SWA_HEAD_DIM
SWA_KV_WIDTH = SWA_KV_HEADS * SWA_HEAD_DIM
WINDOW = 128
WINDOW_CHUNKS = WINDOW // CHUNK
SWA_BLOCK = 128

MEM_HEADS = 4
MEM_HEAD_DIM = 128
MEM_WIDTH = MEM_HEADS * MEM_HEAD_DIM

MIX_WIDTH = MLA_WIDTH + SWA_WIDTH + MEM_WIDTH

N_BUCKETS = 32
MAX_DISTANCE = 128

IN_SIZES = (
    MLA_QLORA,
    MLA_KVLORA,
    MLA_ROPE,
    MLA_WIDTH,
    SWA_WIDTH,
    SWA_KV_WIDTH,
    SWA_KV_WIDTH,
    SWA_WIDTH,
    MEM_WIDTH,
    MEM_WIDTH,
)
IN_WIDTH = 4160

kernel_name = "hybrid_mla_swa_sink_memory_block"


def rms_norm(x, g):
    xf = x.astype(jnp.float32)
    y = xf * lax.rsqrt(jnp.mean(xf * xf, axis=-1, keepdims=True) + EPS)
    return (y * g.astype(jnp.float32)).astype(x.dtype)


def split_cols(t, sizes):
    idx = []
    acc = 0
    for s in sizes[:-1]:
        acc += s
        idx.append(acc)
    return jnp.split(t, idx, axis=-1)


def rope_tables(seq, dim):
    inv = 1.0 / (ROPE_THETA ** (jnp.arange(0, dim, 2, dtype=jnp.float32) / dim))
    ang = jnp.arange(seq, dtype=jnp.float32)[:, None] * inv[None, :]
    return jnp.cos(ang), jnp.sin(ang)


def apply_rope(x, cos, sin):
    half = x.shape[-1] // 2
    x1 = x[..., :half].astype(jnp.float32)
    x2 = x[..., half:].astype(jnp.float32)
    out = jnp.concatenate([x1 * cos - x2 * sin, x2 * cos + x1 * sin], axis=-1)
    return out.astype(x.dtype)


def t5_bucket(rel):
    nb = N_BUCKETS // 2
    max_exact = nb // 2
    bucket = jnp.where(rel > 0, nb, 0)
    n = jnp.abs(rel)
    nf = jnp.maximum(n, 1).astype(jnp.float32)
    large = max_exact + (jnp.log(nf / max_exact) / math.log(MAX_DISTANCE / max_exact)
                         * (nb - max_exact)).astype(jnp.int32)
    large = jnp.minimum(large, nb - 1)
    return bucket + jnp.where(n < max_exact, n, large)


def mla_attention(c_q, c_kv, k_pe_in, g_q, g_kv, w_uq, w_ukv, cos, sin):
    B, S, _ = c_q.shape
    q = (rms_norm(c_q, g_q) @ w_uq).reshape(B, S, MLA_HEADS, MLA_NOPE + MLA_ROPE)
    q_nope, q_pe = q[..., :MLA_NOPE], q[..., MLA_NOPE:]
    q_pe = apply_rope(q_pe, cos[:, None, :], sin[:, None, :])
    kv = (rms_norm(c_kv, g_kv) @ w_ukv).reshape(B, S, MLA_HEADS, MLA_NOPE + MLA_V)
    k_nope, v = kv[..., :MLA_NOPE], kv[..., MLA_NOPE:]
    k_pe = apply_rope(k_pe_in, cos, sin)
    scale = (MLA_NOPE + MLA_ROPE) ** -0.5
    nblk = S // QUERY_BLOCK
    qn_b = q_nope.reshape(B, nblk, QUERY_BLOCK, MLA_HEADS, MLA_NOPE).transpose(1, 0, 2, 3, 4)
    qp_b = q_pe.reshape(B, nblk, QUERY_BLOCK, MLA_HEADS, MLA_ROPE).transpose(1, 0, 2, 3, 4)
    k_chunk = jnp.arange(S) // CHUNK

    def block(args):
        qn, qp, b = args
        q_chunk = (b * QUERY_BLOCK + jnp.arange(QUERY_BLOCK)) // CHUNK
        s = (jnp.einsum('bqhd,bkhd->bhqk', qn, k_nope)
             + jnp.einsum('bqhd,bkd->bhqk', qp, k_pe)).astype(jnp.float32) * scale
        mask = k_chunk[None, :] <= q_chunk[:, None]
        s = jnp.where(mask[None, None], s, NEG)
        p = jax.nn.softmax(s, axis=-1).astype(v.dtype)
        return jnp.einsum('bhqk,bkhd->bqhd', p, v)

    out = lax.map(block, (qn_b, qp_b, jnp.arange(nblk)))
    return out.transpose(1, 0, 2, 3, 4).reshape(B, S, MLA_WIDTH)


def swa_attention(q, k, v, sinks, rel_table):
    B, S, _ = q.shape
    nblk = S // SWA_BLOCK
    G = SWA_HEADS // SWA_KV_HEADS
    q = q.reshape(B, nblk, SWA_BLOCK, SWA_KV_HEADS, G, SWA_HEAD_DIM)

    def band(t):
        t = t.reshape(B, S, SWA_KV_HEADS, SWA_HEAD_DIM)
        t = jnp.pad(t, ((0, 0), (SWA_BLOCK, 0), (0, 0), (0, 0)))
        t = t.reshape(B, nblk + 1, SWA_BLOCK, SWA_KV_HEADS, SWA_HEAD_DIM)
        return jnp.concatenate([t[:, :-1], t[:, 1:]], axis=2)

    k_band, v_band = band(k), band(v)
    s = jnp.einsum('bnqkgd,bnjkd->bnkgqj', q, k_band).astype(jnp.float32) * (SWA_HEAD_DIM ** -0.5)
    qi = jnp.arange(SWA_BLOCK)
    kj = jnp.arange(2 * SWA_BLOCK)
    rel = kj[None, :] - SWA_BLOCK - qi[:, None]
    bias = rel_table.astype(jnp.float32)[t5_bucket(rel)]
    bias = bias.transpose(2, 0, 1).reshape(SWA_KV_HEADS, G, SWA_BLOCK, 2 * SWA_BLOCK)
    s = s + bias
    blk = jnp.arange(nblk)[:, None, None]
    q_pos = blk * SWA_BLOCK + qi[None, :, None]
    k_pos = (blk - 1) * SWA_BLOCK + kj[None, None, :]
    q_c = q_pos // CHUNK
    k_c = k_pos // CHUNK
    valid = (k_pos >= 0) & (k_c <= q_c) & (k_c >= q_c - WINDOW_CHUNKS)
    s = jnp.where(valid[None, :, None, None], s, NEG)
    sink = sinks.astype(jnp.float32).reshape(SWA_KV_HEADS, G)[None, None, :, :, None, None]
    m = jnp.maximum(jnp.max(s, axis=-1, keepdims=True), sink)
    p = jnp.exp(s - m)
    p = p / (jnp.sum(p, axis=-1, keepdims=True) + jnp.exp(sink - m))
    o = jnp.einsum('bnkgqj,bnjkd->bnqkgd', p.astype(v_band.dtype), v_band)
    return o.reshape(B, S, SWA_WIDTH)


def memory_attention(q, mem_n, w_mem_kv):
    B, S, _ = q.shape
    M = mem_n.shape[1]
    kv = mem_n @ w_mem_kv
    k = kv[..., :MEM_WIDTH].reshape(B, M, MEM_HEADS, MEM_HEAD_DIM)
    v = kv[..., MEM_WIDTH:].reshape(B, M, MEM_HEADS, MEM_HEAD_DIM)
    q = q.reshape(B, S, MEM_HEADS, MEM_HEAD_DIM)
    s = jnp.einsum('bshd,bmhd->bhsm', q, k).astype(jnp.float32) * (MEM_HEAD_DIM ** -0.5)
    p = jax.nn.softmax(s, axis=-1).astype(v.dtype)
    return jnp.einsum('bhsm,bmhd->bshd', p, v).reshape(B, S, MEM_WIDTH)


def hybrid_layer(x, mem, g_in, w_in, g_q, g_kv, w_uq, w_ukv, sinks, rel_table,
                 g_mem, w_mem_kv, w_out, cos, sin):
    h = rms_norm(x, g_in)
    (c_q, c_kv, k_pe, z_mla, q_swa, k_swa, v_swa, z_swa, q_mem, z_mem) = split_cols(h @ w_in, IN_SIZES)
    y_mla = mla_attention(c_q, c_kv, k_pe, g_q, g_kv, w_uq, w_ukv, cos, sin) * jax.nn.silu(z_mla)
    y_swa = swa_attention(q_swa, k_swa, v_swa, sinks, rel_table) * jax.nn.silu(z_swa)
    y_mem = memory_attention(q_mem, rms_norm(mem, g_mem), w_mem_kv) * jax.nn.silu(z_mem)
    y = jnp.concatenate([y_mla, y_swa, y_mem], axis=-1) @ w_out
    return x + y


def setup_inputs(seed: int = 0) -> dict:
    key = jax.random.key(seed)
    ks = jax.random.split(key, 16)
    f32 = jnp.float32

    def nrm(k, shape, scale):
        return jax.random.normal(k, shape, f32) * scale

    def gain(k, shape):
        return 1.0 + 0.05 * jax.random.normal(k, shape, f32)

    return {
        "x": jax.random.normal(ks[0], (BATCH, SEQ, D_MODEL), f32),
        "mem": jax.random.normal(ks[1], (BATCH, N_MEM, D_MODEL), f32),
        "norm_in": gain(ks[2], (DEPTH, D_MODEL)),
        "w_in": nrm(ks[3], (DEPTH, D_MODEL, IN_WIDTH), D_MODEL ** -0.5),
        "norm_q": gain(ks[4], (DEPTH, MLA_QLORA)),
        "norm_kv": gain(ks[5], (DEPTH, MLA_KVLORA)),
        "w_uq": nrm(ks[6], (DEPTH, MLA_QLORA, MLA_HEADS * (MLA_NOPE + MLA_ROPE)), MLA_QLORA ** -0.5),
        "w_ukv": nrm(ks[7], (DEPTH, MLA_KVLORA, MLA_HEADS * (MLA_NOPE + MLA_V)), MLA_KVLORA ** -0.5),
        "attn_sinks": nrm(ks[8], (DEPTH, SWA_HEADS), 0.5),
        "rel_bias": nrm(ks[9], (N_BUCKETS, SWA_HEADS), 0.5),
        "norm_mem": gain(ks[10], (DEPTH, D_MODEL)),
        "w_mem_kv": nrm(ks[11], (DEPTH, D_MODEL, 2 * MEM_WIDTH), D_MODEL ** -0.5),
        "w_out": nrm(ks[12], (DEPTH, MIX_WIDTH, D_MODEL), MIX_WIDTH ** -0.5),
        "norm_final": gain(ks[13], (D_MODEL,)),
    }


def reference(x, mem, norm_in, w_in, norm_q, norm_kv, w_uq, w_ukv, attn_sinks, rel_bias,
              norm_mem, w_mem_kv, w_out, norm_final):
    cos, sin = rope_tables(x.shape[1], MLA_ROPE)
    h = x
    for l in range(DEPTH):
        h = hybrid_layer(h, mem, norm_in[l], w_in[l], norm_q[l], norm_kv[l], w_uq[l], w_ukv[l],
                         attn_sinks[l], rel_bias, norm_mem[l], w_mem_kv[l], w_out[l], cos, sin)
    return rms_norm(h, norm_final)
```

```python
import functools
import math

import jax
import jax.numpy as jnp
from jax import lax
from jax.experimental import pallas as pl
from jax.experimental.pallas import tpu as pltpu

D_MODEL = 2048
CHUNK = 64
N_MEM = 256
EPS = 1e-6
NEG = -1e30

MLA_HEADS = 8
MLA_NOPE = 128
MLA_ROPE = 64
MLA_V = 128
MLA_QLORA = 512
MLA_KVLORA = 256
MLA_WIDTH = MLA_HEADS * MLA_V
MLA_QK_PAD = 256
ROPE_THETA = 10000.0

SWA_HEADS = 8
SWA_KV_HEADS = 2
SWA_GROUP = SWA_HEADS // SWA_KV_HEADS
SWA_HEAD_DIM = 64
SWA_WIDTH = SWA_HEADS * SWA_HEAD_DIM
SWA_KV_WIDTH = SWA_KV_HEADS * SWA_HEAD_DIM
WINDOW_CHUNKS = 2
SWA_BLOCK = 128

MEM_HEADS = 4
MEM_HEAD_DIM = 128
MEM_WIDTH = MEM_HEADS * MEM_HEAD_DIM

MIX_WIDTH = MLA_WIDTH + SWA_WIDTH + MEM_WIDTH

N_BUCKETS = 32
MAX_DISTANCE = 128

BF16 = jnp.bfloat16
F32 = jnp.float32

_C_CQ = 0
_C_CKV = _C_CQ + MLA_QLORA
_C_QS = _C_CKV + MLA_KVLORA
_C_Z = _C_QS + SWA_WIDTH + MEM_WIDTH
_C_KSWA = _C_Z + MIX_WIDTH
_C_VSWA = _C_KSWA + SWA_KV_WIDTH
_C_KPE = _C_VSWA + SWA_KV_WIDTH
_C_END = _C_KPE + 2 * MLA_ROPE

_VMEM_LIMIT = 56 * 1024 * 1024


def _params(sem):
    return pltpu.CompilerParams(dimension_semantics=sem, vmem_limit_bytes=_VMEM_LIMIT)


def _rms(v, g):
    return (v * lax.rsqrt(jnp.mean(v * v, axis=-1, keepdims=True) + EPS)) * g


def _dot(a, b):
    return jnp.dot(a, b, preferred_element_type=F32)


def _dot_nt(a, b):
    return lax.dot_general(a, b, (((1,), (1,)), ((), ())), preferred_element_type=F32)


def _chunk_of(pos):
    return jnp.right_shift(pos, CHUNK.bit_length() - 1)


def _rope128(v, cos_t, sin_t):
    return v * cos_t + pltpu.roll(v, 64, 1) * sin_t


def _proj_kernel(x_ref, gin_ref, w1_ref, gq_ref, gkv_ref, wuq_ref, wukv_ref, cos_ref, sin_ref,
                 q_ref, k_ref, v_ref, qs_ref, z_ref, ks_ref, vs_ref):
    h = _rms(x_ref[0], gin_ref[...]).astype(BF16)
    cos_t = cos_ref[...]
    sin_t = sin_ref[...]

    pa = _dot(h, w1_ref[:, _C_CQ:_C_QS])
    cq = _rms(pa[:, :MLA_QLORA], gq_ref[...]).astype(BF16)
    ckv = _rms(pa[:, MLA_QLORA:], gkv_ref[...]).astype(BF16)

    qs_ref[0] = _dot(h, w1_ref[:, _C_QS:_C_Z]).astype(BF16)
    z_ref[0] = _dot(h, w1_ref[:, _C_Z:_C_KSWA]).astype(BF16)
    pd = _dot(h, w1_ref[:, _C_KSWA:_C_END])
    ks_ref[0] = pd[:, :SWA_KV_WIDTH].astype(BF16)
    vs_ref[0] = pd[:, SWA_KV_WIDTH:2 * SWA_KV_WIDTH].astype(BF16)
    kpe = _rope128(pd[:, 2 * SWA_KV_WIDTH:], cos_t, sin_t).astype(BF16)

    qall = _dot(cq, wuq_ref[...])
    kv = _dot(ckv, wukv_ref[...])
    for hd in range(MLA_HEADS):
        o = hd * MLA_QK_PAD
        q_ref[0, :, o:o + MLA_NOPE] = qall[:, o:o + MLA_NOPE].astype(BF16)
        q_ref[0, :, o + MLA_NOPE:o + MLA_QK_PAD] = _rope128(
            qall[:, o + MLA_NOPE:o + MLA_QK_PAD], cos_t, sin_t).astype(BF16)
        k_ref[0, :, o:o + MLA_NOPE] = kv[:, o:o + MLA_NOPE].astype(BF16)
        k_ref[0, :, o + MLA_NOPE:o + MLA_QK_PAD] = kpe
        v_ref[0, :, hd * MLA_V:(hd + 1) * MLA_V] = kv[:, o + MLA_NOPE:o + MLA_QK_PAD].astype(BF16)


def _proj(x, gin, w1, gq, gkv, wuq, wukv, cos_t, sin_t, tm):
    B, S, _ = x.shape
    const = lambda b, i: (0, 0)
    row = lambda b, i: (b, i, 0)
    single = pl.Buffered(1)

    def out(width):
        return jax.ShapeDtypeStruct((B, S, width), BF16)

    return pl.pallas_call(
        _proj_kernel,
        grid=(B, S // tm),
        in_specs=[
            pl.BlockSpec((1, tm, D_MODEL), row),
            pl.BlockSpec((1, D_MODEL), const),
            pl.BlockSpec((D_MODEL, _C_END), const, pipeline_mode=single),
            pl.BlockSpec((1, MLA_QLORA), const),
            pl.BlockSpec((1, MLA_KVLORA), const),
            pl.BlockSpec((MLA_QLORA, MLA_HEADS * MLA_QK_PAD), const, pipeline_mode=single),
            pl.BlockSpec((MLA_KVLORA, MLA_HEADS * (MLA_NOPE + MLA_V)), const, pipeline_mode=single),
            pl.BlockSpec((tm, 128), lambda b, i: (i, 0)),
            pl.BlockSpec((tm, 128), lambda b, i: (i, 0)),
        ],
        out_specs=[
            pl.BlockSpec((1, tm, MLA_HEADS * MLA_QK_PAD), row),
            pl.BlockSpec((1, tm, MLA_HEADS * MLA_QK_PAD), row),
            pl.BlockSpec((1, tm, MLA_WIDTH), row),
            pl.BlockSpec((1, tm, SWA_WIDTH + MEM_WIDTH), row),
            pl.BlockSpec((1, tm, MIX_WIDTH), row),
            pl.BlockSpec((1, tm, SWA_KV_WIDTH), row),
            pl.BlockSpec((1, tm, SWA_KV_WIDTH), row),
        ],
        out_shape=[
            out(MLA_HEADS * MLA_QK_PAD), out(MLA_HEADS * MLA_QK_PAD), out(MLA_WIDTH),
            out(SWA_WIDTH + MEM_WIDTH), out(MIX_WIDTH), out(SWA_KV_WIDTH), out(SWA_KV_WIDTH),
        ],
        compiler_params=_params(("arbitrary", "arbitrary")),
        name="proj",
    )(x, gin, w1, gq, gkv, wuq, wukv, cos_t, sin_t)


def _mla_kernel(q_ref, k_ref, v_ref, o_ref, *, tq, tk):
    qi = pl.program_id(2)
    scale = (MLA_NOPE + MLA_ROPE) ** -0.5
    q = q_ref[0]

    def tile(k, v, mask, m, l, acc):
        s = _dot_nt(q, k) * scale
        if mask is not None:
            s = jnp.where(mask, s, NEG)
        m_new = jnp.maximum(m, jnp.max(s, axis=-1, keepdims=True))
        alpha = jnp.exp(m - m_new)
        p = jnp.exp(s - m_new)
        l = alpha * l + jnp.sum(p, axis=-1, keepdims=True)
        acc = alpha * acc + _dot(p.astype(BF16), v)
        return m_new, l, acc

    d0 = pl.multiple_of(qi * tq, tq)
    rq = _chunk_of(lax.broadcasted_iota(jnp.int32, (tq, tq), 0))
    ck = _chunk_of(lax.broadcasted_iota(jnp.int32, (tq, tq), 1))
    m0 = jnp.full((tq, 1), NEG, F32)
    l0 = jnp.zeros((tq, 1), F32)
    a0 = jnp.zeros((tq, MLA_V), F32)
    carry = tile(k_ref[0, pl.ds(d0, tq), :], v_ref[0, pl.ds(d0, tq), :], ck <= rq, m0, l0, a0)

    def body(j, c):
        s0 = pl.multiple_of(j * tk, tk)
        return tile(k_ref[0, pl.ds(s0, tk), :], v_ref[0, pl.ds(s0, tk), :], None, *c)

    m, l, acc = lax.fori_loop(0, qi * (tq // tk), body, carry)
    o_ref[0] = (acc / l).astype(BF16)


def _mla(q, k, v, tq, tk):
    B, S, _ = q.shape
    return pl.pallas_call(
        functools.partial(_mla_kernel, tq=tq, tk=tk),
        grid=(B, MLA_HEADS, S // tq),
        in_specs=[
            pl.BlockSpec((1, tq, MLA_QK_PAD), lambda b, h, i: (b, i, h)),
            pl.BlockSpec((1, S, MLA_QK_PAD), lambda b, h, i: (b, 0, h)),
            pl.BlockSpec((1, S, MLA_V), lambda b, h, i: (b, 0, h)),
        ],
        out_specs=pl.BlockSpec((1, tq, MLA_V), lambda b, h, i: (b, i, h)),
        out_shape=jax.ShapeDtypeStruct((B, S, MLA_WIDTH), BF16),
        compiler_params=_params(("arbitrary", "arbitrary", "arbitrary")),
        name="mla",
    )(q, k, v)


def _bias_kernel(table_ref, bucket_ref, o_ref):
    bucket = bucket_ref[...]
    for hd in range(SWA_HEADS):
        acc = jnp.zeros(bucket.shape, F32)
        for b in range(N_BUCKETS):
            acc = acc + jnp.where(bucket == b, table_ref[b, hd], 0.0)
        o_ref[hd] = acc


def _bias(rel_table, bucket):
    return pl.pallas_call(
        _bias_kernel,
        in_specs=[
            pl.BlockSpec(memory_space=pltpu.SMEM),
            pl.BlockSpec(memory_space=pltpu.VMEM),
        ],
        out_specs=pl.BlockSpec(memory_space=pltpu.VMEM),
        out_shape=jax.ShapeDtypeStruct((SWA_HEADS,) + bucket.shape, F32),
        name="t5bias",
    )(rel_table, bucket)


def _swa_kernel(sink_ref, q_ref, kp_ref, kc_ref, vp_ref, vc_ref, bias_ref, o_ref, *, nsub):
    t = pl.program_id(1)
    scale = SWA_HEAD_DIM ** -0.5
    sb = SWA_BLOCK
    qc = _chunk_of(lax.broadcasted_iota(jnp.int32, (sb, sb), 0))
    kc = _chunk_of(lax.broadcasted_iota(jnp.int32, (sb, sb), 1))
    valid_cur = kc <= qc
    first_gap = jnp.where(t > 0, 0, WINDOW_CHUNKS)

    for r in range(nsub):
        rows = slice(r * sb, (r + 1) * sb)
        if r == 0:
            k_prev, v_prev = kp_ref[0], vp_ref[0]
            valid_prev = (kc - 2) >= (qc - WINDOW_CHUNKS + first_gap)
        else:
            prev = slice((r - 1) * sb, r * sb)
            k_prev, v_prev = kc_ref[0, prev, :], vc_ref[0, prev, :]
            valid_prev = (kc - 2) >= (qc - WINDOW_CHUNKS)
        k_cur, v_cur = kc_ref[0, rows, :], vc_ref[0, rows, :]
        for hd in range(SWA_HEADS):
            kvh = hd // SWA_GROUP
            cols = slice(hd * SWA_HEAD_DIM, (hd + 1) * SWA_HEAD_DIM)
            kcols = slice(kvh * SWA_HEAD_DIM, (kvh + 1) * SWA_HEAD_DIM)
            q = q_ref[0, rows, cols]
            sp = _dot_nt(q, k_prev[:, kcols]) * scale + bias_ref[hd, :, :sb]
            sc = _dot_nt(q, k_cur[:, kcols]) * scale + bias_ref[hd, :, sb:]
            sp = jnp.where(valid_prev, sp, NEG)
            sc = jnp.where(valid_cur, sc, NEG)
            sink = sink_ref[hd]
            m = jnp.maximum(jnp.maximum(jnp.max(sp, axis=-1, keepdims=True),
                                        jnp.max(sc, axis=-1, keepdims=True)), sink)
            pp = jnp.exp(sp - m)
            pc = jnp.exp(sc - m)
            den = (jnp.sum(pp, axis=-1, keepdims=True) + jnp.sum(pc, axis=-1, keepdims=True)
                   + jnp.exp(sink - m))
            inv = 1.0 / den
            o = (_dot((pp * inv).astype(BF16), v_prev[:, kcols])
                 + _dot((pc * inv).astype(BF16), v_cur[:, kcols]))
            o_ref[0, rows, cols] = o.astype(BF16)


def _swa(sinks, qs, ks, vs, bias, nsub):
    B, S, _ = qs.shape
    tq = nsub * SWA_BLOCK
    prev_map = lambda b, t: (b, jnp.maximum(t * nsub - 1, 0), 0)
    cur_map = lambda b, t: (b, t, 0)
    return pl.pallas_call(
        functools.partial(_swa_kernel, nsub=nsub),
        grid=(B, S // tq),
        in_specs=[
            pl.BlockSpec(memory_space=pltpu.SMEM),
            pl.BlockSpec((1, tq, SWA_WIDTH), cur_map),
            pl.BlockSpec((1, SWA_BLOCK, SWA_KV_WIDTH), prev_map),
            pl.BlockSpec((1, tq, SWA_KV_WIDTH), cur_map),
            pl.BlockSpec((1, SWA_BLOCK, SWA_KV_WIDTH), prev_map),
            pl.BlockSpec((1, tq, SWA_KV_WIDTH), cur_map),
            pl.BlockSpec((SWA_HEADS, SWA_BLOCK, 2 * SWA_BLOCK), lambda b, t: (0, 0, 0)),
        ],
        out_specs=pl.BlockSpec((1, tq, SWA_WIDTH), cur_map),
        out_shape=jax.ShapeDtypeStruct((B, S, SWA_WIDTH), BF16),
        compiler_params=_params(("arbitrary", "arbitrary")),
        name="swa",
    )(sinks, qs, ks, ks, vs, vs, bias)


def _memkv_kernel(mem_ref, g_ref, w_ref, k_ref, v_ref):
    mn = _rms(mem_ref[0], g_ref[...]).astype(BF16)
    kv = _dot(mn, w_ref[...])
    k_ref[0] = kv[:, :MEM_WIDTH].astype(BF16)
    v_ref[0] = kv[:, MEM_WIDTH:].astype(BF16)


def _memkv(mem, g, w):
    B, M, _ = mem.shape
    const = lambda b: (0, 0)
    row = lambda b: (b, 0, 0)
    return pl.pallas_call(
        _memkv_kernel,
        grid=(B,),
        in_specs=[
            pl.BlockSpec((1, M, D_MODEL), row),
            pl.BlockSpec((1, D_MODEL), const),
            pl.BlockSpec((D_MODEL, 2 * MEM_WIDTH), const),
        ],
        out_specs=[pl.BlockSpec((1, M, MEM_WIDTH), row)] * 2,
        out_shape=[jax.ShapeDtypeStruct((B, M, MEM_WIDTH), BF16)] * 2,
        compiler_params=_params(("arbitrary",)),
        name="memkv",
    )(mem, g, w)


def _mem_kernel(q_ref, k_ref, v_ref, o_ref):
    scale = MEM_HEAD_DIM ** -0.5
    for hd in range(MEM_HEADS):
        cols = slice(hd * MEM_HEAD_DIM, (hd + 1) * MEM_HEAD_DIM)
        s = _dot_nt(q_ref[0, :, cols], k_ref[0, :, cols]) * scale
        m = jnp.max(s, axis=-1, keepdims=True)
        p = jnp.exp(s - m)
        inv = 1.0 / jnp.sum(p, axis=-1, keepdims=True)
        o_ref[0, :, cols] = _dot((p * inv).astype(BF16), v_ref[0, :, cols]).astype(BF16)


def _mem_attn(qs, kmem, vmem, tm):
    B, S, _ = qs.shape
    M = kmem.shape[1]
    return pl.pallas_call(
        _mem_kernel,
        grid=(B, S // tm),
        in_specs=[
            pl.BlockSpec((1, tm, MEM_WIDTH), lambda b, i: (b, i, 1)),
            pl.BlockSpec((1, M, MEM_WIDTH), lambda b, i: (b, 0, 0)),
            pl.BlockSpec((1, M, MEM_WIDTH), lambda b, i: (b, 0, 0)),
        ],
        out_specs=pl.BlockSpec((1, tm, MEM_WIDTH), lambda b, i: (b, i, 0)),
        out_shape=jax.ShapeDtypeStruct((B, S, MEM_WIDTH), BF16),
        compiler_params=_params(("arbitrary", "arbitrary")),
        name="memattn",
    )(qs, kmem, vmem)


def _out_kernel(x_ref, ya_ref, yb_ref, yc_ref, z_ref, w_ref, g_ref, o_ref):
    def gated(y_ref, lo, hi):
        z = z_ref[0, :, lo:hi].astype(F32)
        return (y_ref[0].astype(F32) * (z * jax.nn.sigmoid(z))).astype(BF16)

    y = _dot(gated(ya_ref, 0, MLA_WIDTH), w_ref[:MLA_WIDTH, :])
    y = y + _dot(gated(yb_ref, MLA_WIDTH, MLA_WIDTH + SWA_WIDTH),
                 w_ref[MLA_WIDTH:MLA_WIDTH + SWA_WIDTH, :])
    y = y + _dot(gated(yc_ref, MLA_WIDTH + SWA_WIDTH, MIX_WIDTH),
                 w_ref[MLA_WIDTH + SWA_WIDTH:, :])
    o_ref[0] = _rms(x_ref[0] + y, g_ref[...])


def _out(x, ya, yb, yc, z, w, g, tm):
    B, S, _ = x.shape
    row = lambda b, i: (b, i, 0)
    const = lambda b, i: (0, 0)
    return pl.pallas_call(
        _out_kernel,
        grid=(B, S // tm),
        in_specs=[
            pl.BlockSpec((1, tm, D_MODEL), row),
            pl.BlockSpec((1, tm, MLA_WIDTH), row),
            pl.BlockSpec((1, tm, SWA_WIDTH), row),
            pl.BlockSpec((1, tm, MEM_WIDTH), row),
            pl.BlockSpec((1, tm, MIX_WIDTH), row),
            pl.BlockSpec((MIX_WIDTH, D_MODEL), const, pipeline_mode=pl.Buffered(1)),
            pl.BlockSpec((1, D_MODEL), const),
        ],
        out_specs=pl.BlockSpec((1, tm, D_MODEL), row),
        out_shape=jax.ShapeDtypeStruct((B, S, D_MODEL), F32),
        compiler_params=_params(("arbitrary", "arbitrary")),
        name="outproj",
    )(x, ya, yb, yc, z, w, g)


def _rope_tables(seq):
    inv = 1.0 / (ROPE_THETA ** (jnp.arange(0, MLA_ROPE, 2, dtype=F32) / MLA_ROPE))
    ang = jnp.arange(seq, dtype=F32)[:, None] * inv[None, :]
    cos, sin = jnp.cos(ang), jnp.sin(ang)
    zero = jnp.zeros_like(cos)
    return (jnp.concatenate([cos, cos, zero, zero], axis=-1),
            jnp.concatenate([sin, sin, zero, zero], axis=-1))


def _t5_bucket(rel):
    nb = N_BUCKETS // 2
    max_exact = nb // 2
    bucket = jnp.where(rel > 0, nb, 0)
    n = jnp.abs(rel)
    nf = jnp.maximum(n, 1).astype(F32)
    large = max_exact + (jnp.log(nf / max_exact) / math.log(MAX_DISTANCE / max_exact)
                         * (nb - max_exact)).astype(jnp.int32)
    large = jnp.minimum(large, nb - 1)
    return bucket + jnp.where(n < max_exact, n, large)


def _rotate_half_cols(w):
    half = w.shape[-1] // 2
    return jnp.concatenate([-w[..., half:], w[..., :half]], axis=-1)


def _split_cols(w, sizes):
    out, acc = [], 0
    for s in sizes:
        out.append(w[:, acc:acc + s])
        acc += s
    return out


def kernel(x, mem, norm_in, w_in, norm_q, norm_kv, w_uq, w_ukv, attn_sinks, rel_bias,
           norm_mem, w_mem_kv, w_out, norm_final):
    B, S, _ = x.shape
    assert norm_in.shape[0] == 1, "single-layer trunk"

    (w_cq, w_ckv, w_kpe, w_zmla, w_qswa, w_kswa, w_vswa, w_zswa, w_qmem, w_zmem) = _split_cols(
        w_in[0], (MLA_QLORA, MLA_KVLORA, MLA_ROPE, MLA_WIDTH, SWA_WIDTH, SWA_KV_WIDTH,
                  SWA_KV_WIDTH, SWA_WIDTH, MEM_WIDTH, MEM_WIDTH))
    w1 = jnp.concatenate([w_cq, w_ckv, w_qswa, w_qmem, w_zmla, w_zswa, w_zmem, w_kswa, w_vswa,
                          w_kpe, _rotate_half_cols(w_kpe)], axis=-1).astype(BF16)
    wq = w_uq[0].reshape(MLA_QLORA, MLA_HEADS, MLA_NOPE + MLA_ROPE)
    wq_pe = wq[..., MLA_NOPE:]
    wuq = jnp.concatenate([wq[..., :MLA_NOPE], wq_pe, _rotate_half_cols(wq_pe)], axis=-1)
    wuq = wuq.reshape(MLA_QLORA, MLA_HEADS * MLA_QK_PAD).astype(BF16)
    wukv = w_ukv[0].astype(BF16)

    cos_t, sin_t = _rope_tables(S)
    q, k, v, qs, z, ks, vs = _proj(
        x, norm_in, w1, norm_q, norm_kv, wuq, wukv, cos_t, sin_t, tm=256)

    y_mla = _mla(q, k, v, tq=256, tk=256)

    qi = jnp.arange(SWA_BLOCK)
    kj = jnp.arange(2 * SWA_BLOCK)
    bucket = _t5_bucket(kj[None, :] - SWA_BLOCK - qi[:, None]).astype(jnp.int32)
    bias = _bias(rel_bias, bucket)
    y_swa = _swa(attn_sinks[0], qs, ks, vs, bias, nsub=2)

    kmem, vmem = _memkv(mem, norm_mem, w_mem_kv[0].astype(BF16))
    y_mem = _mem_attn(qs, kmem, vmem, tm=512)

    return _out(x, y_mla, y_swa, y_mem, z, w_out[0].astype(BF16), norm_final[None, :], tm=256)
```

```python
import functools
import math

import jax
import jax.numpy as jnp
from jax import lax
from jax.experimental import pallas as pl
from jax.experimental.pallas import tpu as pltpu

D_MODEL = 2048
CHUNK = 64
N_MEM = 256
EPS = 1e-6
NEG = -1e30

MLA_HEADS = 8
MLA_NOPE = 128
MLA_ROPE = 64
MLA_V = 128
MLA_QLORA = 512
MLA_KVLORA = 256
MLA_WIDTH = MLA_HEADS * MLA_V
MLA_QK_PAD = 256
ROPE_THETA = 10000.0

SWA_HEADS = 8
SWA_KV_HEADS = 2
SWA_GROUP = SWA_HEADS // SWA_KV_HEADS
SWA_HEAD_DIM = 64
SWA_WIDTH = SWA_HEADS * SWA_HEAD_DIM
SWA_KV_WIDTH = SWA_KV_HEADS * SWA_HEAD_DIM
WINDOW_CHUNKS = 2
SWA_BLOCK = 128

MEM_HEADS = 4
MEM_HEAD_DIM = 128
MEM_WIDTH = MEM_HEADS * MEM_HEAD_DIM

MIX_WIDTH = MLA_WIDTH + SWA_WIDTH + MEM_WIDTH

N_BUCKETS = 32
MAX_DISTANCE = 128

BF16 = jnp.bfloat16
F32 = jnp.float32

_C_CQ = 0
_C_CKV = _C_CQ + MLA_QLORA
_C_QS = _C_CKV + MLA_KVLORA
_C_Z = _C_QS + SWA_WIDTH + MEM_WIDTH
_C_KSWA = _C_Z + MIX_WIDTH
_C_VSWA = _C_KSWA + SWA_KV_WIDTH
_C_KPE = _C_VSWA + SWA_KV_WIDTH
_C_END = _C_KPE + 2 * MLA_ROPE

_MLA_Q_SCALE = (MLA_NOPE + MLA_ROPE) ** -0.5 * math.log2(math.e)

_VMEM_LIMIT = 56 * 1024 * 1024


def _params(sem):
    return pltpu.CompilerParams(dimension_semantics=sem, vmem_limit_bytes=_VMEM_LIMIT)


def _rms(v, g):
    return (v * lax.rsqrt(jnp.mean(v * v, axis=-1, keepdims=True) + EPS)) * g


def _dot(a, b):
    return jnp.dot(a, b, preferred_element_type=F32)


def _dot_nt(a, b):
    return lax.dot_general(a, b, (((1,), (1,)), ((), ())), preferred_element_type=F32)


def _chunk_of(pos):
    return jnp.right_shift(pos, CHUNK.bit_length() - 1)


def _rope128(v, cos_t, sin_t):
    return v * cos_t + pltpu.roll(v, 64, 1) * sin_t


def _proj_kernel(x_ref, gin_ref, w1_ref, gq_ref, gkv_ref, wuq_ref, wukv_ref, cos_ref, sin_ref,
                 q_ref, k_ref, v_ref, qs_ref, z_ref, ks_ref, vs_ref):
    h = _rms(x_ref[0], gin_ref[...]).astype(BF16)
    cos_t = cos_ref[...]
    sin_t = sin_ref[...]

    pa = _dot(h, w1_ref[:, _C_CQ:_C_QS])
    cq = _rms(pa[:, :MLA_QLORA], gq_ref[...]).astype(BF16)
    ckv = _rms(pa[:, MLA_QLORA:], gkv_ref[...]).astype(BF16)

    qs_ref[0] = _dot(h, w1_ref[:, _C_QS:_C_Z]).astype(BF16)
    z_ref[0] = _dot(h, w1_ref[:, _C_Z:_C_KSWA]).astype(BF16)
    pd = _dot(h, w1_ref[:, _C_KSWA:_C_END])
    ks_ref[0] = pd[:, :SWA_KV_WIDTH].astype(BF16)
    vs_ref[0] = pd[:, SWA_KV_WIDTH:2 * SWA_KV_WIDTH].astype(BF16)
    kpe = _rope128(pd[:, 2 * SWA_KV_WIDTH:], cos_t, sin_t).astype(BF16)

    qall = _dot(cq, wuq_ref[...]) * _MLA_Q_SCALE
    kv = _dot(ckv, wukv_ref[...])
    ones = jnp.ones((kv.shape[0], MLA_V), BF16)
    for hd in range(MLA_HEADS):
        o = hd * MLA_QK_PAD
        q_ref[0, :, o:o + MLA_NOPE] = qall[:, o:o + MLA_NOPE].astype(BF16)
        q_ref[0, :, o + MLA_NOPE:o + MLA_QK_PAD] = _rope128(
            qall[:, o + MLA_NOPE:o + MLA_QK_PAD], cos_t, sin_t).astype(BF16)
        k_ref[0, :, o:o + MLA_NOPE] = kv[:, o:o + MLA_NOPE].astype(BF16)
        k_ref[0, :, o + MLA_NOPE:o + MLA_QK_PAD] = kpe
        v_ref[0, :, o:o + MLA_V] = kv[:, o + MLA_NOPE:o + MLA_QK_PAD].astype(BF16)
        v_ref[0, :, o + MLA_V:o + MLA_QK_PAD] = ones


def _proj(x, gin, w1, gq, gkv, wuq, wukv, cos_t, sin_t, tm):
    B, S, _ = x.shape
    const = lambda b, i: (0, 0)
    row = lambda b, i: (b, i, 0)
    single = pl.Buffered(1)

    def out(width):
        return jax.ShapeDtypeStruct((B, S, width), BF16)

    return pl.pallas_call(
        _proj_kernel,
        grid=(B, S // tm),
        in_specs=[
            pl.BlockSpec((1, tm, D_MODEL), row),
            pl.BlockSpec((1, D_MODEL), const),
            pl.BlockSpec((D_MODEL, _C_END), const, pipeline_mode=single),
            pl.BlockSpec((1, MLA_QLORA), const),
            pl.BlockSpec((1, MLA_KVLORA), const),
            pl.BlockSpec((MLA_QLORA, MLA_HEADS * MLA_QK_PAD), const, pipeline_mode=single),
            pl.BlockSpec((MLA_KVLORA, MLA_HEADS * (MLA_NOPE + MLA_V)), const, pipeline_mode=single),
            pl.BlockSpec((tm, 128), lambda b, i: (i, 0)),
            pl.BlockSpec((tm, 128), lambda b, i: (i, 0)),
        ],
        out_specs=[
            pl.BlockSpec((1, tm, MLA_HEADS * MLA_QK_PAD), row),
            pl.BlockSpec((1, tm, MLA_HEADS * MLA_QK_PAD), row),
            pl.BlockSpec((1, tm, MLA_HEADS * MLA_QK_PAD), row),
            pl.BlockSpec((1, tm, SWA_WIDTH + MEM_WIDTH), row),
            pl.BlockSpec((1, tm, MIX_WIDTH), row),
            pl.BlockSpec((1, tm, SWA_KV_WIDTH), row),
            pl.BlockSpec((1, tm, SWA_KV_WIDTH), row),
        ],
        out_shape=[
            out(MLA_HEADS * MLA_QK_PAD), out(MLA_HEADS * MLA_QK_PAD), out(MLA_HEADS * MLA_QK_PAD),
            out(SWA_WIDTH + MEM_WIDTH), out(MIX_WIDTH), out(SWA_KV_WIDTH), out(SWA_KV_WIDTH),
        ],
        compiler_params=_params(("arbitrary", "arbitrary")),
        name="proj",
    )(x, gin, w1, gq, gkv, wuq, wukv, cos_t, sin_t)


def _mla_kernel(q_ref, k_ref, v_ref, o_ref, acc_ref, m_ref, s_ref, *, tq, tk, g):
    qi = pl.program_id(2)
    acc_ref[...] = jnp.zeros(acc_ref.shape, F32)
    m_ref[...] = jnp.full(m_ref.shape, NEG, F32)

    def scores(c, slot):
        start = pl.multiple_of(c * tk, tk)
        for hd in range(g):
            cols = slice(hd * MLA_QK_PAD, (hd + 1) * MLA_QK_PAD)
            s_ref[slot, hd] = _dot_nt(q_ref[0, :, cols], k_ref[0, pl.ds(start, tk), cols])

    def update(c, slot, masked):
        start = pl.multiple_of(c * tk, tk)
        if masked:
            q_chunk = _chunk_of(qi * tq + lax.broadcasted_iota(jnp.int32, (tq, tk), 0))
            k_chunk = _chunk_of(start + lax.broadcasted_iota(jnp.int32, (tq, tk), 1))
            mask = k_chunk <= q_chunk
        for hd in range(g):
            cols = slice(hd * MLA_QK_PAD, (hd + 1) * MLA_QK_PAD)
            s = s_ref[slot, hd]
            if masked:
                s = jnp.where(mask, s, NEG)
            m_old = m_ref[hd]
            m_new = jnp.maximum(m_old, jnp.max(s, axis=-1, keepdims=True))
            p = jnp.exp2(s - m_new).astype(BF16)
            acc_ref[hd] = (jnp.exp2(m_old - m_new) * acc_ref[hd]
                           + _dot(p, v_ref[0, pl.ds(start, tk), cols]))
            m_ref[hd] = m_new

    n_full = jnp.right_shift(qi, (tk // tq).bit_length() - 1)
    n_pairs = jnp.right_shift(n_full, 1)
    scores(0, 0)

    def body(jj, carry):
        c = 2 * jj
        scores(c + 1, 1)
        update(c, 0, False)
        scores(c + 2, 0)
        update(c + 1, 1, False)
        return carry

    lax.fori_loop(0, n_pairs, body, 0)
    c = 2 * n_pairs

    @pl.when(n_full > c)
    def _():
        scores(c + 1, 1)
        update(c, 0, False)
        update(c + 1, 1, True)

    @pl.when(n_full == c)
    def _():
        update(c, 0, True)

    for hd in range(g):
        acc = acc_ref[hd]
        o_ref[0, :, hd * MLA_V:(hd + 1) * MLA_V] = (acc[:, :MLA_V] / acc[:, MLA_V:]).astype(BF16)


def _mla(q, k, v, tq, tk, g):
    B, S, _ = q.shape
    assert tk % tq == 0 and (tk // tq) & (tk // tq - 1) == 0 and S % tk == 0
    return pl.pallas_call(
        functools.partial(_mla_kernel, tq=tq, tk=tk, g=g),
        grid=(B, MLA_HEADS // g, S // tq),
        in_specs=[
            pl.BlockSpec((1, tq, g * MLA_QK_PAD), lambda b, h, i: (b, i, h)),
            pl.BlockSpec((1, S, g * MLA_QK_PAD), lambda b, h, i: (b, 0, h)),
            pl.BlockSpec((1, S, g * MLA_QK_PAD), lambda b, h, i: (b, 0, h)),
        ],
        out_specs=pl.BlockSpec((1, tq, g * MLA_V), lambda b, h, i: (b, i, h)),
        out_shape=jax.ShapeDtypeStruct((B, S, MLA_WIDTH), BF16),
        scratch_shapes=[
            pltpu.VMEM((g, tq, MLA_QK_PAD), F32),
            pltpu.VMEM((g, tq, 1), F32),
            pltpu.VMEM((2, g, tq, tk), F32),
        ],
        compiler_params=_params(("arbitrary", "arbitrary", "arbitrary")),
        name="mla",
    )(q, k, v)


def _bias_kernel(table_ref, bucket_ref, o_ref):
    bucket = bucket_ref[...]
    for hd in range(SWA_HEADS):
        acc = jnp.zeros(bucket.shape, F32)
        for b in range(N_BUCKETS):
            acc = acc + jnp.where(bucket == b, table_ref[b, hd], 0.0)
        o_ref[hd] = acc


def _bias(rel_table, bucket):
    return pl.pallas_call(
        _bias_kernel,
        in_specs=[
            pl.BlockSpec(memory_space=pltpu.SMEM),
            pl.BlockSpec(memory_space=pltpu.VMEM),
        ],
        out_specs=pl.BlockSpec(memory_space=pltpu.VMEM),
        out_shape=jax.ShapeDtypeStruct((SWA_HEADS,) + bucket.shape, F32),
        name="t5bias",
    )(rel_table, bucket)


def _swa_kernel(sink_ref, q_ref, kp_ref, kc_ref, vp_ref, vc_ref, bias_ref, o_ref, *, nsub):
    t = pl.program_id(1)
    scale = SWA_HEAD_DIM ** -0.5
    sb = SWA_BLOCK
    qc = _chunk_of(lax.broadcasted_iota(jnp.int32, (sb, sb), 0))
    kc = _chunk_of(lax.broadcasted_iota(jnp.int32, (sb, sb), 1))
    valid_cur = kc <= qc
    first_gap = jnp.where(t > 0, 0, WINDOW_CHUNKS)

    for r in range(nsub):
        rows = slice(r * sb, (r + 1) * sb)
        if r == 0:
            k_prev, v_prev = kp_ref[0], vp_ref[0]
            valid_prev = (kc - 2) >= (qc - WINDOW_CHUNKS + first_gap)
        else:
            prev = slice((r - 1) * sb, r * sb)
            k_prev, v_prev = kc_ref[0, prev, :], vc_ref[0, prev, :]
            valid_prev = (kc - 2) >= (qc - WINDOW_CHUNKS)
        k_cur, v_cur = kc_ref[0, rows, :], vc_ref[0, rows, :]
        for hd in range(SWA_HEADS):
            kvh = hd // SWA_GROUP
            cols = slice(hd * SWA_HEAD_DIM, (hd + 1) * SWA_HEAD_DIM)
            kcols = slice(kvh * SWA_HEAD_DIM, (kvh + 1) * SWA_HEAD_DIM)
            q = q_ref[0, rows, cols]
            sp = _dot_nt(q, k_prev[:, kcols]) * scale + bias_ref[hd, :, :sb]
            sc = _dot_nt(q, k_cur[:, kcols]) * scale + bias_ref[hd, :, sb:]
            sp = jnp.where(valid_prev, sp, NEG)
            sc = jnp.where(valid_cur, sc, NEG)
            sink = sink_ref[hd]
            m = jnp.maximum(jnp.maximum(jnp.max(sp, axis=-1, keepdims=True),
                                        jnp.max(sc, axis=-1, keepdims=True)), sink)
            pp = jnp.exp(sp - m)
            pc = jnp.exp(sc - m)
            den = (jnp.sum(pp, axis=-1, keepdims=True) + jnp.sum(pc, axis=-1, keepdims=True)
                   + jnp.exp(sink - m))
            inv = 1.0 / den
            o = (_dot((pp * inv).astype(BF16), v_prev[:, kcols])
                 + _dot((pc * inv).astype(BF16), v_cur[:, kcols]))
            o_ref[0, rows, cols] = o.astype(BF16)


def _swa(sinks, qs, ks, vs, bias, nsub):
    B, S, _ = qs.shape
    tq = nsub * SWA_BLOCK
    prev_map = lambda b, t: (b, jnp.maximum(t * nsub - 1, 0), 0)
    cur_map = lambda b, t: (b, t, 0)
    return pl.pallas_call(
        functools.partial(_swa_kernel, nsub=nsub),
        grid=(B, S // tq),
        in_specs=[
            pl.BlockSpec(memory_space=pltpu.SMEM),
            pl.BlockSpec((1, tq, SWA_WIDTH), cur_map),
            pl.BlockSpec((1, SWA_BLOCK, SWA_KV_WIDTH), prev_map),
            pl.BlockSpec((1, tq, SWA_KV_WIDTH), cur_map),
            pl.BlockSpec((1, SWA_BLOCK, SWA_KV_WIDTH), prev_map),
            pl.BlockSpec((1, tq, SWA_KV_WIDTH), cur_map),
            pl.BlockSpec((SWA_HEADS, SWA_BLOCK, 2 * SWA_BLOCK), lambda b, t: (0, 0, 0)),
        ],
        out_specs=pl.BlockSpec((1, tq, SWA_WIDTH), cur_map),
        out_shape=jax.ShapeDtypeStruct((B, S, SWA_WIDTH), BF16),
        compiler_params=_params(("arbitrary", "arbitrary")),
        name="swa",
    )(sinks, qs, ks, ks, vs, vs, bias)


def _memkv_kernel(mem_ref, g_ref, w_ref, k_ref, v_ref):
    mn = _rms(mem_ref[0], g_ref[...]).astype(BF16)
    kv = _dot(mn, w_ref[...])
    k_ref[0] = kv[:, :MEM_WIDTH].astype(BF16)
    v_ref[0] = kv[:, MEM_WIDTH:].astype(BF16)


def _memkv(mem, g, w):
    B, M, _ = mem.shape
    const = lambda b: (0, 0)
    row = lambda b: (b, 0, 0)
    return pl.pallas_call(
        _memkv_kernel,
        grid=(B,),
        in_specs=[
            pl.BlockSpec((1, M, D_MODEL), row),
            pl.BlockSpec((1, D_MODEL), const),
            pl.BlockSpec((D_MODEL, 2 * MEM_WIDTH), const),
        ],
        out_specs=[pl.BlockSpec((1, M, MEM_WIDTH), row)] * 2,
        out_shape=[jax.ShapeDtypeStruct((B, M, MEM_WIDTH), BF16)] * 2,
        compiler_params=_params(("arbitrary",)),
        name="memkv",
    )(mem, g, w)


def _mem_kernel(q_ref, k_ref, v_ref, o_ref):
    scale = MEM_HEAD_DIM ** -0.5
    for hd in range(MEM_HEADS):
        cols = slice(hd * MEM_HEAD_DIM, (hd + 1) * MEM_HEAD_DIM)
        s = _dot_nt(q_ref[0, :, cols], k_ref[0, :, cols]) * scale
        m = jnp.max(s, axis=-1, keepdims=True)
        p = jnp.exp(s - m)
        inv = 1.0 / jnp.sum(p, axis=-1, keepdims=True)
        o_ref[0, :, cols] = _dot((p * inv).astype(BF16), v_ref[0, :, cols]).astype(BF16)


def _mem_attn(qs, kmem, vmem, tm):
    B, S, _ = qs.shape
    M = kmem.shape[1]
    return pl.pallas_call(
        _mem_kernel,
        grid=(B, S // tm),
        in_specs=[
            pl.BlockSpec((1, tm, MEM_WIDTH), lambda b, i: (b, i, 1)),
            pl.BlockSpec((1, M, MEM_WIDTH), lambda b, i: (b, 0, 0)),
            pl.BlockSpec((1, M, MEM_WIDTH), lambda b, i: (b, 0, 0)),
        ],
        out_specs=pl.BlockSpec((1, tm, MEM_WIDTH), lambda b, i: (b, i, 0)),
        out_shape=jax.ShapeDtypeStruct((B, S, MEM_WIDTH), BF16),
        compiler_params=_params(("arbitrary", "arbitrary")),
        name="memattn",
    )(qs, kmem, vmem)


def _out_kernel(x_ref, ya_ref, yb_ref, yc_ref, z_ref, w_ref, g_ref, o_ref):
    def gated(y_ref, lo, hi):
        z = z_ref[0, :, lo:hi].astype(F32)
        return (y_ref[0].astype(F32) * (z * jax.nn.sigmoid(z))).astype(BF16)

    y = _dot(gated(ya_ref, 0, MLA_WIDTH), w_ref[:MLA_WIDTH, :])
    y = y + _dot(gated(yb_ref, MLA_WIDTH, MLA_WIDTH + SWA_WIDTH),
                 w_ref[MLA_WIDTH:MLA_WIDTH + SWA_WIDTH, :])
    y = y + _dot(gated(yc_ref, MLA_WIDTH + SWA_WIDTH, MIX_WIDTH),
                 w_ref[MLA_WIDTH + SWA_WIDTH:, :])
    o_ref[0] = _rms(x_ref[0] + y, g_ref[...])


def _out(x, ya, yb, yc, z, w, g, tm):
    B, S, _ = x.shape
    row = lambda b, i: (b, i, 0)
    const = lambda b, i: (0, 0)
    return pl.pallas_call(
        _out_kernel,
        grid=(B, S // tm),
        in_specs=[
            pl.BlockSpec((1, tm, D_MODEL), row),
            pl.BlockSpec((1, tm, MLA_WIDTH), row),
            pl.BlockSpec((1, tm, SWA_WIDTH), row),
            pl.BlockSpec((1, tm, MEM_WIDTH), row),
            pl.BlockSpec((1, tm, MIX_WIDTH), row),
            pl.BlockSpec((MIX_WIDTH, D_MODEL), const, pipeline_mode=pl.Buffered(1)),
            pl.BlockSpec((1, D_MODEL), const),
        ],
        out_specs=pl.BlockSpec((1, tm, D_MODEL), row),
        out_shape=jax.ShapeDtypeStruct((B, S, D_MODEL), F32),
        compiler_params=_params(("arbitrary", "arbitrary")),
        name="outproj",
    )(x, ya, yb, yc, z, w, g)


def _rope_tables(seq):
    inv = 1.0 / (ROPE_THETA ** (jnp.arange(0, MLA_ROPE, 2, dtype=F32) / MLA_ROPE))
    ang = jnp.arange(seq, dtype=F32)[:, None] * inv[None, :]
    cos, sin = jnp.cos(ang), jnp.sin(ang)
    zero = jnp.zeros_like(cos)
    return (jnp.concatenate([cos, cos, zero, zero], axis=-1),
            jnp.concatenate([sin, sin, zero, zero], axis=-1))


def _t5_bucket(rel):
    nb = N_BUCKETS // 2
    max_exact = nb // 2
    bucket = jnp.where(rel > 0, nb, 0)
    n = jnp.abs(rel)
    nf = jnp.maximum(n, 1).astype(F32)
    large = max_exact + (jnp.log(nf / max_exact) / math.log(MAX_DISTANCE / max_exact)
                         * (nb - max_exact)).astype(jnp.int32)
    large = jnp.minimum(large, nb - 1)
    return bucket + jnp.where(n < max_exact, n, large)


def _rotate_half_cols(w):
    half = w.shape[-1] // 2
    return jnp.concatenate([-w[..., half:], w[..., :half]], axis=-1)


def _split_cols(w, sizes):
    out, acc = [], 0
    for s in sizes:
        out.append(w[:, acc:acc + s])
        acc += s
    return out


def kernel(x, mem, norm_in, w_in, norm_q, norm_kv, w_uq, w_ukv, attn_sinks, rel_bias,
           norm_mem, w_mem_kv, w_out, norm_final):
    B, S, _ = x.shape
    assert norm_in.shape[0] == 1, "single-layer trunk"

    (w_cq, w_ckv, w_kpe, w_zmla, w_qswa, w_kswa, w_vswa, w_zswa, w_qmem, w_zmem) = _split_cols(
        w_in[0], (MLA_QLORA, MLA_KVLORA, MLA_ROPE, MLA_WIDTH, SWA_WIDTH, SWA_KV_WIDTH,
                  SWA_KV_WIDTH, SWA_WIDTH, MEM_WIDTH, MEM_WIDTH))
    w1 = jnp.concatenate([w_cq, w_ckv, w_qswa, w_qmem, w_zmla, w_zswa, w_zmem, w_kswa, w_vswa,
                          w_kpe, _rotate_half_cols(w_kpe)], axis=-1).astype(BF16)
    wq = w_uq[0].reshape(MLA_QLORA, MLA_HEADS, MLA_NOPE + MLA_ROPE)
    wq_pe = wq[..., MLA_NOPE:]
    wuq = jnp.concatenate([wq[..., :MLA_NOPE], wq_pe, _rotate_half_cols(wq_pe)], axis=-1)
    wuq = wuq.reshape(MLA_QLORA, MLA_HEADS * MLA_QK_PAD).astype(BF16)
    wukv = w_ukv[0].astype(BF16)

    cos_t, sin_t = _rope_tables(S)
    q, k, v, qs, z, ks, vs = _proj(
        x, norm_in, w1, norm_q, norm_kv, wuq, wukv, cos_t, sin_t, tm=256)

    y_mla = _mla(q, k, v, tq=256, tk=512, g=4)

    qi = jnp.arange(SWA_BLOCK)
    kj = jnp.arange(2 * SWA_BLOCK)
    bucket = _t5_bucket(kj[None, :] - SWA_BLOCK - qi[:, None]).astype(jnp.int32)
    bias = _bias(rel_bias, bucket)
    y_swa = _swa(attn_sinks[0], qs, ks, vs, bias, nsub=2)

    kmem, vmem = _memkv(mem, norm_mem, w_mem_kv[0].astype(BF16))
    y_mem = _mem_attn(qs, kmem, vmem, tm=512)

    return _out(x, y_mla, y_swa, y_mem, z, w_out[0].astype(BF16), norm_final[None, :], tm=256)
```

```python
import functools
import math

import jax
import jax.numpy as jnp
from jax import lax
from jax.experimental import pallas as pl
from jax.experimental.pallas import tpu as pltpu

D_MODEL = 2048
CHUNK = 64
N_MEM = 256
EPS = 1e-6
NEG = -1e30

MLA_HEADS = 8
MLA_NOPE = 128
MLA_ROPE = 64
MLA_V = 128
MLA_QLORA = 512
MLA_KVLORA = 256
MLA_WIDTH = MLA_HEADS * MLA_V
MLA_QK_PAD = 256
ROPE_THETA = 10000.0

SWA_HEADS = 8
SWA_KV_HEADS = 2
SWA_GROUP = SWA_HEADS // SWA_KV_HEADS
SWA_HEAD_DIM = 64
SWA_WIDTH = SWA_HEADS * SWA_HEAD_DIM
SWA_KV_WIDTH = SWA_KV_HEADS * SWA_HEAD_DIM
WINDOW_CHUNKS = 2
SWA_BLOCK = 128

MEM_HEADS = 4
MEM_HEAD_DIM = 128
MEM_WIDTH = MEM_HEADS * MEM_HEAD_DIM

MIX_WIDTH = MLA_WIDTH + SWA_WIDTH + MEM_WIDTH

N_BUCKETS = 32
MAX_DISTANCE = 128

BF16 = jnp.bfloat16
F32 = jnp.float32

_C_CQ = 0
_C_CKV = _C_CQ + MLA_QLORA
_C_QS = _C_CKV + MLA_KVLORA
_C_Z = _C_QS + SWA_WIDTH + MEM_WIDTH
_C_KSWA = _C_Z + MIX_WIDTH
_C_VSWA = _C_KSWA + SWA_KV_WIDTH
_C_KPE = _C_VSWA + SWA_KV_WIDTH
_C_END = _C_KPE + 2 * MLA_ROPE

_MLA_Q_SCALE = (MLA_NOPE + MLA_ROPE) ** -0.5 * math.log2(math.e)

_SWA_KV_PAD = 2 * SWA_KV_HEADS * 128

_VMEM_LIMIT = 56 * 1024 * 1024


def _params(sem):
    return pltpu.CompilerParams(dimension_semantics=sem, vmem_limit_bytes=_VMEM_LIMIT)


def _rms(v, g):
    return (v * lax.rsqrt(jnp.mean(v * v, axis=-1, keepdims=True) + EPS)) * g


def _dot(a, b):
    return jnp.dot(a, b, preferred_element_type=F32)


def _dot_nt(a, b):
    return lax.dot_general(a, b, (((1,), (1,)), ((), ())), preferred_element_type=F32)


def _chunk_of(pos):
    return jnp.right_shift(pos, CHUNK.bit_length() - 1)


def _rope128(v, cos_t, sin_t):
    return v * cos_t + pltpu.roll(v, 64, 1) * sin_t


def _store_swa_kv(ref, kv):
    low = lax.broadcasted_iota(jnp.int32, kv.shape, 1) < SWA_HEAD_DIM
    swapped = pltpu.roll(kv, SWA_HEAD_DIM, 1)
    pieces = (jnp.where(low, kv, 0.0), jnp.where(low, 0.0, swapped),
              jnp.where(low, swapped, 0.0), jnp.where(low, 0.0, kv))
    for i, piece in enumerate(pieces):
        ref[0, :, i * 128:(i + 1) * 128] = piece.astype(BF16)


def _proj_kernel(x_ref, gin_ref, w1_ref, gq_ref, gkv_ref, wuq_ref, wukv_ref, cos_ref, sin_ref,
                 q_ref, k_ref, v_ref, qs_ref, z_ref, ks_ref, vs_ref):
    h = _rms(x_ref[0], gin_ref[...]).astype(BF16)
    cos_t = cos_ref[...]
    sin_t = sin_ref[...]

    pa = _dot(h, w1_ref[:, _C_CQ:_C_QS])
    cq = _rms(pa[:, :MLA_QLORA], gq_ref[...]).astype(BF16)
    ckv = _rms(pa[:, MLA_QLORA:], gkv_ref[...]).astype(BF16)

    qs_ref[0] = _dot(h, w1_ref[:, _C_QS:_C_Z]).astype(BF16)
    z_ref[0] = _dot(h, w1_ref[:, _C_Z:_C_KSWA]).astype(BF16)
    pd = _dot(h, w1_ref[:, _C_KSWA:_C_END])
    _store_swa_kv(ks_ref, pd[:, :SWA_KV_WIDTH])
    _store_swa_kv(vs_ref, pd[:, SWA_KV_WIDTH:2 * SWA_KV_WIDTH])
    kpe = _rope128(pd[:, 2 * SWA_KV_WIDTH:], cos_t, sin_t).astype(BF16)

    qall = _dot(cq, wuq_ref[...]) * _MLA_Q_SCALE
    kv = _dot(ckv, wukv_ref[...])
    ones = jnp.ones((kv.shape[0], MLA_V), BF16)
    for hd in range(MLA_HEADS):
        o = hd * MLA_QK_PAD
        q_ref[0, :, o:o + MLA_NOPE] = qall[:, o:o + MLA_NOPE].astype(BF16)
        q_ref[0, :, o + MLA_NOPE:o + MLA_QK_PAD] = _rope128(
            qall[:, o + MLA_NOPE:o + MLA_QK_PAD], cos_t, sin_t).astype(BF16)
        k_ref[0, :, o:o + MLA_NOPE] = kv[:, o:o + MLA_NOPE].astype(BF16)
        k_ref[0, :, o + MLA_NOPE:o + MLA_QK_PAD] = kpe
        v_ref[0, :, o:o + MLA_V] = kv[:, o + MLA_NOPE:o + MLA_QK_PAD].astype(BF16)
        v_ref[0, :, o + MLA_V:o + MLA_QK_PAD] = ones


def _proj(x, gin, w1, gq, gkv, wuq, wukv, cos_t, sin_t, tm):
    B, S, _ = x.shape
    const = lambda b, i: (0, 0)
    row = lambda b, i: (b, i, 0)
    single = pl.Buffered(1)

    def out(width):
        return jax.ShapeDtypeStruct((B, S, width), BF16)

    return pl.pallas_call(
        _proj_kernel,
        grid=(B, S // tm),
        in_specs=[
            pl.BlockSpec((1, tm, D_MODEL), row),
            pl.BlockSpec((1, D_MODEL), const),
            pl.BlockSpec((D_MODEL, _C_END), const, pipeline_mode=single),
            pl.BlockSpec((1, MLA_QLORA), const),
            pl.BlockSpec((1, MLA_KVLORA), const),
            pl.BlockSpec((MLA_QLORA, MLA_HEADS * MLA_QK_PAD), const, pipeline_mode=single),
            pl.BlockSpec((MLA_KVLORA, MLA_HEADS * (MLA_NOPE + MLA_V)), const, pipeline_mode=single),
            pl.BlockSpec((tm, 128), lambda b, i: (i, 0)),
            pl.BlockSpec((tm, 128), lambda b, i: (i, 0)),
        ],
        out_specs=[
            pl.BlockSpec((1, tm, MLA_HEADS * MLA_QK_PAD), row),
            pl.BlockSpec((1, tm, MLA_HEADS * MLA_QK_PAD), row),
            pl.BlockSpec((1, tm, MLA_HEADS * MLA_QK_PAD), row),
            pl.BlockSpec((1, tm, SWA_WIDTH + MEM_WIDTH), row),
            pl.BlockSpec((1, tm, MIX_WIDTH), row),
            pl.BlockSpec((1, tm, _SWA_KV_PAD), row),
            pl.BlockSpec((1, tm, _SWA_KV_PAD), row),
        ],
        out_shape=[
            out(MLA_HEADS * MLA_QK_PAD), out(MLA_HEADS * MLA_QK_PAD), out(MLA_HEADS * MLA_QK_PAD),
            out(SWA_WIDTH + MEM_WIDTH), out(MIX_WIDTH), out(_SWA_KV_PAD), out(_SWA_KV_PAD),
        ],
        compiler_params=_params(("arbitrary", "arbitrary")),
        name="proj",
    )(x, gin, w1, gq, gkv, wuq, wukv, cos_t, sin_t)


def _mla_kernel(q_ref, k_ref, v_ref, o_ref, acc_ref, m_ref, s_ref, *, tq, tk, g):
    qi = pl.program_id(2)
    acc_ref[...] = jnp.zeros(acc_ref.shape, F32)
    m_ref[...] = jnp.full(m_ref.shape, NEG, F32)

    def scores(c, slot):
        start = pl.multiple_of(c * tk, tk)
        for hd in range(g):
            cols = slice(hd * MLA_QK_PAD, (hd + 1) * MLA_QK_PAD)
            s_ref[slot, hd] = _dot_nt(q_ref[0, :, cols], k_ref[0, pl.ds(start, tk), cols])

    def update(c, slot, masked):
        start = pl.multiple_of(c * tk, tk)
        if masked:
            q_chunk = _chunk_of(qi * tq + lax.broadcasted_iota(jnp.int32, (tq, tk), 0))
            k_chunk = _chunk_of(start + lax.broadcasted_iota(jnp.int32, (tq, tk), 1))
            mask = k_chunk <= q_chunk
        for hd in range(g):
            cols = slice(hd * MLA_QK_PAD, (hd + 1) * MLA_QK_PAD)
            s = s_ref[slot, hd]
            if masked:
                s = jnp.where(mask, s, NEG)
            m_old = m_ref[hd]
            m_new = jnp.maximum(m_old, jnp.max(s, axis=-1, keepdims=True))
            p = jnp.exp2(s - m_new).astype(BF16)
            acc_ref[hd] = (jnp.exp2(m_old - m_new) * acc_ref[hd]
                           + _dot(p, v_ref[0, pl.ds(start, tk), cols]))
            m_ref[hd] = m_new

    n_full = jnp.right_shift(qi, (tk // tq).bit_length() - 1)
    n_pairs = jnp.right_shift(n_full, 1)
    scores(0, 0)

    def body(jj, carry):
        c = 2 * jj
        scores(c + 1, 1)
        update(c, 0, False)
        scores(c + 2, 0)
        update(c + 1, 1, False)
        return carry

    lax.fori_loop(0, n_pairs, body, 0)
    c = 2 * n_pairs

    @pl.when(n_full > c)
    def _():
        scores(c + 1, 1)
        update(c, 0, False)
        update(c + 1, 1, True)

    @pl.when(n_full == c)
    def _():
        update(c, 0, True)

    for hd in range(g):
        acc = acc_ref[hd]
        o_ref[0, :, hd * MLA_V:(hd + 1) * MLA_V] = (acc[:, :MLA_V] / acc[:, MLA_V:]).astype(BF16)


def _mla(q, k, v, tq, tk, g):
    B, S, _ = q.shape
    assert tk % tq == 0 and (tk // tq) & (tk // tq - 1) == 0 and S % tk == 0
    return pl.pallas_call(
        functools.partial(_mla_kernel, tq=tq, tk=tk, g=g),
        grid=(B, MLA_HEADS // g, S // tq),
        in_specs=[
            pl.BlockSpec((1, tq, g * MLA_QK_PAD), lambda b, h, i: (b, i, h)),
            pl.BlockSpec((1, S, g * MLA_QK_PAD), lambda b, h, i: (b, 0, h)),
            pl.BlockSpec((1, S, g * MLA_QK_PAD), lambda b, h, i: (b, 0, h)),
        ],
        out_specs=pl.BlockSpec((1, tq, g * MLA_V), lambda b, h, i: (b, i, h)),
        out_shape=jax.ShapeDtypeStruct((B, S, MLA_WIDTH), BF16),
        scratch_shapes=[
            pltpu.VMEM((g, tq, MLA_QK_PAD), F32),
            pltpu.VMEM((g, tq, 1), F32),
            pltpu.VMEM((2, g, tq, tk), F32),
        ],
        compiler_params=_params(("arbitrary", "arbitrary", "arbitrary")),
        name="mla",
    )(q, k, v)


def _bias_kernel(table_ref, bucket_ref, o_ref):
    bucket = bucket_ref[...]
    for hd in range(SWA_HEADS):
        acc = jnp.zeros(bucket.shape, F32)
        for b in range(N_BUCKETS):
            acc = acc + jnp.where(bucket == b, table_ref[b, hd], 0.0)
        kvh, within = divmod(hd, SWA_GROUP)
        par, half = within % 2, within // 2
        o_ref[kvh, par, half * SWA_BLOCK:(half + 1) * SWA_BLOCK, :] = acc


def _bias(rel_table, bucket):
    return pl.pallas_call(
        _bias_kernel,
        in_specs=[
            pl.BlockSpec(memory_space=pltpu.SMEM),
            pl.BlockSpec(memory_space=pltpu.VMEM),
        ],
        out_specs=pl.BlockSpec(memory_space=pltpu.VMEM),
        out_shape=jax.ShapeDtypeStruct((SWA_KV_HEADS, 2, 2 * SWA_BLOCK, 2 * SWA_BLOCK), F32),
        name="t5bias",
    )(rel_table, bucket)


def _swa_kernel(sink_ref, q_ref, kp_ref, kc_ref, vp_ref, vc_ref, bias_ref, o_ref, kb_ref, vb_ref,
                *, nsub):
    t = pl.program_id(1)
    scale = SWA_HEAD_DIM ** -0.5
    sb = SWA_BLOCK
    kb_ref[:sb] = kp_ref[0]
    kb_ref[sb:] = kc_ref[0]
    vb_ref[:sb] = vp_ref[0]
    vb_ref[sb:] = vc_ref[0]

    row = lax.broadcasted_iota(jnp.int32, (2 * sb, 2 * sb), 0)
    q_chunk = _chunk_of(row & (sb - 1))
    b_chunk = _chunk_of(lax.broadcasted_iota(jnp.int32, (2 * sb, 2 * sb), 1))
    valid_any = jnp.logical_and(b_chunk >= q_chunk, b_chunk <= q_chunk + WINDOW_CHUNKS)
    first_lo = jnp.where(t > 0, 0, 2)
    valid_first = jnp.logical_and(valid_any, b_chunk >= first_lo)
    upper = lax.broadcasted_iota(jnp.int32, (2 * sb, 1), 0) < sb

    for r in range(nsub):
        valid = valid_first if r == 0 else valid_any
        rows = slice(r * sb, (r + 1) * sb)
        band = slice(r * sb, (r + 2) * sb)
        for kvh in range(SWA_KV_HEADS):
            pair0 = slice(2 * kvh * 128, (2 * kvh + 1) * 128)
            pair1 = slice((2 * kvh + 1) * 128, (2 * kvh + 2) * 128)
            q = jnp.concatenate([q_ref[0, rows, pair0], q_ref[0, rows, pair1]], axis=0)
            o = None
            for par in range(2):
                kcols = slice((2 * kvh + par) * 128, (2 * kvh + par + 1) * 128)
                hd = SWA_GROUP * kvh + par
                s = _dot_nt(q, kb_ref[band, kcols]) * scale + bias_ref[kvh, par]
                s = jnp.where(valid, s, NEG)
                sink = jnp.where(upper, sink_ref[hd], sink_ref[hd + 2])
                m = jnp.maximum(jnp.max(s, axis=-1, keepdims=True), sink)
                p = jnp.exp(s - m)
                den = jnp.sum(p, axis=-1, keepdims=True) + jnp.exp(sink - m)
                pv = _dot((p * (1.0 / den)).astype(BF16), vb_ref[band, kcols])
                o = pv if o is None else o + pv
            o_ref[0, rows, pair0] = o[:sb].astype(BF16)
            o_ref[0, rows, pair1] = o[sb:].astype(BF16)


def _swa(sinks, qs, ks, vs, bias, nsub):
    B, S, _ = qs.shape
    tq = nsub * SWA_BLOCK
    prev_map = lambda b, t: (b, jnp.maximum(t * nsub - 1, 0), 0)
    cur_map = lambda b, t: (b, t, 0)
    return pl.pallas_call(
        functools.partial(_swa_kernel, nsub=nsub),
        grid=(B, S // tq),
        in_specs=[
            pl.BlockSpec(memory_space=pltpu.SMEM),
            pl.BlockSpec((1, tq, SWA_WIDTH), cur_map),
            pl.BlockSpec((1, SWA_BLOCK, _SWA_KV_PAD), prev_map),
            pl.BlockSpec((1, tq, _SWA_KV_PAD), cur_map),
            pl.BlockSpec((1, SWA_BLOCK, _SWA_KV_PAD), prev_map),
            pl.BlockSpec((1, tq, _SWA_KV_PAD), cur_map),
            pl.BlockSpec((SWA_KV_HEADS, 2, 2 * SWA_BLOCK, 2 * SWA_BLOCK), lambda b, t: (0, 0, 0, 0)),
        ],
        out_specs=pl.BlockSpec((1, tq, SWA_WIDTH), cur_map),
        out_shape=jax.ShapeDtypeStruct((B, S, SWA_WIDTH), BF16),
        scratch_shapes=[
            pltpu.VMEM((tq + SWA_BLOCK, _SWA_KV_PAD), BF16),
            pltpu.VMEM((tq + SWA_BLOCK, _SWA_KV_PAD), BF16),
        ],
        compiler_params=_params(("arbitrary", "arbitrary")),
        name="swa",
    )(sinks, qs, ks, ks, vs, vs, bias)


def _memkv_kernel(mem_ref, g_ref, w_ref, k_ref, v_ref):
    mn = _rms(mem_ref[0], g_ref[...]).astype(BF16)
    kv = _dot(mn, w_ref[...])
    k_ref[0] = kv[:, :MEM_WIDTH].astype(BF16)
    v_ref[0] = kv[:, MEM_WIDTH:].astype(BF16)


def _memkv(mem, g, w):
    B, M, _ = mem.shape
    const = lambda b: (0, 0)
    row = lambda b: (b, 0, 0)
    return pl.pallas_call(
        _memkv_kernel,
        grid=(B,),
        in_specs=[
            pl.BlockSpec((1, M, D_MODEL), row),
            pl.BlockSpec((1, D_MODEL), const),
            pl.BlockSpec((D_MODEL, 2 * MEM_WIDTH), const),
        ],
        out_specs=[pl.BlockSpec((1, M, MEM_WIDTH), row)] * 2,
        out_shape=[jax.ShapeDtypeStruct((B, M, MEM_WIDTH), BF16)] * 2,
        compiler_params=_params(("arbitrary",)),
        name="memkv",
    )(mem, g, w)


def _mem_kernel(q_ref, k_ref, v_ref, o_ref):
    scale = MEM_HEAD_DIM ** -0.5
    for hd in range(MEM_HEADS):
        cols = slice(hd * MEM_HEAD_DIM, (hd + 1) * MEM_HEAD_DIM)
        s = _dot_nt(q_ref[0, :, cols], k_ref[0, :, cols]) * scale
        m = jnp.max(s, axis=-1, keepdims=True)
        p = jnp.exp(s - m)
        inv = 1.0 / jnp.sum(p, axis=-1, keepdims=True)
        o_ref[0, :, cols] = _dot((p * inv).astype(BF16), v_ref[0, :, cols]).astype(BF16)


def _mem_attn(qs, kmem, vmem, tm):
    B, S, _ = qs.shape
    M = kmem.shape[1]
    return pl.pallas_call(
        _mem_kernel,
        grid=(B, S // tm),
        in_specs=[
            pl.BlockSpec((1, tm, MEM_WIDTH), lambda b, i: (b, i, 1)),
            pl.BlockSpec((1, M, MEM_WIDTH), lambda b, i: (b, 0, 0)),
            pl.BlockSpec((1, M, MEM_WIDTH), lambda b, i: (b, 0, 0)),
        ],
        out_specs=pl.BlockSpec((1, tm, MEM_WIDTH), lambda b, i: (b, i, 0)),
        out_shape=jax.ShapeDtypeStruct((B, S, MEM_WIDTH), BF16),
        compiler_params=_params(("arbitrary", "arbitrary")),
        name="memattn",
    )(qs, kmem, vmem)


def _out_kernel(x_ref, ya_ref, yb_ref, yc_ref, z_ref, w_ref, g_ref, o_ref):
    def gated(y_ref, lo, hi):
        z = z_ref[0, :, lo:hi].astype(F32)
        return (y_ref[0].astype(F32) * (z * jax.nn.sigmoid(z))).astype(BF16)

    y = _dot(gated(ya_ref, 0, MLA_WIDTH), w_ref[:MLA_WIDTH, :])
    y = y + _dot(gated(yb_ref, MLA_WIDTH, MLA_WIDTH + SWA_WIDTH),
                 w_ref[MLA_WIDTH:MLA_WIDTH + SWA_WIDTH, :])
    y = y + _dot(gated(yc_ref, MLA_WIDTH + SWA_WIDTH, MIX_WIDTH),
                 w_ref[MLA_WIDTH + SWA_WIDTH:, :])
    o_ref[0] = _rms(x_ref[0] + y, g_ref[...])


def _out(x, ya, yb, yc, z, w, g, tm):
    B, S, _ = x.shape
    row = lambda b, i: (b, i, 0)
    const = lambda b, i: (0, 0)
    return pl.pallas_call(
        _out_kernel,
        grid=(B, S // tm),
        in_specs=[
            pl.BlockSpec((1, tm, D_MODEL), row),
            pl.BlockSpec((1, tm, MLA_WIDTH), row),
            pl.BlockSpec((1, tm, SWA_WIDTH), row),
            pl.BlockSpec((1, tm, MEM_WIDTH), row),
            pl.BlockSpec((1, tm, MIX_WIDTH), row),
            pl.BlockSpec((MIX_WIDTH, D_MODEL), const, pipeline_mode=pl.Buffered(1)),
            pl.BlockSpec((1, D_MODEL), const),
        ],
        out_specs=pl.BlockSpec((1, tm, D_MODEL), row),
        out_shape=jax.ShapeDtypeStruct((B, S, D_MODEL), F32),
        compiler_params=_params(("arbitrary", "arbitrary")),
        name="outproj",
    )(x, ya, yb, yc, z, w, g)


def _rope_tables(seq):
    inv = 1.0 / (ROPE_THETA ** (jnp.arange(0, MLA_ROPE, 2, dtype=F32) / MLA_ROPE))
    ang = jnp.arange(seq, dtype=F32)[:, None] * inv[None, :]
    cos, sin = jnp.cos(ang), jnp.sin(ang)
    zero = jnp.zeros_like(cos)
    return (jnp.concatenate([cos, cos, zero, zero], axis=-1),
            jnp.concatenate([sin, sin, zero, zero], axis=-1))


def _t5_bucket(rel):
    nb = N_BUCKETS // 2
    max_exact = nb // 2
    bucket = jnp.where(rel > 0, nb, 0)
    n = jnp.abs(rel)
    nf = jnp.maximum(n, 1).astype(F32)
    large = max_exact + (jnp.log(nf / max_exact) / math.log(MAX_DISTANCE / max_exact)
                         * (nb - max_exact)).astype(jnp.int32)
    large = jnp.minimum(large, nb - 1)
    return bucket + jnp.where(n < max_exact, n, large)


def _rotate_half_cols(w):
    half = w.shape[-1] // 2
    return jnp.concatenate([-w[..., half:], w[..., :half]], axis=-1)


def _split_cols(w, sizes):
    out, acc = [], 0
    for s in sizes:
        out.append(w[:, acc:acc + s])
        acc += s
    return out


def kernel(x, mem, norm_in, w_in, norm_q, norm_kv, w_uq, w_ukv, attn_sinks, rel_bias,
           norm_mem, w_mem_kv, w_out, norm_final):
    B, S, _ = x.shape
    assert norm_in.shape[0] == 1, "single-layer trunk"

    (w_cq, w_ckv, w_kpe, w_zmla, w_qswa, w_kswa, w_vswa, w_zswa, w_qmem, w_zmem) = _split_cols(
        w_in[0], (MLA_QLORA, MLA_KVLORA, MLA_ROPE, MLA_WIDTH, SWA_WIDTH, SWA_KV_WIDTH,
                  SWA_KV_WIDTH, SWA_WIDTH, MEM_WIDTH, MEM_WIDTH))
    w1 = jnp.concatenate([w_cq, w_ckv, w_qswa, w_qmem, w_zmla, w_zswa, w_zmem, w_kswa, w_vswa,
                          w_kpe, _rotate_half_cols(w_kpe)], axis=-1).astype(BF16)
    wq = w_uq[0].reshape(MLA_QLORA, MLA_HEADS, MLA_NOPE + MLA_ROPE)
    wq_pe = wq[..., MLA_NOPE:]
    wuq = jnp.concatenate([wq[..., :MLA_NOPE], wq_pe, _rotate_half_cols(wq_pe)], axis=-1)
    wuq = wuq.reshape(MLA_QLORA, MLA_HEADS * MLA_QK_PAD).astype(BF16)
    wukv = w_ukv[0].astype(BF16)

    cos_t, sin_t = _rope_tables(S)
    q, k, v, qs, z, ks, vs = _proj(
        x, norm_in, w1, norm_q, norm_kv, wuq, wukv, cos_t, sin_t, tm=256)

    y_mla = _mla(q, k, v, tq=256, tk=512, g=4)

    qi = jnp.arange(SWA_BLOCK)
    kj = jnp.arange(2 * SWA_BLOCK)
    bucket = _t5_bucket(kj[None, :] - SWA_BLOCK - qi[:, None]).astype(jnp.int32)
    bias = _bias(rel_bias, bucket)
    y_swa = _swa(attn_sinks[0], qs, ks, vs, bias, nsub=4)

    kmem, vmem = _memkv(mem, norm_mem, w_mem_kv[0].astype(BF16))
    y_mem = _mem_attn(qs, kmem, vmem, tm=512)

    return _out(x, y_mla, y_swa, y_mem, z, w_out[0].astype(BF16), norm_final[None, :], tm=256)
```

```python
import functools
import math

import jax
import jax.numpy as jnp
from jax import lax
from jax.experimental import pallas as pl
from jax.experimental.pallas import tpu as pltpu

D_MODEL = 2048
CHUNK = 64
N_MEM = 256
EPS = 1e-6
NEG = -1e30

MLA_HEADS = 8
MLA_NOPE = 128
MLA_ROPE = 64
MLA_V = 128
MLA_QLORA = 512
MLA_KVLORA = 256
MLA_WIDTH = MLA_HEADS * MLA_V
MLA_QK_PAD = 256
ROPE_THETA = 10000.0

SWA_HEADS = 8
SWA_KV_HEADS = 2
SWA_GROUP = SWA_HEADS // SWA_KV_HEADS
SWA_HEAD_DIM = 64
SWA_WIDTH = SWA_HEADS * SWA_HEAD_DIM
SWA_KV_WIDTH = SWA_KV_HEADS * SWA_HEAD_DIM
WINDOW_CHUNKS = 2
SWA_BLOCK = 128

MEM_HEADS = 4
MEM_HEAD_DIM = 128
MEM_WIDTH = MEM_HEADS * MEM_HEAD_DIM

MIX_WIDTH = MLA_WIDTH + SWA_WIDTH + MEM_WIDTH

N_BUCKETS = 32
MAX_DISTANCE = 128

BF16 = jnp.bfloat16
F32 = jnp.float32

_R_CQ = 0
_R_CKV = _R_CQ + MLA_QLORA
_R_KPE = _R_CKV + MLA_KVLORA
_R_ZMLA = _R_KPE + MLA_ROPE
_R_QSWA = _R_ZMLA + MLA_WIDTH
_R_KSWA = _R_QSWA + SWA_WIDTH
_R_VSWA = _R_KSWA + SWA_KV_WIDTH
_R_ZSWA = _R_VSWA + SWA_KV_WIDTH
_R_QMEM = _R_ZSWA + SWA_WIDTH
_R_ZMEM = _R_QMEM + MEM_WIDTH
IN_WIDTH = _R_ZMEM + MEM_WIDTH

_MLA_Q_SCALE = (MLA_NOPE + MLA_ROPE) ** -0.5 * math.log2(math.e)

_SWA_KV_PAD = 2 * SWA_KV_HEADS * 128

_VMEM_LIMIT = 56 * 1024 * 1024


def _params(sem):
    return pltpu.CompilerParams(dimension_semantics=sem, vmem_limit_bytes=_VMEM_LIMIT)


def _rms(v, g):
    return (v * lax.rsqrt(jnp.mean(v * v, axis=-1, keepdims=True) + EPS)) * g


def _dot(a, b):
    return jnp.dot(a, b, preferred_element_type=F32)


def _dot_nt(a, b):
    return lax.dot_general(a, b, (((1,), (1,)), ((), ())), preferred_element_type=F32)


def _chunk_of(pos):
    return jnp.right_shift(pos, CHUNK.bit_length() - 1)


def _rope128(v, cos_t, sin_t):
    return v * cos_t + pltpu.roll(v, MLA_ROPE // 2, 1) * sin_t


def _store_swa_kv(ref, kv):
    low = lax.broadcasted_iota(jnp.int32, kv.shape, 1) < SWA_HEAD_DIM
    swapped = pltpu.roll(kv, SWA_HEAD_DIM, 1)
    pieces = (jnp.where(low, kv, 0.0), jnp.where(low, 0.0, swapped),
              jnp.where(low, swapped, 0.0), jnp.where(low, 0.0, kv))
    for i, piece in enumerate(pieces):
        ref[0, :, i * 128:(i + 1) * 128] = piece.astype(BF16)


def _proj_kernel(x_ref, gin_ref, wt_ref, gq_ref, gkv_ref, wuq_ref, wukv_ref, cos_ref, sin_ref,
                 q_ref, k_ref, v_ref, qs_ref, z_ref, ks_ref, vs_ref):
    h = _rms(x_ref[0], gin_ref[...]).astype(BF16)
    cos_t = cos_ref[...]
    sin_t = sin_ref[...]

    def proj(lo, hi):
        return _dot_nt(h, wt_ref[lo:hi, :])

    pa = proj(_R_CQ, _R_KPE)
    cq = _rms(pa[:, :MLA_QLORA], gq_ref[...]).astype(BF16)
    ckv = _rms(pa[:, MLA_QLORA:], gkv_ref[...]).astype(BF16)

    qs_ref[0, :, :SWA_WIDTH] = proj(_R_QSWA, _R_KSWA).astype(BF16)
    qs_ref[0, :, SWA_WIDTH:] = proj(_R_QMEM, _R_ZMEM).astype(BF16)
    z_ref[0, :, :MLA_WIDTH] = proj(_R_ZMLA, _R_QSWA).astype(BF16)
    z_ref[0, :, MLA_WIDTH:MLA_WIDTH + SWA_WIDTH] = proj(_R_ZSWA, _R_QMEM).astype(BF16)
    z_ref[0, :, MLA_WIDTH + SWA_WIDTH:] = proj(_R_ZMEM, IN_WIDTH).astype(BF16)
    kvs = proj(_R_KSWA, _R_ZSWA)
    _store_swa_kv(ks_ref, kvs[:, :SWA_KV_WIDTH])
    _store_swa_kv(vs_ref, kvs[:, SWA_KV_WIDTH:])
    pe = proj(_R_KPE, _R_ZMLA)
    kpe = _rope128(jnp.concatenate([pe, pe], axis=1), cos_t, sin_t).astype(BF16)

    qall = _dot(cq, wuq_ref[...]) * _MLA_Q_SCALE
    kv = _dot(ckv, wukv_ref[...])
    ones = jnp.ones((kv.shape[0], MLA_V), BF16)
    for hd in range(MLA_HEADS):
        o = hd * MLA_QK_PAD
        q_ref[0, :, o:o + MLA_NOPE] = qall[:, o:o + MLA_NOPE].astype(BF16)
        q_ref[0, :, o + MLA_NOPE:o + MLA_QK_PAD] = _rope128(
            qall[:, o + MLA_NOPE:o + MLA_QK_PAD], cos_t, sin_t).astype(BF16)
        k_ref[0, :, o:o + MLA_NOPE] = kv[:, o:o + MLA_NOPE].astype(BF16)
        k_ref[0, :, o + MLA_NOPE:o + MLA_QK_PAD] = kpe
        v_ref[0, :, o:o + MLA_V] = kv[:, o + MLA_NOPE:o + MLA_QK_PAD].astype(BF16)
        v_ref[0, :, o + MLA_V:o + MLA_QK_PAD] = ones


def _proj(x, gin, wt, gq, gkv, wuq, wukv, cos_t, sin_t, tm):
    B, S, _ = x.shape
    const = lambda b, i: (0, 0)
    row = lambda b, i: (b, i, 0)
    single = pl.Buffered(1)

    def out(width):
        return jax.ShapeDtypeStruct((B, S, width), BF16)

    return pl.pallas_call(
        _proj_kernel,
        grid=(B, S // tm),
        in_specs=[
            pl.BlockSpec((1, tm, D_MODEL), row),
            pl.BlockSpec((1, D_MODEL), const),
            pl.BlockSpec((IN_WIDTH, D_MODEL), const, pipeline_mode=single),
            pl.BlockSpec((1, MLA_QLORA), const),
            pl.BlockSpec((1, MLA_KVLORA), const),
            pl.BlockSpec((MLA_QLORA, MLA_HEADS * MLA_QK_PAD), const, pipeline_mode=single),
            pl.BlockSpec((MLA_KVLORA, MLA_HEADS * (MLA_NOPE + MLA_V)), const, pipeline_mode=single),
            pl.BlockSpec((tm, 128), lambda b, i: (i, 0)),
            pl.BlockSpec((tm, 128), lambda b, i: (i, 0)),
        ],
        out_specs=[
            pl.BlockSpec((1, tm, MLA_HEADS * MLA_QK_PAD), row),
            pl.BlockSpec((1, tm, MLA_HEADS * MLA_QK_PAD), row),
            pl.BlockSpec((1, tm, MLA_HEADS * MLA_QK_PAD), row),
            pl.BlockSpec((1, tm, SWA_WIDTH + MEM_WIDTH), row),
            pl.BlockSpec((1, tm, MIX_WIDTH), row),
            pl.BlockSpec((1, tm, _SWA_KV_PAD), row),
            pl.BlockSpec((1, tm, _SWA_KV_PAD), row),
        ],
        out_shape=[
            out(MLA_HEADS * MLA_QK_PAD), out(MLA_HEADS * MLA_QK_PAD), out(MLA_HEADS * MLA_QK_PAD),
            out(SWA_WIDTH + MEM_WIDTH), out(MIX_WIDTH), out(_SWA_KV_PAD), out(_SWA_KV_PAD),
        ],
        compiler_params=_params(("arbitrary", "arbitrary")),
        name="proj",
    )(x, gin, wt, gq, gkv, wuq, wukv, cos_t, sin_t)


def _mla_kernel(q_ref, k_ref, v_ref, o_ref, acc_ref, m_ref, s_ref, *, tq, tk, g):
    qi = pl.program_id(2)
    acc_ref[...] = jnp.zeros(acc_ref.shape, F32)
    m_ref[...] = jnp.full(m_ref.shape, NEG, F32)

    def scores(c, slot):
        start = pl.multiple_of(c * tk, tk)
        for hd in range(g):
            cols = slice(hd * MLA_QK_PAD, (hd + 1) * MLA_QK_PAD)
            s_ref[slot, hd] = _dot_nt(q_ref[0, :, cols], k_ref[0, pl.ds(start, tk), cols])

    def update(c, slot, masked):
        start = pl.multiple_of(c * tk, tk)
        if masked:
            q_chunk = _chunk_of(qi * tq + lax.broadcasted_iota(jnp.int32, (tq, tk), 0))
            k_chunk = _chunk_of(start + lax.broadcasted_iota(jnp.int32, (tq, tk), 1))
            mask = k_chunk <= q_chunk
        for hd in range(g):
            cols = slice(hd * MLA_QK_PAD, (hd + 1) * MLA_QK_PAD)
            s = s_ref[slot, hd]
            if masked:
                s = jnp.where(mask, s, NEG)
            m_old = m_ref[hd]
            m_new = jnp.maximum(m_old, jnp.max(s, axis=-1, keepdims=True))
            p = jnp.exp2(s - m_new).astype(BF16)
            acc_ref[hd] = (jnp.exp2(m_old - m_new) * acc_ref[hd]
                           + _dot(p, v_ref[0, pl.ds(start, tk), cols]))
            m_ref[hd] = m_new

    n_full = jnp.right_shift(qi, (tk // tq).bit_length() - 1)
    n_pairs = jnp.right_shift(n_full, 1)
    scores(0, 0)

    def body(jj, carry):
        c = 2 * jj
        scores(c + 1, 1)
        update(c, 0, False)
        scores(c + 2, 0)
        update(c + 1, 1, False)
        return carry

    lax.fori_loop(0, n_pairs, body, 0)
    c = 2 * n_pairs

    @pl.when(n_full > c)
    def _():
        scores(c + 1, 1)
        update(c, 0, False)
        update(c + 1, 1, True)

    @pl.when(n_full == c)
    def _():
        update(c, 0, True)

    for hd in range(g):
        acc = acc_ref[hd]
        o_ref[0, :, hd * MLA_V:(hd + 1) * MLA_V] = (acc[:, :MLA_V] / acc[:, MLA_V:]).astype(BF16)


def _mla(q, k, v, tq, tk, g):
    B, S, _ = q.shape
    assert tk % tq == 0 and (tk // tq) & (tk // tq - 1) == 0 and S % tk == 0
    return pl.pallas_call(
        functools.partial(_mla_kernel, tq=tq, tk=tk, g=g),
        grid=(B, MLA_HEADS // g, S // tq),
        in_specs=[
            pl.BlockSpec((1, tq, g * MLA_QK_PAD), lambda b, h, i: (b, i, h)),
            pl.BlockSpec((1, S, g * MLA_QK_PAD), lambda b, h, i: (b, 0, h)),
            pl.BlockSpec((1, S, g * MLA_QK_PAD), lambda b, h, i: (b, 0, h)),
        ],
        out_specs=pl.BlockSpec((1, tq, g * MLA_V), lambda b, h, i: (b, i, h)),
        out_shape=jax.ShapeDtypeStruct((B, S, MLA_WIDTH), BF16),
        scratch_shapes=[
            pltpu.VMEM((g, tq, MLA_QK_PAD), F32),
            pltpu.VMEM((g, tq, 1), F32),
            pltpu.VMEM((2, g, tq, tk), F32),
        ],
        compiler_params=_params(("arbitrary", "arbitrary", "arbitrary")),
        name="mla",
    )(q, k, v)


def _bias_kernel(table_ref, bucket_ref, o_ref):
    bucket = bucket_ref[...]
    for hd in range(SWA_HEADS):
        acc = jnp.zeros(bucket.shape, F32)
        for b in range(N_BUCKETS):
            acc = acc + jnp.where(bucket == b, table_ref[b, hd], 0.0)
        kvh, within = divmod(hd, SWA_GROUP)
        par, half = within % 2, within // 2
        o_ref[kvh, par, half * SWA_BLOCK:(half + 1) * SWA_BLOCK, :] = acc


def _bias(rel_table, bucket):
    return pl.pallas_call(
        _bias_kernel,
        in_specs=[
            pl.BlockSpec(memory_space=pltpu.SMEM),
            pl.BlockSpec(memory_space=pltpu.VMEM),
        ],
        out_specs=pl.BlockSpec(memory_space=pltpu.VMEM),
        out_shape=jax.ShapeDtypeStruct((SWA_KV_HEADS, 2, 2 * SWA_BLOCK, 2 * SWA_BLOCK), F32),
        name="t5bias",
    )(rel_table, bucket)


def _swa_kernel(sink_ref, q_ref, kp_ref, kc_ref, vp_ref, vc_ref, bias_ref, o_ref, kb_ref, vb_ref,
                *, nsub):
    t = pl.program_id(1)
    scale = SWA_HEAD_DIM ** -0.5
    sb = SWA_BLOCK
    kb_ref[:sb] = kp_ref[0]
    kb_ref[sb:] = kc_ref[0]
    vb_ref[:sb] = vp_ref[0]
    vb_ref[sb:] = vc_ref[0]

    row = lax.broadcasted_iota(jnp.int32, (2 * sb, 2 * sb), 0)
    q_chunk = _chunk_of(row & (sb - 1))
    b_chunk = _chunk_of(lax.broadcasted_iota(jnp.int32, (2 * sb, 2 * sb), 1))
    valid_any = jnp.logical_and(b_chunk >= q_chunk, b_chunk <= q_chunk + WINDOW_CHUNKS)
    first_lo = jnp.where(t > 0, 0, 2)
    valid_first = jnp.logical_and(valid_any, b_chunk >= first_lo)
    upper = lax.broadcasted_iota(jnp.int32, (2 * sb, 1), 0) < sb

    for r in range(nsub):
        valid = valid_first if r == 0 else valid_any
        rows = slice(r * sb, (r + 1) * sb)
        band = slice(r * sb, (r + 2) * sb)
        for kvh in range(SWA_KV_HEADS):
            pair0 = slice(2 * kvh * 128, (2 * kvh + 1) * 128)
            pair1 = slice((2 * kvh + 1) * 128, (2 * kvh + 2) * 128)
            q = jnp.concatenate([q_ref[0, rows, pair0], q_ref[0, rows, pair1]], axis=0)
            o = None
            for par in range(2):
                kcols = slice((2 * kvh + par) * 128, (2 * kvh + par + 1) * 128)
                hd = SWA_GROUP * kvh + par
                s = _dot_nt(q, kb_ref[band, kcols]) * scale + bias_ref[kvh, par]
                s = jnp.where(valid, s, NEG)
                sink = jnp.where(upper, sink_ref[hd], sink_ref[hd + 2])
                m = jnp.maximum(jnp.max(s, axis=-1, keepdims=True), sink)
                p = jnp.exp(s - m)
                den = jnp.sum(p, axis=-1, keepdims=True) + jnp.exp(sink - m)
                pv = _dot((p * (1.0 / den)).astype(BF16), vb_ref[band, kcols])
                o = pv if o is None else o + pv
            o_ref[0, rows, pair0] = o[:sb].astype(BF16)
            o_ref[0, rows, pair1] = o[sb:].astype(BF16)


def _swa(sinks, qs, ks, vs, bias, nsub):
    B, S, _ = qs.shape
    tq = nsub * SWA_BLOCK
    prev_map = lambda b, t: (b, jnp.maximum(t * nsub - 1, 0), 0)
    cur_map = lambda b, t: (b, t, 0)
    return pl.pallas_call(
        functools.partial(_swa_kernel, nsub=nsub),
        grid=(B, S // tq),
        in_specs=[
            pl.BlockSpec(memory_space=pltpu.SMEM),
            pl.BlockSpec((1, tq, SWA_WIDTH), cur_map),
            pl.BlockSpec((1, SWA_BLOCK, _SWA_KV_PAD), prev_map),
            pl.BlockSpec((1, tq, _SWA_KV_PAD), cur_map),
            pl.BlockSpec((1, SWA_BLOCK, _SWA_KV_PAD), prev_map),
            pl.BlockSpec((1, tq, _SWA_KV_PAD), cur_map),
            pl.BlockSpec((SWA_KV_HEADS, 2, 2 * SWA_BLOCK, 2 * SWA_BLOCK), lambda b, t: (0, 0, 0, 0)),
        ],
        out_specs=pl.BlockSpec((1, tq, SWA_WIDTH), cur_map),
        out_shape=jax.ShapeDtypeStruct((B, S, SWA_WIDTH), BF16),
        scratch_shapes=[
            pltpu.VMEM((tq + SWA_BLOCK, _SWA_KV_PAD), BF16),
            pltpu.VMEM((tq + SWA_BLOCK, _SWA_KV_PAD), BF16),
        ],
        compiler_params=_params(("arbitrary", "arbitrary")),
        name="swa",
    )(sinks, qs, ks, ks, vs, vs, bias)


def _memkv_kernel(mem_ref, g_ref, w_ref, k_ref, v_ref):
    mn = _rms(mem_ref[0], g_ref[...]).astype(BF16)
    kv = _dot(mn, w_ref[...])
    k_ref[0] = kv[:, :MEM_WIDTH].astype(BF16)
    v_ref[0] = kv[:, MEM_WIDTH:].astype(BF16)


def _memkv(mem, g, w):
    B, M, _ = mem.shape
    const = lambda b: (0, 0)
    row = lambda b: (b, 0, 0)
    return pl.pallas_call(
        _memkv_kernel,
        grid=(B,),
        in_specs=[
            pl.BlockSpec((1, M, D_MODEL), row),
            pl.BlockSpec((1, D_MODEL), const),
            pl.BlockSpec((D_MODEL, 2 * MEM_WIDTH), const),
        ],
        out_specs=[pl.BlockSpec((1, M, MEM_WIDTH), row)] * 2,
        out_shape=[jax.ShapeDtypeStruct((B, M, MEM_WIDTH), BF16)] * 2,
        compiler_params=_params(("arbitrary",)),
        name="memkv",
    )(mem, g, w)


def _mem_kernel(q_ref, k_ref, v_ref, o_ref):
    scale = MEM_HEAD_DIM ** -0.5
    for hd in range(MEM_HEADS):
        cols = slice(hd * MEM_HEAD_DIM, (hd + 1) * MEM_HEAD_DIM)
        s = _dot_nt(q_ref[0, :, cols], k_ref[0, :, cols]) * scale
        m = jnp.max(s, axis=-1, keepdims=True)
        p = jnp.exp(s - m)
        inv = 1.0 / jnp.sum(p, axis=-1, keepdims=True)
        o_ref[0, :, cols] = _dot((p * inv).astype(BF16), v_ref[0, :, cols]).astype(BF16)


def _mem_attn(qs, kmem, vmem, tm):
    B, S, _ = qs.shape
    M = kmem.shape[1]
    return pl.pallas_call(
        _mem_kernel,
        grid=(B, S // tm),
        in_specs=[
            pl.BlockSpec((1, tm, MEM_WIDTH), lambda b, i: (b, i, 1)),
            pl.BlockSpec((1, M, MEM_WIDTH), lambda b, i: (b, 0, 0)),
            pl.BlockSpec((1, M, MEM_WIDTH), lambda b, i: (b, 0, 0)),
        ],
        out_specs=pl.BlockSpec((1, tm, MEM_WIDTH), lambda b, i: (b, i, 0)),
        out_shape=jax.ShapeDtypeStruct((B, S, MEM_WIDTH), BF16),
        compiler_params=_params(("arbitrary", "arbitrary")),
        name="memattn",
    )(qs, kmem, vmem)


def _out_kernel(x_ref, ya_ref, yb_ref, yc_ref, z_ref, w_ref, g_ref, o_ref):
    def gated(y_ref, lo, hi):
        z = z_ref[0, :, lo:hi].astype(F32)
        return (y_ref[0].astype(F32) * (z * jax.nn.sigmoid(z))).astype(BF16)

    y = _dot(gated(ya_ref, 0, MLA_WIDTH), w_ref[:MLA_WIDTH, :])
    y = y + _dot(gated(yb_ref, MLA_WIDTH, MLA_WIDTH + SWA_WIDTH),
                 w_ref[MLA_WIDTH:MLA_WIDTH + SWA_WIDTH, :])
    y = y + _dot(gated(yc_ref, MLA_WIDTH + SWA_WIDTH, MIX_WIDTH),
                 w_ref[MLA_WIDTH + SWA_WIDTH:, :])
    o_ref[0] = _rms(x_ref[0] + y, g_ref[...])


def _out(x, ya, yb, yc, z, w, g, tm):
    B, S, _ = x.shape
    row = lambda b, i: (b, i, 0)
    const = lambda b, i: (0, 0)
    return pl.pallas_call(
        _out_kernel,
        grid=(B, S // tm),
        in_specs=[
            pl.BlockSpec((1, tm, D_MODEL), row),
            pl.BlockSpec((1, tm, MLA_WIDTH), row),
            pl.BlockSpec((1, tm, SWA_WIDTH), row),
            pl.BlockSpec((1, tm, MEM_WIDTH), row),
            pl.BlockSpec((1, tm, MIX_WIDTH), row),
            pl.BlockSpec((MIX_WIDTH, D_MODEL), const, pipeline_mode=pl.Buffered(1)),
            pl.BlockSpec((1, D_MODEL), const),
        ],
        out_specs=pl.BlockSpec((1, tm, D_MODEL), row),
        out_shape=jax.ShapeDtypeStruct((B, S, D_MODEL), F32),
        compiler_params=_params(("arbitrary", "arbitrary")),
        name="outproj",
    )(x, ya, yb, yc, z, w, g)


def _rope_tables(seq):
    inv = 1.0 / (ROPE_THETA ** (jnp.arange(0, MLA_ROPE, 2, dtype=F32) / MLA_ROPE))
    ang = jnp.arange(seq, dtype=F32)[:, None] * inv[None, :]
    cos, sin = jnp.cos(ang), jnp.sin(ang)
    zero = jnp.zeros_like(cos)
    return (jnp.concatenate([cos, cos, zero, zero], axis=-1),
            jnp.concatenate([-sin, sin, zero, zero], axis=-1))


def _t5_bucket(rel):
    nb = N_BUCKETS // 2
    max_exact = nb // 2
    bucket = jnp.where(rel > 0, nb, 0)
    n = jnp.abs(rel)
    nf = jnp.maximum(n, 1).astype(F32)
    large = max_exact + (jnp.log(nf / max_exact) / math.log(MAX_DISTANCE / max_exact)
                         * (nb - max_exact)).astype(jnp.int32)
    large = jnp.minimum(large, nb - 1)
    return bucket + jnp.where(n < max_exact, n, large)


def kernel(x, mem, norm_in, w_in, norm_q, norm_kv, w_uq, w_ukv, attn_sinks, rel_bias,
           norm_mem, w_mem_kv, w_out, norm_final):
    B, S, _ = x.shape
    assert norm_in.shape[0] == 1, "single-layer trunk"

    wt = w_in[0].T.astype(BF16)
    wq = w_uq[0].reshape(MLA_QLORA, MLA_HEADS, MLA_NOPE + MLA_ROPE)
    wq_pe = wq[..., MLA_NOPE:]
    wuq = jnp.concatenate([wq, wq_pe], axis=-1)
    wuq = wuq.reshape(MLA_QLORA, MLA_HEADS * MLA_QK_PAD).astype(BF16)
    wukv = w_ukv[0].astype(BF16)

    cos_t, sin_t = _rope_tables(S)
    q, k, v, qs, z, ks, vs = _proj(
        x, norm_in, wt, norm_q, norm_kv, wuq, wukv, cos_t, sin_t, tm=256)

    y_mla = _mla(q, k, v, tq=256, tk=512, g=4)

    qi = jnp.arange(SWA_BLOCK)
    kj = jnp.arange(2 * SWA_BLOCK)
    bucket = _t5_bucket(kj[None, :] - SWA_BLOCK - qi[:, None]).astype(jnp.int32)
    bias = _bias(rel_bias, bucket)
    y_swa = _swa(attn_sinks[0], qs, ks, vs, bias, nsub=4)

    kmem, vmem = _memkv(mem, norm_mem, w_mem_kv[0].astype(BF16))
    y_mem = _mem_attn(qs, kmem, vmem, tm=512)

    return _out(x, y_mla, y_swa, y_mem, z, w_out[0].astype(BF16), norm_final[None, :], tm=256)
```

```python
import functools
import math

import jax
import jax.numpy as jnp
from jax import lax
from jax.experimental import pallas as pl
from jax.experimental.pallas import tpu as pltpu

D_MODEL = 2048
CHUNK = 64
N_MEM = 256
EPS = 1e-6
NEG = -1e30

MLA_HEADS = 8
MLA_NOPE = 128
MLA_ROPE = 64
MLA_V = 128
MLA_QLORA = 512
MLA_KVLORA = 256
MLA_WIDTH = MLA_HEADS * MLA_V
MLA_QK_PAD = 256
ROPE_THETA = 10000.0

SWA_HEADS = 8
SWA_KV_HEADS = 2
SWA_GROUP = SWA_HEADS // SWA_KV_HEADS
SWA_HEAD_DIM = 64
SWA_WIDTH = SWA_HEADS * SWA_HEAD_DIM
SWA_KV_WIDTH = SWA_KV_HEADS * SWA_HEAD_DIM
WINDOW_CHUNKS = 2
SWA_BLOCK = 128

MEM_HEADS = 4
MEM_HEAD_DIM = 128
MEM_WIDTH = MEM_HEADS * MEM_HEAD_DIM

MIX_WIDTH = MLA_WIDTH + SWA_WIDTH + MEM_WIDTH

N_BUCKETS = 32
MAX_DISTANCE = 128

BF16 = jnp.bfloat16
F32 = jnp.float32

_R_CQ = 0
_R_CKV = _R_CQ + MLA_QLORA
_R_KPE = _R_CKV + MLA_KVLORA
_R_ZMLA = _R_KPE + MLA_ROPE
_R_QSWA = _R_ZMLA + MLA_WIDTH
_R_KSWA = _R_QSWA + SWA_WIDTH
_R_VSWA = _R_KSWA + SWA_KV_WIDTH
_R_ZSWA = _R_VSWA + SWA_KV_WIDTH
_R_QMEM = _R_ZSWA + SWA_WIDTH
_R_ZMEM = _R_QMEM + MEM_WIDTH
IN_WIDTH = _R_ZMEM + MEM_WIDTH

_MLA_Q_SCALE = (MLA_NOPE + MLA_ROPE) ** -0.5 * math.log2(math.e)

_SWA_KV_PAD = 2 * SWA_KV_HEADS * 128

_VMEM_LIMIT = 56 * 1024 * 1024


def _params(sem, flags=None):
    return pltpu.CompilerParams(dimension_semantics=sem, vmem_limit_bytes=_VMEM_LIMIT, flags=flags)


def _rms(v, g):
    return (v * lax.rsqrt(jnp.mean(v * v, axis=-1, keepdims=True) + EPS)) * g


def _dot(a, b):
    return jnp.dot(a, b, preferred_element_type=F32)


def _dot_nt(a, b):
    return lax.dot_general(a, b, (((1,), (1,)), ((), ())), preferred_element_type=F32)


def _lane_tile(v, n):
    return jnp.concatenate([v] * n, axis=1)


def _chunk_of(pos):
    return jnp.right_shift(pos, CHUNK.bit_length() - 1)


def _rope128(v, cos_t, sin_t):
    return v * cos_t + pltpu.roll(v, MLA_ROPE // 2, 1) * sin_t


def _store_swa_kv(ref, kv):
    low = lax.broadcasted_iota(jnp.int32, kv.shape, 1) < SWA_HEAD_DIM
    swapped = pltpu.roll(kv, SWA_HEAD_DIM, 1)
    pieces = (jnp.where(low, kv, 0.0), jnp.where(low, 0.0, swapped),
              jnp.where(low, swapped, 0.0), jnp.where(low, 0.0, kv))
    for i, piece in enumerate(pieces):
        ref[0, :, i * 128:(i + 1) * 128] = piece.astype(BF16)


def _proj_kernel(x_ref, gin_ref, wt_ref, gq_ref, gkv_ref, wuq_ref, wukv_ref, cos_ref, sin_ref,
                 q_ref, k_ref, v_ref, qs_ref, z_ref, ks_ref, vs_ref):
    h = _rms(x_ref[0], gin_ref[...]).astype(BF16)
    cos_t = cos_ref[...]
    sin_t = sin_ref[...]

    def proj(lo, hi):
        return _dot_nt(h, wt_ref[lo:hi, :])

    pa = proj(_R_CQ, _R_KPE)
    cq = _rms(pa[:, :MLA_QLORA], gq_ref[...]).astype(BF16)
    ckv = _rms(pa[:, MLA_QLORA:], gkv_ref[...]).astype(BF16)

    qs_ref[0, :, :SWA_WIDTH] = proj(_R_QSWA, _R_KSWA).astype(BF16)
    qs_ref[0, :, SWA_WIDTH:] = proj(_R_QMEM, _R_ZMEM).astype(BF16)
    z_ref[0, :, :MLA_WIDTH] = proj(_R_ZMLA, _R_QSWA).astype(BF16)
    z_ref[0, :, MLA_WIDTH:MLA_WIDTH + SWA_WIDTH] = proj(_R_ZSWA, _R_QMEM).astype(BF16)
    z_ref[0, :, MLA_WIDTH + SWA_WIDTH:] = proj(_R_ZMEM, IN_WIDTH).astype(BF16)
    kvs = proj(_R_KSWA, _R_ZSWA)
    _store_swa_kv(ks_ref, kvs[:, :SWA_KV_WIDTH])
    _store_swa_kv(vs_ref, kvs[:, SWA_KV_WIDTH:])
    pe = proj(_R_KPE, _R_ZMLA)
    kpe = _rope128(jnp.concatenate([pe, pe], axis=1), cos_t, sin_t).astype(BF16)

    qall = _dot(cq, wuq_ref[...]) * _MLA_Q_SCALE
    kv = _dot(ckv, wukv_ref[...])
    ones = jnp.ones((kv.shape[0], MLA_V), BF16)
    for hd in range(MLA_HEADS):
        o = hd * MLA_QK_PAD
        q_ref[0, hd, :, :MLA_NOPE] = qall[:, o:o + MLA_NOPE].astype(BF16)
        q_ref[0, hd, :, MLA_NOPE:] = _rope128(
            qall[:, o + MLA_NOPE:o + MLA_QK_PAD], cos_t, sin_t).astype(BF16)
        k_ref[0, hd, :, :MLA_NOPE] = kv[:, o:o + MLA_NOPE].astype(BF16)
        k_ref[0, hd, :, MLA_NOPE:] = kpe
        v_ref[0, hd, :, :MLA_V] = kv[:, o + MLA_NOPE:o + MLA_QK_PAD].astype(BF16)
        v_ref[0, hd, :, MLA_V:] = ones


def _proj(x, gin, wt, gq, gkv, wuq, wukv, cos_t, sin_t, tm):
    B, S, _ = x.shape
    const = lambda b, i: (0, 0)
    row = lambda b, i: (b, i, 0)
    single = pl.Buffered(1)

    def out(width):
        return jax.ShapeDtypeStruct((B, S, width), BF16)

    head_spec = pl.BlockSpec((1, MLA_HEADS, tm, MLA_QK_PAD), lambda b, i: (b, 0, i, 0))
    head_shape = jax.ShapeDtypeStruct((B, MLA_HEADS, S, MLA_QK_PAD), BF16)

    return pl.pallas_call(
        _proj_kernel,
        grid=(B, S // tm),
        in_specs=[
            pl.BlockSpec((1, tm, D_MODEL), row),
            pl.BlockSpec((1, D_MODEL), const),
            pl.BlockSpec((IN_WIDTH, D_MODEL), const, pipeline_mode=single),
            pl.BlockSpec((1, MLA_QLORA), const),
            pl.BlockSpec((1, MLA_KVLORA), const),
            pl.BlockSpec((MLA_QLORA, MLA_HEADS * MLA_QK_PAD), const, pipeline_mode=single),
            pl.BlockSpec((MLA_KVLORA, MLA_HEADS * (MLA_NOPE + MLA_V)), const, pipeline_mode=single),
            pl.BlockSpec((tm, 128), lambda b, i: (i, 0)),
            pl.BlockSpec((tm, 128), lambda b, i: (i, 0)),
        ],
        out_specs=[
            head_spec, head_spec, head_spec,
            pl.BlockSpec((1, tm, SWA_WIDTH + MEM_WIDTH), row),
            pl.BlockSpec((1, tm, MIX_WIDTH), row),
            pl.BlockSpec((1, tm, _SWA_KV_PAD), row),
            pl.BlockSpec((1, tm, _SWA_KV_PAD), row),
        ],
        out_shape=[
            head_shape, head_shape, head_shape,
            out(SWA_WIDTH + MEM_WIDTH), out(MIX_WIDTH), out(_SWA_KV_PAD), out(_SWA_KV_PAD),
        ],
        compiler_params=_params(("arbitrary", "arbitrary")),
        name="proj",
    )(x, gin, wt, gq, gkv, wuq, wukv, cos_t, sin_t)


def _mla_kernel(q_ref, k_ref, v_ref, o_ref, acc_ref, m_ref, s_ref, cm_ref, *, tq, tk, g):
    qi = pl.program_id(2)
    acc_ref[...] = jnp.zeros(acc_ref.shape, F32)
    m_ref[...] = jnp.full(m_ref.shape, NEG, F32)

    def row_max(s):
        return jnp.broadcast_to(jnp.max(s, axis=-1, keepdims=True), (tq, 128))

    def scores(c, slot):
        start = pl.multiple_of(c * tk, tk)
        for hd in range(g):
            s = _dot_nt(q_ref[0, hd], k_ref[0, hd, pl.ds(start, tk), :])
            s_ref[slot, hd] = s
            cm_ref[slot, hd] = row_max(s)

    def update(c, slot, masked):
        start = pl.multiple_of(c * tk, tk)
        if masked:
            q_chunk = _chunk_of(qi * tq + lax.broadcasted_iota(jnp.int32, (tq, tk), 0))
            k_chunk = _chunk_of(start + lax.broadcasted_iota(jnp.int32, (tq, tk), 1))
            mask = k_chunk <= q_chunk
        for hd in range(g):
            s = s_ref[slot, hd]
            if masked:
                s = jnp.where(mask, s, NEG)
                m_cur = row_max(s)
            else:
                m_cur = cm_ref[slot, hd]
            m_old = m_ref[hd]
            m_new = jnp.maximum(m_old, m_cur)
            alpha = jnp.exp2(m_old - m_new)
            p = jnp.exp2(s - _lane_tile(m_new, tk // 128)).astype(BF16)
            acc_ref[hd] = (_lane_tile(alpha, MLA_QK_PAD // 128) * acc_ref[hd]
                           + _dot(p, v_ref[0, hd, pl.ds(start, tk), :]))
            m_ref[hd] = m_new

    n_full = jnp.right_shift(qi, (tk // tq).bit_length() - 1)
    n_pairs = jnp.right_shift(n_full, 1)
    scores(0, 0)

    def body(jj, carry):
        c = 2 * jj
        scores(c + 1, 1)
        update(c, 0, False)
        scores(c + 2, 0)
        update(c + 1, 1, False)
        return carry

    lax.fori_loop(0, n_pairs, body, 0)
    c = 2 * n_pairs

    @pl.when(n_full > c)
    def _():
        scores(c + 1, 1)
        update(c, 0, False)
        update(c + 1, 1, True)

    @pl.when(n_full == c)
    def _():
        update(c, 0, True)

    for hd in range(g):
        acc = acc_ref[hd]
        o_ref[0, :, hd * MLA_V:(hd + 1) * MLA_V] = (acc[:, :MLA_V] / acc[:, MLA_V:]).astype(BF16)


def _mla(q, k, v, tq, tk, g):
    B, _, S, _ = q.shape
    assert tk % tq == 0 and (tk // tq) & (tk // tq - 1) == 0 and S % tk == 0
    return pl.pallas_call(
        functools.partial(_mla_kernel, tq=tq, tk=tk, g=g),
        grid=(B, MLA_HEADS // g, S // tq),
        in_specs=[
            pl.BlockSpec((1, g, tq, MLA_QK_PAD), lambda b, h, i: (b, h, i, 0)),
            pl.BlockSpec((1, g, S, MLA_QK_PAD), lambda b, h, i: (b, h, 0, 0)),
            pl.BlockSpec((1, g, S, MLA_QK_PAD), lambda b, h, i: (b, h, 0, 0)),
        ],
        out_specs=pl.BlockSpec((1, tq, g * MLA_V), lambda b, h, i: (b, i, h)),
        out_shape=jax.ShapeDtypeStruct((B, S, MLA_WIDTH), BF16),
        scratch_shapes=[
            pltpu.VMEM((g, tq, MLA_QK_PAD), F32),
            pltpu.VMEM((g, tq, 128), F32),
            pltpu.VMEM((2, g, tq, tk), F32),
            pltpu.VMEM((2, g, tq, 128), F32),
        ],
        compiler_params=_params(("arbitrary", "arbitrary", "arbitrary")),
        name="mla",
    )(q, k, v)


def _bias_kernel(table_ref, bucket_ref, o_ref):
    bucket = bucket_ref[...]
    for hd in range(SWA_HEADS):
        acc = jnp.zeros(bucket.shape, F32)
        for b in range(N_BUCKETS):
            acc = acc + jnp.where(bucket == b, table_ref[b, hd], 0.0)
        kvh, within = divmod(hd, SWA_GROUP)
        par, half = within % 2, within // 2
        o_ref[kvh, par, half * SWA_BLOCK:(half + 1) * SWA_BLOCK, :] = acc


def _bias(rel_table, bucket):
    return pl.pallas_call(
        _bias_kernel,
        in_specs=[
            pl.BlockSpec(memory_space=pltpu.SMEM),
            pl.BlockSpec(memory_space=pltpu.VMEM),
        ],
        out_specs=pl.BlockSpec(memory_space=pltpu.VMEM),
        out_shape=jax.ShapeDtypeStruct((SWA_KV_HEADS, 2, 2 * SWA_BLOCK, 2 * SWA_BLOCK), F32),
        name="t5bias",
    )(rel_table, bucket)


def _swa_kernel(sink_ref, q_ref, kp_ref, kc_ref, vp_ref, vc_ref, bias_ref, o_ref, kb_ref, vb_ref,
                *, nsub):
    t = pl.program_id(1)
    scale = SWA_HEAD_DIM ** -0.5
    sb = SWA_BLOCK
    kb_ref[:sb] = kp_ref[0]
    kb_ref[sb:] = kc_ref[0]
    vb_ref[:sb] = vp_ref[0]
    vb_ref[sb:] = vc_ref[0]

    row = lax.broadcasted_iota(jnp.int32, (2 * sb, 2 * sb), 0)
    q_chunk = _chunk_of(row & (sb - 1))
    b_chunk = _chunk_of(lax.broadcasted_iota(jnp.int32, (2 * sb, 2 * sb), 1))
    valid_any = jnp.logical_and(b_chunk >= q_chunk, b_chunk <= q_chunk + WINDOW_CHUNKS)
    first_lo = jnp.where(t > 0, 0, 2)
    valid_first = jnp.logical_and(valid_any, b_chunk >= first_lo)
    upper = lax.broadcasted_iota(jnp.int32, (2 * sb, 1), 0) < sb

    for r in range(nsub):
        valid = valid_first if r == 0 else valid_any
        rows = slice(r * sb, (r + 1) * sb)
        band = slice(r * sb, (r + 2) * sb)
        for kvh in range(SWA_KV_HEADS):
            pair0 = slice(2 * kvh * 128, (2 * kvh + 1) * 128)
            pair1 = slice((2 * kvh + 1) * 128, (2 * kvh + 2) * 128)
            q = jnp.concatenate([q_ref[0, rows, pair0], q_ref[0, rows, pair1]], axis=0)
            o = None
            for par in range(2):
                kcols = slice((2 * kvh + par) * 128, (2 * kvh + par + 1) * 128)
                hd = SWA_GROUP * kvh + par
                s = _dot_nt(q, kb_ref[band, kcols]) * scale + bias_ref[kvh, par]
                s = jnp.where(valid, s, NEG)
                sink = jnp.where(upper, sink_ref[hd], sink_ref[hd + 2])
                m = jnp.maximum(jnp.max(s, axis=-1, keepdims=True), sink)
                p = jnp.exp(s - m)
                den = jnp.sum(p, axis=-1, keepdims=True) + jnp.exp(sink - m)
                pv = _dot((p * (1.0 / den)).astype(BF16), vb_ref[band, kcols])
                o = pv if o is None else o + pv
            o_ref[0, rows, pair0] = o[:sb].astype(BF16)
            o_ref[0, rows, pair1] = o[sb:].astype(BF16)


def _swa(sinks, qs, ks, vs, bias, nsub):
    B, S, _ = qs.shape
    tq = nsub * SWA_BLOCK
    prev_map = lambda b, t: (b, jnp.maximum(t * nsub - 1, 0), 0)
    cur_map = lambda b, t: (b, t, 0)
    return pl.pallas_call(
        functools.partial(_swa_kernel, nsub=nsub),
        grid=(B, S // tq),
        in_specs=[
            pl.BlockSpec(memory_space=pltpu.SMEM),
            pl.BlockSpec((1, tq, SWA_WIDTH), cur_map),
            pl.BlockSpec((1, SWA_BLOCK, _SWA_KV_PAD), prev_map),
            pl.BlockSpec((1, tq, _SWA_KV_PAD), cur_map),
            pl.BlockSpec((1, SWA_BLOCK, _SWA_KV_PAD), prev_map),
            pl.BlockSpec((1, tq, _SWA_KV_PAD), cur_map),
            pl.BlockSpec((SWA_KV_HEADS, 2, 2 * SWA_BLOCK, 2 * SWA_BLOCK), lambda b, t: (0, 0, 0, 0)),
        ],
        out_specs=pl.BlockSpec((1, tq, SWA_WIDTH), cur_map),
        out_shape=jax.ShapeDtypeStruct((B, S, SWA_WIDTH), BF16),
        scratch_shapes=[
            pltpu.VMEM((tq + SWA_BLOCK, _SWA_KV_PAD), BF16),
            pltpu.VMEM((tq + SWA_BLOCK, _SWA_KV_PAD), BF16),
        ],
        compiler_params=_params(("arbitrary", "arbitrary")),
        name="swa",
    )(sinks, qs, ks, ks, vs, vs, bias)


def _memkv_kernel(mem_ref, g_ref, w_ref, k_ref, v_ref):
    mn = _rms(mem_ref[0], g_ref[...]).astype(BF16)
    kv = _dot(mn, w_ref[...])
    k_ref[0] = kv[:, :MEM_WIDTH].astype(BF16)
    v_ref[0] = kv[:, MEM_WIDTH:].astype(BF16)


def _memkv(mem, g, w):
    B, M, _ = mem.shape
    const = lambda b: (0, 0)
    row = lambda b: (b, 0, 0)
    return pl.pallas_call(
        _memkv_kernel,
        grid=(B,),
        in_specs=[
            pl.BlockSpec((1, M, D_MODEL), row),
            pl.BlockSpec((1, D_MODEL), const),
            pl.BlockSpec((D_MODEL, 2 * MEM_WIDTH), const),
        ],
        out_specs=[pl.BlockSpec((1, M, MEM_WIDTH), row)] * 2,
        out_shape=[jax.ShapeDtypeStruct((B, M, MEM_WIDTH), BF16)] * 2,
        compiler_params=_params(("arbitrary",)),
        name="memkv",
    )(mem, g, w)


def _mem_kernel(q_ref, k_ref, v_ref, o_ref):
    scale = MEM_HEAD_DIM ** -0.5
    for hd in range(MEM_HEADS):
        cols = slice(hd * MEM_HEAD_DIM, (hd + 1) * MEM_HEAD_DIM)
        s = _dot_nt(q_ref[0, :, cols], k_ref[0, :, cols]) * scale
        m = jnp.max(s, axis=-1, keepdims=True)
        p = jnp.exp(s - m)
        inv = 1.0 / jnp.sum(p, axis=-1, keepdims=True)
        o_ref[0, :, cols] = _dot((p * inv).astype(BF16), v_ref[0, :, cols]).astype(BF16)


def _mem_attn(qs, kmem, vmem, tm):
    B, S, _ = qs.shape
    M = kmem.shape[1]
    return pl.pallas_call(
        _mem_kernel,
        grid=(B, S // tm),
        in_specs=[
            pl.BlockSpec((1, tm, MEM_WIDTH), lambda b, i: (b, i, 1)),
            pl.BlockSpec((1, M, MEM_WIDTH), lambda b, i: (b, 0, 0)),
            pl.BlockSpec((1, M, MEM_WIDTH), lambda b, i: (b, 0, 0)),
        ],
        out_specs=pl.BlockSpec((1, tm, MEM_WIDTH), lambda b, i: (b, i, 0)),
        out_shape=jax.ShapeDtypeStruct((B, S, MEM_WIDTH), BF16),
        compiler_params=_params(("arbitrary", "arbitrary")),
        name="memattn",
    )(qs, kmem, vmem)


def _out_kernel(x_ref, ya_ref, yb_ref, yc_ref, z_ref, w_ref, g_ref, o_ref):
    def gated(y_ref, lo, hi):
        z = z_ref[0, :, lo:hi].astype(F32)
        return (y_ref[0].astype(F32) * (z * jax.nn.sigmoid(z))).astype(BF16)

    y = _dot(gated(ya_ref, 0, MLA_WIDTH), w_ref[:MLA_WIDTH, :])
    y = y + _dot(gated(yb_ref, MLA_WIDTH, MLA_WIDTH + SWA_WIDTH),
                 w_ref[MLA_WIDTH:MLA_WIDTH + SWA_WIDTH, :])
    y = y + _dot(gated(yc_ref, MLA_WIDTH + SWA_WIDTH, MIX_WIDTH),
                 w_ref[MLA_WIDTH + SWA_WIDTH:, :])
    o_ref[0] = _rms(x_ref[0] + y, g_ref[...])


def _out(x, ya, yb, yc, z, w, g, tm):
    B, S, _ = x.shape
    row = lambda b, i: (b, i, 0)
    const = lambda b, i: (0, 0)
    return pl.pallas_call(
        _out_kernel,
        grid=(B, S // tm),
        in_specs=[
            pl.BlockSpec((1, tm, D_MODEL), row),
            pl.BlockSpec((1, tm, MLA_WIDTH), row),
            pl.BlockSpec((1, tm, SWA_WIDTH), row),
            pl.BlockSpec((1, tm, MEM_WIDTH), row),
            pl.BlockSpec((1, tm, MIX_WIDTH), row),
            pl.BlockSpec((MIX_WIDTH, D_MODEL), const, pipeline_mode=pl.Buffered(1)),
            pl.BlockSpec((1, D_MODEL), const),
        ],
        out_specs=pl.BlockSpec((1, tm, D_MODEL), row),
        out_shape=jax.ShapeDtypeStruct((B, S, D_MODEL), F32),
        compiler_params=_params(("arbitrary", "arbitrary")),
        name="outproj",
    )(x, ya, yb, yc, z, w, g)


def _rope_tables(seq):
    inv = 1.0 / (ROPE_THETA ** (jnp.arange(0, MLA_ROPE, 2, dtype=F32) / MLA_ROPE))
    ang = jnp.arange(seq, dtype=F32)[:, None] * inv[None, :]
    cos, sin = jnp.cos(ang), jnp.sin(ang)
    zero = jnp.zeros_like(cos)
    return (jnp.concatenate([cos, cos, zero, zero], axis=-1),
            jnp.concatenate([-sin, sin, zero, zero], axis=-1))


def _t5_bucket(rel):
    nb = N_BUCKETS // 2
    max_exact = nb // 2
    bucket = jnp.where(rel > 0, nb, 0)
    n = jnp.abs(rel)
    nf = jnp.maximum(n, 1).astype(F32)
    large = max_exact + (jnp.log(nf / max_exact) / math.log(MAX_DISTANCE / max_exact)
                         * (nb - max_exact)).astype(jnp.int32)
    large = jnp.minimum(large, nb - 1)
    return bucket + jnp.where(n < max_exact, n, large)


def kernel(x, mem, norm_in, w_in, norm_q, norm_kv, w_uq, w_ukv, attn_sinks, rel_bias,
           norm_mem, w_mem_kv, w_out, norm_final):
    B, S, _ = x.shape
    assert norm_in.shape[0] == 1, "single-layer trunk"

    wt = w_in[0].T.astype(BF16)
    wq = w_uq[0].reshape(MLA_QLORA, MLA_HEADS, MLA_NOPE + MLA_ROPE)
    wq_pe = wq[..., MLA_NOPE:]
    wuq = jnp.concatenate([wq, wq_pe], axis=-1)
    wuq = wuq.reshape(MLA_QLORA, MLA_HEADS * MLA_QK_PAD).astype(BF16)
    wukv = w_ukv[0].astype(BF16)

    cos_t, sin_t = _rope_tables(S)
    q, k, v, qs, z, ks, vs = _proj(
        x, norm_in, wt, norm_q, norm_kv, wuq, wukv, cos_t, sin_t, tm=256)

    y_mla = _mla(q, k, v, tq=512, tk=512, g=4)

    qi = jnp.arange(SWA_BLOCK)
    kj = jnp.arange(2 * SWA_BLOCK)
    bucket = _t5_bucket(kj[None, :] - SWA_BLOCK - qi[:, None])
    bucket = jnp.arange(N_BUCKETS, dtype=jnp.int32)[bucket]
    bias = _bias(rel_bias, bucket)
    y_swa = _swa(attn_sinks[0], qs, ks, vs, bias, nsub=4)

    kmem, vmem = _memkv(mem, norm_mem, w_mem_kv[0].astype(BF16))
    y_mem = _mem_attn(qs, kmem, vmem, tm=512)

    return _out(x, y_mla, y_swa, y_mem, z, w_out[0].astype(BF16), norm_final[None, :], tm=256)
```

```python
import functools
import math

import jax
import jax.numpy as jnp
from jax import lax
from jax.experimental import pallas as pl
from jax.experimental.pallas import tpu as pltpu

D_MODEL = 2048
CHUNK = 64
N_MEM = 256
EPS = 1e-6
NEG = -1e30

MLA_HEADS = 8
MLA_NOPE = 128
MLA_ROPE = 64
MLA_V = 128
MLA_QLORA = 512
MLA_KVLORA = 256
MLA_WIDTH = MLA_HEADS * MLA_V
MLA_QK_PAD = 256
ROPE_THETA = 10000.0

SWA_HEADS = 8
SWA_KV_HEADS = 2
SWA_GROUP = SWA_HEADS // SWA_KV_HEADS
SWA_HEAD_DIM = 64
SWA_WIDTH = SWA_HEADS * SWA_HEAD_DIM
SWA_KV_WIDTH = SWA_KV_HEADS * SWA_HEAD_DIM
WINDOW_CHUNKS = 2
SWA_BLOCK = 128

MEM_HEADS = 4
MEM_HEAD_DIM = 128
MEM_WIDTH = MEM_HEADS * MEM_HEAD_DIM

MIX_WIDTH = MLA_WIDTH + SWA_WIDTH + MEM_WIDTH

N_BUCKETS = 32
MAX_DISTANCE = 128

BF16 = jnp.bfloat16
F32 = jnp.float32

_R_CQ = 0
_R_CKV = _R_CQ + MLA_QLORA
_R_KPE = _R_CKV + MLA_KVLORA
_R_ZMLA = _R_KPE + MLA_ROPE
_R_QSWA = _R_ZMLA + MLA_WIDTH
_R_KSWA = _R_QSWA + SWA_WIDTH
_R_VSWA = _R_KSWA + SWA_KV_WIDTH
_R_ZSWA = _R_VSWA + SWA_KV_WIDTH
_R_QMEM = _R_ZSWA + SWA_WIDTH
_R_ZMEM = _R_QMEM + MEM_WIDTH
IN_WIDTH = _R_ZMEM + MEM_WIDTH

_MLA_Q_SCALE = (MLA_NOPE + MLA_ROPE) ** -0.5 * math.log2(math.e)

_SWA_KV_PAD = 2 * SWA_KV_HEADS * 128

_VMEM_LIMIT = 56 * 1024 * 1024


def _params(sem, flags=None):
    return pltpu.CompilerParams(dimension_semantics=sem, vmem_limit_bytes=_VMEM_LIMIT, flags=flags)


def _rms(v, g):
    return (v * lax.rsqrt(jnp.mean(v * v, axis=-1, keepdims=True) + EPS)) * g


def _dot(a, b):
    return jnp.dot(a, b, preferred_element_type=F32)


def _dot_nt(a, b):
    return lax.dot_general(a, b, (((1,), (1,)), ((), ())), preferred_element_type=F32)


def _lane_tile(v, n):
    return jnp.concatenate([v] * n, axis=1)


def _chunk_of(pos):
    return jnp.right_shift(pos, CHUNK.bit_length() - 1)


def _rope128(v, cos_t, sin_t):
    return v * cos_t + pltpu.roll(v, MLA_ROPE // 2, 1) * sin_t


def _store_swa_kv(ref, kv):
    low = lax.broadcasted_iota(jnp.int32, kv.shape, 1) < SWA_HEAD_DIM
    swapped = pltpu.roll(kv, SWA_HEAD_DIM, 1)
    pieces = (jnp.where(low, kv, 0.0), jnp.where(low, 0.0, swapped),
              jnp.where(low, swapped, 0.0), jnp.where(low, 0.0, kv))
    for i, piece in enumerate(pieces):
        ref[0, :, i * 128:(i + 1) * 128] = piece.astype(BF16)


def _proj_kernel(x_ref, gin_ref, wt_ref, gq_ref, gkv_ref, wuq_ref, wukv_ref, cos_ref, sin_ref,
                 q_ref, k_ref, v_ref, qs_ref, z_ref, ks_ref, vs_ref):
    h = _rms(x_ref[0], gin_ref[...]).astype(BF16)
    cos_t = cos_ref[...]
    sin_t = sin_ref[...]

    def proj(lo, hi):
        return _dot_nt(h, wt_ref[lo:hi, :])

    pa = proj(_R_CQ, _R_KPE)
    cq = _rms(pa[:, :MLA_QLORA], gq_ref[...]).astype(BF16)
    ckv = _rms(pa[:, MLA_QLORA:], gkv_ref[...]).astype(BF16)

    qs_ref[0, :, :SWA_WIDTH] = proj(_R_QSWA, _R_KSWA).astype(BF16)
    qs_ref[0, :, SWA_WIDTH:] = proj(_R_QMEM, _R_ZMEM).astype(BF16)
    z_ref[0, :, :MLA_WIDTH] = proj(_R_ZMLA, _R_QSWA).astype(BF16)
    z_ref[0, :, MLA_WIDTH:MLA_WIDTH + SWA_WIDTH] = proj(_R_ZSWA, _R_QMEM).astype(BF16)
    z_ref[0, :, MLA_WIDTH + SWA_WIDTH:] = proj(_R_ZMEM, IN_WIDTH).astype(BF16)
    kvs = proj(_R_KSWA, _R_ZSWA)
    _store_swa_kv(ks_ref, kvs[:, :SWA_KV_WIDTH])
    _store_swa_kv(vs_ref, kvs[:, SWA_KV_WIDTH:])
    pe = proj(_R_KPE, _R_ZMLA)
    kpe = _rope128(jnp.concatenate([pe, pe], axis=1), cos_t, sin_t).astype(BF16)

    qall = _dot(cq, wuq_ref[...]) * _MLA_Q_SCALE
    kv = _dot(ckv, wukv_ref[...])
    ones = jnp.ones((kv.shape[0], MLA_V), BF16)
    for hd in range(MLA_HEADS):
        o = hd * MLA_QK_PAD
        q_ref[0, hd, :, :MLA_NOPE] = qall[:, o:o + MLA_NOPE].astype(BF16)
        q_ref[0, hd, :, MLA_NOPE:] = _rope128(
            qall[:, o + MLA_NOPE:o + MLA_QK_PAD], cos_t, sin_t).astype(BF16)
        k_ref[0, hd, :, :MLA_NOPE] = kv[:, o:o + MLA_NOPE].astype(BF16)
        k_ref[0, hd, :, MLA_NOPE:] = kpe
        v_ref[0, hd, :, :MLA_V] = kv[:, o + MLA_NOPE:o + MLA_QK_PAD].astype(BF16)
        v_ref[0, hd, :, MLA_V:] = ones


def _proj(x, gin, wt, gq, gkv, wuq, wukv, cos_t, sin_t, tm):
    B, S, _ = x.shape
    const = lambda b, i: (0, 0)
    row = lambda b, i: (b, i, 0)
    single = pl.Buffered(1)

    def out(width):
        return jax.ShapeDtypeStruct((B, S, width), BF16)

    head_spec = pl.BlockSpec((1, MLA_HEADS, tm, MLA_QK_PAD), lambda b, i: (b, 0, i, 0))
    head_shape = jax.ShapeDtypeStruct((B, MLA_HEADS, S, MLA_QK_PAD), BF16)

    return pl.pallas_call(
        _proj_kernel,
        grid=(B, S // tm),
        in_specs=[
            pl.BlockSpec((1, tm, D_MODEL), row),
            pl.BlockSpec((1, D_MODEL), const),
            pl.BlockSpec((IN_WIDTH, D_MODEL), const, pipeline_mode=single),
            pl.BlockSpec((1, MLA_QLORA), const),
            pl.BlockSpec((1, MLA_KVLORA), const),
            pl.BlockSpec((MLA_QLORA, MLA_HEADS * MLA_QK_PAD), const, pipeline_mode=single),
            pl.BlockSpec((MLA_KVLORA, MLA_HEADS * (MLA_NOPE + MLA_V)), const, pipeline_mode=single),
            pl.BlockSpec((tm, 128), lambda b, i: (i, 0)),
            pl.BlockSpec((tm, 128), lambda b, i: (i, 0)),
        ],
        out_specs=[
            head_spec, head_spec, head_spec,
            pl.BlockSpec((1, tm, SWA_WIDTH + MEM_WIDTH), row),
            pl.BlockSpec((1, tm, MIX_WIDTH), row),
            pl.BlockSpec((1, tm, _SWA_KV_PAD), row),
            pl.BlockSpec((1, tm, _SWA_KV_PAD), row),
        ],
        out_shape=[
            head_shape, head_shape, head_shape,
            out(SWA_WIDTH + MEM_WIDTH), out(MIX_WIDTH), out(_SWA_KV_PAD), out(_SWA_KV_PAD),
        ],
        compiler_params=_params(("arbitrary", "arbitrary")),
        name="proj",
    )(x, gin, wt, gq, gkv, wuq, wukv, cos_t, sin_t)


def _mla_kernel(q_ref, k_ref, v_ref, o_ref, acc_ref, m_ref, s_ref, cm_ref, *, tq, tk, g):
    qi = pl.program_id(2)
    acc_ref[...] = jnp.zeros(acc_ref.shape, F32)
    m_ref[...] = jnp.full(m_ref.shape, NEG, F32)

    def row_max(s):
        return jnp.broadcast_to(jnp.max(s, axis=-1, keepdims=True), (tq, 128))

    def scores(c, slot):
        start = pl.multiple_of(c * tk, tk)
        for hd in range(g):
            s = _dot_nt(q_ref[0, hd], k_ref[0, hd, pl.ds(start, tk), :])
            s_ref[slot, hd] = s
            cm_ref[slot, hd] = row_max(s)

    def update(c, slot, masked):
        start = pl.multiple_of(c * tk, tk)
        if masked:
            q_chunk = _chunk_of(qi * tq + lax.broadcasted_iota(jnp.int32, (tq, tk), 0))
            k_chunk = _chunk_of(start + lax.broadcasted_iota(jnp.int32, (tq, tk), 1))
            mask = k_chunk <= q_chunk
        for hd in range(g):
            s = s_ref[slot, hd]
            if masked:
                s = jnp.where(mask, s, NEG)
                m_cur = row_max(s)
            else:
                m_cur = cm_ref[slot, hd]
            m_old = m_ref[hd]
            m_new = jnp.maximum(m_old, m_cur)
            alpha = jnp.exp2(m_old - m_new)
            p = jnp.exp2(s - _lane_tile(m_new, tk // 128)).astype(BF16)
            acc_ref[hd] = (_lane_tile(alpha, MLA_QK_PAD // 128) * acc_ref[hd]
                           + _dot(p, v_ref[0, hd, pl.ds(start, tk), :]))
            m_ref[hd] = m_new

    n_full = jnp.right_shift(qi, (tk // tq).bit_length() - 1)
    n_pairs = jnp.right_shift(n_full, 1)
    scores(0, 0)

    def body(jj, carry):
        c = 2 * jj
        scores(c + 1, 1)
        update(c, 0, False)
        scores(c + 2, 0)
        update(c + 1, 1, False)
        return carry

    lax.fori_loop(0, n_pairs, body, 0)
    c = 2 * n_pairs

    @pl.when(n_full > c)
    def _():
        scores(c + 1, 1)
        update(c, 0, False)
        update(c + 1, 1, True)

    @pl.when(n_full == c)
    def _():
        update(c, 0, True)

    for hd in range(g):
        acc = acc_ref[hd]
        o_ref[0, :, hd * MLA_V:(hd + 1) * MLA_V] = (acc[:, :MLA_V] / acc[:, MLA_V:]).astype(BF16)


def _mla(q, k, v, tq, tk, g):
    B, _, S, _ = q.shape
    assert tk % tq == 0 and (tk // tq) & (tk // tq - 1) == 0 and S % tk == 0
    return pl.pallas_call(
        functools.partial(_mla_kernel, tq=tq, tk=tk, g=g),
        grid=(B, MLA_HEADS // g, S // tq),
        in_specs=[
            pl.BlockSpec((1, g, tq, MLA_QK_PAD), lambda b, h, i: (b, h, i, 0)),
            pl.BlockSpec((1, g, S, MLA_QK_PAD), lambda b, h, i: (b, h, 0, 0)),
            pl.BlockSpec((1, g, S, MLA_QK_PAD), lambda b, h, i: (b, h, 0, 0)),
        ],
        out_specs=pl.BlockSpec((1, tq, g * MLA_V), lambda b, h, i: (b, i, h)),
        out_shape=jax.ShapeDtypeStruct((B, S, MLA_WIDTH), BF16),
        scratch_shapes=[
            pltpu.VMEM((g, tq, MLA_QK_PAD), F32),
            pltpu.VMEM((g, tq, 128), F32),
            pltpu.VMEM((2, g, tq, tk), F32),
            pltpu.VMEM((2, g, tq, 128), F32),
        ],
        compiler_params=_params(("arbitrary", "arbitrary", "arbitrary")),
        name="mla",
    )(q, k, v)


_REL_SPAN = 4 * SWA_BLOCK


def _bias_kernel(table_ref, bucket_ref, o_ref):
    bucket = bucket_ref[...]
    for hd in range(SWA_HEADS):
        per_offset = jnp.zeros(bucket.shape, F32)
        for b in range(N_BUCKETS):
            per_offset = per_offset + jnp.where(bucket == b, table_ref[b, hd], 0.0)
        rows = jnp.broadcast_to(per_offset, (SWA_BLOCK, _REL_SPAN))
        band = pltpu.roll(rows, _REL_SPAN - (SWA_BLOCK - 1), 1, stride=1, stride_axis=0)
        kvh, within = divmod(hd, SWA_GROUP)
        par, half = within % 2, within // 2
        o_ref[kvh, par, half * SWA_BLOCK:(half + 1) * SWA_BLOCK, :] = band[:, :2 * SWA_BLOCK]


def _bias(rel_table, bucket):
    return pl.pallas_call(
        _bias_kernel,
        in_specs=[
            pl.BlockSpec(memory_space=pltpu.SMEM),
            pl.BlockSpec(memory_space=pltpu.VMEM),
        ],
        out_specs=pl.BlockSpec(memory_space=pltpu.VMEM),
        out_shape=jax.ShapeDtypeStruct((SWA_KV_HEADS, 2, 2 * SWA_BLOCK, 2 * SWA_BLOCK), F32),
        name="t5bias",
    )(rel_table, bucket)


def _swa_kernel(sink_ref, q_ref, kp_ref, kc_ref, vp_ref, vc_ref, bias_ref, o_ref, kb_ref, vb_ref,
                *, nsub):
    t = pl.program_id(1)
    scale = SWA_HEAD_DIM ** -0.5
    sb = SWA_BLOCK
    kb_ref[:sb] = kp_ref[0]
    kb_ref[sb:] = kc_ref[0]
    vb_ref[:sb] = vp_ref[0]
    vb_ref[sb:] = vc_ref[0]

    row = lax.broadcasted_iota(jnp.int32, (2 * sb, 2 * sb), 0)
    q_chunk = _chunk_of(row & (sb - 1))
    b_chunk = _chunk_of(lax.broadcasted_iota(jnp.int32, (2 * sb, 2 * sb), 1))
    valid_any = jnp.logical_and(b_chunk >= q_chunk, b_chunk <= q_chunk + WINDOW_CHUNKS)
    first_lo = jnp.where(t > 0, 0, 2)
    valid_first = jnp.logical_and(valid_any, b_chunk >= first_lo)
    upper = lax.broadcasted_iota(jnp.int32, (2 * sb, 1), 0) < sb

    for r in range(nsub):
        valid = valid_first if r == 0 else valid_any
        rows = slice(r * sb, (r + 1) * sb)
        band = slice(r * sb, (r + 2) * sb)
        for kvh in range(SWA_KV_HEADS):
            pair0 = slice(2 * kvh * 128, (2 * kvh + 1) * 128)
            pair1 = slice((2 * kvh + 1) * 128, (2 * kvh + 2) * 128)
            q = jnp.concatenate([q_ref[0, rows, pair0], q_ref[0, rows, pair1]], axis=0)
            o = None
            for par in range(2):
                kcols = slice((2 * kvh + par) * 128, (2 * kvh + par + 1) * 128)
                hd = SWA_GROUP * kvh + par
                s = _dot_nt(q, kb_ref[band, kcols]) * scale + bias_ref[kvh, par]
                s = jnp.where(valid, s, NEG)
                sink = jnp.where(upper, sink_ref[hd], sink_ref[hd + 2])
                m = jnp.maximum(jnp.max(s, axis=-1, keepdims=True), sink)
                p = jnp.exp(s - m)
                den = jnp.sum(p, axis=-1, keepdims=True) + jnp.exp(sink - m)
                pv = _dot((p * (1.0 / den)).astype(BF16), vb_ref[band, kcols])
                o = pv if o is None else o + pv
            o_ref[0, rows, pair0] = o[:sb].astype(BF16)
            o_ref[0, rows, pair1] = o[sb:].astype(BF16)


def _swa(sinks, qs, ks, vs, bias, nsub):
    B, S, _ = qs.shape
    tq = nsub * SWA_BLOCK
    prev_map = lambda b, t: (b, jnp.maximum(t * nsub - 1, 0), 0)
    cur_map = lambda b, t: (b, t, 0)
    return pl.pallas_call(
        functools.partial(_swa_kernel, nsub=nsub),
        grid=(B, S // tq),
        in_specs=[
            pl.BlockSpec(memory_space=pltpu.SMEM),
            pl.BlockSpec((1, tq, SWA_WIDTH), cur_map),
            pl.BlockSpec((1, SWA_BLOCK, _SWA_KV_PAD), prev_map),
            pl.BlockSpec((1, tq, _SWA_KV_PAD), cur_map),
            pl.BlockSpec((1, SWA_BLOCK, _SWA_KV_PAD), prev_map),
            pl.BlockSpec((1, tq, _SWA_KV_PAD), cur_map),
            pl.BlockSpec((SWA_KV_HEADS, 2, 2 * SWA_BLOCK, 2 * SWA_BLOCK), lambda b, t: (0, 0, 0, 0)),
        ],
        out_specs=pl.BlockSpec((1, tq, SWA_WIDTH), cur_map),
        out_shape=jax.ShapeDtypeStruct((B, S, SWA_WIDTH), BF16),
        scratch_shapes=[
            pltpu.VMEM((tq + SWA_BLOCK, _SWA_KV_PAD), BF16),
            pltpu.VMEM((tq + SWA_BLOCK, _SWA_KV_PAD), BF16),
        ],
        compiler_params=_params(("arbitrary", "arbitrary")),
        name="swa",
    )(sinks, qs, ks, ks, vs, vs, bias)


def _memkv_kernel(mem_ref, g_ref, w_ref, k_ref, v_ref):
    mn = _rms(mem_ref[0], g_ref[...]).astype(BF16)
    kv = _dot(mn, w_ref[...])
    k_ref[0] = kv[:, :MEM_WIDTH].astype(BF16)
    v_ref[0] = kv[:, MEM_WIDTH:].astype(BF16)


def _memkv(mem, g, w):
    B, M, _ = mem.shape
    const = lambda b: (0, 0)
    row = lambda b: (b, 0, 0)
    return pl.pallas_call(
        _memkv_kernel,
        grid=(B,),
        in_specs=[
            pl.BlockSpec((1, M, D_MODEL), row),
            pl.BlockSpec((1, D_MODEL), const),
            pl.BlockSpec((D_MODEL, 2 * MEM_WIDTH), const),
        ],
        out_specs=[pl.BlockSpec((1, M, MEM_WIDTH), row)] * 2,
        out_shape=[jax.ShapeDtypeStruct((B, M, MEM_WIDTH), BF16)] * 2,
        compiler_params=_params(("arbitrary",)),
        name="memkv",
    )(mem, g, w)


def _mem_kernel(q_ref, k_ref, v_ref, o_ref):
    scale = MEM_HEAD_DIM ** -0.5
    for hd in range(MEM_HEADS):
        cols = slice(hd * MEM_HEAD_DIM, (hd + 1) * MEM_HEAD_DIM)
        s = _dot_nt(q_ref[0, :, cols], k_ref[0, :, cols]) * scale
        m = jnp.max(s, axis=-1, keepdims=True)
        p = jnp.exp(s - m)
        inv = 1.0 / jnp.sum(p, axis=-1, keepdims=True)
        o_ref[0, :, cols] = _dot((p * inv).astype(BF16), v_ref[0, :, cols]).astype(BF16)


def _mem_attn(qs, kmem, vmem, tm):
    B, S, _ = qs.shape
    M = kmem.shape[1]
    return pl.pallas_call(
        _mem_kernel,
        grid=(B, S // tm),
        in_specs=[
            pl.BlockSpec((1, tm, MEM_WIDTH), lambda b, i: (b, i, 1)),
            pl.BlockSpec((1, M, MEM_WIDTH), lambda b, i: (b, 0, 0)),
            pl.BlockSpec((1, M, MEM_WIDTH), lambda b, i: (b, 0, 0)),
        ],
        out_specs=pl.BlockSpec((1, tm, MEM_WIDTH), lambda b, i: (b, i, 0)),
        out_shape=jax.ShapeDtypeStruct((B, S, MEM_WIDTH), BF16),
        compiler_params=_params(("arbitrary", "arbitrary")),
        name="memattn",
    )(qs, kmem, vmem)


def _out_kernel(x_ref, ya_ref, yb_ref, yc_ref, z_ref, w_ref, g_ref, o_ref):
    def gated(y_ref, lo, hi):
        z = z_ref[0, :, lo:hi].astype(F32)
        return (y_ref[0].astype(F32) * (z * jax.nn.sigmoid(z))).astype(BF16)

    y = _dot(gated(ya_ref, 0, MLA_WIDTH), w_ref[:MLA_WIDTH, :])
    y = y + _dot(gated(yb_ref, MLA_WIDTH, MLA_WIDTH + SWA_WIDTH),
                 w_ref[MLA_WIDTH:MLA_WIDTH + SWA_WIDTH, :])
    y = y + _dot(gated(yc_ref, MLA_WIDTH + SWA_WIDTH, MIX_WIDTH),
                 w_ref[MLA_WIDTH + SWA_WIDTH:, :])
    o_ref[0] = _rms(x_ref[0] + y, g_ref[...])


def _out(x, ya, yb, yc, z, w, g, tm):
    B, S, _ = x.shape
    row = lambda b, i: (b, i, 0)
    const = lambda b, i: (0, 0)
    return pl.pallas_call(
        _out_kernel,
        grid=(B, S // tm),
        in_specs=[
            pl.BlockSpec((1, tm, D_MODEL), row),
            pl.BlockSpec((1, tm, MLA_WIDTH), row),
            pl.BlockSpec((1, tm, SWA_WIDTH), row),
            pl.BlockSpec((1, tm, MEM_WIDTH), row),
            pl.BlockSpec((1, tm, MIX_WIDTH), row),
            pl.BlockSpec((MIX_WIDTH, D_MODEL), const, pipeline_mode=pl.Buffered(1)),
            pl.BlockSpec((1, D_MODEL), const),
        ],
        out_specs=pl.BlockSpec((1, tm, D_MODEL), row),
        out_shape=jax.ShapeDtypeStruct((B, S, D_MODEL), F32),
        compiler_params=_params(("arbitrary", "arbitrary")),
        name="outproj",
    )(x, ya, yb, yc, z, w, g)


def _rope_tables(seq):
    inv = 1.0 / (ROPE_THETA ** (jnp.arange(0, MLA_ROPE, 2, dtype=F32) / MLA_ROPE))
    ang = jnp.arange(seq, dtype=F32)[:, None] * inv[None, :]
    cos, sin = jnp.cos(ang), jnp.sin(ang)
    zero = jnp.zeros_like(cos)
    return (jnp.concatenate([cos, cos, zero, zero], axis=-1),
            jnp.concatenate([-sin, sin, zero, zero], axis=-1))


def _t5_bucket(rel):
    nb = N_BUCKETS // 2
    max_exact = nb // 2
    bucket = jnp.where(rel > 0, nb, 0)
    n = jnp.abs(rel)
    nf = jnp.maximum(n, 1).astype(F32)
    large = max_exact + (jnp.log(nf / max_exact) / math.log(MAX_DISTANCE / max_exact)
                         * (nb - max_exact)).astype(jnp.int32)
    large = jnp.minimum(large, nb - 1)
    return bucket + jnp.where(n < max_exact, n, large)


def kernel(x, mem, norm_in, w_in, norm_q, norm_kv, w_uq, w_ukv, attn_sinks, rel_bias,
           norm_mem, w_mem_kv, w_out, norm_final):
    B, S, _ = x.shape
    assert norm_in.shape[0] == 1, "single-layer trunk"

    wt = w_in[0].T.astype(BF16)
    wq = w_uq[0].reshape(MLA_QLORA, MLA_HEADS, MLA_NOPE + MLA_ROPE)
    wq_pe = wq[..., MLA_NOPE:]
    wuq = jnp.concatenate([wq, wq_pe], axis=-1)
    wuq = wuq.reshape(MLA_QLORA, MLA_HEADS * MLA_QK_PAD).astype(BF16)
    wukv = w_ukv[0].astype(BF16)

    cos_t, sin_t = _rope_tables(S)
    q, k, v, qs, z, ks, vs = _proj(
        x, norm_in, wt, norm_q, norm_kv, wuq, wukv, cos_t, sin_t, tm=256)

    y_mla = _mla(q, k, v, tq=512, tk=512, g=4)

    bucket = _t5_bucket(jnp.arange(_REL_SPAN) - (2 * SWA_BLOCK - 1))
    bucket = jnp.arange(N_BUCKETS, dtype=jnp.int32)[bucket][None, :]
    bias = _bias(rel_bias, bucket)
    y_swa = _swa(attn_sinks[0], qs, ks, vs, bias, nsub=4)

    kmem, vmem = _memkv(mem, norm_mem, w_mem_kv[0].astype(BF16))
    y_mem = _mem_attn(qs, kmem, vmem, tm=512)

    return _out(x, y_mla, y_swa, y_mem, z, w_out[0].astype(BF16), norm_final[None, :], tm=256)
```

```python
import functools
import math

import jax
import jax.numpy as jnp
from jax import lax
from jax.experimental import pallas as pl
from jax.experimental.pallas import tpu as pltpu

D_MODEL = 2048
CHUNK = 64
N_MEM = 256
EPS = 1e-6
NEG = -1e30

MLA_HEADS = 8
MLA_NOPE = 128
MLA_ROPE = 64
MLA_V = 128
MLA_QLORA = 512
MLA_KVLORA = 256
MLA_WIDTH = MLA_HEADS * MLA_V
MLA_QK_PAD = 256
ROPE_THETA = 10000.0

SWA_HEADS = 8
SWA_KV_HEADS = 2
SWA_GROUP = SWA_HEADS // SWA_KV_HEADS
SWA_HEAD_DIM = 64
SWA_WIDTH = SWA_HEADS * SWA_HEAD_DIM
SWA_KV_WIDTH = SWA_KV_HEADS * SWA_HEAD_DIM
WINDOW_CHUNKS = 2
SWA_BLOCK = 128

MEM_HEADS = 4
MEM_HEAD_DIM = 128
MEM_WIDTH = MEM_HEADS * MEM_HEAD_DIM

MIX_WIDTH = MLA_WIDTH + SWA_WIDTH + MEM_WIDTH

N_BUCKETS = 32
MAX_DISTANCE = 128

BF16 = jnp.bfloat16
F32 = jnp.float32

_R_CQ = 0
_R_CKV = _R_CQ + MLA_QLORA
_R_KPE = _R_CKV + MLA_KVLORA
_R_ZMLA = _R_KPE + MLA_ROPE
_R_QSWA = _R_ZMLA + MLA_WIDTH
_R_KSWA = _R_QSWA + SWA_WIDTH
_R_VSWA = _R_KSWA + SWA_KV_WIDTH
_R_ZSWA = _R_VSWA + SWA_KV_WIDTH
_R_QMEM = _R_ZSWA + SWA_WIDTH
_R_ZMEM = _R_QMEM + MEM_WIDTH
IN_WIDTH = _R_ZMEM + MEM_WIDTH

_MLA_Q_SCALE = (MLA_NOPE + MLA_ROPE) ** -0.5 * math.log2(math.e)

_SWA_KV_PAD = 2 * SWA_KV_HEADS * 128

_VMEM_LIMIT = 56 * 1024 * 1024


def _params(sem, flags=None):
    return pltpu.CompilerParams(dimension_semantics=sem, vmem_limit_bytes=_VMEM_LIMIT, flags=flags)


def _rms(v, g):
    return (v * lax.rsqrt(jnp.mean(v * v, axis=-1, keepdims=True) + EPS)) * g


def _dot(a, b):
    return jnp.dot(a, b, preferred_element_type=F32)


def _dot_nt(a, b):
    return lax.dot_general(a, b, (((1,), (1,)), ((), ())), preferred_element_type=F32)


def _lane_tile(v, n):
    return jnp.concatenate([v] * n, axis=1)


def _chunk_of(pos):
    return jnp.right_shift(pos, CHUNK.bit_length() - 1)


def _rope128(v, cos_t, sin_t):
    return v * cos_t + pltpu.roll(v, MLA_ROPE // 2, 1) * sin_t


def _store_swa_kv(ref, kv):
    low = lax.broadcasted_iota(jnp.int32, kv.shape, 1) < SWA_HEAD_DIM
    swapped = pltpu.roll(kv, SWA_HEAD_DIM, 1)
    pieces = (jnp.where(low, kv, 0.0), jnp.where(low, 0.0, swapped),
              jnp.where(low, swapped, 0.0), jnp.where(low, 0.0, kv))
    for i, piece in enumerate(pieces):
        ref[0, :, i * 128:(i + 1) * 128] = piece.astype(BF16)


def _proj_kernel(x_ref, gin_ref, wt_ref, gq_ref, gkv_ref, wuq_ref, wukv_ref, cos_ref, sin_ref,
                 q_ref, k_ref, v_ref, qs_ref, z_ref, ks_ref, vs_ref):
    h = _rms(x_ref[0], gin_ref[...]).astype(BF16)
    cos_t = cos_ref[...]
    sin_t = sin_ref[...]

    def proj(lo, hi):
        return _dot_nt(h, wt_ref[lo:hi, :])

    pa = proj(_R_CQ, _R_KPE)
    cq = _rms(pa[:, :MLA_QLORA], gq_ref[...]).astype(BF16)
    ckv = _rms(pa[:, MLA_QLORA:], gkv_ref[...]).astype(BF16)

    qs_ref[0, :, :SWA_WIDTH] = proj(_R_QSWA, _R_KSWA).astype(BF16)
    qs_ref[0, :, SWA_WIDTH:] = proj(_R_QMEM, _R_ZMEM).astype(BF16)
    z_ref[0, :, :MLA_WIDTH] = proj(_R_ZMLA, _R_QSWA).astype(BF16)
    z_ref[0, :, MLA_WIDTH:MLA_WIDTH + SWA_WIDTH] = proj(_R_ZSWA, _R_QMEM).astype(BF16)
    z_ref[0, :, MLA_WIDTH + SWA_WIDTH:] = proj(_R_ZMEM, IN_WIDTH).astype(BF16)
    kvs = proj(_R_KSWA, _R_ZSWA)
    _store_swa_kv(ks_ref, kvs[:, :SWA_KV_WIDTH])
    _store_swa_kv(vs_ref, kvs[:, SWA_KV_WIDTH:])
    pe = proj(_R_KPE, _R_ZMLA)
    kpe = _rope128(jnp.concatenate([pe, pe], axis=1), cos_t, sin_t).astype(BF16)

    qall = _dot(cq, wuq_ref[...]) * _MLA_Q_SCALE
    kv = _dot(ckv, wukv_ref[...])
    ones = jnp.ones((kv.shape[0], MLA_V), BF16)
    for hd in range(MLA_HEADS):
        o = hd * MLA_QK_PAD
        q_ref[0, hd, :, :MLA_NOPE] = qall[:, o:o + MLA_NOPE].astype(BF16)
        q_ref[0, hd, :, MLA_NOPE:] = _rope128(
            qall[:, o + MLA_NOPE:o + MLA_QK_PAD], cos_t, sin_t).astype(BF16)
        k_ref[0, hd, :, :MLA_NOPE] = kv[:, o:o + MLA_NOPE].astype(BF16)
        k_ref[0, hd, :, MLA_NOPE:] = kpe
        v_ref[0, hd, :, :MLA_V] = kv[:, o + MLA_NOPE:o + MLA_QK_PAD].astype(BF16)
        v_ref[0, hd, :, MLA_V:] = ones


def _proj(x, gin, wt, gq, gkv, wuq, wukv, cos_t, sin_t, tm):
    B, S, _ = x.shape
    const = lambda b, i: (0, 0)
    row = lambda b, i: (b, i, 0)
    single = pl.Buffered(1)

    def out(width):
        return jax.ShapeDtypeStruct((B, S, width), BF16)

    head_spec = pl.BlockSpec((1, MLA_HEADS, tm, MLA_QK_PAD), lambda b, i: (b, 0, i, 0))
    head_shape = jax.ShapeDtypeStruct((B, MLA_HEADS, S, MLA_QK_PAD), BF16)

    return pl.pallas_call(
        _proj_kernel,
        grid=(B, S // tm),
        in_specs=[
            pl.BlockSpec((1, tm, D_MODEL), row),
            pl.BlockSpec((1, D_MODEL), const),
            pl.BlockSpec((IN_WIDTH, D_MODEL), const, pipeline_mode=single),
            pl.BlockSpec((1, MLA_QLORA), const),
            pl.BlockSpec((1, MLA_KVLORA), const),
            pl.BlockSpec((MLA_QLORA, MLA_HEADS * MLA_QK_PAD), const, pipeline_mode=single),
            pl.BlockSpec((MLA_KVLORA, MLA_HEADS * (MLA_NOPE + MLA_V)), const, pipeline_mode=single),
            pl.BlockSpec((tm, 128), lambda b, i: (i, 0)),
            pl.BlockSpec((tm, 128), lambda b, i: (i, 0)),
        ],
        out_specs=[
            head_spec, head_spec, head_spec,
            pl.BlockSpec((1, tm, SWA_WIDTH + MEM_WIDTH), row),
            pl.BlockSpec((1, tm, MIX_WIDTH), row),
            pl.BlockSpec((1, tm, _SWA_KV_PAD), row),
            pl.BlockSpec((1, tm, _SWA_KV_PAD), row),
        ],
        out_shape=[
            head_shape, head_shape, head_shape,
            out(SWA_WIDTH + MEM_WIDTH), out(MIX_WIDTH), out(_SWA_KV_PAD), out(_SWA_KV_PAD),
        ],
        compiler_params=_params(("arbitrary", "arbitrary")),
        name="proj",
    )(x, gin, wt, gq, gkv, wuq, wukv, cos_t, sin_t)


def _mla_kernel(q_ref, k_ref, v_ref, o_ref, acc_ref, m_ref, s_ref, cm_ref, *, tq, tk, g):
    qi = pl.program_id(2)
    acc_ref[...] = jnp.zeros(acc_ref.shape, F32)
    m_ref[...] = jnp.full(m_ref.shape, NEG, F32)

    def row_max(s):
        return jnp.broadcast_to(jnp.max(s, axis=-1, keepdims=True), (tq, 128))

    def scores(c, slot):
        start = pl.multiple_of(c * tk, tk)
        for hd in range(g):
            s = _dot_nt(q_ref[0, hd], k_ref[0, hd, pl.ds(start, tk), :])
            s_ref[slot, hd] = s
            cm_ref[slot, hd] = row_max(s)

    def update(c, slot, masked):
        start = pl.multiple_of(c * tk, tk)
        if masked:
            q_chunk = _chunk_of(qi * tq + lax.broadcasted_iota(jnp.int32, (tq, tk), 0))
            k_chunk = _chunk_of(start + lax.broadcasted_iota(jnp.int32, (tq, tk), 1))
            mask = k_chunk <= q_chunk
        for hd in range(g):
            s = s_ref[slot, hd]
            if masked:
                s = jnp.where(mask, s, NEG)
                m_cur = row_max(s)
            else:
                m_cur = cm_ref[slot, hd]
            m_old = m_ref[hd]
            m_new = jnp.maximum(m_old, m_cur)
            alpha = jnp.exp2(m_old - m_new)
            p = jnp.exp2(s - _lane_tile(m_new, tk // 128)).astype(BF16)
            acc_ref[hd] = (_lane_tile(alpha, MLA_QK_PAD // 128) * acc_ref[hd]
                           + _dot(p, v_ref[0, hd, pl.ds(start, tk), :]))
            m_ref[hd] = m_new

    n_full = jnp.right_shift(qi, (tk // tq).bit_length() - 1)
    n_pairs = jnp.right_shift(n_full, 1)
    scores(0, 0)

    def body(jj, carry):
        c = 2 * jj
        scores(c + 1, 1)
        update(c, 0, False)
        scores(c + 2, 0)
        update(c + 1, 1, False)
        return carry

    lax.fori_loop(0, n_pairs, body, 0)
    c = 2 * n_pairs

    @pl.when(n_full > c)
    def _():
        scores(c + 1, 1)
        update(c, 0, False)
        update(c + 1, 1, True)

    @pl.when(n_full == c)
    def _():
        update(c, 0, True)

    for hd in range(g):
        acc = acc_ref[hd]
        o_ref[0, :, hd * MLA_V:(hd + 1) * MLA_V] = (acc[:, :MLA_V] / acc[:, MLA_V:]).astype(BF16)


def _mla(q, k, v, tq, tk, g):
    B, _, S, _ = q.shape
    assert tk % tq == 0 and (tk // tq) & (tk // tq - 1) == 0 and S % tk == 0
    return pl.pallas_call(
        functools.partial(_mla_kernel, tq=tq, tk=tk, g=g),
        grid=(B, MLA_HEADS // g, S // tq),
        in_specs=[
            pl.BlockSpec((1, g, tq, MLA_QK_PAD), lambda b, h, i: (b, h, i, 0)),
            pl.BlockSpec((1, g, S, MLA_QK_PAD), lambda b, h, i: (b, h, 0, 0)),
            pl.BlockSpec((1, g, S, MLA_QK_PAD), lambda b, h, i: (b, h, 0, 0)),
        ],
        out_specs=pl.BlockSpec((1, tq, g * MLA_V), lambda b, h, i: (b, i, h)),
        out_shape=jax.ShapeDtypeStruct((B, S, MLA_WIDTH), BF16),
        scratch_shapes=[
            pltpu.VMEM((g, tq, MLA_QK_PAD), F32),
            pltpu.VMEM((g, tq, 128), F32),
            pltpu.VMEM((2, g, tq, tk), F32),
            pltpu.VMEM((2, g, tq, 128), F32),
        ],
        compiler_params=_params(("arbitrary", "arbitrary", "arbitrary")),
        name="mla",
    )(q, k, v)


_REL_SPAN = 4 * SWA_BLOCK


_N_OFFSETS = 3 * SWA_BLOCK - 1


def _bias_kernel(table_ref, bucket_ref, o_ref):
    head = lax.broadcasted_iota(jnp.int32, (SWA_HEADS, 128), 0)
    lane = lax.broadcasted_iota(jnp.int32, (SWA_HEADS, _REL_SPAN), 1)

    def lookup(c, per_offset):
        b = bucket_ref[c]
        col = jnp.zeros((SWA_HEADS, 128), F32)
        for hd in range(SWA_HEADS):
            col = jnp.where(head == hd, table_ref[b, hd], col)
        return jnp.where(lane == c, _lane_tile(col, _REL_SPAN // 128), per_offset)

    per_offset_all = lax.fori_loop(0, _N_OFFSETS, lookup, jnp.zeros((SWA_HEADS, _REL_SPAN), F32))
    for hd in range(SWA_HEADS):
        per_offset = per_offset_all[hd:hd + 1, :]
        rows = jnp.broadcast_to(per_offset, (SWA_BLOCK, _REL_SPAN))
        band = pltpu.roll(rows, _REL_SPAN - (SWA_BLOCK - 1), 1, stride=1, stride_axis=0)
        kvh, within = divmod(hd, SWA_GROUP)
        par, half = within % 2, within // 2
        o_ref[kvh, par, half * SWA_BLOCK:(half + 1) * SWA_BLOCK, :] = band[:, :2 * SWA_BLOCK]


def _bias(rel_table, bucket):
    return pl.pallas_call(
        _bias_kernel,
        in_specs=[
            pl.BlockSpec(memory_space=pltpu.SMEM),
            pl.BlockSpec(memory_space=pltpu.SMEM),
        ],
        out_specs=pl.BlockSpec(memory_space=pltpu.VMEM),
        out_shape=jax.ShapeDtypeStruct((SWA_KV_HEADS, 2, 2 * SWA_BLOCK, 2 * SWA_BLOCK), F32),
        name="t5bias",
    )(rel_table, bucket)


def _swa_kernel(sink_ref, q_ref, kp_ref, kc_ref, vp_ref, vc_ref, bias_ref, o_ref, kb_ref, vb_ref,
                *, nsub):
    t = pl.program_id(1)
    scale = SWA_HEAD_DIM ** -0.5
    sb = SWA_BLOCK
    kb_ref[:sb] = kp_ref[0]
    kb_ref[sb:] = kc_ref[0]
    vb_ref[:sb] = vp_ref[0]
    vb_ref[sb:] = vc_ref[0]

    row = lax.broadcasted_iota(jnp.int32, (2 * sb, 2 * sb), 0)
    q_chunk = _chunk_of(row & (sb - 1))
    b_chunk = _chunk_of(lax.broadcasted_iota(jnp.int32, (2 * sb, 2 * sb), 1))
    valid_any = jnp.logical_and(b_chunk >= q_chunk, b_chunk <= q_chunk + WINDOW_CHUNKS)
    first_lo = jnp.where(t > 0, 0, 2)
    valid_first = jnp.logical_and(valid_any, b_chunk >= first_lo)
    upper = lax.broadcasted_iota(jnp.int32, (2 * sb, 1), 0) < sb

    for r in range(nsub):
        valid = valid_first if r == 0 else valid_any
        rows = slice(r * sb, (r + 1) * sb)
        band = slice(r * sb, (r + 2) * sb)
        for kvh in range(SWA_KV_HEADS):
            pair0 = slice(2 * kvh * 128, (2 * kvh + 1) * 128)
            pair1 = slice((2 * kvh + 1) * 128, (2 * kvh + 2) * 128)
            q = jnp.concatenate([q_ref[0, rows, pair0], q_ref[0, rows, pair1]], axis=0)
            o = None
            for par in range(2):
                kcols = slice((2 * kvh + par) * 128, (2 * kvh + par + 1) * 128)
                hd = SWA_GROUP * kvh + par
                s = _dot_nt(q, kb_ref[band, kcols]) * scale + bias_ref[kvh, par]
                s = jnp.where(valid, s, NEG)
                sink = jnp.where(upper, sink_ref[hd], sink_ref[hd + 2])
                m = jnp.maximum(jnp.max(s, axis=-1, keepdims=True), sink)
                p = jnp.exp(s - m)
                den = jnp.sum(p, axis=-1, keepdims=True) + jnp.exp(sink - m)
                pv = _dot((p * (1.0 / den)).astype(BF16), vb_ref[band, kcols])
                o = pv if o is None else o + pv
            o_ref[0, rows, pair0] = o[:sb].astype(BF16)
            o_ref[0, rows, pair1] = o[sb:].astype(BF16)


def _swa(sinks, qs, ks, vs, bias, nsub):
    B, S, _ = qs.shape
    tq = nsub * SWA_BLOCK
    prev_map = lambda b, t: (b, jnp.maximum(t * nsub - 1, 0), 0)
    cur_map = lambda b, t: (b, t, 0)
    return pl.pallas_call(
        functools.partial(_swa_kernel, nsub=nsub),
        grid=(B, S // tq),
        in_specs=[
            pl.BlockSpec(memory_space=pltpu.SMEM),
            pl.BlockSpec((1, tq, SWA_WIDTH), cur_map),
            pl.BlockSpec((1, SWA_BLOCK, _SWA_KV_PAD), prev_map),
            pl.BlockSpec((1, tq, _SWA_KV_PAD), cur_map),
            pl.BlockSpec((1, SWA_BLOCK, _SWA_KV_PAD), prev_map),
            pl.BlockSpec((1, tq, _SWA_KV_PAD), cur_map),
            pl.BlockSpec((SWA_KV_HEADS, 2, 2 * SWA_BLOCK, 2 * SWA_BLOCK), lambda b, t: (0, 0, 0, 0)),
        ],
        out_specs=pl.BlockSpec((1, tq, SWA_WIDTH), cur_map),
        out_shape=jax.ShapeDtypeStruct((B, S, SWA_WIDTH), BF16),
        scratch_shapes=[
            pltpu.VMEM((tq + SWA_BLOCK, _SWA_KV_PAD), BF16),
            pltpu.VMEM((tq + SWA_BLOCK, _SWA_KV_PAD), BF16),
        ],
        compiler_params=_params(("arbitrary", "arbitrary")),
        name="swa",
    )(sinks, qs, ks, ks, vs, vs, bias)


def _memkv_kernel(mem_ref, g_ref, w_ref, k_ref, v_ref):
    mn = _rms(mem_ref[0], g_ref[...]).astype(BF16)
    kv = _dot(mn, w_ref[...])
    k_ref[0] = kv[:, :MEM_WIDTH].astype(BF16)
    v_ref[0] = kv[:, MEM_WIDTH:].astype(BF16)


def _memkv(mem, g, w):
    B, M, _ = mem.shape
    const = lambda b: (0, 0)
    row = lambda b: (b, 0, 0)
    return pl.pallas_call(
        _memkv_kernel,
        grid=(B,),
        in_specs=[
            pl.BlockSpec((1, M, D_MODEL), row),
            pl.BlockSpec((1, D_MODEL), const),
            pl.BlockSpec((D_MODEL, 2 * MEM_WIDTH), const),
        ],
        out_specs=[pl.BlockSpec((1, M, MEM_WIDTH), row)] * 2,
        out_shape=[jax.ShapeDtypeStruct((B, M, MEM_WIDTH), BF16)] * 2,
        compiler_params=_params(("arbitrary",)),
        name="memkv",
    )(mem, g, w)


def _mem_kernel(q_ref, k_ref, v_ref, o_ref):
    scale = MEM_HEAD_DIM ** -0.5
    for hd in range(MEM_HEADS):
        cols = slice(hd * MEM_HEAD_DIM, (hd + 1) * MEM_HEAD_DIM)
        s = _dot_nt(q_ref[0, :, cols], k_ref[0, :, cols]) * scale
        m = jnp.max(s, axis=-1, keepdims=True)
        p = jnp.exp(s - m)
        inv = 1.0 / jnp.sum(p, axis=-1, keepdims=True)
        o_ref[0, :, cols] = _dot((p * inv).astype(BF16), v_ref[0, :, cols]).astype(BF16)


def _mem_attn(qs, kmem, vmem, tm):
    B, S, _ = qs.shape
    M = kmem.shape[1]
    return pl.pallas_call(
        _mem_kernel,
        grid=(B, S // tm),
        in_specs=[
            pl.BlockSpec((1, tm, MEM_WIDTH), lambda b, i: (b, i, 1)),
            pl.BlockSpec((1, M, MEM_WIDTH), lambda b, i: (b, 0, 0)),
            pl.BlockSpec((1, M, MEM_WIDTH), lambda b, i: (b, 0, 0)),
        ],
        out_specs=pl.BlockSpec((1, tm, MEM_WIDTH), lambda b, i: (b, i, 0)),
        out_shape=jax.ShapeDtypeStruct((B, S, MEM_WIDTH), BF16),
        compiler_params=_params(("arbitrary", "arbitrary")),
        name="memattn",
    )(qs, kmem, vmem)


def _out_kernel(x_ref, ya_ref, yb_ref, yc_ref, z_ref, w_ref, g_ref, o_ref):
    def gated(y_ref, lo, hi):
        z = z_ref[0, :, lo:hi].astype(F32)
        return (y_ref[0].astype(F32) * (z * jax.nn.sigmoid(z))).astype(BF16)

    y = _dot(gated(ya_ref, 0, MLA_WIDTH), w_ref[:MLA_WIDTH, :])
    y = y + _dot(gated(yb_ref, MLA_WIDTH, MLA_WIDTH + SWA_WIDTH),
                 w_ref[MLA_WIDTH:MLA_WIDTH + SWA_WIDTH, :])
    y = y + _dot(gated(yc_ref, MLA_WIDTH + SWA_WIDTH, MIX_WIDTH),
                 w_ref[MLA_WIDTH + SWA_WIDTH:, :])
    o_ref[0] = _rms(x_ref[0] + y, g_ref[...])


def _out(x, ya, yb, yc, z, w, g, tm):
    B, S, _ = x.shape
    row = lambda b, i: (b, i, 0)
    const = lambda b, i: (0, 0)
    return pl.pallas_call(
        _out_kernel,
        grid=(B, S // tm),
        in_specs=[
            pl.BlockSpec((1, tm, D_MODEL), row),
            pl.BlockSpec((1, tm, MLA_WIDTH), row),
            pl.BlockSpec((1, tm, SWA_WIDTH), row),
            pl.BlockSpec((1, tm, MEM_WIDTH), row),
            pl.BlockSpec((1, tm, MIX_WIDTH), row),
            pl.BlockSpec((MIX_WIDTH, D_MODEL), const, pipeline_mode=pl.Buffered(1)),
            pl.BlockSpec((1, D_MODEL), const),
        ],
        out_specs=pl.BlockSpec((1, tm, D_MODEL), row),
        out_shape=jax.ShapeDtypeStruct((B, S, D_MODEL), F32),
        compiler_params=_params(("arbitrary", "arbitrary")),
        name="outproj",
    )(x, ya, yb, yc, z, w, g)


def _rope_tables(seq):
    inv = 1.0 / (ROPE_THETA ** (jnp.arange(0, MLA_ROPE, 2, dtype=F32) / MLA_ROPE))
    ang = jnp.arange(seq, dtype=F32)[:, None] * inv[None, :]
    cos, sin = jnp.cos(ang), jnp.sin(ang)
    zero = jnp.zeros_like(cos)
    return (jnp.concatenate([cos, cos, zero, zero], axis=-1),
            jnp.concatenate([-sin, sin, zero, zero], axis=-1))


def _t5_bucket(rel):
    nb = N_BUCKETS // 2
    max_exact = nb // 2
    bucket = jnp.where(rel > 0, nb, 0)
    n = jnp.abs(rel)
    nf = jnp.maximum(n, 1).astype(F32)
    large = max_exact + (jnp.log(nf / max_exact) / math.log(MAX_DISTANCE / max_exact)
                         * (nb - max_exact)).astype(jnp.int32)
    large = jnp.minimum(large, nb - 1)
    return bucket + jnp.where(n < max_exact, n, large)


def kernel(x, mem, norm_in, w_in, norm_q, norm_kv, w_uq, w_ukv, attn_sinks, rel_bias,
           norm_mem, w_mem_kv, w_out, norm_final):
    B, S, _ = x.shape
    assert norm_in.shape[0] == 1, "single-layer trunk"

    wt = w_in[0].T.astype(BF16)
    wq = w_uq[0].reshape(MLA_QLORA, MLA_HEADS, MLA_NOPE + MLA_ROPE)
    wq_pe = wq[..., MLA_NOPE:]
    wuq = jnp.concatenate([wq, wq_pe], axis=-1)
    wuq = wuq.reshape(MLA_QLORA, MLA_HEADS * MLA_QK_PAD).astype(BF16)
    wukv = w_ukv[0].astype(BF16)

    cos_t, sin_t = _rope_tables(S)
    q, k, v, qs, z, ks, vs = _proj(
        x, norm_in, wt, norm_q, norm_kv, wuq, wukv, cos_t, sin_t, tm=256)

    y_mla = _mla(q, k, v, tq=512, tk=512, g=4)

    bucket = _t5_bucket(jnp.arange(_REL_SPAN) - (2 * SWA_BLOCK - 1)).astype(jnp.int32)
    bias = _bias(rel_bias, bucket)
    y_swa = _swa(attn_sinks[0], qs, ks, vs, bias, nsub=4)

    kmem, vmem = _memkv(mem, norm_mem, w_mem_kv[0].astype(BF16))
    y_mem = _mem_attn(qs, kmem, vmem, tm=512)

    return _out(x, y_mla, y_swa, y_mem, z, w_out[0].astype(BF16), norm_final[None, :], tm=256)
```

```python
import functools
import math

import jax
import jax.numpy as jnp
from jax import lax
from jax.experimental import pallas as pl
from jax.experimental.pallas import tpu as pltpu

D_MODEL = 2048
CHUNK = 64
N_MEM = 256
EPS = 1e-6
NEG = -1e30

MLA_HEADS = 8
MLA_NOPE = 128
MLA_ROPE = 64
MLA_V = 128
MLA_QLORA = 512
MLA_KVLORA = 256
MLA_WIDTH = MLA_HEADS * MLA_V
MLA_QK_PAD = 256
ROPE_THETA = 10000.0

SWA_HEADS = 8
SWA_KV_HEADS = 2
SWA_GROUP = SWA_HEADS // SWA_KV_HEADS
SWA_HEAD_DIM = 64
SWA_WIDTH = SWA_HEADS * SWA_HEAD_DIM
SWA_KV_WIDTH = SWA_KV_HEADS * SWA_HEAD_DIM
WINDOW_CHUNKS = 2
SWA_BLOCK = 128

MEM_HEADS = 4
MEM_HEAD_DIM = 128
MEM_WIDTH = MEM_HEADS * MEM_HEAD_DIM

MIX_WIDTH = MLA_WIDTH + SWA_WIDTH + MEM_WIDTH

N_BUCKETS = 32
MAX_DISTANCE = 128

BF16 = jnp.bfloat16
F32 = jnp.float32

_R_CQ = 0
_R_CKV = _R_CQ + MLA_QLORA
_R_KPE = _R_CKV + MLA_KVLORA
_R_ZMLA = _R_KPE + MLA_ROPE
_R_QSWA = _R_ZMLA + MLA_WIDTH
_R_KSWA = _R_QSWA + SWA_WIDTH
_R_VSWA = _R_KSWA + SWA_KV_WIDTH
_R_ZSWA = _R_VSWA + SWA_KV_WIDTH
_R_QMEM = _R_ZSWA + SWA_WIDTH
_R_ZMEM = _R_QMEM + MEM_WIDTH
IN_WIDTH = _R_ZMEM + MEM_WIDTH

_MLA_Q_SCALE = (MLA_NOPE + MLA_ROPE) ** -0.5 * math.log2(math.e)

_SWA_KV_PAD = 2 * SWA_KV_HEADS * 128

_VMEM_LIMIT = 56 * 1024 * 1024


def _params(sem, flags=None):
    return pltpu.CompilerParams(dimension_semantics=sem, vmem_limit_bytes=_VMEM_LIMIT, flags=flags)


def _rms(v, g):
    return (v * lax.rsqrt(jnp.mean(v * v, axis=-1, keepdims=True) + EPS)) * g


def _dot(a, b):
    return jnp.dot(a, b, preferred_element_type=F32)


def _dot_nt(a, b):
    return lax.dot_general(a, b, (((1,), (1,)), ((), ())), preferred_element_type=F32)


def _lane_tile(v, n):
    return jnp.concatenate([v] * n, axis=1)


def _chunk_of(pos):
    return jnp.right_shift(pos, CHUNK.bit_length() - 1)


def _rope128(v, cos_t, sin_t):
    return v * cos_t + pltpu.roll(v, MLA_ROPE // 2, 1) * sin_t


def _store_swa_kv(ref, kv):
    low = lax.broadcasted_iota(jnp.int32, kv.shape, 1) < SWA_HEAD_DIM
    swapped = pltpu.roll(kv, SWA_HEAD_DIM, 1)
    pieces = (jnp.where(low, kv, 0.0), jnp.where(low, 0.0, swapped),
              jnp.where(low, swapped, 0.0), jnp.where(low, 0.0, kv))
    for i, piece in enumerate(pieces):
        ref[0, :, i * 128:(i + 1) * 128] = piece.astype(BF16)


def _proj_kernel(x_ref, gin_ref, wt_ref, gq_ref, gkv_ref, wuq_ref, wukv_ref, cos_ref, sin_ref,
                 q_ref, k_ref, v_ref, qs_ref, z_ref, ks_ref, vs_ref):
    h = _rms(x_ref[0], gin_ref[...]).astype(BF16)
    cos_t = cos_ref[...]
    sin_t = sin_ref[...]

    def proj(lo, hi):
        return _dot_nt(h, wt_ref[lo:hi, :])

    pa = proj(_R_CQ, _R_KPE)
    cq = _rms(pa[:, :MLA_QLORA], gq_ref[...]).astype(BF16)
    ckv = _rms(pa[:, MLA_QLORA:], gkv_ref[...]).astype(BF16)

    qs_ref[0, :, :SWA_WIDTH] = proj(_R_QSWA, _R_KSWA).astype(BF16)
    qs_ref[0, :, SWA_WIDTH:] = proj(_R_QMEM, _R_ZMEM).astype(BF16)
    z_ref[0, :, :MLA_WIDTH] = proj(_R_ZMLA, _R_QSWA).astype(BF16)
    z_ref[0, :, MLA_WIDTH:MLA_WIDTH + SWA_WIDTH] = proj(_R_ZSWA, _R_QMEM).astype(BF16)
    z_ref[0, :, MLA_WIDTH + SWA_WIDTH:] = proj(_R_ZMEM, IN_WIDTH).astype(BF16)
    kvs = proj(_R_KSWA, _R_ZSWA)
    _store_swa_kv(ks_ref, kvs[:, :SWA_KV_WIDTH])
    _store_swa_kv(vs_ref, kvs[:, SWA_KV_WIDTH:])
    pe = proj(_R_KPE, _R_ZMLA)
    kpe = _rope128(jnp.concatenate([pe, pe], axis=1), cos_t, sin_t).astype(BF16)

    qall = _dot(cq, wuq_ref[...]) * _MLA_Q_SCALE
    kv = _dot(ckv, wukv_ref[...])
    ones = jnp.ones((kv.shape[0], MLA_V), BF16)
    for hd in range(MLA_HEADS):
        o = hd * MLA_QK_PAD
        q_ref[0, hd, :, :MLA_NOPE] = qall[:, o:o + MLA_NOPE].astype(BF16)
        q_ref[0, hd, :, MLA_NOPE:] = _rope128(
            qall[:, o + MLA_NOPE:o + MLA_QK_PAD], cos_t, sin_t).astype(BF16)
        k_ref[0, hd, :, :MLA_NOPE] = kv[:, o:o + MLA_NOPE].astype(BF16)
        k_ref[0, hd, :, MLA_NOPE:] = kpe
        v_ref[0, hd, :, :MLA_V] = kv[:, o + MLA_NOPE:o + MLA_QK_PAD].astype(BF16)
        v_ref[0, hd, :, MLA_V:] = ones


def _proj(x, gin, wt, gq, gkv, wuq, wukv, cos_t, sin_t, tm):
    B, S, _ = x.shape
    const = lambda b, i: (0, 0)
    row = lambda b, i: (b, i, 0)
    single = pl.Buffered(1)

    def out(width):
        return jax.ShapeDtypeStruct((B, S, width), BF16)

    head_spec = pl.BlockSpec((1, MLA_HEADS, tm, MLA_QK_PAD), lambda b, i: (b, 0, i, 0))
    head_shape = jax.ShapeDtypeStruct((B, MLA_HEADS, S, MLA_QK_PAD), BF16)

    return pl.pallas_call(
        _proj_kernel,
        grid=(B, S // tm),
        in_specs=[
            pl.BlockSpec((1, tm, D_MODEL), row),
            pl.BlockSpec((1, D_MODEL), const),
            pl.BlockSpec((IN_WIDTH, D_MODEL), const, pipeline_mode=single),
            pl.BlockSpec((1, MLA_QLORA), const),
            pl.BlockSpec((1, MLA_KVLORA), const),
            pl.BlockSpec((MLA_QLORA, MLA_HEADS * MLA_QK_PAD), const, pipeline_mode=single),
            pl.BlockSpec((MLA_KVLORA, MLA_HEADS * (MLA_NOPE + MLA_V)), const, pipeline_mode=single),
            pl.BlockSpec((tm, 128), lambda b, i: (i, 0)),
            pl.BlockSpec((tm, 128), lambda b, i: (i, 0)),
        ],
        out_specs=[
            head_spec, head_spec, head_spec,
            pl.BlockSpec((1, tm, SWA_WIDTH + MEM_WIDTH), row),
            pl.BlockSpec((1, tm, MIX_WIDTH), row),
            pl.BlockSpec((1, tm, _SWA_KV_PAD), row),
            pl.BlockSpec((1, tm, _SWA_KV_PAD), row),
        ],
        out_shape=[
            head_shape, head_shape, head_shape,
            out(SWA_WIDTH + MEM_WIDTH), out(MIX_WIDTH), out(_SWA_KV_PAD), out(_SWA_KV_PAD),
        ],
        compiler_params=_params(("arbitrary", "arbitrary")),
        name="proj",
    )(x, gin, wt, gq, gkv, wuq, wukv, cos_t, sin_t)


def _mla_kernel(q_ref, k_ref, v_ref, o_ref, acc_ref, m_ref, s_ref, cm_ref, *, tq, tk, g):
    qi = pl.program_id(2)
    acc_ref[...] = jnp.zeros(acc_ref.shape, F32)
    m_ref[...] = jnp.full(m_ref.shape, NEG, F32)

    def row_max(s):
        return jnp.broadcast_to(jnp.max(s, axis=-1, keepdims=True), (tq, 128))

    def scores(c, slot):
        start = pl.multiple_of(c * tk, tk)
        for hd in range(g):
            s = _dot_nt(q_ref[0, hd], k_ref[0, hd, pl.ds(start, tk), :])
            s_ref[slot, hd] = s
            cm_ref[slot, hd] = row_max(s)

    def update(c, slot, masked):
        start = pl.multiple_of(c * tk, tk)
        if masked:
            q_chunk = _chunk_of(qi * tq + lax.broadcasted_iota(jnp.int32, (tq, tk), 0))
            k_chunk = _chunk_of(start + lax.broadcasted_iota(jnp.int32, (tq, tk), 1))
            mask = k_chunk <= q_chunk
        for hd in range(g):
            s = s_ref[slot, hd]
            if masked:
                s = jnp.where(mask, s, NEG)
                m_cur = row_max(s)
            else:
                m_cur = cm_ref[slot, hd]
            m_old = m_ref[hd]
            m_new = jnp.maximum(m_old, m_cur)
            alpha = jnp.exp2(m_old - m_new)
            p = jnp.exp2(s - _lane_tile(m_new, tk // 128)).astype(BF16)
            acc_ref[hd] = (_lane_tile(alpha, MLA_QK_PAD // 128) * acc_ref[hd]
                           + _dot(p, v_ref[0, hd, pl.ds(start, tk), :]))
            m_ref[hd] = m_new

    n_full = jnp.right_shift(qi, (tk // tq).bit_length() - 1)
    n_pairs = jnp.right_shift(n_full, 1)
    scores(0, 0)

    def body(jj, carry):
        c = 2 * jj
        scores(c + 1, 1)
        update(c, 0, False)
        scores(c + 2, 0)
        update(c + 1, 1, False)
        return carry

    lax.fori_loop(0, n_pairs, body, 0)
    c = 2 * n_pairs

    @pl.when(n_full > c)
    def _():
        scores(c + 1, 1)
        update(c, 0, False)
        update(c + 1, 1, True)

    @pl.when(n_full == c)
    def _():
        update(c, 0, True)

    for hd in range(g):
        acc = acc_ref[hd]
        o_ref[0, :, hd * MLA_V:(hd + 1) * MLA_V] = (acc[:, :MLA_V] / acc[:, MLA_V:]).astype(BF16)


def _mla(q, k, v, tq, tk, g):
    B, _, S, _ = q.shape
    assert tk % tq == 0 and (tk // tq) & (tk // tq - 1) == 0 and S % tk == 0
    return pl.pallas_call(
        functools.partial(_mla_kernel, tq=tq, tk=tk, g=g),
        grid=(B, MLA_HEADS // g, S // tq),
        in_specs=[
            pl.BlockSpec((1, g, tq, MLA_QK_PAD), lambda b, h, i: (b, h, i, 0)),
            pl.BlockSpec((1, g, S, MLA_QK_PAD), lambda b, h, i: (b, h, 0, 0)),
            pl.BlockSpec((1, g, S, MLA_QK_PAD), lambda b, h, i: (b, h, 0, 0)),
        ],
        out_specs=pl.BlockSpec((1, tq, g * MLA_V), lambda b, h, i: (b, i, h)),
        out_shape=jax.ShapeDtypeStruct((B, S, MLA_WIDTH), BF16),
        scratch_shapes=[
            pltpu.VMEM((g, tq, MLA_QK_PAD), F32),
            pltpu.VMEM((g, tq, 128), F32),
            pltpu.VMEM((2, g, tq, tk), F32),
            pltpu.VMEM((2, g, tq, 128), F32),
        ],
        compiler_params=_params(("arbitrary", "arbitrary", "arbitrary")),
        name="mla",
    )(q, k, v)


_REL_SPAN = 4 * SWA_BLOCK


_N_OFFSETS = 3 * SWA_BLOCK - 1


def _bias_kernel(table_ref, bucket_ref, o_ref):
    head = lax.broadcasted_iota(jnp.int32, (SWA_HEADS, 128), 0)
    lane = lax.broadcasted_iota(jnp.int32, (SWA_HEADS, _REL_SPAN), 1)

    def lookup(c, per_offset):
        b = bucket_ref[c]
        col = jnp.zeros((SWA_HEADS, 128), F32)
        for hd in range(SWA_HEADS):
            col = jnp.where(head == hd, table_ref[b, hd], col)
        return jnp.where(lane == c, _lane_tile(col, _REL_SPAN // 128), per_offset)

    per_offset_all = lax.fori_loop(0, _N_OFFSETS, lookup, jnp.zeros((SWA_HEADS, _REL_SPAN), F32))
    for hd in range(SWA_HEADS):
        per_offset = per_offset_all[hd:hd + 1, :]
        rows = jnp.broadcast_to(per_offset, (SWA_BLOCK, _REL_SPAN))
        band = pltpu.roll(rows, _REL_SPAN - (SWA_BLOCK - 1), 1, stride=1, stride_axis=0)
        kvh, within = divmod(hd, SWA_GROUP)
        par, half = within % 2, within // 2
        o_ref[kvh, par, half * SWA_BLOCK:(half + 1) * SWA_BLOCK, :] = band[:, :2 * SWA_BLOCK]


def _bias(rel_table, bucket):
    return pl.pallas_call(
        _bias_kernel,
        in_specs=[
            pl.BlockSpec(memory_space=pltpu.SMEM),
            pl.BlockSpec(memory_space=pltpu.SMEM),
        ],
        out_specs=pl.BlockSpec(memory_space=pltpu.VMEM),
        out_shape=jax.ShapeDtypeStruct((SWA_KV_HEADS, 2, 2 * SWA_BLOCK, 2 * SWA_BLOCK), F32),
        name="t5bias",
    )(rel_table, bucket)


def _memkv_kernel(mem_ref, g_ref, w_ref, k_ref, v_ref):
    mn = _rms(mem_ref[0], g_ref[...]).astype(BF16)
    kv = _dot(mn, w_ref[...])
    k_ref[0] = kv[:, :MEM_WIDTH].astype(BF16)
    v_ref[0] = kv[:, MEM_WIDTH:].astype(BF16)


def _memkv(mem, g, w):
    B, M, _ = mem.shape
    const = lambda b: (0, 0)
    row = lambda b: (b, 0, 0)
    return pl.pallas_call(
        _memkv_kernel,
        grid=(B,),
        in_specs=[
            pl.BlockSpec((1, M, D_MODEL), row),
            pl.BlockSpec((1, D_MODEL), const),
            pl.BlockSpec((D_MODEL, 2 * MEM_WIDTH), const),
        ],
        out_specs=[pl.BlockSpec((1, M, MEM_WIDTH), row)] * 2,
        out_shape=[jax.ShapeDtypeStruct((B, M, MEM_WIDTH), BF16)] * 2,
        compiler_params=_params(("arbitrary",)),
        name="memkv",
    )(mem, g, w)


def _swa_into(y_ref, sink_ref, q_ref, kp_ref, kc_ref, vp_ref, vc_ref, bias_ref, kb_ref, vb_ref, nsub,
              after_unit):
    t = pl.program_id(1)
    scale = SWA_HEAD_DIM ** -0.5
    sb = SWA_BLOCK
    kb_ref[:sb] = kp_ref[0]
    kb_ref[sb:] = kc_ref[0]
    vb_ref[:sb] = vp_ref[0]
    vb_ref[sb:] = vc_ref[0]

    row = lax.broadcasted_iota(jnp.int32, (2 * sb, 2 * sb), 0)
    q_chunk = _chunk_of(row & (sb - 1))
    b_chunk = _chunk_of(lax.broadcasted_iota(jnp.int32, (2 * sb, 2 * sb), 1))
    valid_any = jnp.logical_and(b_chunk >= q_chunk, b_chunk <= q_chunk + WINDOW_CHUNKS)
    first_lo = jnp.where(t > 0, 0, 2)
    valid_first = jnp.logical_and(valid_any, b_chunk >= first_lo)
    upper = lax.broadcasted_iota(jnp.int32, (2 * sb, 1), 0) < sb

    for r in range(nsub):
        valid = valid_first if r == 0 else valid_any
        rows = slice(r * sb, (r + 1) * sb)
        band = slice(r * sb, (r + 2) * sb)
        for kvh in range(SWA_KV_HEADS):
            pair0 = slice(2 * kvh * 128, (2 * kvh + 1) * 128)
            pair1 = slice((2 * kvh + 1) * 128, (2 * kvh + 2) * 128)
            q = jnp.concatenate([q_ref[0, rows, pair0], q_ref[0, rows, pair1]], axis=0)
            o = None
            for par in range(2):
                kcols = slice((2 * kvh + par) * 128, (2 * kvh + par + 1) * 128)
                hd = SWA_GROUP * kvh + par
                s = _dot_nt(q, kb_ref[band, kcols]) * scale + bias_ref[kvh, par]
                s = jnp.where(valid, s, NEG)
                sink = jnp.where(upper, sink_ref[hd], sink_ref[hd + 2])
                m = jnp.maximum(jnp.max(s, axis=-1, keepdims=True), sink)
                p = jnp.exp(s - m)
                den = jnp.sum(p, axis=-1, keepdims=True) + jnp.exp(sink - m)
                pv = _dot((p * (1.0 / den)).astype(BF16), vb_ref[band, kcols])
                o = pv if o is None else o + pv
            y_ref[rows, pair0] = o[:sb]
            y_ref[rows, pair1] = o[sb:]
            after_unit(r * SWA_KV_HEADS + kvh)


def _mem_into(y_ref, q_ref, k_ref, v_ref, after_head):
    scale = MEM_HEAD_DIM ** -0.5
    for hd in range(MEM_HEADS):
        cols = slice(hd * MEM_HEAD_DIM, (hd + 1) * MEM_HEAD_DIM)
        s = _dot_nt(q_ref[0, :, cols], k_ref[0, :, cols]) * scale
        m = jnp.max(s, axis=-1, keepdims=True)
        p = jnp.exp(s - m)
        inv = 1.0 / jnp.sum(p, axis=-1, keepdims=True)
        y_ref[:, cols] = _dot((p * inv).astype(BF16), v_ref[0, :, cols])
        after_head(hd)


def _tail_kernel(sink_ref, x_ref, ya_ref, za_ref, zb_ref, zc_ref, qs_ref, qm_ref,
                 kp_ref, kc_ref, vp_ref, vc_ref, bias_ref, km_ref, vm_ref, w_ref, g_ref,
                 o_ref, kb_ref, vb_ref, yb_ref, yc_ref, yacc_ref, *, nsub):
    def gated(y, z_ref):
        z = z_ref[0].astype(F32)
        return (y * (z * jax.nn.sigmoid(z))).astype(BF16)

    swa_lo, mem_lo = MLA_WIDTH, MLA_WIDTH + SWA_WIDTH

    g_mla = gated(ya_ref[0].astype(F32), za_ref)
    mla_chunk = D_MODEL // (nsub * SWA_KV_HEADS)

    def project_mla(u):
        cols = slice(u * mla_chunk, (u + 1) * mla_chunk)
        yacc_ref[:, cols] = _dot(g_mla, w_ref[:swa_lo, cols])

    _swa_into(yb_ref, sink_ref, qs_ref, kp_ref, kc_ref, vp_ref, vc_ref, bias_ref, kb_ref, vb_ref, nsub,
              project_mla)

    g_swa = gated(yb_ref[...], zb_ref)
    swa_chunk = D_MODEL // MEM_HEADS

    def project_swa(u):
        cols = slice(u * swa_chunk, (u + 1) * swa_chunk)
        yacc_ref[:, cols] += _dot(g_swa, w_ref[swa_lo:mem_lo, cols])

    _mem_into(yc_ref, qm_ref, km_ref, vm_ref, project_swa)

    g_mem = gated(yc_ref[...], zc_ref)
    half = g_mem.shape[0] // 2
    ys = [yacc_ref[rows, :] + _dot(g_mem[rows], w_ref[mem_lo:, :])
          for rows in (slice(0, half), slice(half, 2 * half))]
    o_ref[0, :half, :] = _rms(x_ref[0, :half, :] + ys[0], g_ref[...])
    o_ref[0, half:, :] = _rms(x_ref[0, half:, :] + ys[1], g_ref[...])


def _tail(sinks, x, y_mla, z, qs, ks, vs, bias, kmem, vmem, w, g, nsub):
    B, S, _ = x.shape
    tm = nsub * SWA_BLOCK
    M = kmem.shape[1]
    row = lambda b, i: (b, i, 0)
    prev = lambda b, i: (b, jnp.maximum(i * nsub - 1, 0), 0)
    col = lambda c: (lambda b, i: (b, i, c))
    return pl.pallas_call(
        functools.partial(_tail_kernel, nsub=nsub),
        grid=(B, S // tm),
        in_specs=[
            pl.BlockSpec(memory_space=pltpu.SMEM),
            pl.BlockSpec((1, tm, D_MODEL), row),
            pl.BlockSpec((1, tm, MLA_WIDTH), row),
            pl.BlockSpec((1, tm, MLA_WIDTH), col(0)),
            pl.BlockSpec((1, tm, SWA_WIDTH), col(MLA_WIDTH // SWA_WIDTH)),
            pl.BlockSpec((1, tm, MEM_WIDTH), col((MLA_WIDTH + SWA_WIDTH) // MEM_WIDTH)),
            pl.BlockSpec((1, tm, SWA_WIDTH), col(0)),
            pl.BlockSpec((1, tm, MEM_WIDTH), col(1)),
            pl.BlockSpec((1, SWA_BLOCK, _SWA_KV_PAD), prev),
            pl.BlockSpec((1, tm, _SWA_KV_PAD), row),
            pl.BlockSpec((1, SWA_BLOCK, _SWA_KV_PAD), prev),
            pl.BlockSpec((1, tm, _SWA_KV_PAD), row),
            pl.BlockSpec((SWA_KV_HEADS, 2, 2 * SWA_BLOCK, 2 * SWA_BLOCK), lambda b, i: (0, 0, 0, 0)),
            pl.BlockSpec((1, M, MEM_WIDTH), lambda b, i: (b, 0, 0)),
            pl.BlockSpec((1, M, MEM_WIDTH), lambda b, i: (b, 0, 0)),
            pl.BlockSpec((MIX_WIDTH, D_MODEL), lambda b, i: (0, 0), pipeline_mode=pl.Buffered(1)),
            pl.BlockSpec((1, D_MODEL), lambda b, i: (0, 0)),
        ],
        out_specs=pl.BlockSpec((1, tm, D_MODEL), row),
        out_shape=jax.ShapeDtypeStruct((B, S, D_MODEL), F32),
        scratch_shapes=[
            pltpu.VMEM((tm + SWA_BLOCK, _SWA_KV_PAD), BF16),
            pltpu.VMEM((tm + SWA_BLOCK, _SWA_KV_PAD), BF16),
            pltpu.VMEM((tm, SWA_WIDTH), F32),
            pltpu.VMEM((tm, MEM_WIDTH), F32),
            pltpu.VMEM((tm, D_MODEL), F32),
        ],
        compiler_params=_params(("arbitrary", "arbitrary")),
        name="tail",
    )(sinks, x, y_mla, z, z, z, qs, qs, ks, ks, vs, vs, bias, kmem, vmem, w, g)


def _rope_tables(seq):
    inv = 1.0 / (ROPE_THETA ** (jnp.arange(0, MLA_ROPE, 2, dtype=F32) / MLA_ROPE))
    ang = jnp.arange(seq, dtype=F32)[:, None] * inv[None, :]
    cos, sin = jnp.cos(ang), jnp.sin(ang)
    zero = jnp.zeros_like(cos)
    return (jnp.concatenate([cos, cos, zero, zero], axis=-1),
            jnp.concatenate([-sin, sin, zero, zero], axis=-1))


def _t5_bucket(rel):
    nb = N_BUCKETS // 2
    max_exact = nb // 2
    bucket = jnp.where(rel > 0, nb, 0)
    n = jnp.abs(rel)
    nf = jnp.maximum(n, 1).astype(F32)
    large = max_exact + (jnp.log(nf / max_exact) / math.log(MAX_DISTANCE / max_exact)
                         * (nb - max_exact)).astype(jnp.int32)
    large = jnp.minimum(large, nb - 1)
    return bucket + jnp.where(n < max_exact, n, large)


def kernel(x, mem, norm_in, w_in, norm_q, norm_kv, w_uq, w_ukv, attn_sinks, rel_bias,
           norm_mem, w_mem_kv, w_out, norm_final):
    B, S, _ = x.shape
    assert norm_in.shape[0] == 1, "single-layer trunk"

    wt = w_in[0].T.astype(BF16)
    wq = w_uq[0].reshape(MLA_QLORA, MLA_HEADS, MLA_NOPE + MLA_ROPE)
    wq_pe = wq[..., MLA_NOPE:]
    wuq = jnp.concatenate([wq, wq_pe], axis=-1)
    wuq = wuq.reshape(MLA_QLORA, MLA_HEADS * MLA_QK_PAD).astype(BF16)
    wukv = w_ukv[0].astype(BF16)

    cos_t, sin_t = _rope_tables(S)
    q, k, v, qs, z, ks, vs = _proj(
        x, norm_in, wt, norm_q, norm_kv, wuq, wukv, cos_t, sin_t, tm=256)

    y_mla = _mla(q, k, v, tq=512, tk=512, g=4)

    bucket = _t5_bucket(jnp.arange(_REL_SPAN) - (2 * SWA_BLOCK - 1)).astype(jnp.int32)
    bias = _bias(rel_bias, bucket)
    kmem, vmem = _memkv(mem, norm_mem, w_mem_kv[0].astype(BF16))
    return _tail(attn_sinks[0], x, y_mla, z, qs, ks, vs, bias, kmem, vmem,
                 w_out[0].astype(BF16), norm_final[None, :], nsub=4)
```

```python
import functools
import math

import jax
import jax.numpy as jnp
from jax import lax
from jax.experimental import pallas as pl
from jax.experimental.pallas import tpu as pltpu

D_MODEL = 2048
CHUNK = 64
N_MEM = 256
EPS = 1e-6
NEG = -1e30

MLA_HEADS = 8
MLA_NOPE = 128
MLA_ROPE = 64
MLA_V = 128
MLA_QLORA = 512
MLA_KVLORA = 256
MLA_WIDTH = MLA_HEADS * MLA_V
MLA_QK_PAD = 256
ROPE_THETA = 10000.0

SWA_HEADS = 8
SWA_KV_HEADS = 2
SWA_GROUP = SWA_HEADS // SWA_KV_HEADS
SWA_HEAD_DIM = 64
SWA_WIDTH = SWA_HEADS * SWA_HEAD_DIM
SWA_KV_WIDTH = SWA_KV_HEADS * SWA_HEAD_DIM
WINDOW_CHUNKS = 2
SWA_BLOCK = 128

MEM_HEADS = 4
MEM_HEAD_DIM = 128
MEM_WIDTH = MEM_HEADS * MEM_HEAD_DIM

MIX_WIDTH = MLA_WIDTH + SWA_WIDTH + MEM_WIDTH

N_BUCKETS = 32
MAX_DISTANCE = 128

BF16 = jnp.bfloat16
F32 = jnp.float32

_R_CQ = 0
_R_CKV = _R_CQ + MLA_QLORA
_R_KPE = _R_CKV + MLA_KVLORA
_R_ZMLA = _R_KPE + MLA_ROPE
_R_QSWA = _R_ZMLA + MLA_WIDTH
_R_KSWA = _R_QSWA + SWA_WIDTH
_R_VSWA = _R_KSWA + SWA_KV_WIDTH
_R_ZSWA = _R_VSWA + SWA_KV_WIDTH
_R_QMEM = _R_ZSWA + SWA_WIDTH
_R_ZMEM = _R_QMEM + MEM_WIDTH
IN_WIDTH = _R_ZMEM + MEM_WIDTH

_MLA_Q_SCALE = (MLA_NOPE + MLA_ROPE) ** -0.5 * math.log2(math.e)

_SWA_KV_PAD = 2 * SWA_KV_HEADS * 128

_VMEM_LIMIT = 56 * 1024 * 1024


def _params(sem, flags=None):
    return pltpu.CompilerParams(dimension_semantics=sem, vmem_limit_bytes=_VMEM_LIMIT, flags=flags)


def _rms(v, g):
    return (v * lax.rsqrt(jnp.mean(v * v, axis=-1, keepdims=True) + EPS)) * g


def _dot(a, b):
    return jnp.dot(a, b, preferred_element_type=F32)


def _dot_nt(a, b):
    return lax.dot_general(a, b, (((1,), (1,)), ((), ())), preferred_element_type=F32)


def _lane_tile(v, n):
    return jnp.concatenate([v] * n, axis=1)


def _chunk_of(pos):
    return jnp.right_shift(pos, CHUNK.bit_length() - 1)


def _rope128(v, cos_t, sin_t):
    return v * cos_t + pltpu.roll(v, MLA_ROPE // 2, 1) * sin_t


def _store_swa_kv(ref, kv):
    low = lax.broadcasted_iota(jnp.int32, kv.shape, 1) < SWA_HEAD_DIM
    swapped = pltpu.roll(kv, SWA_HEAD_DIM, 1)
    pieces = (jnp.where(low, kv, 0.0), jnp.where(low, 0.0, swapped),
              jnp.where(low, swapped, 0.0), jnp.where(low, 0.0, kv))
    for i, piece in enumerate(pieces):
        ref[0, :, i * 128:(i + 1) * 128] = piece.astype(BF16)


def _proj_kernel(x_ref, gin_ref, wt_ref, gq_ref, gkv_ref, wuq_ref, wukv_ref, cos_ref, sin_ref,
                 q_ref, k_ref, v_ref, qs_ref, z_ref, ks_ref, vs_ref):
    h = _rms(x_ref[0], gin_ref[...]).astype(BF16)
    cos_t = cos_ref[...]
    sin_t = sin_ref[...]

    def proj(lo, hi):
        return _dot_nt(h, wt_ref[lo:hi, :])

    pa = proj(_R_CQ, _R_KPE)
    cq = _rms(pa[:, :MLA_QLORA], gq_ref[...]).astype(BF16)
    ckv = _rms(pa[:, MLA_QLORA:], gkv_ref[...]).astype(BF16)

    qs_ref[0, :, :SWA_WIDTH] = proj(_R_QSWA, _R_KSWA).astype(BF16)
    qs_ref[0, :, SWA_WIDTH:] = proj(_R_QMEM, _R_ZMEM).astype(BF16)
    z_ref[0, :, :MLA_WIDTH] = proj(_R_ZMLA, _R_QSWA).astype(BF16)
    z_ref[0, :, MLA_WIDTH:MLA_WIDTH + SWA_WIDTH] = proj(_R_ZSWA, _R_QMEM).astype(BF16)
    z_ref[0, :, MLA_WIDTH + SWA_WIDTH:] = proj(_R_ZMEM, IN_WIDTH).astype(BF16)
    kvs = proj(_R_KSWA, _R_ZSWA)
    _store_swa_kv(ks_ref, kvs[:, :SWA_KV_WIDTH])
    _store_swa_kv(vs_ref, kvs[:, SWA_KV_WIDTH:])
    pe = proj(_R_KPE, _R_ZMLA)
    kpe = _rope128(jnp.concatenate([pe, pe], axis=1), cos_t, sin_t).astype(BF16)

    qall = _dot(cq, wuq_ref[...]) * _MLA_Q_SCALE
    kv = _dot(ckv, wukv_ref[...])
    ones = jnp.ones((kv.shape[0], MLA_V), BF16)
    for hd in range(MLA_HEADS):
        o = hd * MLA_QK_PAD
        q_ref[0, hd, :, :MLA_NOPE] = qall[:, o:o + MLA_NOPE].astype(BF16)
        q_ref[0, hd, :, MLA_NOPE:] = _rope128(
            qall[:, o + MLA_NOPE:o + MLA_QK_PAD], cos_t, sin_t).astype(BF16)
        k_ref[0, hd, :, :MLA_NOPE] = kv[:, o:o + MLA_NOPE].astype(BF16)
        k_ref[0, hd, :, MLA_NOPE:] = kpe
        v_ref[0, hd, :, :MLA_V] = kv[:, o + MLA_NOPE:o + MLA_QK_PAD].astype(BF16)
        v_ref[0, hd, :, MLA_V:] = ones


def _proj(x, gin, wt, gq, gkv, wuq, wukv, cos_t, sin_t, tm):
    B, S, _ = x.shape
    const = lambda b, i: (0, 0)
    row = lambda b, i: (b, i, 0)
    single = pl.Buffered(1)

    def out(width):
        return jax.ShapeDtypeStruct((B, S, width), BF16)

    head_spec = pl.BlockSpec((1, MLA_HEADS, tm, MLA_QK_PAD), lambda b, i: (b, 0, i, 0))
    head_shape = jax.ShapeDtypeStruct((B, MLA_HEADS, S, MLA_QK_PAD), BF16)

    return pl.pallas_call(
        _proj_kernel,
        grid=(B, S // tm),
        in_specs=[
            pl.BlockSpec((1, tm, D_MODEL), row),
            pl.BlockSpec((1, D_MODEL), const),
            pl.BlockSpec((IN_WIDTH, D_MODEL), const, pipeline_mode=single),
            pl.BlockSpec((1, MLA_QLORA), const),
            pl.BlockSpec((1, MLA_KVLORA), const),
            pl.BlockSpec((MLA_QLORA, MLA_HEADS * MLA_QK_PAD), const, pipeline_mode=single),
            pl.BlockSpec((MLA_KVLORA, MLA_HEADS * (MLA_NOPE + MLA_V)), const, pipeline_mode=single),
            pl.BlockSpec((tm, 128), lambda b, i: (i, 0)),
            pl.BlockSpec((tm, 128), lambda b, i: (i, 0)),
        ],
        out_specs=[
            head_spec, head_spec, head_spec,
            pl.BlockSpec((1, tm, SWA_WIDTH + MEM_WIDTH), row),
            pl.BlockSpec((1, tm, MIX_WIDTH), row),
            pl.BlockSpec((1, tm, _SWA_KV_PAD), row),
            pl.BlockSpec((1, tm, _SWA_KV_PAD), row),
        ],
        out_shape=[
            head_shape, head_shape, head_shape,
            out(SWA_WIDTH + MEM_WIDTH), out(MIX_WIDTH), out(_SWA_KV_PAD), out(_SWA_KV_PAD),
        ],
        compiler_params=_params(("arbitrary", "arbitrary")),
        name="proj",
    )(x, gin, wt, gq, gkv, wuq, wukv, cos_t, sin_t)


def _mla_kernel(q_ref, k_ref, v_ref, o_ref, acc_ref, m_ref, s_ref, cm_ref, *, tq, tk, g):
    qi = pl.program_id(2)
    acc_ref[...] = jnp.zeros(acc_ref.shape, F32)
    m_ref[...] = jnp.full(m_ref.shape, NEG, F32)

    def row_max(s):
        return jnp.broadcast_to(jnp.max(s, axis=-1, keepdims=True), (tq, 128))

    def scores(c, slot, hd):
        start = pl.multiple_of(c * tk, tk)
        s = _dot_nt(q_ref[0, hd], k_ref[0, hd, pl.ds(start, tk), :])
        s_ref[slot, hd] = s
        cm_ref[slot, hd] = row_max(s)

    def update(c, slot, hd, mask):
        start = pl.multiple_of(c * tk, tk)
        s = s_ref[slot, hd]
        if mask is not None:
            s = jnp.where(mask, s, NEG)
            m_cur = row_max(s)
        else:
            m_cur = cm_ref[slot, hd]
        m_old = m_ref[hd]
        m_new = jnp.maximum(m_old, m_cur)
        alpha = jnp.exp2(m_old - m_new)
        p = jnp.exp2(s - _lane_tile(m_new, tk // 128)).astype(BF16)
        acc_ref[hd] = (_lane_tile(alpha, MLA_QK_PAD // 128) * acc_ref[hd]
                       + _dot(p, v_ref[0, hd, pl.ds(start, tk), :]))
        m_ref[hd] = m_new

    def step(c, slot, masked=False, prefetch=True):
        mask = None
        if masked:
            q_chunk = _chunk_of(qi * tq + lax.broadcasted_iota(jnp.int32, (tq, tk), 0))
            k_chunk = _chunk_of(c * tk + lax.broadcasted_iota(jnp.int32, (tq, tk), 1))
            mask = k_chunk <= q_chunk
        for hd in range(g):
            if prefetch:
                scores(c + 1, 1 - slot, hd)
            update(c, slot, hd, mask)

    n_full = jnp.right_shift(qi, (tk // tq).bit_length() - 1)
    n_pairs = jnp.right_shift(n_full, 1)
    for hd in range(g):
        scores(0, 0, hd)

    def body(jj, carry):
        step(2 * jj, 0)
        step(2 * jj + 1, 1)
        return carry

    lax.fori_loop(0, n_pairs, body, 0)
    c = 2 * n_pairs

    @pl.when(n_full > c)
    def _():
        step(c, 0)
        step(c + 1, 1, masked=True, prefetch=False)

    @pl.when(n_full == c)
    def _():
        step(c, 0, masked=True, prefetch=False)

    for hd in range(g):
        acc = acc_ref[hd]
        o_ref[0, :, hd * MLA_V:(hd + 1) * MLA_V] = (acc[:, :MLA_V] / acc[:, MLA_V:]).astype(BF16)


def _mla(q, k, v, tq, tk, g):
    B, _, S, _ = q.shape
    assert tk % tq == 0 and (tk // tq) & (tk // tq - 1) == 0 and S % tk == 0
    return pl.pallas_call(
        functools.partial(_mla_kernel, tq=tq, tk=tk, g=g),
        grid=(B, MLA_HEADS // g, S // tq),
        in_specs=[
            pl.BlockSpec((1, g, tq, MLA_QK_PAD), lambda b, h, i: (b, h, i, 0)),
            pl.BlockSpec((1, g, S, MLA_QK_PAD), lambda b, h, i: (b, h, 0, 0)),
            pl.BlockSpec((1, g, S, MLA_QK_PAD), lambda b, h, i: (b, h, 0, 0)),
        ],
        out_specs=pl.BlockSpec((1, tq, g * MLA_V), lambda b, h, i: (b, i, h)),
        out_shape=jax.ShapeDtypeStruct((B, S, MLA_WIDTH), BF16),
        scratch_shapes=[
            pltpu.VMEM((g, tq, MLA_QK_PAD), F32),
            pltpu.VMEM((g, tq, 128), F32),
            pltpu.VMEM((2, g, tq, tk), F32),
            pltpu.VMEM((2, g, tq, 128), F32),
        ],
        compiler_params=_params(("arbitrary", "arbitrary", "arbitrary")),
        name="mla",
    )(q, k, v)


_REL_SPAN = 4 * SWA_BLOCK


_N_OFFSETS = 3 * SWA_BLOCK - 1


def _bias_kernel(table_ref, bucket_ref, o_ref):
    head = lax.broadcasted_iota(jnp.int32, (SWA_HEADS, 128), 0)
    lane = lax.broadcasted_iota(jnp.int32, (SWA_HEADS, _REL_SPAN), 1)

    def lookup(c, per_offset):
        b = bucket_ref[c]
        col = jnp.zeros((SWA_HEADS, 128), F32)
        for hd in range(SWA_HEADS):
            col = jnp.where(head == hd, table_ref[b, hd], col)
        return jnp.where(lane == c, _lane_tile(col, _REL_SPAN // 128), per_offset)

    per_offset_all = lax.fori_loop(0, _N_OFFSETS, lookup, jnp.zeros((SWA_HEADS, _REL_SPAN), F32))
    for hd in range(SWA_HEADS):
        per_offset = per_offset_all[hd:hd + 1, :]
        rows = jnp.broadcast_to(per_offset, (SWA_BLOCK, _REL_SPAN))
        band = pltpu.roll(rows, _REL_SPAN - (SWA_BLOCK - 1), 1, stride=1, stride_axis=0)
        kvh, within = divmod(hd, SWA_GROUP)
        par, half = within % 2, within // 2
        o_ref[kvh, par, half * SWA_BLOCK:(half + 1) * SWA_BLOCK, :] = band[:, :2 * SWA_BLOCK]


def _bias(rel_table, bucket):
    return pl.pallas_call(
        _bias_kernel,
        in_specs=[
            pl.BlockSpec(memory_space=pltpu.SMEM),
            pl.BlockSpec(memory_space=pltpu.SMEM),
        ],
        out_specs=pl.BlockSpec(memory_space=pltpu.VMEM),
        out_shape=jax.ShapeDtypeStruct((SWA_KV_HEADS, 2, 2 * SWA_BLOCK, 2 * SWA_BLOCK), F32),
        name="t5bias",
    )(rel_table, bucket)


def _memkv_kernel(mem_ref, g_ref, w_ref, k_ref, v_ref):
    mn = _rms(mem_ref[0], g_ref[...]).astype(BF16)
    kv = _dot(mn, w_ref[...])
    k_ref[0] = kv[:, :MEM_WIDTH].astype(BF16)
    v_ref[0] = kv[:, MEM_WIDTH:].astype(BF16)


def _memkv(mem, g, w):
    B, M, _ = mem.shape
    const = lambda b: (0, 0)
    row = lambda b: (b, 0, 0)
    return pl.pallas_call(
        _memkv_kernel,
        grid=(B,),
        in_specs=[
            pl.BlockSpec((1, M, D_MODEL), row),
            pl.BlockSpec((1, D_MODEL), const),
            pl.BlockSpec((D_MODEL, 2 * MEM_WIDTH), const),
        ],
        out_specs=[pl.BlockSpec((1, M, MEM_WIDTH), row)] * 2,
        out_shape=[jax.ShapeDtypeStruct((B, M, MEM_WIDTH), BF16)] * 2,
        compiler_params=_params(("arbitrary",)),
        name="memkv",
    )(mem, g, w)


def _swa_into(y_ref, sink_ref, q_ref, kp_ref, kc_ref, vp_ref, vc_ref, bias_ref, kb_ref, vb_ref, nsub,
              after_unit):
    t = pl.program_id(1)
    scale = SWA_HEAD_DIM ** -0.5
    sb = SWA_BLOCK
    kb_ref[:sb] = kp_ref[0]
    kb_ref[sb:] = kc_ref[0]
    vb_ref[:sb] = vp_ref[0]
    vb_ref[sb:] = vc_ref[0]

    row = lax.broadcasted_iota(jnp.int32, (2 * sb, 2 * sb), 0)
    q_chunk = _chunk_of(row & (sb - 1))
    b_chunk = _chunk_of(lax.broadcasted_iota(jnp.int32, (2 * sb, 2 * sb), 1))
    valid_any = jnp.logical_and(b_chunk >= q_chunk, b_chunk <= q_chunk + WINDOW_CHUNKS)
    first_lo = jnp.where(t > 0, 0, 2)
    valid_first = jnp.logical_and(valid_any, b_chunk >= first_lo)
    upper = lax.broadcasted_iota(jnp.int32, (2 * sb, 1), 0) < sb

    for r in range(nsub):
        valid = valid_first if r == 0 else valid_any
        rows = slice(r * sb, (r + 1) * sb)
        band = slice(r * sb, (r + 2) * sb)
        for kvh in range(SWA_KV_HEADS):
            pair0 = slice(2 * kvh * 128, (2 * kvh + 1) * 128)
            pair1 = slice((2 * kvh + 1) * 128, (2 * kvh + 2) * 128)
            q = jnp.concatenate([q_ref[0, rows, pair0], q_ref[0, rows, pair1]], axis=0)
            o = None
            for par in range(2):
                kcols = slice((2 * kvh + par) * 128, (2 * kvh + par + 1) * 128)
                hd = SWA_GROUP * kvh + par
                s = _dot_nt(q, kb_ref[band, kcols]) * scale + bias_ref[kvh, par]
                s = jnp.where(valid, s, NEG)
                sink = jnp.where(upper, sink_ref[hd], sink_ref[hd + 2])
                m = jnp.maximum(jnp.max(s, axis=-1, keepdims=True), sink)
                p = jnp.exp(s - m)
                den = jnp.sum(p, axis=-1, keepdims=True) + jnp.exp(sink - m)
                pv = _dot((p * (1.0 / den)).astype(BF16), vb_ref[band, kcols])
                o = pv if o is None else o + pv
            y_ref[rows, pair0] = o[:sb]
            y_ref[rows, pair1] = o[sb:]
            after_unit(r * SWA_KV_HEADS + kvh)


def _mem_into(y_ref, q_ref, k_ref, v_ref, after_head):
    scale = MEM_HEAD_DIM ** -0.5
    for hd in range(MEM_HEADS):
        cols = slice(hd * MEM_HEAD_DIM, (hd + 1) * MEM_HEAD_DIM)
        s = _dot_nt(q_ref[0, :, cols], k_ref[0, :, cols]) * scale
        m = jnp.max(s, axis=-1, keepdims=True)
        p = jnp.exp(s - m)
        inv = 1.0 / jnp.sum(p, axis=-1, keepdims=True)
        y_ref[:, cols] = _dot((p * inv).astype(BF16), v_ref[0, :, cols])
        after_head(hd)


def _tail_kernel(sink_ref, x_ref, ya_ref, za_ref, zb_ref, zc_ref, qs_ref, qm_ref,
                 kp_ref, kc_ref, vp_ref, vc_ref, bias_ref, km_ref, vm_ref, w_ref, g_ref,
                 o_ref, kb_ref, vb_ref, yb_ref, yc_ref, yacc_ref, *, nsub):
    def gated(y, z_ref):
        z = z_ref[0].astype(F32)
        return (y * (z * jax.nn.sigmoid(z))).astype(BF16)

    swa_lo, mem_lo = MLA_WIDTH, MLA_WIDTH + SWA_WIDTH

    g_mla = gated(ya_ref[0].astype(F32), za_ref)
    mla_chunk = D_MODEL // (nsub * SWA_KV_HEADS)

    def project_mla(u):
        cols = slice(u * mla_chunk, (u + 1) * mla_chunk)
        yacc_ref[:, cols] = _dot(g_mla, w_ref[:swa_lo, cols])

    _swa_into(yb_ref, sink_ref, qs_ref, kp_ref, kc_ref, vp_ref, vc_ref, bias_ref, kb_ref, vb_ref, nsub,
              project_mla)

    g_swa = gated(yb_ref[...], zb_ref)
    swa_chunk = D_MODEL // MEM_HEADS

    def project_swa(u):
        cols = slice(u * swa_chunk, (u + 1) * swa_chunk)
        yacc_ref[:, cols] += _dot(g_swa, w_ref[swa_lo:mem_lo, cols])

    _mem_into(yc_ref, qm_ref, km_ref, vm_ref, project_swa)

    g_mem = gated(yc_ref[...], zc_ref)
    half = g_mem.shape[0] // 2
    ys = [yacc_ref[rows, :] + _dot(g_mem[rows], w_ref[mem_lo:, :])
          for rows in (slice(0, half), slice(half, 2 * half))]
    o_ref[0, :half, :] = _rms(x_ref[0, :half, :] + ys[0], g_ref[...])
    o_ref[0, half:, :] = _rms(x_ref[0, half:, :] + ys[1], g_ref[...])


def _tail(sinks, x, y_mla, z, qs, ks, vs, bias, kmem, vmem, w, g, nsub):
    B, S, _ = x.shape
    tm = nsub * SWA_BLOCK
    M = kmem.shape[1]
    row = lambda b, i: (b, i, 0)
    prev = lambda b, i: (b, jnp.maximum(i * nsub - 1, 0), 0)
    col = lambda c: (lambda b, i: (b, i, c))
    return pl.pallas_call(
        functools.partial(_tail_kernel, nsub=nsub),
        grid=(B, S // tm),
        in_specs=[
            pl.BlockSpec(memory_space=pltpu.SMEM),
            pl.BlockSpec((1, tm, D_MODEL), row),
            pl.BlockSpec((1, tm, MLA_WIDTH), row),
            pl.BlockSpec((1, tm, MLA_WIDTH), col(0)),
            pl.BlockSpec((1, tm, SWA_WIDTH), col(MLA_WIDTH // SWA_WIDTH)),
            pl.BlockSpec((1, tm, MEM_WIDTH), col((MLA_WIDTH + SWA_WIDTH) // MEM_WIDTH)),
            pl.BlockSpec((1, tm, SWA_WIDTH), col(0)),
            pl.BlockSpec((1, tm, MEM_WIDTH), col(1)),
            pl.BlockSpec((1, SWA_BLOCK, _SWA_KV_PAD), prev),
            pl.BlockSpec((1, tm, _SWA_KV_PAD), row),
            pl.BlockSpec((1, SWA_BLOCK, _SWA_KV_PAD), prev),
            pl.BlockSpec((1, tm, _SWA_KV_PAD), row),
            pl.BlockSpec((SWA_KV_HEADS, 2, 2 * SWA_BLOCK, 2 * SWA_BLOCK), lambda b, i: (0, 0, 0, 0)),
            pl.BlockSpec((1, M, MEM_WIDTH), lambda b, i: (b, 0, 0)),
            pl.BlockSpec((1, M, MEM_WIDTH), lambda b, i: (b, 0, 0)),
            pl.BlockSpec((MIX_WIDTH, D_MODEL), lambda b, i: (0, 0), pipeline_mode=pl.Buffered(1)),
            pl.BlockSpec((1, D_MODEL), lambda b, i: (0, 0)),
        ],
        out_specs=pl.BlockSpec((1, tm, D_MODEL), row),
        out_shape=jax.ShapeDtypeStruct((B, S, D_MODEL), F32),
        scratch_shapes=[
            pltpu.VMEM((tm + SWA_BLOCK, _SWA_KV_PAD), BF16),
            pltpu.VMEM((tm + SWA_BLOCK, _SWA_KV_PAD), BF16),
            pltpu.VMEM((tm, SWA_WIDTH), F32),
            pltpu.VMEM((tm, MEM_WIDTH), F32),
            pltpu.VMEM((tm, D_MODEL), F32),
        ],
        compiler_params=_params(("arbitrary", "arbitrary")),
        name="tail",
    )(sinks, x, y_mla, z, z, z, qs, qs, ks, ks, vs, vs, bias, kmem, vmem, w, g)


def _rope_tables(seq):
    inv = 1.0 / (ROPE_THETA ** (jnp.arange(0, MLA_ROPE, 2, dtype=F32) / MLA_ROPE))
    ang = jnp.arange(seq, dtype=F32)[:, None] * inv[None, :]
    cos, sin = jnp.cos(ang), jnp.sin(ang)
    zero = jnp.zeros_like(cos)
    return (jnp.concatenate([cos, cos, zero, zero], axis=-1),
            jnp.concatenate([-sin, sin, zero, zero], axis=-1))


def _t5_bucket(rel):
    nb = N_BUCKETS // 2
    max_exact = nb // 2
    bucket = jnp.where(rel > 0, nb, 0)
    n = jnp.abs(rel)
    nf = jnp.maximum(n, 1).astype(F32)
    large = max_exact + (jnp.log(nf / max_exact) / math.log(MAX_DISTANCE / max_exact)
                         * (nb - max_exact)).astype(jnp.int32)
    large = jnp.minimum(large, nb - 1)
    return bucket + jnp.where(n < max_exact, n, large)


def kernel(x, mem, norm_in, w_in, norm_q, norm_kv, w_uq, w_ukv, attn_sinks, rel_bias,
           norm_mem, w_mem_kv, w_out, norm_final):
    B, S, _ = x.shape
    assert norm_in.shape[0] == 1, "single-layer trunk"

    wt = w_in[0].T.astype(BF16)
    wq = w_uq[0].reshape(MLA_QLORA, MLA_HEADS, MLA_NOPE + MLA_ROPE)
    wq_pe = wq[..., MLA_NOPE:]
    wuq = jnp.concatenate([wq, wq_pe], axis=-1)
    wuq = wuq.reshape(MLA_QLORA, MLA_HEADS * MLA_QK_PAD).astype(BF16)
    wukv = w_ukv[0].astype(BF16)

    cos_t, sin_t = _rope_tables(S)
    q, k, v, qs, z, ks, vs = _proj(
        x, norm_in, wt, norm_q, norm_kv, wuq, wukv, cos_t, sin_t, tm=256)

    y_mla = _mla(q, k, v, tq=512, tk=512, g=4)

    bucket = _t5_bucket(jnp.arange(_REL_SPAN) - (2 * SWA_BLOCK - 1)).astype(jnp.int32)
    bias = _bias(rel_bias, bucket)
    kmem, vmem = _memkv(mem, norm_mem, w_mem_kv[0].astype(BF16))
    return _tail(attn_sinks[0], x, y_mla, z, qs, ks, vs, bias, kmem, vmem,
                 w_out[0].astype(BF16), norm_final[None, :], nsub=4)
```

```python
import functools
import math

import jax
import jax.numpy as jnp
from jax import lax
from jax.experimental import pallas as pl
from jax.experimental.pallas import tpu as pltpu

D_MODEL = 2048
CHUNK = 64
N_MEM = 256
EPS = 1e-6
NEG = -1e30

MLA_HEADS = 8
MLA_NOPE = 128
MLA_ROPE = 64
MLA_V = 128
MLA_QLORA = 512
MLA_KVLORA = 256
MLA_WIDTH = MLA_HEADS * MLA_V
MLA_QK_PAD = 256
ROPE_THETA = 10000.0

SWA_HEADS = 8
SWA_KV_HEADS = 2
SWA_GROUP = SWA_HEADS // SWA_KV_HEADS
SWA_HEAD_DIM = 64
SWA_WIDTH = SWA_HEADS * SWA_HEAD_DIM
SWA_KV_WIDTH = SWA_KV_HEADS * SWA_HEAD_DIM
WINDOW_CHUNKS = 2
SWA_BLOCK = 128

MEM_HEADS = 4
MEM_HEAD_DIM = 128
MEM_WIDTH = MEM_HEADS * MEM_HEAD_DIM

MIX_WIDTH = MLA_WIDTH + SWA_WIDTH + MEM_WIDTH

N_BUCKETS = 32
MAX_DISTANCE = 128

BF16 = jnp.bfloat16
F32 = jnp.float32

_R_CQ = 0
_R_CKV = _R_CQ + MLA_QLORA
_R_KPE = _R_CKV + MLA_KVLORA
_R_ZMLA = _R_KPE + MLA_ROPE
_R_QSWA = _R_ZMLA + MLA_WIDTH
_R_KSWA = _R_QSWA + SWA_WIDTH
_R_VSWA = _R_KSWA + SWA_KV_WIDTH
_R_ZSWA = _R_VSWA + SWA_KV_WIDTH
_R_QMEM = _R_ZSWA + SWA_WIDTH
_R_ZMEM = _R_QMEM + MEM_WIDTH
IN_WIDTH = _R_ZMEM + MEM_WIDTH

_MLA_Q_SCALE = (MLA_NOPE + MLA_ROPE) ** -0.5 * math.log2(math.e)

_SWA_KV_PAD = 2 * SWA_KV_HEADS * 128

_VMEM_LIMIT = 56 * 1024 * 1024


def _params(sem, flags=None):
    return pltpu.CompilerParams(dimension_semantics=sem, vmem_limit_bytes=_VMEM_LIMIT, flags=flags)


def _rms(v, g):
    return (v * lax.rsqrt(jnp.mean(v * v, axis=-1, keepdims=True) + EPS)) * g


def _dot(a, b):
    return jnp.dot(a, b, preferred_element_type=F32)


def _dot_nt(a, b):
    return lax.dot_general(a, b, (((1,), (1,)), ((), ())), preferred_element_type=F32)


def _lane_tile(v, n):
    return jnp.concatenate([v] * n, axis=1)


def _chunk_of(pos):
    return jnp.right_shift(pos, CHUNK.bit_length() - 1)


def _rope128(v, cos_t, sin_t):
    return v * cos_t + pltpu.roll(v, MLA_ROPE // 2, 1) * sin_t


def _store_swa_kv(ref, kv):
    low = lax.broadcasted_iota(jnp.int32, kv.shape, 1) < SWA_HEAD_DIM
    swapped = pltpu.roll(kv, SWA_HEAD_DIM, 1)
    pieces = (jnp.where(low, kv, 0.0), jnp.where(low, 0.0, swapped),
              jnp.where(low, swapped, 0.0), jnp.where(low, 0.0, kv))
    for i, piece in enumerate(pieces):
        ref[0, :, i * 128:(i + 1) * 128] = piece.astype(BF16)


def _proj_kernel(x_ref, gin_ref, wt_ref, gq_ref, gkv_ref, wuq_ref, wukv_ref, cos_ref, sin_ref,
                 q_ref, k_ref, v_ref, qs_ref, z_ref, ks_ref, vs_ref):
    h = _rms(x_ref[0], gin_ref[...]).astype(BF16)
    cos_t = cos_ref[...]
    sin_t = sin_ref[...]

    def proj(lo, hi):
        return _dot_nt(h, wt_ref[lo:hi, :])

    pa = proj(_R_CQ, _R_KPE)
    cq = _rms(pa[:, :MLA_QLORA], gq_ref[...]).astype(BF16)
    ckv = _rms(pa[:, MLA_QLORA:], gkv_ref[...]).astype(BF16)

    qs_ref[0, :, :SWA_WIDTH] = proj(_R_QSWA, _R_KSWA).astype(BF16)
    qs_ref[0, :, SWA_WIDTH:] = proj(_R_QMEM, _R_ZMEM).astype(BF16)
    z_ref[0, :, :MLA_WIDTH] = proj(_R_ZMLA, _R_QSWA).astype(BF16)
    z_ref[0, :, MLA_WIDTH:MLA_WIDTH + SWA_WIDTH] = proj(_R_ZSWA, _R_QMEM).astype(BF16)
    z_ref[0, :, MLA_WIDTH + SWA_WIDTH:] = proj(_R_ZMEM, IN_WIDTH).astype(BF16)
    kvs = proj(_R_KSWA, _R_ZSWA)
    _store_swa_kv(ks_ref, kvs[:, :SWA_KV_WIDTH])
    _store_swa_kv(vs_ref, kvs[:, SWA_KV_WIDTH:])
    pe = proj(_R_KPE, _R_ZMLA)
    kpe = _rope128(jnp.concatenate([pe, pe], axis=1), cos_t, sin_t).astype(BF16)

    qall = _dot(cq, wuq_ref[...]) * _MLA_Q_SCALE
    kv = _dot(ckv, wukv_ref[...])
    for hd in range(MLA_HEADS):
        o = hd * MLA_QK_PAD
        q_ref[0, hd, :, :MLA_NOPE] = qall[:, o:o + MLA_NOPE].astype(BF16)
        q_ref[0, hd, :, MLA_NOPE:] = _rope128(
            qall[:, o + MLA_NOPE:o + MLA_QK_PAD], cos_t, sin_t).astype(BF16)
        k_ref[0, hd, :, :MLA_NOPE] = kv[:, o:o + MLA_NOPE].astype(BF16)
        k_ref[0, hd, :, MLA_NOPE:] = kpe
        v_ref[0, hd] = kv[:, o + MLA_NOPE:o + MLA_QK_PAD].astype(BF16)


def _proj(x, gin, wt, gq, gkv, wuq, wukv, cos_t, sin_t, tm):
    B, S, _ = x.shape
    const = lambda b, i: (0, 0)
    row = lambda b, i: (b, i, 0)
    single = pl.Buffered(1)

    def out(width):
        return jax.ShapeDtypeStruct((B, S, width), BF16)

    head_spec = pl.BlockSpec((1, MLA_HEADS, tm, MLA_QK_PAD), lambda b, i: (b, 0, i, 0))
    head_shape = jax.ShapeDtypeStruct((B, MLA_HEADS, S, MLA_QK_PAD), BF16)

    return pl.pallas_call(
        _proj_kernel,
        grid=(B, S // tm),
        in_specs=[
            pl.BlockSpec((1, tm, D_MODEL), row),
            pl.BlockSpec((1, D_MODEL), const),
            pl.BlockSpec((IN_WIDTH, D_MODEL), const, pipeline_mode=single),
            pl.BlockSpec((1, MLA_QLORA), const),
            pl.BlockSpec((1, MLA_KVLORA), const),
            pl.BlockSpec((MLA_QLORA, MLA_HEADS * MLA_QK_PAD), const, pipeline_mode=single),
            pl.BlockSpec((MLA_KVLORA, MLA_HEADS * (MLA_NOPE + MLA_V)), const, pipeline_mode=single),
            pl.BlockSpec((tm, 128), lambda b, i: (i, 0)),
            pl.BlockSpec((tm, 128), lambda b, i: (i, 0)),
        ],
        out_specs=[
            head_spec, head_spec,
            pl.BlockSpec((1, MLA_HEADS, tm, MLA_V), lambda b, i: (b, 0, i, 0)),
            pl.BlockSpec((1, tm, SWA_WIDTH + MEM_WIDTH), row),
            pl.BlockSpec((1, tm, MIX_WIDTH), row),
            pl.BlockSpec((1, tm, _SWA_KV_PAD), row),
            pl.BlockSpec((1, tm, _SWA_KV_PAD), row),
        ],
        out_shape=[
            head_shape, head_shape, jax.ShapeDtypeStruct((B, MLA_HEADS, S, MLA_V), BF16),
            out(SWA_WIDTH + MEM_WIDTH), out(MIX_WIDTH), out(_SWA_KV_PAD), out(_SWA_KV_PAD),
        ],
        compiler_params=_params(("arbitrary", "arbitrary")),
        name="proj",
    )(x, gin, wt, gq, gkv, wuq, wukv, cos_t, sin_t)


def _mla_kernel(q_ref, k_ref, v_ref, o_ref, acc_ref, m_ref, s_ref, cm_ref, *, t, g, nq):
    qi0 = pl.program_id(2) * nq
    qi_end = qi0 + nq - 1
    acc_ref[...] = jnp.zeros(acc_ref.shape, F32)
    m_ref[...] = jnp.full(m_ref.shape, NEG, F32)
    ones = jnp.ones((t, MLA_V), BF16)

    def row_max(s):
        return jnp.broadcast_to(jnp.max(s, axis=-1, keepdims=True), (t, 128))

    def scores(qi, c, slot, hd):
        rows = pl.multiple_of((qi - qi0) * t, t)
        keys = pl.multiple_of(c * t, t)
        s = _dot_nt(q_ref[0, hd, pl.ds(rows, t), :], k_ref[0, hd, pl.ds(keys, t), :])
        s_ref[slot, hd] = s
        cm_ref[slot, hd] = row_max(s)

    def update(c, slot, hd, mask):
        keys = pl.multiple_of(c * t, t)
        s = s_ref[slot, hd]
        if mask is not None:
            s = jnp.where(mask, s, NEG)
            m_cur = row_max(s)
        else:
            m_cur = cm_ref[slot, hd]
        m_old = m_ref[hd]
        m_new = jnp.maximum(m_old, m_cur)
        alpha = jnp.exp2(m_old - m_new)
        p = jnp.exp2(s - _lane_tile(m_new, t // 128)).astype(BF16)
        v_ones = jnp.concatenate([v_ref[0, hd, pl.ds(keys, t), :], ones], axis=1)
        acc_ref[hd] = _lane_tile(alpha, MLA_QK_PAD // 128) * acc_ref[hd] + _dot(p, v_ones)
        m_ref[hd] = m_new

    def finish_tile(qi, hd):
        rows = pl.multiple_of((qi - qi0) * t, t)
        acc = acc_ref[hd]
        o_ref[0, pl.ds(rows, t), hd * MLA_V:(hd + 1) * MLA_V] = (
            acc[:, :MLA_V] / acc[:, MLA_V:]).astype(BF16)
        acc_ref[hd] = jnp.zeros(acc.shape, F32)
        m_ref[hd] = jnp.full((t, 128), NEG, F32)

    def following(qi, c):
        is_last = c == qi
        return jnp.where(is_last, jnp.minimum(qi + 1, qi_end), qi), jnp.where(is_last, 0, c + 1)

    def step(qi, c, slot, diagonal):
        qn, cn = following(qi, c)
        mask = None
        if diagonal:
            q_chunk = _chunk_of(qi * t + lax.broadcasted_iota(jnp.int32, (t, t), 0))
            k_chunk = _chunk_of(c * t + lax.broadcasted_iota(jnp.int32, (t, t), 1))
            mask = k_chunk <= q_chunk
        for hd in range(g):
            scores(qn, cn, 1 - slot, hd)
            update(c, slot, hd, mask)
            if diagonal:
                finish_tile(qi, hd)

    for hd in range(g):
        scores(qi0, 0, 0, hd)

    def body(_, pair):
        first, second = pair, following(*pair)
        diag0 = first[1] == first[0]
        diag1 = second[1] == second[0]
        for d0, d1, cond in ((True, False, diag0), (False, True, diag1),
                             (False, False, jnp.logical_not(jnp.logical_or(diag0, diag1)))):
            @pl.when(cond)
            def _(d0=d0, d1=d1):
                step(*first, 0, d0)
                step(*second, 1, d1)
        return following(*second)

    n_steps = nq * qi0 + nq * (nq + 1) // 2
    lax.fori_loop(0, n_steps // 2, body, (qi0, 0))


def _mla(q, k, v, t, g, nq):
    B, _, S, _ = q.shape
    assert S % (nq * t) == 0 and nq % 2 == 0 and (nq * (nq + 1) // 2) % 2 == 0
    return pl.pallas_call(
        functools.partial(_mla_kernel, t=t, g=g, nq=nq),
        grid=(B, MLA_HEADS // g, S // (nq * t)),
        in_specs=[
            pl.BlockSpec((1, g, nq * t, MLA_QK_PAD), lambda b, h, i: (b, h, i, 0)),
            pl.BlockSpec((1, g, S, MLA_QK_PAD), lambda b, h, i: (b, h, 0, 0)),
            pl.BlockSpec((1, g, S, MLA_V), lambda b, h, i: (b, h, 0, 0)),
        ],
        out_specs=pl.BlockSpec((1, nq * t, g * MLA_V), lambda b, h, i: (b, i, h)),
        out_shape=jax.ShapeDtypeStruct((B, S, MLA_WIDTH), BF16),
        scratch_shapes=[
            pltpu.VMEM((g, t, MLA_QK_PAD), F32),
            pltpu.VMEM((g, t, 128), F32),
            pltpu.VMEM((2, g, t, t), F32),
            pltpu.VMEM((2, g, t, 128), F32),
        ],
        compiler_params=_params(("arbitrary", "arbitrary", "arbitrary")),
        name="mla",
    )(q, k, v)


_REL_SPAN = 4 * SWA_BLOCK


_N_OFFSETS = 3 * SWA_BLOCK - 1


def _bias_kernel(table_ref, bucket_ref, o_ref):
    head = lax.broadcasted_iota(jnp.int32, (SWA_HEADS, 128), 0)
    lane = lax.broadcasted_iota(jnp.int32, (SWA_HEADS, _REL_SPAN), 1)

    def lookup(c, per_offset):
        b = bucket_ref[c]
        col = jnp.zeros((SWA_HEADS, 128), F32)
        for hd in range(SWA_HEADS):
            col = jnp.where(head == hd, table_ref[b, hd], col)
        return jnp.where(lane == c, _lane_tile(col, _REL_SPAN // 128), per_offset)

    per_offset_all = lax.fori_loop(0, _N_OFFSETS, lookup, jnp.zeros((SWA_HEADS, _REL_SPAN), F32))
    for hd in range(SWA_HEADS):
        per_offset = per_offset_all[hd:hd + 1, :]
        rows = jnp.broadcast_to(per_offset, (SWA_BLOCK, _REL_SPAN))
        band = pltpu.roll(rows, _REL_SPAN - (SWA_BLOCK - 1), 1, stride=1, stride_axis=0)
        kvh, within = divmod(hd, SWA_GROUP)
        par, half = within % 2, within // 2
        o_ref[kvh, par, half * SWA_BLOCK:(half + 1) * SWA_BLOCK, :] = band[:, :2 * SWA_BLOCK]


def _bias(rel_table, bucket):
    return pl.pallas_call(
        _bias_kernel,
        in_specs=[
            pl.BlockSpec(memory_space=pltpu.SMEM),
            pl.BlockSpec(memory_space=pltpu.SMEM),
        ],
        out_specs=pl.BlockSpec(memory_space=pltpu.VMEM),
        out_shape=jax.ShapeDtypeStruct((SWA_KV_HEADS, 2, 2 * SWA_BLOCK, 2 * SWA_BLOCK), F32),
        name="t5bias",
    )(rel_table, bucket)


def _memkv_kernel(mem_ref, g_ref, w_ref, k_ref, v_ref):
    mn = _rms(mem_ref[0], g_ref[...]).astype(BF16)
    kv = _dot(mn, w_ref[...])
    k_ref[0] = kv[:, :MEM_WIDTH].astype(BF16)
    v_ref[0] = kv[:, MEM_WIDTH:].astype(BF16)


def _memkv(mem, g, w):
    B, M, _ = mem.shape
    const = lambda b: (0, 0)
    row = lambda b: (b, 0, 0)
    return pl.pallas_call(
        _memkv_kernel,
        grid=(B,),
        in_specs=[
            pl.BlockSpec((1, M, D_MODEL), row),
            pl.BlockSpec((1, D_MODEL), const),
            pl.BlockSpec((D_MODEL, 2 * MEM_WIDTH), const),
        ],
        out_specs=[pl.BlockSpec((1, M, MEM_WIDTH), row)] * 2,
        out_shape=[jax.ShapeDtypeStruct((B, M, MEM_WIDTH), BF16)] * 2,
        compiler_params=_params(("arbitrary",)),
        name="memkv",
    )(mem, g, w)


def _swa_into(y_ref, sink_ref, q_ref, kp_ref, kc_ref, vp_ref, vc_ref, bias_ref, kb_ref, vb_ref, nsub,
              after_unit):
    t = pl.program_id(1)
    scale = SWA_HEAD_DIM ** -0.5
    sb = SWA_BLOCK
    kb_ref[:sb] = kp_ref[0]
    kb_ref[sb:] = kc_ref[0]
    vb_ref[:sb] = vp_ref[0]
    vb_ref[sb:] = vc_ref[0]

    row = lax.broadcasted_iota(jnp.int32, (2 * sb, 2 * sb), 0)
    q_chunk = _chunk_of(row & (sb - 1))
    b_chunk = _chunk_of(lax.broadcasted_iota(jnp.int32, (2 * sb, 2 * sb), 1))
    valid_any = jnp.logical_and(b_chunk >= q_chunk, b_chunk <= q_chunk + WINDOW_CHUNKS)
    first_lo = jnp.where(t > 0, 0, 2)
    valid_first = jnp.logical_and(valid_any, b_chunk >= first_lo)
    upper = lax.broadcasted_iota(jnp.int32, (2 * sb, 1), 0) < sb

    for r in range(nsub):
        valid = valid_first if r == 0 else valid_any
        rows = slice(r * sb, (r + 1) * sb)
        band = slice(r * sb, (r + 2) * sb)
        for kvh in range(SWA_KV_HEADS):
            pair0 = slice(2 * kvh * 128, (2 * kvh + 1) * 128)
            pair1 = slice((2 * kvh + 1) * 128, (2 * kvh + 2) * 128)
            q = jnp.concatenate([q_ref[0, rows, pair0], q_ref[0, rows, pair1]], axis=0)
            o = None
            for par in range(2):
                kcols = slice((2 * kvh + par) * 128, (2 * kvh + par + 1) * 128)
                hd = SWA_GROUP * kvh + par
                s = _dot_nt(q, kb_ref[band, kcols]) * scale + bias_ref[kvh, par]
                s = jnp.where(valid, s, NEG)
                sink = jnp.where(upper, sink_ref[hd], sink_ref[hd + 2])
                m = jnp.maximum(jnp.max(s, axis=-1, keepdims=True), sink)
                p = jnp.exp(s - m)
                den = jnp.sum(p, axis=-1, keepdims=True) + jnp.exp(sink - m)
                pv = _dot((p * (1.0 / den)).astype(BF16), vb_ref[band, kcols])
                o = pv if o is None else o + pv
            y_ref[rows, pair0] = o[:sb]
            y_ref[rows, pair1] = o[sb:]
            after_unit(r * SWA_KV_HEADS + kvh)


def _mem_into(y_ref, q_ref, k_ref, v_ref, after_head):
    scale = MEM_HEAD_DIM ** -0.5
    for hd in range(MEM_HEADS):
        cols = slice(hd * MEM_HEAD_DIM, (hd + 1) * MEM_HEAD_DIM)
        s = _dot_nt(q_ref[0, :, cols], k_ref[0, :, cols]) * scale
        m = jnp.max(s, axis=-1, keepdims=True)
        p = jnp.exp(s - m)
        inv = 1.0 / jnp.sum(p, axis=-1, keepdims=True)
        y_ref[:, cols] = _dot((p * inv).astype(BF16), v_ref[0, :, cols])
        after_head(hd)


def _tail_kernel(sink_ref, x_ref, ya_ref, za_ref, zb_ref, zc_ref, qs_ref, qm_ref,
                 kp_ref, kc_ref, vp_ref, vc_ref, bias_ref, km_ref, vm_ref, w_ref, g_ref,
                 o_ref, kb_ref, vb_ref, yb_ref, yc_ref, yacc_ref, *, nsub):
    def gated(y, z_ref):
        z = z_ref[0].astype(F32)
        return (y * (z * jax.nn.sigmoid(z))).astype(BF16)

    swa_lo, mem_lo = MLA_WIDTH, MLA_WIDTH + SWA_WIDTH

    g_mla = gated(ya_ref[0].astype(F32), za_ref)
    mla_chunk = D_MODEL // (nsub * SWA_KV_HEADS)

    def project_mla(u):
        cols = slice(u * mla_chunk, (u + 1) * mla_chunk)
        yacc_ref[:, cols] = _dot(g_mla, w_ref[:swa_lo, cols])

    _swa_into(yb_ref, sink_ref, qs_ref, kp_ref, kc_ref, vp_ref, vc_ref, bias_ref, kb_ref, vb_ref, nsub,
              project_mla)

    g_swa = gated(yb_ref[...], zb_ref)
    swa_chunk = D_MODEL // MEM_HEADS

    def project_swa(u):
        cols = slice(u * swa_chunk, (u + 1) * swa_chunk)
        yacc_ref[:, cols] += _dot(g_swa, w_ref[swa_lo:mem_lo, cols])

    _mem_into(yc_ref, qm_ref, km_ref, vm_ref, project_swa)

    g_mem = gated(yc_ref[...], zc_ref)
    half = g_mem.shape[0] // 2
    ys = [yacc_ref[rows, :] + _dot(g_mem[rows], w_ref[mem_lo:, :])
          for rows in (slice(0, half), slice(half, 2 * half))]
    o_ref[0, :half, :] = _rms(x_ref[0, :half, :] + ys[0], g_ref[...])
    o_ref[0, half:, :] = _rms(x_ref[0, half:, :] + ys[1], g_ref[...])


def _tail(sinks, x, y_mla, z, qs, ks, vs, bias, kmem, vmem, w, g, nsub):
    B, S, _ = x.shape
    tm = nsub * SWA_BLOCK
    M = kmem.shape[1]
    row = lambda b, i: (b, i, 0)
    prev = lambda b, i: (b, jnp.maximum(i * nsub - 1, 0), 0)
    col = lambda c: (lambda b, i: (b, i, c))
    return pl.pallas_call(
        functools.partial(_tail_kernel, nsub=nsub),
        grid=(B, S // tm),
        in_specs=[
            pl.BlockSpec(memory_space=pltpu.SMEM),
            pl.BlockSpec((1, tm, D_MODEL), row),
            pl.BlockSpec((1, tm, MLA_WIDTH), row),
            pl.BlockSpec((1, tm, MLA_WIDTH), col(0)),
            pl.BlockSpec((1, tm, SWA_WIDTH), col(MLA_WIDTH // SWA_WIDTH)),
            pl.BlockSpec((1, tm, MEM_WIDTH), col((MLA_WIDTH + SWA_WIDTH) // MEM_WIDTH)),
            pl.BlockSpec((1, tm, SWA_WIDTH), col(0)),
            pl.BlockSpec((1, tm, MEM_WIDTH), col(1)),
            pl.BlockSpec((1, SWA_BLOCK, _SWA_KV_PAD), prev),
            pl.BlockSpec((1, tm, _SWA_KV_PAD), row),
            pl.BlockSpec((1, SWA_BLOCK, _SWA_KV_PAD), prev),
            pl.BlockSpec((1, tm, _SWA_KV_PAD), row),
            pl.BlockSpec((SWA_KV_HEADS, 2, 2 * SWA_BLOCK, 2 * SWA_BLOCK), lambda b, i: (0, 0, 0, 0)),
            pl.BlockSpec((1, M, MEM_WIDTH), lambda b, i: (b, 0, 0)),
            pl.BlockSpec((1, M, MEM_WIDTH), lambda b, i: (b, 0, 0)),
            pl.BlockSpec((MIX_WIDTH, D_MODEL), lambda b, i: (0, 0), pipeline_mode=pl.Buffered(1)),
            pl.BlockSpec((1, D_MODEL), lambda b, i: (0, 0)),
        ],
        out_specs=pl.BlockSpec((1, tm, D_MODEL), row),
        out_shape=jax.ShapeDtypeStruct((B, S, D_MODEL), F32),
        scratch_shapes=[
            pltpu.VMEM((tm + SWA_BLOCK, _SWA_KV_PAD), BF16),
            pltpu.VMEM((tm + SWA_BLOCK, _SWA_KV_PAD), BF16),
            pltpu.VMEM((tm, SWA_WIDTH), F32),
            pltpu.VMEM((tm, MEM_WIDTH), F32),
            pltpu.VMEM((tm, D_MODEL), F32),
        ],
        compiler_params=_params(("arbitrary", "arbitrary")),
        name="tail",
    )(sinks, x, y_mla, z, z, z, qs, qs, ks, ks, vs, vs, bias, kmem, vmem, w, g)


def _rope_tables(seq):
    inv = 1.0 / (ROPE_THETA ** (jnp.arange(0, MLA_ROPE, 2, dtype=F32) / MLA_ROPE))
    ang = jnp.arange(seq, dtype=F32)[:, None] * inv[None, :]
    cos, sin = jnp.cos(ang), jnp.sin(ang)
    zero = jnp.zeros_like(cos)
    return (jnp.concatenate([cos, cos, zero, zero], axis=-1),
            jnp.concatenate([-sin, sin, zero, zero], axis=-1))


def _t5_bucket(rel):
    nb = N_BUCKETS // 2
    max_exact = nb // 2
    bucket = jnp.where(rel > 0, nb, 0)
    n = jnp.abs(rel)
    nf = jnp.maximum(n, 1).astype(F32)
    large = max_exact + (jnp.log(nf / max_exact) / math.log(MAX_DISTANCE / max_exact)
                         * (nb - max_exact)).astype(jnp.int32)
    large = jnp.minimum(large, nb - 1)
    return bucket + jnp.where(n < max_exact, n, large)


def kernel(x, mem, norm_in, w_in, norm_q, norm_kv, w_uq, w_ukv, attn_sinks, rel_bias,
           norm_mem, w_mem_kv, w_out, norm_final):
    B, S, _ = x.shape
    assert norm_in.shape[0] == 1, "single-layer trunk"

    wt = w_in[0].T.astype(BF16)
    wq = w_uq[0].reshape(MLA_QLORA, MLA_HEADS, MLA_NOPE + MLA_ROPE)
    wq_pe = wq[..., MLA_NOPE:]
    wuq = jnp.concatenate([wq, wq_pe], axis=-1)
    wuq = wuq.reshape(MLA_QLORA, MLA_HEADS * MLA_QK_PAD).astype(BF16)
    wukv = w_ukv[0].astype(BF16)

    cos_t, sin_t = _rope_tables(S)
    q, k, v, qs, z, ks, vs = _proj(
        x, norm_in, wt, norm_q, norm_kv, wuq, wukv, cos_t, sin_t, tm=256)

    y_mla = _mla(q, k, v, t=512, g=4, nq=4)

    bucket = _t5_bucket(jnp.arange(_REL_SPAN) - (2 * SWA_BLOCK - 1)).astype(jnp.int32)
    bias = _bias(rel_bias, bucket)
    kmem, vmem = _memkv(mem, norm_mem, w_mem_kv[0].astype(BF16))
    return _tail(attn_sinks[0], x, y_mla, z, qs, ks, vs, bias, kmem, vmem,
                 w_out[0].astype(BF16), norm_final[None, :], nsub=4)
```

```python
import functools
import math

import jax
import jax.numpy as jnp
from jax import lax
from jax.experimental import pallas as pl
from jax.experimental.pallas import tpu as pltpu

D_MODEL = 2048
CHUNK = 64
N_MEM = 256
EPS = 1e-6
NEG = -1e30

MLA_HEADS = 8
MLA_NOPE = 128
MLA_ROPE = 64
MLA_V = 128
MLA_QLORA = 512
MLA_KVLORA = 256
MLA_WIDTH = MLA_HEADS * MLA_V
MLA_QK_PAD = 256
ROPE_THETA = 10000.0

SWA_HEADS = 8
SWA_KV_HEADS = 2
SWA_GROUP = SWA_HEADS // SWA_KV_HEADS
SWA_HEAD_DIM = 64
SWA_WIDTH = SWA_HEADS * SWA_HEAD_DIM
SWA_KV_WIDTH = SWA_KV_HEADS * SWA_HEAD_DIM
WINDOW_CHUNKS = 2
SWA_BLOCK = 128

MEM_HEADS = 4
MEM_HEAD_DIM = 128
MEM_WIDTH = MEM_HEADS * MEM_HEAD_DIM

MIX_WIDTH = MLA_WIDTH + SWA_WIDTH + MEM_WIDTH

N_BUCKETS = 32
MAX_DISTANCE = 128

BF16 = jnp.bfloat16
F32 = jnp.float32

_R_CQ = 0
_R_CKV = _R_CQ + MLA_QLORA
_R_KPE = _R_CKV + MLA_KVLORA
_R_ZMLA = _R_KPE + MLA_ROPE
_R_QSWA = _R_ZMLA + MLA_WIDTH
_R_KSWA = _R_QSWA + SWA_WIDTH
_R_VSWA = _R_KSWA + SWA_KV_WIDTH
_R_ZSWA = _R_VSWA + SWA_KV_WIDTH
_R_QMEM = _R_ZSWA + SWA_WIDTH
_R_ZMEM = _R_QMEM + MEM_WIDTH
IN_WIDTH = _R_ZMEM + MEM_WIDTH

_MLA_Q_SCALE = (MLA_NOPE + MLA_ROPE) ** -0.5 * math.log2(math.e)

_SWA_KV_PAD = 2 * SWA_KV_HEADS * 128

_W_IN_CHUNK = 208
_W_OUT_CHUNK = 256

_VMEM_LIMIT = 56 * 1024 * 1024


def _params(sem, flags=None):
    return pltpu.CompilerParams(dimension_semantics=sem, vmem_limit_bytes=_VMEM_LIMIT, flags=flags)


def _rms(v, g):
    return (v * lax.rsqrt(jnp.mean(v * v, axis=-1, keepdims=True) + EPS)) * g


def _dot(a, b):
    return jnp.dot(a, b, preferred_element_type=F32)


def _dot_nt(a, b):
    return lax.dot_general(a, b, (((1,), (1,)), ((), ())), preferred_element_type=F32)


def _load_weight_bf16(w_hbm, w_ref, stage_ref, sem_ref):
    rows = stage_ref.shape[1]
    n_chunks = w_ref.shape[0] // rows

    def chunk_copy(c):
        return pltpu.make_async_copy(w_hbm.at[pl.ds(c * rows, rows), :], stage_ref.at[c % 2],
                                     sem_ref.at[c % 2])

    chunk_copy(0).start()
    for c in range(n_chunks):
        if c + 1 < n_chunks:
            chunk_copy(c + 1).start()
        chunk_copy(c).wait()
        w_ref[c * rows:(c + 1) * rows, :] = stage_ref[c % 2].astype(BF16)


def _first_grid_step():
    return jnp.logical_and(pl.program_id(0) == 0, pl.program_id(1) == 0)


def _lane_tile(v, n):
    return jnp.concatenate([v] * n, axis=1)


def _chunk_of(pos):
    return jnp.right_shift(pos, CHUNK.bit_length() - 1)


def _rope128(v, cos_t, sin_t):
    return v * cos_t + pltpu.roll(v, MLA_ROPE // 2, 1) * sin_t


def _store_swa_kv(ref, kv):
    low = lax.broadcasted_iota(jnp.int32, kv.shape, 1) < SWA_HEAD_DIM
    swapped = pltpu.roll(kv, SWA_HEAD_DIM, 1)
    pieces = (jnp.where(low, kv, 0.0), jnp.where(low, 0.0, swapped),
              jnp.where(low, swapped, 0.0), jnp.where(low, 0.0, kv))
    for i, piece in enumerate(pieces):
        ref[0, :, i * 128:(i + 1) * 128] = piece.astype(BF16)


def _proj_kernel(x_ref, gin_ref, wt_hbm, gq_ref, gkv_ref, wuq_ref, wukv_ref, cos_ref, sin_ref,
                 q_ref, k_ref, v_ref, qs_ref, z_ref, ks_ref, vs_ref, wt_ref, stage_ref, sem_ref):
    @pl.when(_first_grid_step())
    def _():
        _load_weight_bf16(wt_hbm, wt_ref, stage_ref, sem_ref)

    h = _rms(x_ref[0], gin_ref[...]).astype(BF16)
    cos_t = cos_ref[...]
    sin_t = sin_ref[...]

    def proj(lo, hi):
        return _dot_nt(h, wt_ref[lo:hi, :])

    pa = proj(_R_CQ, _R_KPE)
    cq = _rms(pa[:, :MLA_QLORA], gq_ref[...]).astype(BF16)
    ckv = _rms(pa[:, MLA_QLORA:], gkv_ref[...]).astype(BF16)

    qs_ref[0, :, :SWA_WIDTH] = proj(_R_QSWA, _R_KSWA).astype(BF16)
    qs_ref[0, :, SWA_WIDTH:] = proj(_R_QMEM, _R_ZMEM).astype(BF16)
    z_ref[0, :, :MLA_WIDTH] = proj(_R_ZMLA, _R_QSWA).astype(BF16)
    z_ref[0, :, MLA_WIDTH:MLA_WIDTH + SWA_WIDTH] = proj(_R_ZSWA, _R_QMEM).astype(BF16)
    z_ref[0, :, MLA_WIDTH + SWA_WIDTH:] = proj(_R_ZMEM, IN_WIDTH).astype(BF16)
    kvs = proj(_R_KSWA, _R_ZSWA)
    _store_swa_kv(ks_ref, kvs[:, :SWA_KV_WIDTH])
    _store_swa_kv(vs_ref, kvs[:, SWA_KV_WIDTH:])
    pe = proj(_R_KPE, _R_ZMLA)
    kpe = _rope128(jnp.concatenate([pe, pe], axis=1), cos_t, sin_t).astype(BF16)

    qall = _dot(cq, wuq_ref[...]) * _MLA_Q_SCALE
    kv = _dot(ckv, wukv_ref[...])
    for hd in range(MLA_HEADS):
        o = hd * MLA_QK_PAD
        q_ref[0, hd, :, :MLA_NOPE] = qall[:, o:o + MLA_NOPE].astype(BF16)
        q_ref[0, hd, :, MLA_NOPE:] = _rope128(
            qall[:, o + MLA_NOPE:o + MLA_QK_PAD], cos_t, sin_t).astype(BF16)
        k_ref[0, hd, :, :MLA_NOPE] = kv[:, o:o + MLA_NOPE].astype(BF16)
        k_ref[0, hd, :, MLA_NOPE:] = kpe
        v_ref[0, hd] = kv[:, o + MLA_NOPE:o + MLA_QK_PAD].astype(BF16)


def _proj(x, gin, wt, gq, gkv, wuq, wukv, cos_t, sin_t, tm):
    B, S, _ = x.shape
    const = lambda b, i: (0, 0)
    row = lambda b, i: (b, i, 0)
    single = pl.Buffered(1)

    def out(width):
        return jax.ShapeDtypeStruct((B, S, width), BF16)

    head_spec = pl.BlockSpec((1, MLA_HEADS, tm, MLA_QK_PAD), lambda b, i: (b, 0, i, 0))
    head_shape = jax.ShapeDtypeStruct((B, MLA_HEADS, S, MLA_QK_PAD), BF16)

    return pl.pallas_call(
        _proj_kernel,
        grid=(B, S // tm),
        in_specs=[
            pl.BlockSpec((1, tm, D_MODEL), row),
            pl.BlockSpec((1, D_MODEL), const),
            pl.BlockSpec(memory_space=pl.ANY),
            pl.BlockSpec((1, MLA_QLORA), const),
            pl.BlockSpec((1, MLA_KVLORA), const),
            pl.BlockSpec((MLA_QLORA, MLA_HEADS * MLA_QK_PAD), const, pipeline_mode=single),
            pl.BlockSpec((MLA_KVLORA, MLA_HEADS * (MLA_NOPE + MLA_V)), const, pipeline_mode=single),
            pl.BlockSpec((tm, 128), lambda b, i: (i, 0)),
            pl.BlockSpec((tm, 128), lambda b, i: (i, 0)),
        ],
        out_specs=[
            head_spec, head_spec,
            pl.BlockSpec((1, MLA_HEADS, tm, MLA_V), lambda b, i: (b, 0, i, 0)),
            pl.BlockSpec((1, tm, SWA_WIDTH + MEM_WIDTH), row),
            pl.BlockSpec((1, tm, MIX_WIDTH), row),
            pl.BlockSpec((1, tm, _SWA_KV_PAD), row),
            pl.BlockSpec((1, tm, _SWA_KV_PAD), row),
        ],
        out_shape=[
            head_shape, head_shape, jax.ShapeDtypeStruct((B, MLA_HEADS, S, MLA_V), BF16),
            out(SWA_WIDTH + MEM_WIDTH), out(MIX_WIDTH), out(_SWA_KV_PAD), out(_SWA_KV_PAD),
        ],
        scratch_shapes=[
            pltpu.VMEM((IN_WIDTH, D_MODEL), BF16),
            pltpu.VMEM((2, _W_IN_CHUNK, D_MODEL), F32),
            pltpu.SemaphoreType.DMA((2,)),
        ],
        compiler_params=_params(("arbitrary", "arbitrary")),
        name="proj",
    )(x, gin, wt, gq, gkv, wuq, wukv, cos_t, sin_t)


def _mla_kernel(q_ref, k_ref, v_ref, o_ref, acc_ref, m_ref, s_ref, cm_ref, *, t, g, nq):
    qi0 = pl.program_id(2) * nq
    qi_end = qi0 + nq - 1
    acc_ref[...] = jnp.zeros(acc_ref.shape, F32)
    m_ref[...] = jnp.full(m_ref.shape, NEG, F32)
    ones = jnp.ones((t, MLA_V), BF16)

    def row_max(s):
        return jnp.broadcast_to(jnp.max(s, axis=-1, keepdims=True), (t, 128))

    def scores(qi, c, slot, hd):
        rows = pl.multiple_of((qi - qi0) * t, t)
        keys = pl.multiple_of(c * t, t)
        s = _dot_nt(q_ref[0, hd, pl.ds(rows, t), :], k_ref[0, hd, pl.ds(keys, t), :])
        s_ref[slot, hd] = s
        cm_ref[slot, hd] = row_max(s)

    def update(c, slot, hd, mask):
        keys = pl.multiple_of(c * t, t)
        s = s_ref[slot, hd]
        if mask is not None:
            s = jnp.where(mask, s, NEG)
            m_cur = row_max(s)
        else:
            m_cur = cm_ref[slot, hd]
        m_old = m_ref[hd]
        m_new = jnp.maximum(m_old, m_cur)
        alpha = jnp.exp2(m_old - m_new)
        p = jnp.exp2(s - _lane_tile(m_new, t // 128)).astype(BF16)
        v_ones = jnp.concatenate([v_ref[0, hd, pl.ds(keys, t), :], ones], axis=1)
        acc_ref[hd] = _lane_tile(alpha, MLA_QK_PAD // 128) * acc_ref[hd] + _dot(p, v_ones)
        m_ref[hd] = m_new

    def finish_tile(qi, hd):
        rows = pl.multiple_of((qi - qi0) * t, t)
        acc = acc_ref[hd]
        o_ref[0, pl.ds(rows, t), hd * MLA_V:(hd + 1) * MLA_V] = (
            acc[:, :MLA_V] / acc[:, MLA_V:]).astype(BF16)
        acc_ref[hd] = jnp.zeros(acc.shape, F32)
        m_ref[hd] = jnp.full((t, 128), NEG, F32)

    def following(qi, c):
        is_last = c == qi
        return jnp.where(is_last, jnp.minimum(qi + 1, qi_end), qi), jnp.where(is_last, 0, c + 1)

    def step(qi, c, slot, diagonal):
        qn, cn = following(qi, c)
        mask = None
        if diagonal:
            q_chunk = _chunk_of(qi * t + lax.broadcasted_iota(jnp.int32, (t, t), 0))
            k_chunk = _chunk_of(c * t + lax.broadcasted_iota(jnp.int32, (t, t), 1))
            mask = k_chunk <= q_chunk
        for hd in range(g):
            scores(qn, cn, 1 - slot, hd)
            update(c, slot, hd, mask)
            if diagonal:
                finish_tile(qi, hd)

    for hd in range(g):
        scores(qi0, 0, 0, hd)

    def body(_, pair):
        first, second = pair, following(*pair)
        diag0 = first[1] == first[0]
        diag1 = second[1] == second[0]
        for d0, d1, cond in ((True, False, diag0), (False, True, diag1),
                             (False, False, jnp.logical_not(jnp.logical_or(diag0, diag1)))):
            @pl.when(cond)
            def _(d0=d0, d1=d1):
                step(*first, 0, d0)
                step(*second, 1, d1)
        return following(*second)

    n_steps = nq * qi0 + nq * (nq + 1) // 2
    lax.fori_loop(0, n_steps // 2, body, (qi0, 0))


def _mla(q, k, v, t, g, nq):
    B, _, S, _ = q.shape
    assert S % (nq * t) == 0 and nq % 2 == 0 and (nq * (nq + 1) // 2) % 2 == 0
    return pl.pallas_call(
        functools.partial(_mla_kernel, t=t, g=g, nq=nq),
        grid=(B, MLA_HEADS // g, S // (nq * t)),
        in_specs=[
            pl.BlockSpec((1, g, nq * t, MLA_QK_PAD), lambda b, h, i: (b, h, i, 0)),
            pl.BlockSpec((1, g, S, MLA_QK_PAD), lambda b, h, i: (b, h, 0, 0)),
            pl.BlockSpec((1, g, S, MLA_V), lambda b, h, i: (b, h, 0, 0)),
        ],
        out_specs=pl.BlockSpec((1, nq * t, g * MLA_V), lambda b, h, i: (b, i, h)),
        out_shape=jax.ShapeDtypeStruct((B, S, MLA_WIDTH), BF16),
        scratch_shapes=[
            pltpu.VMEM((g, t, MLA_QK_PAD), F32),
            pltpu.VMEM((g, t, 128), F32),
            pltpu.VMEM((2, g, t, t), F32),
            pltpu.VMEM((2, g, t, 128), F32),
        ],
        compiler_params=_params(("arbitrary", "arbitrary", "arbitrary")),
        name="mla",
    )(q, k, v)


_REL_SPAN = 4 * SWA_BLOCK


_N_OFFSETS = 3 * SWA_BLOCK - 1


def _bias_kernel(table_ref, bucket_ref, o_ref):
    head = lax.broadcasted_iota(jnp.int32, (SWA_HEADS, 128), 0)
    lane = lax.broadcasted_iota(jnp.int32, (SWA_HEADS, _REL_SPAN), 1)

    def lookup(c, per_offset):
        b = bucket_ref[c]
        col = jnp.zeros((SWA_HEADS, 128), F32)
        for hd in range(SWA_HEADS):
            col = jnp.where(head == hd, table_ref[b, hd], col)
        return jnp.where(lane == c, _lane_tile(col, _REL_SPAN // 128), per_offset)

    per_offset_all = lax.fori_loop(0, _N_OFFSETS, lookup, jnp.zeros((SWA_HEADS, _REL_SPAN), F32))
    for hd in range(SWA_HEADS):
        per_offset = per_offset_all[hd:hd + 1, :]
        rows = jnp.broadcast_to(per_offset, (SWA_BLOCK, _REL_SPAN))
        band = pltpu.roll(rows, _REL_SPAN - (SWA_BLOCK - 1), 1, stride=1, stride_axis=0)
        kvh, within = divmod(hd, SWA_GROUP)
        par, half = within % 2, within // 2
        o_ref[kvh, par, half * SWA_BLOCK:(half + 1) * SWA_BLOCK, :] = band[:, :2 * SWA_BLOCK]


def _bias(rel_table, bucket):
    return pl.pallas_call(
        _bias_kernel,
        in_specs=[
            pl.BlockSpec(memory_space=pltpu.SMEM),
            pl.BlockSpec(memory_space=pltpu.SMEM),
        ],
        out_specs=pl.BlockSpec(memory_space=pltpu.VMEM),
        out_shape=jax.ShapeDtypeStruct((SWA_KV_HEADS, 2, 2 * SWA_BLOCK, 2 * SWA_BLOCK), F32),
        name="t5bias",
    )(rel_table, bucket)


def _memkv_kernel(mem_ref, g_ref, w_ref, k_ref, v_ref):
    mn = _rms(mem_ref[0], g_ref[...]).astype(BF16)
    kv = _dot(mn, w_ref[...])
    k_ref[0] = kv[:, :MEM_WIDTH].astype(BF16)
    v_ref[0] = kv[:, MEM_WIDTH:].astype(BF16)


def _memkv(mem, g, w):
    B, M, _ = mem.shape
    const = lambda b: (0, 0)
    row = lambda b: (b, 0, 0)
    return pl.pallas_call(
        _memkv_kernel,
        grid=(B,),
        in_specs=[
            pl.BlockSpec((1, M, D_MODEL), row),
            pl.BlockSpec((1, D_MODEL), const),
            pl.BlockSpec((D_MODEL, 2 * MEM_WIDTH), const),
        ],
        out_specs=[pl.BlockSpec((1, M, MEM_WIDTH), row)] * 2,
        out_shape=[jax.ShapeDtypeStruct((B, M, MEM_WIDTH), BF16)] * 2,
        compiler_params=_params(("arbitrary",)),
        name="memkv",
    )(mem, g, w)


def _swa_into(y_ref, sink_ref, q_ref, kp_ref, kc_ref, vp_ref, vc_ref, bias_ref, kb_ref, vb_ref, nsub,
              after_unit):
    t = pl.program_id(1)
    scale = SWA_HEAD_DIM ** -0.5
    sb = SWA_BLOCK
    kb_ref[:sb] = kp_ref[0]
    kb_ref[sb:] = kc_ref[0]
    vb_ref[:sb] = vp_ref[0]
    vb_ref[sb:] = vc_ref[0]

    row = lax.broadcasted_iota(jnp.int32, (2 * sb, 2 * sb), 0)
    q_chunk = _chunk_of(row & (sb - 1))
    b_chunk = _chunk_of(lax.broadcasted_iota(jnp.int32, (2 * sb, 2 * sb), 1))
    valid_any = jnp.logical_and(b_chunk >= q_chunk, b_chunk <= q_chunk + WINDOW_CHUNKS)
    first_lo = jnp.where(t > 0, 0, 2)
    valid_first = jnp.logical_and(valid_any, b_chunk >= first_lo)
    upper = lax.broadcasted_iota(jnp.int32, (2 * sb, 1), 0) < sb

    for r in range(nsub):
        valid = valid_first if r == 0 else valid_any
        rows = slice(r * sb, (r + 1) * sb)
        band = slice(r * sb, (r + 2) * sb)
        for kvh in range(SWA_KV_HEADS):
            pair0 = slice(2 * kvh * 128, (2 * kvh + 1) * 128)
            pair1 = slice((2 * kvh + 1) * 128, (2 * kvh + 2) * 128)
            q = jnp.concatenate([q_ref[0, rows, pair0], q_ref[0, rows, pair1]], axis=0)
            o = None
            for par in range(2):
                kcols = slice((2 * kvh + par) * 128, (2 * kvh + par + 1) * 128)
                hd = SWA_GROUP * kvh + par
                s = _dot_nt(q, kb_ref[band, kcols]) * scale + bias_ref[kvh, par]
                s = jnp.where(valid, s, NEG)
                sink = jnp.where(upper, sink_ref[hd], sink_ref[hd + 2])
                m = jnp.maximum(jnp.max(s, axis=-1, keepdims=True), sink)
                p = jnp.exp(s - m)
                den = jnp.sum(p, axis=-1, keepdims=True) + jnp.exp(sink - m)
                pv = _dot((p * (1.0 / den)).astype(BF16), vb_ref[band, kcols])
                o = pv if o is None else o + pv
            y_ref[rows, pair0] = o[:sb]
            y_ref[rows, pair1] = o[sb:]
            after_unit(r * SWA_KV_HEADS + kvh)


def _mem_into(y_ref, q_ref, k_ref, v_ref, after_head):
    scale = MEM_HEAD_DIM ** -0.5
    for hd in range(MEM_HEADS):
        cols = slice(hd * MEM_HEAD_DIM, (hd + 1) * MEM_HEAD_DIM)
        s = _dot_nt(q_ref[0, :, cols], k_ref[0, :, cols]) * scale
        m = jnp.max(s, axis=-1, keepdims=True)
        p = jnp.exp(s - m)
        inv = 1.0 / jnp.sum(p, axis=-1, keepdims=True)
        y_ref[:, cols] = _dot((p * inv).astype(BF16), v_ref[0, :, cols])
        after_head(hd)


def _tail_kernel(sink_ref, x_ref, ya_ref, za_ref, zb_ref, zc_ref, qs_ref, qm_ref,
                 kp_ref, kc_ref, vp_ref, vc_ref, bias_ref, km_ref, vm_ref, w_hbm, g_ref,
                 o_ref, kb_ref, vb_ref, yb_ref, yc_ref, yacc_ref, w_ref, stage_ref, sem_ref, *, nsub):
    @pl.when(_first_grid_step())
    def _():
        _load_weight_bf16(w_hbm, w_ref, stage_ref, sem_ref)

    def gated(y, z_ref):
        z = z_ref[0].astype(F32)
        return (y * (z * jax.nn.sigmoid(z))).astype(BF16)

    swa_lo, mem_lo = MLA_WIDTH, MLA_WIDTH + SWA_WIDTH

    g_mla = gated(ya_ref[0].astype(F32), za_ref)
    mla_chunk = D_MODEL // (nsub * SWA_KV_HEADS)

    def project_mla(u):
        cols = slice(u * mla_chunk, (u + 1) * mla_chunk)
        yacc_ref[:, cols] = _dot(g_mla, w_ref[:swa_lo, cols])

    _swa_into(yb_ref, sink_ref, qs_ref, kp_ref, kc_ref, vp_ref, vc_ref, bias_ref, kb_ref, vb_ref, nsub,
              project_mla)

    g_swa = gated(yb_ref[...], zb_ref)
    swa_chunk = D_MODEL // MEM_HEADS

    def project_swa(u):
        cols = slice(u * swa_chunk, (u + 1) * swa_chunk)
        yacc_ref[:, cols] += _dot(g_swa, w_ref[swa_lo:mem_lo, cols])

    _mem_into(yc_ref, qm_ref, km_ref, vm_ref, project_swa)

    g_mem = gated(yc_ref[...], zc_ref)
    half = g_mem.shape[0] // 2
    ys = [yacc_ref[rows, :] + _dot(g_mem[rows], w_ref[mem_lo:, :])
          for rows in (slice(0, half), slice(half, 2 * half))]
    o_ref[0, :half, :] = _rms(x_ref[0, :half, :] + ys[0], g_ref[...])
    o_ref[0, half:, :] = _rms(x_ref[0, half:, :] + ys[1], g_ref[...])


def _tail(sinks, x, y_mla, z, qs, ks, vs, bias, kmem, vmem, w, g, nsub):
    B, S, _ = x.shape
    tm = nsub * SWA_BLOCK
    M = kmem.shape[1]
    row = lambda b, i: (b, i, 0)
    prev = lambda b, i: (b, jnp.maximum(i * nsub - 1, 0), 0)
    col = lambda c: (lambda b, i: (b, i, c))
    return pl.pallas_call(
        functools.partial(_tail_kernel, nsub=nsub),
        grid=(B, S // tm),
        in_specs=[
            pl.BlockSpec(memory_space=pltpu.SMEM),
            pl.BlockSpec((1, tm, D_MODEL), row),
            pl.BlockSpec((1, tm, MLA_WIDTH), row),
            pl.BlockSpec((1, tm, MLA_WIDTH), col(0)),
            pl.BlockSpec((1, tm, SWA_WIDTH), col(MLA_WIDTH // SWA_WIDTH)),
            pl.BlockSpec((1, tm, MEM_WIDTH), col((MLA_WIDTH + SWA_WIDTH) // MEM_WIDTH)),
            pl.BlockSpec((1, tm, SWA_WIDTH), col(0)),
            pl.BlockSpec((1, tm, MEM_WIDTH), col(1)),
            pl.BlockSpec((1, SWA_BLOCK, _SWA_KV_PAD), prev),
            pl.BlockSpec((1, tm, _SWA_KV_PAD), row),
            pl.BlockSpec((1, SWA_BLOCK, _SWA_KV_PAD), prev),
            pl.BlockSpec((1, tm, _SWA_KV_PAD), row),
            pl.BlockSpec((SWA_KV_HEADS, 2, 2 * SWA_BLOCK, 2 * SWA_BLOCK), lambda b, i: (0, 0, 0, 0)),
            pl.BlockSpec((1, M, MEM_WIDTH), lambda b, i: (b, 0, 0)),
            pl.BlockSpec((1, M, MEM_WIDTH), lambda b, i: (b, 0, 0)),
            pl.BlockSpec(memory_space=pl.ANY),
            pl.BlockSpec((1, D_MODEL), lambda b, i: (0, 0)),
        ],
        out_specs=pl.BlockSpec((1, tm, D_MODEL), row),
        out_shape=jax.ShapeDtypeStruct((B, S, D_MODEL), F32),
        scratch_shapes=[
            pltpu.VMEM((tm + SWA_BLOCK, _SWA_KV_PAD), BF16),
            pltpu.VMEM((tm + SWA_BLOCK, _SWA_KV_PAD), BF16),
            pltpu.VMEM((tm, SWA_WIDTH), F32),
            pltpu.VMEM((tm, MEM_WIDTH), F32),
            pltpu.VMEM((tm, D_MODEL), F32),
            pltpu.VMEM((MIX_WIDTH, D_MODEL), BF16),
            pltpu.VMEM((2, _W_OUT_CHUNK, D_MODEL), F32),
            pltpu.SemaphoreType.DMA((2,)),
        ],
        compiler_params=_params(("arbitrary", "arbitrary")),
        name="tail",
    )(sinks, x, y_mla, z, z, z, qs, qs, ks, ks, vs, vs, bias, kmem, vmem, w, g)


def _rope_tables(seq):
    inv = 1.0 / (ROPE_THETA ** (jnp.arange(0, MLA_ROPE, 2, dtype=F32) / MLA_ROPE))
    ang = jnp.arange(seq, dtype=F32)[:, None] * inv[None, :]
    cos, sin = jnp.cos(ang), jnp.sin(ang)
    zero = jnp.zeros_like(cos)
    return (jnp.concatenate([cos, cos, zero, zero], axis=-1),
            jnp.concatenate([-sin, sin, zero, zero], axis=-1))


def _t5_bucket(rel):
    nb = N_BUCKETS // 2
    max_exact = nb // 2
    bucket = jnp.where(rel > 0, nb, 0)
    n = jnp.abs(rel)
    nf = jnp.maximum(n, 1).astype(F32)
    large = max_exact + (jnp.log(nf / max_exact) / math.log(MAX_DISTANCE / max_exact)
                         * (nb - max_exact)).astype(jnp.int32)
    large = jnp.minimum(large, nb - 1)
    return bucket + jnp.where(n < max_exact, n, large)


def kernel(x, mem, norm_in, w_in, norm_q, norm_kv, w_uq, w_ukv, attn_sinks, rel_bias,
           norm_mem, w_mem_kv, w_out, norm_final):
    B, S, _ = x.shape
    assert norm_in.shape[0] == 1, "single-layer trunk"

    wt = w_in[0].T
    wq = w_uq[0].reshape(MLA_QLORA, MLA_HEADS, MLA_NOPE + MLA_ROPE)
    wq_pe = wq[..., MLA_NOPE:]
    wuq = jnp.concatenate([wq, wq_pe], axis=-1)
    wuq = wuq.reshape(MLA_QLORA, MLA_HEADS * MLA_QK_PAD).astype(BF16)
    wukv = w_ukv[0].astype(BF16)

    cos_t, sin_t = _rope_tables(S)
    q, k, v, qs, z, ks, vs = _proj(
        x, norm_in, wt, norm_q, norm_kv, wuq, wukv, cos_t, sin_t, tm=256)

    y_mla = _mla(q, k, v, t=512, g=4, nq=4)

    bucket = _t5_bucket(jnp.arange(_REL_SPAN) - (2 * SWA_BLOCK - 1)).astype(jnp.int32)
    bias = _bias(rel_bias, bucket)
    kmem, vmem = _memkv(mem, norm_mem, w_mem_kv[0].astype(BF16))
    return _tail(attn_sinks[0], x, y_mla, z, qs, ks, vs, bias, kmem, vmem,
                 w_out[0], norm_final[None, :], nsub=4)
```

```python
import functools
import math

import jax
import jax.numpy as jnp
from jax import lax
from jax.experimental import pallas as pl
from jax.experimental.pallas import tpu as pltpu

D_MODEL = 2048
CHUNK = 64
N_MEM = 256
EPS = 1e-6
NEG = -1e30

MLA_HEADS = 8
MLA_NOPE = 128
MLA_ROPE = 64
MLA_V = 128
MLA_QLORA = 512
MLA_KVLORA = 256
MLA_WIDTH = MLA_HEADS * MLA_V
MLA_QK_PAD = 256
ROPE_THETA = 10000.0

SWA_HEADS = 8
SWA_KV_HEADS = 2
SWA_GROUP = SWA_HEADS // SWA_KV_HEADS
SWA_HEAD_DIM = 64
SWA_WIDTH = SWA_HEADS * SWA_HEAD_DIM
SWA_KV_WIDTH = SWA_KV_HEADS * SWA_HEAD_DIM
WINDOW_CHUNKS = 2
SWA_BLOCK = 128

MEM_HEADS = 4
MEM_HEAD_DIM = 128
MEM_WIDTH = MEM_HEADS * MEM_HEAD_DIM

MIX_WIDTH = MLA_WIDTH + SWA_WIDTH + MEM_WIDTH

N_BUCKETS = 32
MAX_DISTANCE = 128

BF16 = jnp.bfloat16
F32 = jnp.float32

_R_CQ = 0
_R_CKV = _R_CQ + MLA_QLORA
_R_KPE = _R_CKV + MLA_KVLORA
_R_ZMLA = _R_KPE + MLA_ROPE
_R_QSWA = _R_ZMLA + MLA_WIDTH
_R_KSWA = _R_QSWA + SWA_WIDTH
_R_VSWA = _R_KSWA + SWA_KV_WIDTH
_R_ZSWA = _R_VSWA + SWA_KV_WIDTH
_R_QMEM = _R_ZSWA + SWA_WIDTH
_R_ZMEM = _R_QMEM + MEM_WIDTH
IN_WIDTH = _R_ZMEM + MEM_WIDTH

_MLA_Q_SCALE = (MLA_NOPE + MLA_ROPE) ** -0.5 * math.log2(math.e)

_SWA_KV_PAD = 2 * SWA_KV_HEADS * 128

_W_IN_CHUNK = 208
_W_OUT_CHUNK = 256

_VMEM_LIMIT = 56 * 1024 * 1024


def _params(sem, flags=None):
    return pltpu.CompilerParams(dimension_semantics=sem, vmem_limit_bytes=_VMEM_LIMIT, flags=flags)


def _rms(v, g):
    return (v * lax.rsqrt(jnp.mean(v * v, axis=-1, keepdims=True) + EPS)) * g


def _dot(a, b):
    return jnp.dot(a, b, preferred_element_type=F32)


def _dot_nt(a, b):
    return lax.dot_general(a, b, (((1,), (1,)), ((), ())), preferred_element_type=F32)


def _load_weight_bf16(w_hbm, w_ref, stage_ref, sem_ref):
    rows = stage_ref.shape[1]
    n_chunks = w_ref.shape[0] // rows

    def chunk_copy(c):
        return pltpu.make_async_copy(w_hbm.at[pl.ds(c * rows, rows), :], stage_ref.at[c % 2],
                                     sem_ref.at[c % 2])

    chunk_copy(0).start()
    for c in range(n_chunks):
        if c + 1 < n_chunks:
            chunk_copy(c + 1).start()
        chunk_copy(c).wait()
        w_ref[c * rows:(c + 1) * rows, :] = stage_ref[c % 2].astype(BF16)


def _first_grid_step():
    return jnp.logical_and(pl.program_id(0) == 0, pl.program_id(1) == 0)


def _lane_tile(v, n):
    return jnp.concatenate([v] * n, axis=1)


def _chunk_of(pos):
    return jnp.right_shift(pos, CHUNK.bit_length() - 1)


def _rope128(v, cos_t, sin_t):
    return v * cos_t + pltpu.roll(v, MLA_ROPE // 2, 1) * sin_t


def _store_swa_kv(ref, kv):
    low = lax.broadcasted_iota(jnp.int32, kv.shape, 1) < SWA_HEAD_DIM
    swapped = pltpu.roll(kv, SWA_HEAD_DIM, 1)
    pieces = (jnp.where(low, kv, 0.0), jnp.where(low, 0.0, swapped),
              jnp.where(low, swapped, 0.0), jnp.where(low, 0.0, kv))
    for i, piece in enumerate(pieces):
        ref[0, :, i * 128:(i + 1) * 128] = piece.astype(BF16)


def _proj_kernel(x_ref, gin_ref, wt_hbm, gq_ref, gkv_ref, wuq_ref, wukv_ref, freq_ref,
                 q_ref, k_ref, v_ref, qs_ref, z_ref, ks_ref, vs_ref, wt_ref, stage_ref, sem_ref):
    @pl.when(_first_grid_step())
    def _():
        _load_weight_bf16(wt_hbm, wt_ref, stage_ref, sem_ref)

    h = _rms(x_ref[0], gin_ref[...]).astype(BF16)
    tm = h.shape[0]
    pos = pl.program_id(1) * tm + lax.broadcasted_iota(jnp.int32, (tm, 128), 0)
    lane = lax.broadcasted_iota(jnp.int32, (tm, 128), 1)
    ang = pos.astype(F32) * freq_ref[...]
    sin = jnp.sin(ang)
    cos_t = jnp.where(lane < MLA_ROPE, jnp.cos(ang), 0.0)
    sin_t = jnp.where(lane < MLA_ROPE // 2, -sin, sin)

    def proj(lo, hi):
        return _dot_nt(h, wt_ref[lo:hi, :])

    pa = proj(_R_CQ, _R_KPE)
    cq = _rms(pa[:, :MLA_QLORA], gq_ref[...]).astype(BF16)
    ckv = _rms(pa[:, MLA_QLORA:], gkv_ref[...]).astype(BF16)

    qs_ref[0, :, :SWA_WIDTH] = proj(_R_QSWA, _R_KSWA).astype(BF16)
    qs_ref[0, :, SWA_WIDTH:] = proj(_R_QMEM, _R_ZMEM).astype(BF16)
    z_ref[0, :, :MLA_WIDTH] = proj(_R_ZMLA, _R_QSWA).astype(BF16)
    z_ref[0, :, MLA_WIDTH:MLA_WIDTH + SWA_WIDTH] = proj(_R_ZSWA, _R_QMEM).astype(BF16)
    z_ref[0, :, MLA_WIDTH + SWA_WIDTH:] = proj(_R_ZMEM, IN_WIDTH).astype(BF16)
    kvs = proj(_R_KSWA, _R_ZSWA)
    _store_swa_kv(ks_ref, kvs[:, :SWA_KV_WIDTH])
    _store_swa_kv(vs_ref, kvs[:, SWA_KV_WIDTH:])
    pe = proj(_R_KPE, _R_ZMLA)
    kpe = _rope128(jnp.concatenate([pe, pe], axis=1), cos_t, sin_t).astype(BF16)

    qall = _dot(cq, wuq_ref[...]) * _MLA_Q_SCALE
    kv = _dot(ckv, wukv_ref[...])
    for hd in range(MLA_HEADS):
        oq = hd * (MLA_NOPE + MLA_ROPE)
        q_pe = qall[:, oq + MLA_NOPE:oq + MLA_NOPE + MLA_ROPE]
        q_ref[0, hd, :, :MLA_NOPE] = qall[:, oq:oq + MLA_NOPE].astype(BF16)
        q_ref[0, hd, :, MLA_NOPE:] = _rope128(
            jnp.concatenate([q_pe, q_pe], axis=1), cos_t, sin_t).astype(BF16)
        o = hd * (MLA_NOPE + MLA_V)
        k_ref[0, hd, :, :MLA_NOPE] = kv[:, o:o + MLA_NOPE].astype(BF16)
        k_ref[0, hd, :, MLA_NOPE:] = kpe
        v_ref[0, hd] = kv[:, o + MLA_NOPE:o + MLA_QK_PAD].astype(BF16)


def _proj(x, gin, wt, gq, gkv, wuq, wukv, freq, tm):
    B, S, _ = x.shape
    const = lambda b, i: (0, 0)
    row = lambda b, i: (b, i, 0)
    single = pl.Buffered(1)

    def out(width):
        return jax.ShapeDtypeStruct((B, S, width), BF16)

    head_spec = pl.BlockSpec((1, MLA_HEADS, tm, MLA_QK_PAD), lambda b, i: (b, 0, i, 0))
    head_shape = jax.ShapeDtypeStruct((B, MLA_HEADS, S, MLA_QK_PAD), BF16)

    return pl.pallas_call(
        _proj_kernel,
        grid=(B, S // tm),
        in_specs=[
            pl.BlockSpec((1, tm, D_MODEL), row),
            pl.BlockSpec((1, D_MODEL), const),
            pl.BlockSpec(memory_space=pl.ANY),
            pl.BlockSpec((1, MLA_QLORA), const),
            pl.BlockSpec((1, MLA_KVLORA), const),
            pl.BlockSpec((MLA_QLORA, MLA_HEADS * (MLA_NOPE + MLA_ROPE)), const, pipeline_mode=single),
            pl.BlockSpec((MLA_KVLORA, MLA_HEADS * (MLA_NOPE + MLA_V)), const, pipeline_mode=single),
            pl.BlockSpec((1, 128), const),
        ],
        out_specs=[
            head_spec, head_spec,
            pl.BlockSpec((1, MLA_HEADS, tm, MLA_V), lambda b, i: (b, 0, i, 0)),
            pl.BlockSpec((1, tm, SWA_WIDTH + MEM_WIDTH), row),
            pl.BlockSpec((1, tm, MIX_WIDTH), row),
            pl.BlockSpec((1, tm, _SWA_KV_PAD), row),
            pl.BlockSpec((1, tm, _SWA_KV_PAD), row),
        ],
        out_shape=[
            head_shape, head_shape, jax.ShapeDtypeStruct((B, MLA_HEADS, S, MLA_V), BF16),
            out(SWA_WIDTH + MEM_WIDTH), out(MIX_WIDTH), out(_SWA_KV_PAD), out(_SWA_KV_PAD),
        ],
        scratch_shapes=[
            pltpu.VMEM((IN_WIDTH, D_MODEL), BF16),
            pltpu.VMEM((2, _W_IN_CHUNK, D_MODEL), F32),
            pltpu.SemaphoreType.DMA((2,)),
        ],
        compiler_params=_params(("arbitrary", "arbitrary")),
        name="proj",
    )(x, gin, wt, gq, gkv, wuq, wukv, freq)


def _mla_kernel(q_ref, k_ref, v_ref, o_ref, acc_ref, m_ref, s_ref, cm_ref, *, t, g, nq):
    qi0 = pl.program_id(2) * nq
    qi_end = qi0 + nq - 1
    acc_ref[...] = jnp.zeros(acc_ref.shape, F32)
    m_ref[...] = jnp.full(m_ref.shape, NEG, F32)
    ones = jnp.ones((t, MLA_V), BF16)

    def row_max(s):
        return jnp.broadcast_to(jnp.max(s, axis=-1, keepdims=True), (t, 128))

    def scores(qi, c, slot, hd):
        rows = pl.multiple_of((qi - qi0) * t, t)
        keys = pl.multiple_of(c * t, t)
        s = _dot_nt(q_ref[0, hd, pl.ds(rows, t), :], k_ref[0, hd, pl.ds(keys, t), :])
        s_ref[slot, hd] = s
        cm_ref[slot, hd] = row_max(s)

    def update(c, slot, hd, mask):
        keys = pl.multiple_of(c * t, t)
        s = s_ref[slot, hd]
        if mask is not None:
            s = jnp.where(mask, s, NEG)
            m_cur = row_max(s)
        else:
            m_cur = cm_ref[slot, hd]
        m_old = m_ref[hd]
        m_new = jnp.maximum(m_old, m_cur)
        alpha = jnp.exp2(m_old - m_new)
        p = jnp.exp2(s - _lane_tile(m_new, t // 128)).astype(BF16)
        v_ones = jnp.concatenate([v_ref[0, hd, pl.ds(keys, t), :], ones], axis=1)
        acc_ref[hd] = _lane_tile(alpha, MLA_QK_PAD // 128) * acc_ref[hd] + _dot(p, v_ones)
        m_ref[hd] = m_new

    def finish_tile(qi, hd):
        rows = pl.multiple_of((qi - qi0) * t, t)
        acc = acc_ref[hd]
        o_ref[0, pl.ds(rows, t), hd * MLA_V:(hd + 1) * MLA_V] = (
            acc[:, :MLA_V] / acc[:, MLA_V:]).astype(BF16)
        acc_ref[hd] = jnp.zeros(acc.shape, F32)
        m_ref[hd] = jnp.full((t, 128), NEG, F32)

    def following(qi, c):
        is_last = c == qi
        return jnp.where(is_last, jnp.minimum(qi + 1, qi_end), qi), jnp.where(is_last, 0, c + 1)

    def step(qi, c, slot, diagonal):
        qn, cn = following(qi, c)
        mask = None
        if diagonal:
            q_chunk = _chunk_of(qi * t + lax.broadcasted_iota(jnp.int32, (t, t), 0))
            k_chunk = _chunk_of(c * t + lax.broadcasted_iota(jnp.int32, (t, t), 1))
            mask = k_chunk <= q_chunk
        for hd in range(g):
            scores(qn, cn, 1 - slot, hd)
            update(c, slot, hd, mask)
            if diagonal:
                finish_tile(qi, hd)

    for hd in range(g):
        scores(qi0, 0, 0, hd)

    def body(_, pair):
        first, second = pair, following(*pair)
        diag0 = first[1] == first[0]
        diag1 = second[1] == second[0]
        for d0, d1, cond in ((True, False, diag0), (False, True, diag1),
                             (False, False, jnp.logical_not(jnp.logical_or(diag0, diag1)))):
            @pl.when(cond)
            def _(d0=d0, d1=d1):
                step(*first, 0, d0)
                step(*second, 1, d1)
        return following(*second)

    n_steps = nq * qi0 + nq * (nq + 1) // 2
    lax.fori_loop(0, n_steps // 2, body, (qi0, 0))


def _mla(q, k, v, t, g, nq):
    B, _, S, _ = q.shape
    assert S % (nq * t) == 0 and nq % 2 == 0 and (nq * (nq + 1) // 2) % 2 == 0
    return pl.pallas_call(
        functools.partial(_mla_kernel, t=t, g=g, nq=nq),
        grid=(B, MLA_HEADS // g, S // (nq * t)),
        in_specs=[
            pl.BlockSpec((1, g, nq * t, MLA_QK_PAD), lambda b, h, i: (b, h, i, 0)),
            pl.BlockSpec((1, g, S, MLA_QK_PAD), lambda b, h, i: (b, h, 0, 0)),
            pl.BlockSpec((1, g, S, MLA_V), lambda b, h, i: (b, h, 0, 0)),
        ],
        out_specs=pl.BlockSpec((1, nq * t, g * MLA_V), lambda b, h, i: (b, i, h)),
        out_shape=jax.ShapeDtypeStruct((B, S, MLA_WIDTH), BF16),
        scratch_shapes=[
            pltpu.VMEM((g, t, MLA_QK_PAD), F32),
            pltpu.VMEM((g, t, 128), F32),
            pltpu.VMEM((2, g, t, t), F32),
            pltpu.VMEM((2, g, t, 128), F32),
        ],
        compiler_params=_params(("arbitrary", "arbitrary", "arbitrary")),
        name="mla",
    )(q, k, v)


_REL_SPAN = 4 * SWA_BLOCK


_N_OFFSETS = 3 * SWA_BLOCK - 1


def _bias_kernel(table_ref, bucket_ref, o_ref):
    head = lax.broadcasted_iota(jnp.int32, (SWA_HEADS, 128), 0)
    lane = lax.broadcasted_iota(jnp.int32, (SWA_HEADS, _REL_SPAN), 1)

    def lookup(c, per_offset):
        b = bucket_ref[c]
        col = jnp.zeros((SWA_HEADS, 128), F32)
        for hd in range(SWA_HEADS):
            col = jnp.where(head == hd, table_ref[b, hd], col)
        return jnp.where(lane == c, _lane_tile(col, _REL_SPAN // 128), per_offset)

    per_offset_all = lax.fori_loop(0, _N_OFFSETS, lookup, jnp.zeros((SWA_HEADS, _REL_SPAN), F32))
    for hd in range(SWA_HEADS):
        per_offset = per_offset_all[hd:hd + 1, :]
        rows = jnp.broadcast_to(per_offset, (SWA_BLOCK, _REL_SPAN))
        band = pltpu.roll(rows, _REL_SPAN - (SWA_BLOCK - 1), 1, stride=1, stride_axis=0)
        kvh, within = divmod(hd, SWA_GROUP)
        par, half = within % 2, within // 2
        o_ref[kvh, par, half * SWA_BLOCK:(half + 1) * SWA_BLOCK, :] = band[:, :2 * SWA_BLOCK]


def _bias(rel_table, bucket):
    return pl.pallas_call(
        _bias_kernel,
        in_specs=[
            pl.BlockSpec(memory_space=pltpu.SMEM),
            pl.BlockSpec(memory_space=pltpu.SMEM),
        ],
        out_specs=pl.BlockSpec(memory_space=pltpu.VMEM),
        out_shape=jax.ShapeDtypeStruct((SWA_KV_HEADS, 2, 2 * SWA_BLOCK, 2 * SWA_BLOCK), F32),
        name="t5bias",
    )(rel_table, bucket)


def _memkv_kernel(mem_ref, g_ref, w_ref, k_ref, v_ref):
    mn = _rms(mem_ref[0], g_ref[...]).astype(BF16)
    kv = _dot(mn, w_ref[...])
    k_ref[0] = kv[:, :MEM_WIDTH].astype(BF16)
    v_ref[0] = kv[:, MEM_WIDTH:].astype(BF16)


def _memkv(mem, g, w):
    B, M, _ = mem.shape
    const = lambda b: (0, 0)
    row = lambda b: (b, 0, 0)
    return pl.pallas_call(
        _memkv_kernel,
        grid=(B,),
        in_specs=[
            pl.BlockSpec((1, M, D_MODEL), row),
            pl.BlockSpec((1, D_MODEL), const),
            pl.BlockSpec((D_MODEL, 2 * MEM_WIDTH), const),
        ],
        out_specs=[pl.BlockSpec((1, M, MEM_WIDTH), row)] * 2,
        out_shape=[jax.ShapeDtypeStruct((B, M, MEM_WIDTH), BF16)] * 2,
        compiler_params=_params(("arbitrary",)),
        name="memkv",
    )(mem, g, w)


def _swa_into(y_ref, sink_ref, q_ref, kp_ref, kc_ref, vp_ref, vc_ref, bias_ref, kb_ref, vb_ref, nsub,
              after_unit):
    t = pl.program_id(1)
    scale = SWA_HEAD_DIM ** -0.5
    sb = SWA_BLOCK
    kb_ref[:sb] = kp_ref[0]
    kb_ref[sb:] = kc_ref[0]
    vb_ref[:sb] = vp_ref[0]
    vb_ref[sb:] = vc_ref[0]

    row = lax.broadcasted_iota(jnp.int32, (2 * sb, 2 * sb), 0)
    q_chunk = _chunk_of(row & (sb - 1))
    b_chunk = _chunk_of(lax.broadcasted_iota(jnp.int32, (2 * sb, 2 * sb), 1))
    valid_any = jnp.logical_and(b_chunk >= q_chunk, b_chunk <= q_chunk + WINDOW_CHUNKS)
    first_lo = jnp.where(t > 0, 0, 2)
    valid_first = jnp.logical_and(valid_any, b_chunk >= first_lo)
    upper = lax.broadcasted_iota(jnp.int32, (2 * sb, 1), 0) < sb

    for r in range(nsub):
        valid = valid_first if r == 0 else valid_any
        rows = slice(r * sb, (r + 1) * sb)
        band = slice(r * sb, (r + 2) * sb)
        for kvh in range(SWA_KV_HEADS):
            pair0 = slice(2 * kvh * 128, (2 * kvh + 1) * 128)
            pair1 = slice((2 * kvh + 1) * 128, (2 * kvh + 2) * 128)
            q = jnp.concatenate([q_ref[0, rows, pair0], q_ref[0, rows, pair1]], axis=0)
            o = None
            for par in range(2):
                kcols = slice((2 * kvh + par) * 128, (2 * kvh + par + 1) * 128)
                hd = SWA_GROUP * kvh + par
                s = _dot_nt(q, kb_ref[band, kcols]) * scale + bias_ref[kvh, par]
                s = jnp.where(valid, s, NEG)
                sink = jnp.where(upper, sink_ref[hd], sink_ref[hd + 2])
                m = jnp.maximum(jnp.max(s, axis=-1, keepdims=True), sink)
                p = jnp.exp(s - m)
                den = jnp.sum(p, axis=-1, keepdims=True) + jnp.exp(sink - m)
                pv = _dot((p * (1.0 / den)).astype(BF16), vb_ref[band, kcols])
                o = pv if o is None else o + pv
            y_ref[rows, pair0] = o[:sb]
            y_ref[rows, pair1] = o[sb:]
            after_unit(r * SWA_KV_HEADS + kvh)


def _mem_into(y_ref, q_ref, k_ref, v_ref, after_head):
    scale = MEM_HEAD_DIM ** -0.5
    for hd in range(MEM_HEADS):
        cols = slice(hd * MEM_HEAD_DIM, (hd + 1) * MEM_HEAD_DIM)
        s = _dot_nt(q_ref[0, :, cols], k_ref[0, :, cols]) * scale
        m = jnp.max(s, axis=-1, keepdims=True)
        p = jnp.exp(s - m)
        inv = 1.0 / jnp.sum(p, axis=-1, keepdims=True)
        y_ref[:, cols] = _dot((p * inv).astype(BF16), v_ref[0, :, cols])
        after_head(hd)


def _tail_kernel(sink_ref, x_ref, ya_ref, za_ref, zb_ref, zc_ref, qs_ref, qm_ref,
                 kp_ref, kc_ref, vp_ref, vc_ref, bias_ref, km_ref, vm_ref, w_hbm, g_ref,
                 o_ref, kb_ref, vb_ref, yb_ref, yc_ref, yacc_ref, w_ref, stage_ref, sem_ref, *, nsub):
    @pl.when(_first_grid_step())
    def _():
        _load_weight_bf16(w_hbm, w_ref, stage_ref, sem_ref)

    def gated(y, z_ref):
        z = z_ref[0].astype(F32)
        return (y * (z * jax.nn.sigmoid(z))).astype(BF16)

    swa_lo, mem_lo = MLA_WIDTH, MLA_WIDTH + SWA_WIDTH

    g_mla = gated(ya_ref[0].astype(F32), za_ref)
    mla_chunk = D_MODEL // (nsub * SWA_KV_HEADS)

    def project_mla(u):
        cols = slice(u * mla_chunk, (u + 1) * mla_chunk)
        yacc_ref[:, cols] = _dot(g_mla, w_ref[:swa_lo, cols])

    _swa_into(yb_ref, sink_ref, qs_ref, kp_ref, kc_ref, vp_ref, vc_ref, bias_ref, kb_ref, vb_ref, nsub,
              project_mla)

    g_swa = gated(yb_ref[...], zb_ref)
    swa_chunk = D_MODEL // MEM_HEADS

    def project_swa(u):
        cols = slice(u * swa_chunk, (u + 1) * swa_chunk)
        yacc_ref[:, cols] += _dot(g_swa, w_ref[swa_lo:mem_lo, cols])

    _mem_into(yc_ref, qm_ref, km_ref, vm_ref, project_swa)

    g_mem = gated(yc_ref[...], zc_ref)
    half = g_mem.shape[0] // 2
    ys = [yacc_ref[rows, :] + _dot(g_mem[rows], w_ref[mem_lo:, :])
          for rows in (slice(0, half), slice(half, 2 * half))]
    o_ref[0, :half, :] = _rms(x_ref[0, :half, :] + ys[0], g_ref[...])
    o_ref[0, half:, :] = _rms(x_ref[0, half:, :] + ys[1], g_ref[...])


def _tail(sinks, x, y_mla, z, qs, ks, vs, bias, kmem, vmem, w, g, nsub):
    B, S, _ = x.shape
    tm = nsub * SWA_BLOCK
    M = kmem.shape[1]
    row = lambda b, i: (b, i, 0)
    prev = lambda b, i: (b, jnp.maximum(i * nsub - 1, 0), 0)
    col = lambda c: (lambda b, i: (b, i, c))
    return pl.pallas_call(
        functools.partial(_tail_kernel, nsub=nsub),
        grid=(B, S // tm),
        in_specs=[
            pl.BlockSpec(memory_space=pltpu.SMEM),
            pl.BlockSpec((1, tm, D_MODEL), row),
            pl.BlockSpec((1, tm, MLA_WIDTH), row),
            pl.BlockSpec((1, tm, MLA_WIDTH), col(0)),
            pl.BlockSpec((1, tm, SWA_WIDTH), col(MLA_WIDTH // SWA_WIDTH)),
            pl.BlockSpec((1, tm, MEM_WIDTH), col((MLA_WIDTH + SWA_WIDTH) // MEM_WIDTH)),
            pl.BlockSpec((1, tm, SWA_WIDTH), col(0)),
            pl.BlockSpec((1, tm, MEM_WIDTH), col(1)),
            pl.BlockSpec((1, SWA_BLOCK, _SWA_KV_PAD), prev),
            pl.BlockSpec((1, tm, _SWA_KV_PAD), row),
            pl.BlockSpec((1, SWA_BLOCK, _SWA_KV_PAD), prev),
            pl.BlockSpec((1, tm, _SWA_KV_PAD), row),
            pl.BlockSpec((SWA_KV_HEADS, 2, 2 * SWA_BLOCK, 2 * SWA_BLOCK), lambda b, i: (0, 0, 0, 0)),
            pl.BlockSpec((1, M, MEM_WIDTH), lambda b, i: (b, 0, 0)),
            pl.BlockSpec((1, M, MEM_WIDTH), lambda b, i: (b, 0, 0)),
            pl.BlockSpec(memory_space=pl.ANY),
            pl.BlockSpec((1, D_MODEL), lambda b, i: (0, 0)),
        ],
        out_specs=pl.BlockSpec((1, tm, D_MODEL), row),
        out_shape=jax.ShapeDtypeStruct((B, S, D_MODEL), F32),
        scratch_shapes=[
            pltpu.VMEM((tm + SWA_BLOCK, _SWA_KV_PAD), BF16),
            pltpu.VMEM((tm + SWA_BLOCK, _SWA_KV_PAD), BF16),
            pltpu.VMEM((tm, SWA_WIDTH), F32),
            pltpu.VMEM((tm, MEM_WIDTH), F32),
            pltpu.VMEM((tm, D_MODEL), F32),
            pltpu.VMEM((MIX_WIDTH, D_MODEL), BF16),
            pltpu.VMEM((2, _W_OUT_CHUNK, D_MODEL), F32),
            pltpu.SemaphoreType.DMA((2,)),
        ],
        compiler_params=_params(("arbitrary", "arbitrary")),
        name="tail",
    )(sinks, x, y_mla, z, z, z, qs, qs, ks, ks, vs, vs, bias, kmem, vmem, w, g)


def _rope_freq():
    inv = 1.0 / (ROPE_THETA ** (jnp.arange(0, MLA_ROPE, 2, dtype=F32) / MLA_ROPE))
    zero = jnp.zeros_like(inv)
    return jnp.concatenate([inv, inv, zero, zero])[None, :]


def _t5_bucket(rel):
    nb = N_BUCKETS // 2
    max_exact = nb // 2
    bucket = jnp.where(rel > 0, nb, 0)
    n = jnp.abs(rel)
    nf = jnp.maximum(n, 1).astype(F32)
    large = max_exact + (jnp.log(nf / max_exact) / math.log(MAX_DISTANCE / max_exact)
                         * (nb - max_exact)).astype(jnp.int32)
    large = jnp.minimum(large, nb - 1)
    return bucket + jnp.where(n < max_exact, n, large)


def kernel(x, mem, norm_in, w_in, norm_q, norm_kv, w_uq, w_ukv, attn_sinks, rel_bias,
           norm_mem, w_mem_kv, w_out, norm_final):
    B, S, _ = x.shape
    assert norm_in.shape[0] == 1, "single-layer trunk"

    wt = w_in[0].T
    q, k, v, qs, z, ks, vs = _proj(x, norm_in, wt, norm_q, norm_kv, w_uq[0].astype(BF16),
                                   w_ukv[0].astype(BF16), _rope_freq(), tm=256)

    y_mla = _mla(q, k, v, t=512, g=4, nq=4)

    bucket = _t5_bucket(jnp.arange(_REL_SPAN) - (2 * SWA_BLOCK - 1)).astype(jnp.int32)
    bias = _bias(rel_bias, bucket)
    kmem, vmem = _memkv(mem, norm_mem, w_mem_kv[0].astype(BF16))
    return _tail(attn_sinks[0], x, y_mla, z, qs, ks, vs, bias, kmem, vmem,
                 w_out[0], norm_final[None, :], nsub=4)
```

```python
import functools
import math

import jax
import jax.numpy as jnp
from jax import lax
from jax.experimental import pallas as pl
from jax.experimental.pallas import tpu as pltpu

D_MODEL = 2048
CHUNK = 64
N_MEM = 256
EPS = 1e-6
NEG = -1e30

MLA_HEADS = 8
MLA_NOPE = 128
MLA_ROPE = 64
MLA_V = 128
MLA_QLORA = 512
MLA_KVLORA = 256
MLA_WIDTH = MLA_HEADS * MLA_V
MLA_QK_PAD = 256
ROPE_THETA = 10000.0

SWA_HEADS = 8
SWA_KV_HEADS = 2
SWA_GROUP = SWA_HEADS // SWA_KV_HEADS
SWA_HEAD_DIM = 64
SWA_WIDTH = SWA_HEADS * SWA_HEAD_DIM
SWA_KV_WIDTH = SWA_KV_HEADS * SWA_HEAD_DIM
WINDOW_CHUNKS = 2
SWA_BLOCK = 128

MEM_HEADS = 4
MEM_HEAD_DIM = 128
MEM_WIDTH = MEM_HEADS * MEM_HEAD_DIM

MIX_WIDTH = MLA_WIDTH + SWA_WIDTH + MEM_WIDTH

N_BUCKETS = 32
MAX_DISTANCE = 128

BF16 = jnp.bfloat16
F32 = jnp.float32

_R_CQ = 0
_R_CKV = _R_CQ + MLA_QLORA
_R_KPE = _R_CKV + MLA_KVLORA
_R_ZMLA = _R_KPE + MLA_ROPE
_R_QSWA = _R_ZMLA + MLA_WIDTH
_R_KSWA = _R_QSWA + SWA_WIDTH
_R_VSWA = _R_KSWA + SWA_KV_WIDTH
_R_ZSWA = _R_VSWA + SWA_KV_WIDTH
_R_QMEM = _R_ZSWA + SWA_WIDTH
_R_ZMEM = _R_QMEM + MEM_WIDTH
IN_WIDTH = _R_ZMEM + MEM_WIDTH

_MLA_Q_SCALE = (MLA_NOPE + MLA_ROPE) ** -0.5 * math.log2(math.e)

_SWA_KV_PAD = 2 * SWA_KV_HEADS * 128

_W_IN_CHUNK = 208
_W_OUT_CHUNK = 256

_VMEM_LIMIT = 56 * 1024 * 1024


def _params(sem, flags=None):
    return pltpu.CompilerParams(dimension_semantics=sem, vmem_limit_bytes=_VMEM_LIMIT, flags=flags)


def _rms(v, g):
    return (v * lax.rsqrt(jnp.mean(v * v, axis=-1, keepdims=True) + EPS)) * g


def _dot(a, b):
    return jnp.dot(a, b, preferred_element_type=F32)


def _dot_nt(a, b):
    return lax.dot_general(a, b, (((1,), (1,)), ((), ())), preferred_element_type=F32)


def _load_weight_bf16(w_hbm, w_ref, stage_ref, sem_ref):
    rows = stage_ref.shape[1]
    n_chunks = w_ref.shape[0] // rows

    def chunk_copy(c):
        return pltpu.make_async_copy(w_hbm.at[pl.ds(c * rows, rows), :], stage_ref.at[c % 2],
                                     sem_ref.at[c % 2])

    chunk_copy(0).start()
    for c in range(n_chunks):
        if c + 1 < n_chunks:
            chunk_copy(c + 1).start()
        chunk_copy(c).wait()
        w_ref[c * rows:(c + 1) * rows, :] = stage_ref[c % 2].astype(BF16)


def _first_grid_step():
    return jnp.logical_and(pl.program_id(0) == 0, pl.program_id(1) == 0)


def _lane_tile(v, n):
    return jnp.concatenate([v] * n, axis=1)


def _chunk_of(pos):
    return jnp.right_shift(pos, CHUNK.bit_length() - 1)


def _rope128(v, cos_t, sin_t):
    return v * cos_t + pltpu.roll(v, MLA_ROPE // 2, 1) * sin_t


def _store_swa_kv(ref, kv):
    low = lax.broadcasted_iota(jnp.int32, kv.shape, 1) < SWA_HEAD_DIM
    swapped = pltpu.roll(kv, SWA_HEAD_DIM, 1)
    pieces = (jnp.where(low, kv, 0.0), jnp.where(low, 0.0, swapped),
              jnp.where(low, swapped, 0.0), jnp.where(low, 0.0, kv))
    for i, piece in enumerate(pieces):
        ref[0, :, i * 128:(i + 1) * 128] = piece.astype(BF16)


def _proj_kernel(x_ref, gin_ref, wt_hbm, gq_ref, gkv_ref, wuq_ref, wukv_ref, cos_ref, sin_ref,
                 q_ref, k_ref, v_ref, qs_ref, z_ref, ks_ref, vs_ref, wt_ref, stage_ref, sem_ref):
    @pl.when(_first_grid_step())
    def _():
        _load_weight_bf16(wt_hbm, wt_ref, stage_ref, sem_ref)

    h = _rms(x_ref[0], gin_ref[...]).astype(BF16)
    cos_t = cos_ref[...]
    sin_t = sin_ref[...]

    def proj(lo, hi):
        return _dot_nt(h, wt_ref[lo:hi, :])

    pa = proj(_R_CQ, _R_KPE)
    cq = _rms(pa[:, :MLA_QLORA], gq_ref[...]).astype(BF16)
    ckv = _rms(pa[:, MLA_QLORA:], gkv_ref[...]).astype(BF16)

    qs_ref[0, :, :SWA_WIDTH] = proj(_R_QSWA, _R_KSWA).astype(BF16)
    qs_ref[0, :, SWA_WIDTH:] = proj(_R_QMEM, _R_ZMEM).astype(BF16)
    z_ref[0, :, :MLA_WIDTH] = proj(_R_ZMLA, _R_QSWA).astype(BF16)
    z_ref[0, :, MLA_WIDTH:MLA_WIDTH + SWA_WIDTH] = proj(_R_ZSWA, _R_QMEM).astype(BF16)
    z_ref[0, :, MLA_WIDTH + SWA_WIDTH:] = proj(_R_ZMEM, IN_WIDTH).astype(BF16)
    kvs = proj(_R_KSWA, _R_ZSWA)
    _store_swa_kv(ks_ref, kvs[:, :SWA_KV_WIDTH])
    _store_swa_kv(vs_ref, kvs[:, SWA_KV_WIDTH:])
    pe = proj(_R_KPE, _R_ZMLA)
    kpe = _rope128(jnp.concatenate([pe, pe], axis=1), cos_t, sin_t).astype(BF16)

    qall = _dot(cq, wuq_ref[...]) * _MLA_Q_SCALE
    kv = _dot(ckv, wukv_ref[...])
    for hd in range(MLA_HEADS):
        oq = hd * (MLA_NOPE + MLA_ROPE)
        q_pe = qall[:, oq + MLA_NOPE:oq + MLA_NOPE + MLA_ROPE]
        q_ref[0, hd, :, :MLA_NOPE] = qall[:, oq:oq + MLA_NOPE].astype(BF16)
        q_ref[0, hd, :, MLA_NOPE:] = _rope128(
            jnp.concatenate([q_pe, q_pe], axis=1), cos_t, sin_t).astype(BF16)
        o = hd * (MLA_NOPE + MLA_V)
        k_ref[0, hd, :, :MLA_NOPE] = kv[:, o:o + MLA_NOPE].astype(BF16)
        k_ref[0, hd, :, MLA_NOPE:] = kpe
        v_ref[0, hd] = kv[:, o + MLA_NOPE:o + MLA_QK_PAD].astype(BF16)


def _proj(x, gin, wt, gq, gkv, wuq, wukv, cos_t, sin_t, tm):
    B, S, _ = x.shape
    const = lambda b, i: (0, 0)
    row = lambda b, i: (b, i, 0)
    single = pl.Buffered(1)

    def out(width):
        return jax.ShapeDtypeStruct((B, S, width), BF16)

    head_spec = pl.BlockSpec((1, MLA_HEADS, tm, MLA_QK_PAD), lambda b, i: (b, 0, i, 0))
    head_shape = jax.ShapeDtypeStruct((B, MLA_HEADS, S, MLA_QK_PAD), BF16)

    return pl.pallas_call(
        _proj_kernel,
        grid=(B, S // tm),
        in_specs=[
            pl.BlockSpec((1, tm, D_MODEL), row),
            pl.BlockSpec((1, D_MODEL), const),
            pl.BlockSpec(memory_space=pl.ANY),
            pl.BlockSpec((1, MLA_QLORA), const),
            pl.BlockSpec((1, MLA_KVLORA), const),
            pl.BlockSpec((MLA_QLORA, MLA_HEADS * (MLA_NOPE + MLA_ROPE)), const, pipeline_mode=single),
            pl.BlockSpec((MLA_KVLORA, MLA_HEADS * (MLA_NOPE + MLA_V)), const, pipeline_mode=single),
            pl.BlockSpec((tm, 128), lambda b, i: (i, 0)),
            pl.BlockSpec((tm, 128), lambda b, i: (i, 0)),
        ],
        out_specs=[
            head_spec, head_spec,
            pl.BlockSpec((1, MLA_HEADS, tm, MLA_V), lambda b, i: (b, 0, i, 0)),
            pl.BlockSpec((1, tm, SWA_WIDTH + MEM_WIDTH), row),
            pl.BlockSpec((1, tm, MIX_WIDTH), row),
            pl.BlockSpec((1, tm, _SWA_KV_PAD), row),
            pl.BlockSpec((1, tm, _SWA_KV_PAD), row),
        ],
        out_shape=[
            head_shape, head_shape, jax.ShapeDtypeStruct((B, MLA_HEADS, S, MLA_V), BF16),
            out(SWA_WIDTH + MEM_WIDTH), out(MIX_WIDTH), out(_SWA_KV_PAD), out(_SWA_KV_PAD),
        ],
        scratch_shapes=[
            pltpu.VMEM((IN_WIDTH, D_MODEL), BF16),
            pltpu.VMEM((2, _W_IN_CHUNK, D_MODEL), F32),
            pltpu.SemaphoreType.DMA((2,)),
        ],
        compiler_params=_params(("arbitrary", "arbitrary")),
        name="proj",
    )(x, gin, wt, gq, gkv, wuq, wukv, cos_t, sin_t)


def _mla_kernel(q_ref, k_ref, v_ref, o_ref, acc_ref, m_ref, s_ref, cm_ref, *, t, g, nq):
    qi0 = pl.program_id(2) * nq
    qi_end = qi0 + nq - 1
    acc_ref[...] = jnp.zeros(acc_ref.shape, F32)
    m_ref[...] = jnp.full(m_ref.shape, NEG, F32)
    ones = jnp.ones((t, MLA_V), BF16)

    def row_max(s):
        return jnp.broadcast_to(jnp.max(s, axis=-1, keepdims=True), (t, 128))

    def scores(qi, c, slot, hd):
        rows = pl.multiple_of((qi - qi0) * t, t)
        keys = pl.multiple_of(c * t, t)
        s = _dot_nt(q_ref[0, hd, pl.ds(rows, t), :], k_ref[0, hd, pl.ds(keys, t), :])
        s_ref[slot, hd] = s
        cm_ref[slot, hd] = row_max(s)

    def update(c, slot, hd, mask):
        keys = pl.multiple_of(c * t, t)
        s = s_ref[slot, hd]
        if mask is not None:
            s = jnp.where(mask, s, NEG)
            m_cur = row_max(s)
        else:
            m_cur = cm_ref[slot, hd]
        m_old = m_ref[hd]
        m_new = jnp.maximum(m_old, m_cur)
        alpha = jnp.exp2(m_old - m_new)
        p = jnp.exp2(s - _lane_tile(m_new, t // 128)).astype(BF16)
        v_ones = jnp.concatenate([v_ref[0, hd, pl.ds(keys, t), :], ones], axis=1)
        acc_ref[hd] = _lane_tile(alpha, MLA_QK_PAD // 128) * acc_ref[hd] + _dot(p, v_ones)
        m_ref[hd] = m_new

    def finish_tile(qi, hd):
        rows = pl.multiple_of((qi - qi0) * t, t)
        acc = acc_ref[hd]
        o_ref[0, pl.ds(rows, t), hd * MLA_V:(hd + 1) * MLA_V] = (
            acc[:, :MLA_V] / acc[:, MLA_V:]).astype(BF16)
        acc_ref[hd] = jnp.zeros(acc.shape, F32)
        m_ref[hd] = jnp.full((t, 128), NEG, F32)

    def following(qi, c):
        is_last = c == qi
        return jnp.where(is_last, jnp.minimum(qi + 1, qi_end), qi), jnp.where(is_last, 0, c + 1)

    def step(qi, c, slot, diagonal):
        qn, cn = following(qi, c)
        mask = None
        if diagonal:
            q_chunk = _chunk_of(qi * t + lax.broadcasted_iota(jnp.int32, (t, t), 0))
            k_chunk = _chunk_of(c * t + lax.broadcasted_iota(jnp.int32, (t, t), 1))
            mask = k_chunk <= q_chunk
        for hd in range(g):
            scores(qn, cn, 1 - slot, hd)
            update(c, slot, hd, mask)
            if diagonal:
                finish_tile(qi, hd)

    for hd in range(g):
        scores(qi0, 0, 0, hd)

    def body(_, pair):
        first, second = pair, following(*pair)
        diag0 = first[1] == first[0]
        diag1 = second[1] == second[0]
        for d0, d1, cond in ((True, False, diag0), (False, True, diag1),
                             (False, False, jnp.logical_not(jnp.logical_or(diag0, diag1)))):
            @pl.when(cond)
            def _(d0=d0, d1=d1):
                step(*first, 0, d0)
                step(*second, 1, d1)
        return following(*second)

    n_steps = nq * qi0 + nq * (nq + 1) // 2
    lax.fori_loop(0, n_steps // 2, body, (qi0, 0))


def _mla(q, k, v, t, g, nq):
    B, _, S, _ = q.shape
    assert S % (nq * t) == 0 and nq % 2 == 0 and (nq * (nq + 1) // 2) % 2 == 0
    return pl.pallas_call(
        functools.partial(_mla_kernel, t=t, g=g, nq=nq),
        grid=(B, MLA_HEADS // g, S // (nq * t)),
        in_specs=[
            pl.BlockSpec((1, g, nq * t, MLA_QK_PAD), lambda b, h, i: (b, h, i, 0)),
            pl.BlockSpec((1, g, S, MLA_QK_PAD), lambda b, h, i: (b, h, 0, 0)),
            pl.BlockSpec((1, g, S, MLA_V), lambda b, h, i: (b, h, 0, 0)),
        ],
        out_specs=pl.BlockSpec((1, nq * t, g * MLA_V), lambda b, h, i: (b, i, h)),
        out_shape=jax.ShapeDtypeStruct((B, S, MLA_WIDTH), BF16),
        scratch_shapes=[
            pltpu.VMEM((g, t, MLA_QK_PAD), F32),
            pltpu.VMEM((g, t, 128), F32),
            pltpu.VMEM((2, g, t, t), F32),
            pltpu.VMEM((2, g, t, 128), F32),
        ],
        compiler_params=_params(("arbitrary", "arbitrary", "arbitrary")),
        name="mla",
    )(q, k, v)


_REL_SPAN = 4 * SWA_BLOCK


_N_OFFSETS = 3 * SWA_BLOCK - 1


def _bias_kernel(table_ref, bucket_ref, o_ref):
    head = lax.broadcasted_iota(jnp.int32, (SWA_HEADS, 128), 0)
    lane = lax.broadcasted_iota(jnp.int32, (SWA_HEADS, _REL_SPAN), 1)

    def lookup(c, per_offset):
        b = bucket_ref[c]
        col = jnp.zeros((SWA_HEADS, 128), F32)
        for hd in range(SWA_HEADS):
            col = jnp.where(head == hd, table_ref[b, hd], col)
        return jnp.where(lane == c, _lane_tile(col, _REL_SPAN // 128), per_offset)

    per_offset_all = lax.fori_loop(0, _N_OFFSETS, lookup, jnp.zeros((SWA_HEADS, _REL_SPAN), F32))
    for hd in range(SWA_HEADS):
        per_offset = per_offset_all[hd:hd + 1, :]
        rows = jnp.broadcast_to(per_offset, (SWA_BLOCK, _REL_SPAN))
        band = pltpu.roll(rows, _REL_SPAN - (SWA_BLOCK - 1), 1, stride=1, stride_axis=0)
        kvh, within = divmod(hd, SWA_GROUP)
        par, half = within % 2, within // 2
        o_ref[kvh, par, half * SWA_BLOCK:(half + 1) * SWA_BLOCK, :] = band[:, :2 * SWA_BLOCK]


def _bias(rel_table, bucket):
    return pl.pallas_call(
        _bias_kernel,
        in_specs=[
            pl.BlockSpec(memory_space=pltpu.SMEM),
            pl.BlockSpec(memory_space=pltpu.SMEM),
        ],
        out_specs=pl.BlockSpec(memory_space=pltpu.VMEM),
        out_shape=jax.ShapeDtypeStruct((SWA_KV_HEADS, 2, 2 * SWA_BLOCK, 2 * SWA_BLOCK), F32),
        name="t5bias",
    )(rel_table, bucket)


def _memkv_kernel(mem_ref, g_ref, w_ref, k_ref, v_ref):
    mn = _rms(mem_ref[0], g_ref[...]).astype(BF16)
    kv = _dot(mn, w_ref[...])
    k_ref[0] = kv[:, :MEM_WIDTH].astype(BF16)
    v_ref[0] = kv[:, MEM_WIDTH:].astype(BF16)


def _memkv(mem, g, w):
    B, M, _ = mem.shape
    const = lambda b: (0, 0)
    row = lambda b: (b, 0, 0)
    return pl.pallas_call(
        _memkv_kernel,
        grid=(B,),
        in_specs=[
            pl.BlockSpec((1, M, D_MODEL), row),
            pl.BlockSpec((1, D_MODEL), const),
            pl.BlockSpec((D_MODEL, 2 * MEM_WIDTH), const),
        ],
        out_specs=[pl.BlockSpec((1, M, MEM_WIDTH), row)] * 2,
        out_shape=[jax.ShapeDtypeStruct((B, M, MEM_WIDTH), BF16)] * 2,
        compiler_params=_params(("arbitrary",)),
        name="memkv",
    )(mem, g, w)


def _swa_into(y_ref, sink_ref, q_ref, kp_ref, kc_ref, vp_ref, vc_ref, bias_ref, kb_ref, vb_ref, nsub,
              after_unit):
    t = pl.program_id(1)
    scale = SWA_HEAD_DIM ** -0.5
    sb = SWA_BLOCK
    kb_ref[:sb] = kp_ref[0]
    kb_ref[sb:] = kc_ref[0]
    vb_ref[:sb] = vp_ref[0]
    vb_ref[sb:] = vc_ref[0]

    row = lax.broadcasted_iota(jnp.int32, (2 * sb, 2 * sb), 0)
    q_chunk = _chunk_of(row & (sb - 1))
    b_chunk = _chunk_of(lax.broadcasted_iota(jnp.int32, (2 * sb, 2 * sb), 1))
    valid_any = jnp.logical_and(b_chunk >= q_chunk, b_chunk <= q_chunk + WINDOW_CHUNKS)
    first_lo = jnp.where(t > 0, 0, 2)
    valid_first = jnp.logical_and(valid_any, b_chunk >= first_lo)
    upper = lax.broadcasted_iota(jnp.int32, (2 * sb, 1), 0) < sb

    for r in range(nsub):
        valid = valid_first if r == 0 else valid_any
        rows = slice(r * sb, (r + 1) * sb)
        band = slice(r * sb, (r + 2) * sb)
        for kvh in range(SWA_KV_HEADS):
            pair0 = slice(2 * kvh * 128, (2 * kvh + 1) * 128)
            pair1 = slice((2 * kvh + 1) * 128, (2 * kvh + 2) * 128)
            q = jnp.concatenate([q_ref[0, rows, pair0], q_ref[0, rows, pair1]], axis=0)
            o = None
            for par in range(2):
                kcols = slice((2 * kvh + par) * 128, (2 * kvh + par + 1) * 128)
                hd = SWA_GROUP * kvh + par
                s = _dot_nt(q, kb_ref[band, kcols]) * scale + bias_ref[kvh, par]
                s = jnp.where(valid, s, NEG)
                sink = jnp.where(upper, sink_ref[hd], sink_ref[hd + 2])
                m = jnp.maximum(jnp.max(s, axis=-1, keepdims=True), sink)
                p = jnp.exp(s - m)
                den = jnp.sum(p, axis=-1, keepdims=True) + jnp.exp(sink - m)
                pv = _dot((p * (1.0 / den)).astype(BF16), vb_ref[band, kcols])
                o = pv if o is None else o + pv
            y_ref[rows, pair0] = o[:sb]
            y_ref[rows, pair1] = o[sb:]
            after_unit(r * SWA_KV_HEADS + kvh)


def _mem_into(y_ref, q_ref, k_ref, v_ref, after_head):
    scale = MEM_HEAD_DIM ** -0.5
    for hd in range(MEM_HEADS):
        cols = slice(hd * MEM_HEAD_DIM, (hd + 1) * MEM_HEAD_DIM)
        s = _dot_nt(q_ref[0, :, cols], k_ref[0, :, cols]) * scale
        m = jnp.max(s, axis=-1, keepdims=True)
        p = jnp.exp(s - m)
        inv = 1.0 / jnp.sum(p, axis=-1, keepdims=True)
        y_ref[:, cols] = _dot((p * inv).astype(BF16), v_ref[0, :, cols])
        after_head(hd)


def _tail_kernel(sink_ref, x_ref, ya_ref, za_ref, zb_ref, zc_ref, qs_ref, qm_ref,
                 kp_ref, kc_ref, vp_ref, vc_ref, bias_ref, km_ref, vm_ref, w_hbm, g_ref,
                 o_ref, kb_ref, vb_ref, yb_ref, yc_ref, yacc_ref, w_ref, stage_ref, sem_ref, *, nsub):
    @pl.when(_first_grid_step())
    def _():
        _load_weight_bf16(w_hbm, w_ref, stage_ref, sem_ref)

    def gated(y, z_ref):
        z = z_ref[0].astype(F32)
        return (y * (z * jax.nn.sigmoid(z))).astype(BF16)

    swa_lo, mem_lo = MLA_WIDTH, MLA_WIDTH + SWA_WIDTH

    g_mla = gated(ya_ref[0].astype(F32), za_ref)
    mla_chunk = D_MODEL // (nsub * SWA_KV_HEADS)

    def project_mla(u):
        cols = slice(u * mla_chunk, (u + 1) * mla_chunk)
        yacc_ref[:, cols] = _dot(g_mla, w_ref[:swa_lo, cols])

    _swa_into(yb_ref, sink_ref, qs_ref, kp_ref, kc_ref, vp_ref, vc_ref, bias_ref, kb_ref, vb_ref, nsub,
              project_mla)

    g_swa = gated(yb_ref[...], zb_ref)
    swa_chunk = D_MODEL // MEM_HEADS

    def project_swa(u):
        cols = slice(u * swa_chunk, (u + 1) * swa_chunk)
        yacc_ref[:, cols] += _dot(g_swa, w_ref[swa_lo:mem_lo, cols])

    _mem_into(yc_ref, qm_ref, km_ref, vm_ref, project_swa)

    g_mem = gated(yc_ref[...], zc_ref)
    half = g_mem.shape[0] // 2
    ys = [yacc_ref[rows, :] + _dot(g_mem[rows], w_ref[mem_lo:, :])
          for rows in (slice(0, half), slice(half, 2 * half))]
    o_ref[0, :half, :] = _rms(x_ref[0, :half, :] + ys[0], g_ref[...])
    o_ref[0, half:, :] = _rms(x_ref[0, half:, :] + ys[1], g_ref[...])


def _tail(sinks, x, y_mla, z, qs, ks, vs, bias, kmem, vmem, w, g, nsub):
    B, S, _ = x.shape
    tm = nsub * SWA_BLOCK
    M = kmem.shape[1]
    row = lambda b, i: (b, i, 0)
    prev = lambda b, i: (b, jnp.maximum(i * nsub - 1, 0), 0)
    col = lambda c: (lambda b, i: (b, i, c))
    return pl.pallas_call(
        functools.partial(_tail_kernel, nsub=nsub),
        grid=(B, S // tm),
        in_specs=[
            pl.BlockSpec(memory_space=pltpu.SMEM),
            pl.BlockSpec((1, tm, D_MODEL), row),
            pl.BlockSpec((1, tm, MLA_WIDTH), row),
            pl.BlockSpec((1, tm, MLA_WIDTH), col(0)),
            pl.BlockSpec((1, tm, SWA_WIDTH), col(MLA_WIDTH // SWA_WIDTH)),
            pl.BlockSpec((1, tm, MEM_WIDTH), col((MLA_WIDTH + SWA_WIDTH) // MEM_WIDTH)),
            pl.BlockSpec((1, tm, SWA_WIDTH), col(0)),
            pl.BlockSpec((1, tm, MEM_WIDTH), col(1)),
            pl.BlockSpec((1, SWA_BLOCK, _SWA_KV_PAD), prev),
            pl.BlockSpec((1, tm, _SWA_KV_PAD), row),
            pl.BlockSpec((1, SWA_BLOCK, _SWA_KV_PAD), prev),
            pl.BlockSpec((1, tm, _SWA_KV_PAD), row),
            pl.BlockSpec((SWA_KV_HEADS, 2, 2 * SWA_BLOCK, 2 * SWA_BLOCK), lambda b, i: (0, 0, 0, 0)),
            pl.BlockSpec((1, M, MEM_WIDTH), lambda b, i: (b, 0, 0)),
            pl.BlockSpec((1, M, MEM_WIDTH), lambda b, i: (b, 0, 0)),
            pl.BlockSpec(memory_space=pl.ANY),
            pl.BlockSpec((1, D_MODEL), lambda b, i: (0, 0)),
        ],
        out_specs=pl.BlockSpec((1, tm, D_MODEL), row),
        out_shape=jax.ShapeDtypeStruct((B, S, D_MODEL), F32),
        scratch_shapes=[
            pltpu.VMEM((tm + SWA_BLOCK, _SWA_KV_PAD), BF16),
            pltpu.VMEM((tm + SWA_BLOCK, _SWA_KV_PAD), BF16),
            pltpu.VMEM((tm, SWA_WIDTH), F32),
            pltpu.VMEM((tm, MEM_WIDTH), F32),
            pltpu.VMEM((tm, D_MODEL), F32),
            pltpu.VMEM((MIX_WIDTH, D_MODEL), BF16),
            pltpu.VMEM((2, _W_OUT_CHUNK, D_MODEL), F32),
            pltpu.SemaphoreType.DMA((2,)),
        ],
        compiler_params=_params(("arbitrary", "arbitrary")),
        name="tail",
    )(sinks, x, y_mla, z, z, z, qs, qs, ks, ks, vs, vs, bias, kmem, vmem, w, g)


def _rope_tables(seq):
    inv = 1.0 / (ROPE_THETA ** (jnp.arange(0, MLA_ROPE, 2, dtype=F32) / MLA_ROPE))
    ang = jnp.arange(seq, dtype=F32)[:, None] * inv[None, :]
    cos, sin = jnp.cos(ang), jnp.sin(ang)
    zero = jnp.zeros_like(cos)
    return (jnp.concatenate([cos, cos, zero, zero], axis=-1),
            jnp.concatenate([-sin, sin, zero, zero], axis=-1))


def _t5_bucket(rel):
    nb = N_BUCKETS // 2
    max_exact = nb // 2
    bucket = jnp.where(rel > 0, nb, 0)
    n = jnp.abs(rel)
    nf = jnp.maximum(n, 1).astype(F32)
    large = max_exact + (jnp.log(nf / max_exact) / math.log(MAX_DISTANCE / max_exact)
                         * (nb - max_exact)).astype(jnp.int32)
    large = jnp.minimum(large, nb - 1)
    return bucket + jnp.where(n < max_exact, n, large)


def kernel(x, mem, norm_in, w_in, norm_q, norm_kv, w_uq, w_ukv, attn_sinks, rel_bias,
           norm_mem, w_mem_kv, w_out, norm_final):
    B, S, _ = x.shape
    assert norm_in.shape[0] == 1, "single-layer trunk"

    wt = w_in[0].T
    cos_t, sin_t = _rope_tables(S)
    q, k, v, qs, z, ks, vs = _proj(x, norm_in, wt, norm_q, norm_kv, w_uq[0].astype(BF16),
                                   w_ukv[0].astype(BF16), cos_t, sin_t, tm=256)

    y_mla = _mla(q, k, v, t=512, g=4, nq=4)

    bucket = _t5_bucket(jnp.arange(_REL_SPAN) - (2 * SWA_BLOCK - 1)).astype(jnp.int32)
    bias = _bias(rel_bias, bucket)
    kmem, vmem = _memkv(mem, norm_mem, w_mem_kv[0].astype(BF16))
    return _tail(attn_sinks[0], x, y_mla, z, qs, ks, vs, bias, kmem, vmem,
                 w_out[0], norm_final[None, :], nsub=4)
```

```python
import functools
import math

import jax
import jax.numpy as jnp
from jax import lax
from jax.experimental import pallas as pl
from jax.experimental.pallas import tpu as pltpu

D_MODEL = 2048
CHUNK = 64
N_MEM = 256
EPS = 1e-6
NEG = -1e30

MLA_HEADS = 8
MLA_NOPE = 128
MLA_ROPE = 64
MLA_V = 128
MLA_QLORA = 512
MLA_KVLORA = 256
MLA_WIDTH = MLA_HEADS * MLA_V
MLA_QK_PAD = 256
ROPE_THETA = 10000.0

SWA_HEADS = 8
SWA_KV_HEADS = 2
SWA_GROUP = SWA_HEADS // SWA_KV_HEADS
SWA_HEAD_DIM = 64
SWA_WIDTH = SWA_HEADS * SWA_HEAD_DIM
SWA_KV_WIDTH = SWA_KV_HEADS * SWA_HEAD_DIM
WINDOW_CHUNKS = 2
SWA_BLOCK = 128

MEM_HEADS = 4
MEM_HEAD_DIM = 128
MEM_WIDTH = MEM_HEADS * MEM_HEAD_DIM

MIX_WIDTH = MLA_WIDTH + SWA_WIDTH + MEM_WIDTH

N_BUCKETS = 32
MAX_DISTANCE = 128

BF16 = jnp.bfloat16
F32 = jnp.float32

_R_CQ = 0
_R_CKV = _R_CQ + MLA_QLORA
_R_KPE = _R_CKV + MLA_KVLORA
_R_ZMLA = _R_KPE + MLA_ROPE
_R_QSWA = _R_ZMLA + MLA_WIDTH
_R_KSWA = _R_QSWA + SWA_WIDTH
_R_VSWA = _R_KSWA + SWA_KV_WIDTH
_R_ZSWA = _R_VSWA + SWA_KV_WIDTH
_R_QMEM = _R_ZSWA + SWA_WIDTH
_R_ZMEM = _R_QMEM + MEM_WIDTH
IN_WIDTH = _R_ZMEM + MEM_WIDTH

_MLA_Q_SCALE = (MLA_NOPE + MLA_ROPE) ** -0.5 * math.log2(math.e)

_SWA_KV_PAD = 2 * SWA_KV_HEADS * 128

_W_IN_CHUNK = 208
_W_OUT_CHUNK = 256

_VMEM_LIMIT = 56 * 1024 * 1024


def _params(sem, flags=None):
    return pltpu.CompilerParams(dimension_semantics=sem, vmem_limit_bytes=_VMEM_LIMIT, flags=flags)


def _rms(v, g):
    return (v * lax.rsqrt(jnp.mean(v * v, axis=-1, keepdims=True) + EPS)) * g


def _dot(a, b):
    return jnp.dot(a, b, preferred_element_type=F32)


def _dot_nt(a, b):
    return lax.dot_general(a, b, (((1,), (1,)), ((), ())), preferred_element_type=F32)


def _load_weight_bf16(w_hbm, w_ref, stage_ref, sem_ref):
    rows = stage_ref.shape[1]
    n_chunks = w_ref.shape[0] // rows

    def chunk_copy(c):
        return pltpu.make_async_copy(w_hbm.at[pl.ds(c * rows, rows), :], stage_ref.at[c % 2],
                                     sem_ref.at[c % 2])

    chunk_copy(0).start()
    for c in range(n_chunks):
        if c + 1 < n_chunks:
            chunk_copy(c + 1).start()
        chunk_copy(c).wait()
        w_ref[c * rows:(c + 1) * rows, :] = stage_ref[c % 2].astype(BF16)


def _first_grid_step():
    return jnp.logical_and(pl.program_id(0) == 0, pl.program_id(1) == 0)


def _lane_tile(v, n):
    return jnp.concatenate([v] * n, axis=1)


def _chunk_of(pos):
    return jnp.right_shift(pos, CHUNK.bit_length() - 1)


def _rope128(v, cos_t, sin_t):
    return v * cos_t + pltpu.roll(v, MLA_ROPE // 2, 1) * sin_t


def _store_swa_kv(ref, kv):
    low = lax.broadcasted_iota(jnp.int32, kv.shape, 1) < SWA_HEAD_DIM
    swapped = pltpu.roll(kv, SWA_HEAD_DIM, 1)
    pieces = (jnp.where(low, kv, 0.0), jnp.where(low, 0.0, swapped),
              jnp.where(low, swapped, 0.0), jnp.where(low, 0.0, kv))
    for i, piece in enumerate(pieces):
        ref[0, :, i * 128:(i + 1) * 128] = piece.astype(BF16)


def _proj_kernel(x_ref, gin_ref, wt_hbm, gq_ref, gkv_ref, wuq_ref, wukv_ref, cos_ref, sin_ref,
                 q_ref, k_ref, v_ref, qs_ref, z_ref, ks_ref, vs_ref, wt_ref, stage_ref, sem_ref):
    @pl.when(_first_grid_step())
    def _():
        _load_weight_bf16(wt_hbm, wt_ref, stage_ref, sem_ref)

    h = _rms(x_ref[0], gin_ref[...]).astype(BF16)
    cos_t = cos_ref[...]
    sin_t = sin_ref[...]

    def proj(lo, hi):
        return _dot_nt(h, wt_ref[lo:hi, :])

    pa = proj(_R_CQ, _R_KPE)
    cq = _rms(pa[:, :MLA_QLORA], gq_ref[...]).astype(BF16)
    ckv = _rms(pa[:, MLA_QLORA:], gkv_ref[...]).astype(BF16)

    qs_ref[0, :, :SWA_WIDTH] = proj(_R_QSWA, _R_KSWA).astype(BF16)
    qs_ref[0, :, SWA_WIDTH:] = proj(_R_QMEM, _R_ZMEM).astype(BF16)
    z_ref[0, :, :MLA_WIDTH] = proj(_R_ZMLA, _R_QSWA).astype(BF16)
    z_ref[0, :, MLA_WIDTH:MLA_WIDTH + SWA_WIDTH] = proj(_R_ZSWA, _R_QMEM).astype(BF16)
    z_ref[0, :, MLA_WIDTH + SWA_WIDTH:] = proj(_R_ZMEM, IN_WIDTH).astype(BF16)
    kvs = proj(_R_KSWA, _R_ZSWA)
    _store_swa_kv(ks_ref, kvs[:, :SWA_KV_WIDTH])
    _store_swa_kv(vs_ref, kvs[:, SWA_KV_WIDTH:])
    pe = proj(_R_KPE, _R_ZMLA)
    kpe = _rope128(jnp.concatenate([pe, pe], axis=1), cos_t, sin_t).astype(BF16)

    qall = _dot(cq, wuq_ref[...]) * _MLA_Q_SCALE
    kv = _dot(ckv, wukv_ref[...])
    for hd in range(MLA_HEADS):
        oq = hd * (MLA_NOPE + MLA_ROPE)
        q_pe = qall[:, oq + MLA_NOPE:oq + MLA_NOPE + MLA_ROPE]
        q_ref[0, hd, :, :MLA_NOPE] = qall[:, oq:oq + MLA_NOPE].astype(BF16)
        q_ref[0, hd, :, MLA_NOPE:] = _rope128(
            jnp.concatenate([q_pe, q_pe], axis=1), cos_t, sin_t).astype(BF16)
        o = hd * (MLA_NOPE + MLA_V)
        k_ref[0, hd, :, :MLA_NOPE] = kv[:, o:o + MLA_NOPE].astype(BF16)
        k_ref[0, hd, :, MLA_NOPE:] = kpe
        v_ref[0, hd] = kv[:, o + MLA_NOPE:o + MLA_QK_PAD].astype(BF16)


def _proj(x, gin, wt, gq, gkv, wuq, wukv, cos_t, sin_t, tm):
    B, S, _ = x.shape
    const = lambda b, i: (0, 0)
    row = lambda b, i: (b, i, 0)
    single = pl.Buffered(1)

    def out(width):
        return jax.ShapeDtypeStruct((B, S, width), BF16)

    head_spec = pl.BlockSpec((1, MLA_HEADS, tm, MLA_QK_PAD), lambda b, i: (b, 0, i, 0))
    head_shape = jax.ShapeDtypeStruct((B, MLA_HEADS, S, MLA_QK_PAD), BF16)

    return pl.pallas_call(
        _proj_kernel,
        grid=(B, S // tm),
        in_specs=[
            pl.BlockSpec((1, tm, D_MODEL), row),
            pl.BlockSpec((1, D_MODEL), const),
            pl.BlockSpec(memory_space=pl.ANY),
            pl.BlockSpec((1, MLA_QLORA), const),
            pl.BlockSpec((1, MLA_KVLORA), const),
            pl.BlockSpec((MLA_QLORA, MLA_HEADS * (MLA_NOPE + MLA_ROPE)), const, pipeline_mode=single),
            pl.BlockSpec((MLA_KVLORA, MLA_HEADS * (MLA_NOPE + MLA_V)), const, pipeline_mode=single),
            pl.BlockSpec((tm, 128), lambda b, i: (i, 0)),
            pl.BlockSpec((tm, 128), lambda b, i: (i, 0)),
        ],
        out_specs=[
            head_spec, head_spec,
            pl.BlockSpec((1, MLA_HEADS, tm, MLA_V), lambda b, i: (b, 0, i, 0)),
            pl.BlockSpec((1, tm, SWA_WIDTH + MEM_WIDTH), row),
            pl.BlockSpec((1, tm, MIX_WIDTH), row),
            pl.BlockSpec((1, tm, _SWA_KV_PAD), row),
            pl.BlockSpec((1, tm, _SWA_KV_PAD), row),
        ],
        out_shape=[
            head_shape, head_shape, jax.ShapeDtypeStruct((B, MLA_HEADS, S, MLA_V), BF16),
            out(SWA_WIDTH + MEM_WIDTH), out(MIX_WIDTH), out(_SWA_KV_PAD), out(_SWA_KV_PAD),
        ],
        scratch_shapes=[
            pltpu.VMEM((IN_WIDTH, D_MODEL), BF16),
            pltpu.VMEM((2, _W_IN_CHUNK, D_MODEL), F32),
            pltpu.SemaphoreType.DMA((2,)),
        ],
        compiler_params=_params(("arbitrary", "arbitrary")),
        name="proj",
    )(x, gin, wt, gq, gkv, wuq, wukv, cos_t, sin_t)


def _mla_kernel(q_ref, k_ref, v_ref, o_ref, acc_ref, m_ref, s_ref, cm_ref, *, t, g, nq):
    qi0 = pl.program_id(2) * nq
    qi_end = qi0 + nq - 1
    acc_ref[...] = jnp.zeros(acc_ref.shape, F32)
    m_ref[...] = jnp.full(m_ref.shape, NEG, F32)
    ones = jnp.ones((t, MLA_V), BF16)

    def row_max(s):
        return jnp.broadcast_to(jnp.max(s, axis=-1, keepdims=True), (t, 128))

    def scores(qi, c, slot, hd):
        rows = pl.multiple_of((qi - qi0) * t, t)
        keys = pl.multiple_of(c * t, t)
        s = _dot_nt(q_ref[0, hd, pl.ds(rows, t), :], k_ref[0, hd, pl.ds(keys, t), :])
        s_ref[slot, hd] = s
        cm_ref[slot, hd] = row_max(s)

    def update(c, slot, hd, mask):
        keys = pl.multiple_of(c * t, t)
        s = s_ref[slot, hd]
        if mask is not None:
            s = jnp.where(mask, s, NEG)
            m_cur = row_max(s)
        else:
            m_cur = cm_ref[slot, hd]
        m_old = m_ref[hd]
        m_new = jnp.maximum(m_old, m_cur)
        alpha = jnp.exp2(m_old - m_new)
        p = jnp.exp2(s - _lane_tile(m_new, t // 128)).astype(BF16)
        v_ones = jnp.concatenate([v_ref[0, hd, pl.ds(keys, t), :], ones], axis=1)
        acc_ref[hd] = _lane_tile(alpha, MLA_QK_PAD // 128) * acc_ref[hd] + _dot(p, v_ones)
        m_ref[hd] = m_new

    def finish_tile(qi, hd):
        rows = pl.multiple_of((qi - qi0) * t, t)
        acc = acc_ref[hd]
        o_ref[0, pl.ds(rows, t), hd * MLA_V:(hd + 1) * MLA_V] = (
            acc[:, :MLA_V] / acc[:, MLA_V:]).astype(BF16)
        acc_ref[hd] = jnp.zeros(acc.shape, F32)
        m_ref[hd] = jnp.full((t, 128), NEG, F32)

    def following(qi, c):
        is_last = c == qi
        return jnp.where(is_last, jnp.minimum(qi + 1, qi_end), qi), jnp.where(is_last, 0, c + 1)

    def step(qi, c, slot, diagonal):
        qn, cn = following(qi, c)
        mask = None
        if diagonal:
            q_chunk = _chunk_of(qi * t + lax.broadcasted_iota(jnp.int32, (t, t), 0))
            k_chunk = _chunk_of(c * t + lax.broadcasted_iota(jnp.int32, (t, t), 1))
            mask = k_chunk <= q_chunk
        for hd in range(g):
            scores(qn, cn, 1 - slot, hd)
            update(c, slot, hd, mask)
            if diagonal:
                finish_tile(qi, hd)

    for hd in range(g):
        scores(qi0, 0, 0, hd)

    def body(_, pair):
        first, second = pair, following(*pair)
        diag0 = first[1] == first[0]
        diag1 = second[1] == second[0]
        for d0, d1, cond in ((True, False, diag0), (False, True, diag1),
                             (False, False, jnp.logical_not(jnp.logical_or(diag0, diag1)))):
            @pl.when(cond)
            def _(d0=d0, d1=d1):
                step(*first, 0, d0)
                step(*second, 1, d1)
        return following(*second)

    n_steps = nq * qi0 + nq * (nq + 1) // 2
    lax.fori_loop(0, n_steps // 2, body, (qi0, 0))


def _mla(q, k, v, t, g, nq):
    B, _, S, _ = q.shape
    assert S % (nq * t) == 0 and nq % 2 == 0 and (nq * (nq + 1) // 2) % 2 == 0
    return pl.pallas_call(
        functools.partial(_mla_kernel, t=t, g=g, nq=nq),
        grid=(B, MLA_HEADS // g, S // (nq * t)),
        in_specs=[
            pl.BlockSpec((1, g, nq * t, MLA_QK_PAD), lambda b, h, i: (b, h, i, 0)),
            pl.BlockSpec((1, g, S, MLA_QK_PAD), lambda b, h, i: (b, h, 0, 0)),
            pl.BlockSpec((1, g, S, MLA_V), lambda b, h, i: (b, h, 0, 0)),
        ],
        out_specs=pl.BlockSpec((1, nq * t, g * MLA_V), lambda b, h, i: (b, i, h)),
        out_shape=jax.ShapeDtypeStruct((B, S, MLA_WIDTH), BF16),
        scratch_shapes=[
            pltpu.VMEM((g, t, MLA_QK_PAD), F32),
            pltpu.VMEM((g, t, 128), F32),
            pltpu.VMEM((2, g, t, t), F32),
            pltpu.VMEM((2, g, t, 128), F32),
        ],
        compiler_params=_params(("arbitrary", "arbitrary", "arbitrary")),
        name="mla",
    )(q, k, v)


_REL_SPAN = 4 * SWA_BLOCK


_N_OFFSETS = 3 * SWA_BLOCK - 1


def _bias_kernel(table_ref, bucket_ref, o_ref):
    head = lax.broadcasted_iota(jnp.int32, (SWA_HEADS, 128), 0)
    lane = lax.broadcasted_iota(jnp.int32, (SWA_HEADS, _REL_SPAN), 1)

    def lookup(c, per_offset):
        b = bucket_ref[c]
        col = jnp.zeros((SWA_HEADS, 128), F32)
        for hd in range(SWA_HEADS):
            col = jnp.where(head == hd, table_ref[b, hd], col)
        return jnp.where(lane == c, _lane_tile(col, _REL_SPAN // 128), per_offset)

    per_offset_all = lax.fori_loop(0, _N_OFFSETS, lookup, jnp.zeros((SWA_HEADS, _REL_SPAN), F32))
    for hd in range(SWA_HEADS):
        per_offset = per_offset_all[hd:hd + 1, :]
        rows = jnp.broadcast_to(per_offset, (SWA_BLOCK, _REL_SPAN))
        band = pltpu.roll(rows, _REL_SPAN - (SWA_BLOCK - 1), 1, stride=1, stride_axis=0)
        kvh, within = divmod(hd, SWA_GROUP)
        par, half = within % 2, within // 2
        o_ref[kvh, par, half * SWA_BLOCK:(half + 1) * SWA_BLOCK, :] = band[:, :2 * SWA_BLOCK]


def _bias(rel_table, bucket):
    return pl.pallas_call(
        _bias_kernel,
        in_specs=[
            pl.BlockSpec(memory_space=pltpu.SMEM),
            pl.BlockSpec(memory_space=pltpu.SMEM),
        ],
        out_specs=pl.BlockSpec(memory_space=pltpu.VMEM),
        out_shape=jax.ShapeDtypeStruct((SWA_KV_HEADS, 2, 2 * SWA_BLOCK, 2 * SWA_BLOCK), F32),
        name="t5bias",
    )(rel_table, bucket)


def _memkv_kernel(mem_ref, g_ref, w_ref, k_ref, v_ref):
    mn = _rms(mem_ref[0], g_ref[...]).astype(BF16)
    kv = _dot(mn, w_ref[...])
    k_ref[0] = kv[:, :MEM_WIDTH].astype(BF16)
    v_ref[0] = kv[:, MEM_WIDTH:].astype(BF16)


def _memkv(mem, g, w):
    B, M, _ = mem.shape
    const = lambda b: (0, 0)
    row = lambda b: (b, 0, 0)
    return pl.pallas_call(
        _memkv_kernel,
        grid=(B,),
        in_specs=[
            pl.BlockSpec((1, M, D_MODEL), row),
            pl.BlockSpec((1, D_MODEL), const),
            pl.BlockSpec((D_MODEL, 2 * MEM_WIDTH), const),
        ],
        out_specs=[pl.BlockSpec((1, M, MEM_WIDTH), row)] * 2,
        out_shape=[jax.ShapeDtypeStruct((B, M, MEM_WIDTH), BF16)] * 2,
        compiler_params=_params(("arbitrary",)),
        name="memkv",
    )(mem, g, w)


def _swa_into(y_ref, sink_ref, q_ref, kp_ref, kc_ref, vp_ref, vc_ref, bias_ref, kb_ref, vb_ref, nsub,
              after_unit):
    t = pl.program_id(1)
    scale = SWA_HEAD_DIM ** -0.5
    sb = SWA_BLOCK
    kb_ref[:sb] = kp_ref[0]
    kb_ref[sb:] = kc_ref[0]
    vb_ref[:sb] = vp_ref[0]
    vb_ref[sb:] = vc_ref[0]

    row = lax.broadcasted_iota(jnp.int32, (2 * sb, 2 * sb), 0)
    q_chunk = _chunk_of(row & (sb - 1))
    b_chunk = _chunk_of(lax.broadcasted_iota(jnp.int32, (2 * sb, 2 * sb), 1))
    valid_any = jnp.logical_and(b_chunk >= q_chunk, b_chunk <= q_chunk + WINDOW_CHUNKS)
    first_lo = jnp.where(t > 0, 0, 2)
    valid_first = jnp.logical_and(valid_any, b_chunk >= first_lo)
    upper = lax.broadcasted_iota(jnp.int32, (2 * sb, 1), 0) < sb

    def geometry(u):
        r, kvh = divmod(u, SWA_KV_HEADS)
        rows = slice(r * sb, (r + 1) * sb)
        band = slice(r * sb, (r + 2) * sb)
        pair0 = slice(2 * kvh * 128, (2 * kvh + 1) * 128)
        pair1 = slice((2 * kvh + 1) * 128, (2 * kvh + 2) * 128)
        return r, kvh, rows, band, pair0, pair1

    def scores(u):
        r, kvh, rows, band, pair0, pair1 = geometry(u)
        q = jnp.concatenate([q_ref[0, rows, pair0], q_ref[0, rows, pair1]], axis=0)
        return [_dot_nt(q, kb_ref[band, (2 * kvh + par) * 128:(2 * kvh + par + 1) * 128])
                for par in range(2)]

    def attend(u, unit_scores):
        r, kvh, rows, band, pair0, pair1 = geometry(u)
        valid = valid_first if r == 0 else valid_any
        o = None
        for par, s in enumerate(unit_scores):
            kcols = slice((2 * kvh + par) * 128, (2 * kvh + par + 1) * 128)
            hd = SWA_GROUP * kvh + par
            s = jnp.where(valid, s * scale + bias_ref[kvh, par], NEG)
            sink = jnp.where(upper, sink_ref[hd], sink_ref[hd + 2])
            m = jnp.maximum(jnp.max(s, axis=-1, keepdims=True), sink)
            p = jnp.exp(s - m)
            den = jnp.sum(p, axis=-1, keepdims=True) + jnp.exp(sink - m)
            pv = _dot((p * (1.0 / den)).astype(BF16), vb_ref[band, kcols])
            o = pv if o is None else o + pv
        y_ref[rows, pair0] = o[:sb]
        y_ref[rows, pair1] = o[sb:]

    n_units = nsub * SWA_KV_HEADS
    pending = scores(0)
    for u in range(n_units):
        upcoming = scores(u + 1) if u + 1 < n_units else None
        attend(u, pending)
        after_unit(u)
        pending = upcoming


def _mem_into(y_ref, q_ref, k_ref, v_ref, after_head):
    scale = MEM_HEAD_DIM ** -0.5

    def head_cols(hd):
        return slice(hd * MEM_HEAD_DIM, (hd + 1) * MEM_HEAD_DIM)

    def scores(hd):
        return _dot_nt(q_ref[0, :, head_cols(hd)], k_ref[0, :, head_cols(hd)])

    pending = scores(0)
    for hd in range(MEM_HEADS):
        upcoming = scores(hd + 1) if hd + 1 < MEM_HEADS else None
        s = pending * scale
        m = jnp.max(s, axis=-1, keepdims=True)
        p = jnp.exp(s - m)
        inv = 1.0 / jnp.sum(p, axis=-1, keepdims=True)
        y_ref[:, head_cols(hd)] = _dot((p * inv).astype(BF16), v_ref[0, :, head_cols(hd)])
        after_head(hd)
        pending = upcoming


def _tail_kernel(sink_ref, x_ref, ya_ref, za_ref, zb_ref, zc_ref, qs_ref, qm_ref,
                 kp_ref, kc_ref, vp_ref, vc_ref, bias_ref, km_ref, vm_ref, w_hbm, g_ref,
                 o_ref, kb_ref, vb_ref, yb_ref, yc_ref, yacc_ref, w_ref, stage_ref, sem_ref, *, nsub):
    @pl.when(_first_grid_step())
    def _():
        _load_weight_bf16(w_hbm, w_ref, stage_ref, sem_ref)

    def gated(y, z_ref):
        h = 0.5 * z_ref[0].astype(F32)
        return (y * (h + h * jnp.tanh(h))).astype(BF16)

    swa_lo, mem_lo = MLA_WIDTH, MLA_WIDTH + SWA_WIDTH

    g_mla = gated(ya_ref[0].astype(F32), za_ref)
    mla_chunk = D_MODEL // (nsub * SWA_KV_HEADS)

    def project_mla(u):
        cols = slice(u * mla_chunk, (u + 1) * mla_chunk)
        yacc_ref[:, cols] = _dot(g_mla, w_ref[:swa_lo, cols])

    _swa_into(yb_ref, sink_ref, qs_ref, kp_ref, kc_ref, vp_ref, vc_ref, bias_ref, kb_ref, vb_ref, nsub,
              project_mla)

    g_swa = gated(yb_ref[...], zb_ref)
    swa_chunk = D_MODEL // MEM_HEADS

    def project_swa(u):
        cols = slice(u * swa_chunk, (u + 1) * swa_chunk)
        yacc_ref[:, cols] += _dot(g_swa, w_ref[swa_lo:mem_lo, cols])

    _mem_into(yc_ref, qm_ref, km_ref, vm_ref, project_swa)

    g_mem = gated(yc_ref[...], zc_ref)
    sum_sq = jnp.zeros((g_mem.shape[0], 1), F32)
    chunks = [slice(j * swa_chunk, (j + 1) * swa_chunk) for j in range(D_MODEL // swa_chunk)]
    for cols in chunks:
        r = x_ref[0, :, cols] + yacc_ref[:, cols] + _dot(g_mem, w_ref[mem_lo:, cols])
        o_ref[0, :, cols] = r
        sum_sq = sum_sq + jnp.sum(r * r, axis=-1, keepdims=True)
    inv = lax.rsqrt(sum_sq * (1.0 / D_MODEL) + EPS)
    for cols in chunks:
        o_ref[0, :, cols] = (o_ref[0, :, cols] * inv) * g_ref[:, cols]


def _tail(sinks, x, y_mla, z, qs, ks, vs, bias, kmem, vmem, w, g, nsub):
    B, S, _ = x.shape
    tm = nsub * SWA_BLOCK
    M = kmem.shape[1]
    row = lambda b, i: (b, i, 0)
    prev = lambda b, i: (b, jnp.maximum(i * nsub - 1, 0), 0)
    col = lambda c: (lambda b, i: (b, i, c))
    return pl.pallas_call(
        functools.partial(_tail_kernel, nsub=nsub),
        grid=(B, S // tm),
        in_specs=[
            pl.BlockSpec(memory_space=pltpu.SMEM),
            pl.BlockSpec((1, tm, D_MODEL), row),
            pl.BlockSpec((1, tm, MLA_WIDTH), row),
            pl.BlockSpec((1, tm, MLA_WIDTH), col(0)),
            pl.BlockSpec((1, tm, SWA_WIDTH), col(MLA_WIDTH // SWA_WIDTH)),
            pl.BlockSpec((1, tm, MEM_WIDTH), col((MLA_WIDTH + SWA_WIDTH) // MEM_WIDTH)),
            pl.BlockSpec((1, tm, SWA_WIDTH), col(0)),
            pl.BlockSpec((1, tm, MEM_WIDTH), col(1)),
            pl.BlockSpec((1, SWA_BLOCK, _SWA_KV_PAD), prev),
            pl.BlockSpec((1, tm, _SWA_KV_PAD), row),
            pl.BlockSpec((1, SWA_BLOCK, _SWA_KV_PAD), prev),
            pl.BlockSpec((1, tm, _SWA_KV_PAD), row),
            pl.BlockSpec((SWA_KV_HEADS, 2, 2 * SWA_BLOCK, 2 * SWA_BLOCK), lambda b, i: (0, 0, 0, 0)),
            pl.BlockSpec((1, M, MEM_WIDTH), lambda b, i: (b, 0, 0)),
            pl.BlockSpec((1, M, MEM_WIDTH), lambda b, i: (b, 0, 0)),
            pl.BlockSpec(memory_space=pl.ANY),
            pl.BlockSpec((1, D_MODEL), lambda b, i: (0, 0)),
        ],
        out_specs=pl.BlockSpec((1, tm, D_MODEL), row),
        out_shape=jax.ShapeDtypeStruct((B, S, D_MODEL), F32),
        scratch_shapes=[
            pltpu.VMEM((tm + SWA_BLOCK, _SWA_KV_PAD), BF16),
            pltpu.VMEM((tm + SWA_BLOCK, _SWA_KV_PAD), BF16),
            pltpu.VMEM((tm, SWA_WIDTH), F32),
            pltpu.VMEM((tm, MEM_WIDTH), F32),
            pltpu.VMEM((tm, D_MODEL), F32),
            pltpu.VMEM((MIX_WIDTH, D_MODEL), BF16),
            pltpu.VMEM((2, _W_OUT_CHUNK, D_MODEL), F32),
            pltpu.SemaphoreType.DMA((2,)),
        ],
        compiler_params=_params(("arbitrary", "arbitrary")),
        name="tail",
    )(sinks, x, y_mla, z, z, z, qs, qs, ks, ks, vs, vs, bias, kmem, vmem, w, g)


def _rope_tables(seq):
    inv = 1.0 / (ROPE_THETA ** (jnp.arange(0, MLA_ROPE, 2, dtype=F32) / MLA_ROPE))
    ang = jnp.arange(seq, dtype=F32)[:, None] * inv[None, :]
    cos, sin = jnp.cos(ang), jnp.sin(ang)
    zero = jnp.zeros_like(cos)
    return (jnp.concatenate([cos, cos, zero, zero], axis=-1),
            jnp.concatenate([-sin, sin, zero, zero], axis=-1))


def _t5_bucket(rel):
    nb = N_BUCKETS // 2
    max_exact = nb // 2
    bucket = jnp.where(rel > 0, nb, 0)
    n = jnp.abs(rel)
    nf = jnp.maximum(n, 1).astype(F32)
    large = max_exact + (jnp.log(nf / max_exact) / math.log(MAX_DISTANCE / max_exact)
                         * (nb - max_exact)).astype(jnp.int32)
    large = jnp.minimum(large, nb - 1)
    return bucket + jnp.where(n < max_exact, n, large)


def kernel(x, mem, norm_in, w_in, norm_q, norm_kv, w_uq, w_ukv, attn_sinks, rel_bias,
           norm_mem, w_mem_kv, w_out, norm_final):
    B, S, _ = x.shape
    assert norm_in.shape[0] == 1, "single-layer trunk"

    wt = w_in[0].T
    cos_t, sin_t = _rope_tables(S)
    q, k, v, qs, z, ks, vs = _proj(x, norm_in, wt, norm_q, norm_kv, w_uq[0].astype(BF16),
                                   w_ukv[0].astype(BF16), cos_t, sin_t, tm=256)

    y_mla = _mla(q, k, v, t=512, g=4, nq=4)

    bucket = _t5_bucket(jnp.arange(_REL_SPAN) - (2 * SWA_BLOCK - 1)).astype(jnp.int32)
    bias = _bias(rel_bias, bucket)
    kmem, vmem = _memkv(mem, norm_mem, w_mem_kv[0].astype(BF16))
    return _tail(attn_sinks[0], x, y_mla, z, qs, ks, vs, bias, kmem, vmem,
                 w_out[0], norm_final[None, :], nsub=4)
```

```python
import functools
import math

import jax
import jax.numpy as jnp
from jax import lax
from jax.experimental import pallas as pl
from jax.experimental.pallas import tpu as pltpu

D_MODEL = 2048
CHUNK = 64
N_MEM = 256
EPS = 1e-6
NEG = -1e30

MLA_HEADS = 8
MLA_NOPE = 128
MLA_ROPE = 64
MLA_V = 128
MLA_QLORA = 512
MLA_KVLORA = 256
MLA_WIDTH = MLA_HEADS * MLA_V
MLA_QK_PAD = 256
ROPE_THETA = 10000.0

SWA_HEADS = 8
SWA_KV_HEADS = 2
SWA_GROUP = SWA_HEADS // SWA_KV_HEADS
SWA_HEAD_DIM = 64
SWA_WIDTH = SWA_HEADS * SWA_HEAD_DIM
SWA_KV_WIDTH = SWA_KV_HEADS * SWA_HEAD_DIM
WINDOW_CHUNKS = 2
SWA_BLOCK = 128

MEM_HEADS = 4
MEM_HEAD_DIM = 128
MEM_WIDTH = MEM_HEADS * MEM_HEAD_DIM

MIX_WIDTH = MLA_WIDTH + SWA_WIDTH + MEM_WIDTH

N_BUCKETS = 32
MAX_DISTANCE = 128

BF16 = jnp.bfloat16
F32 = jnp.float32

_R_CQ = 0
_R_CKV = _R_CQ + MLA_QLORA
_R_KPE = _R_CKV + MLA_KVLORA
_R_ZMLA = _R_KPE + MLA_ROPE
_R_QSWA = _R_ZMLA + MLA_WIDTH
_R_KSWA = _R_QSWA + SWA_WIDTH
_R_VSWA = _R_KSWA + SWA_KV_WIDTH
_R_ZSWA = _R_VSWA + SWA_KV_WIDTH
_R_QMEM = _R_ZSWA + SWA_WIDTH
_R_ZMEM = _R_QMEM + MEM_WIDTH
IN_WIDTH = _R_ZMEM + MEM_WIDTH

_MLA_Q_SCALE = (MLA_NOPE + MLA_ROPE) ** -0.5 * math.log2(math.e)

_SWA_KV_PAD = 2 * SWA_KV_HEADS * 128

_W_IN_CHUNK = 208
_W_OUT_CHUNK = 256

_VMEM_LIMIT = 56 * 1024 * 1024


def _params(sem, flags=None):
    return pltpu.CompilerParams(dimension_semantics=sem, vmem_limit_bytes=_VMEM_LIMIT, flags=flags)


def _rms(v, g):
    return (v * lax.rsqrt(jnp.mean(v * v, axis=-1, keepdims=True) + EPS)) * g


def _dot(a, b):
    return jnp.dot(a, b, preferred_element_type=F32)


def _dot_nt(a, b):
    return lax.dot_general(a, b, (((1,), (1,)), ((), ())), preferred_element_type=F32)


def _load_weight_bf16(w_hbm, w_ref, stage_ref, sem_ref):
    rows = stage_ref.shape[1]
    n_chunks = w_ref.shape[0] // rows

    def chunk_copy(c):
        return pltpu.make_async_copy(w_hbm.at[pl.ds(c * rows, rows), :], stage_ref.at[c % 2],
                                     sem_ref.at[c % 2])

    chunk_copy(0).start()
    for c in range(n_chunks):
        if c + 1 < n_chunks:
            chunk_copy(c + 1).start()
        chunk_copy(c).wait()
        w_ref[c * rows:(c + 1) * rows, :] = stage_ref[c % 2].astype(BF16)


def _first_grid_step():
    return jnp.logical_and(pl.program_id(0) == 0, pl.program_id(1) == 0)


def _lane_tile(v, n):
    return jnp.concatenate([v] * n, axis=1)


def _chunk_of(pos):
    return jnp.right_shift(pos, CHUNK.bit_length() - 1)


def _rope128(v, cos_t, sin_t):
    return v * cos_t + pltpu.roll(v, MLA_ROPE // 2, 1) * sin_t


def _store_swa_kv(ref, kv):
    low = lax.broadcasted_iota(jnp.int32, kv.shape, 1) < SWA_HEAD_DIM
    swapped = pltpu.roll(kv, SWA_HEAD_DIM, 1)
    pieces = (jnp.where(low, kv, 0.0), jnp.where(low, 0.0, swapped),
              jnp.where(low, swapped, 0.0), jnp.where(low, 0.0, kv))
    for i, piece in enumerate(pieces):
        ref[0, :, i * 128:(i + 1) * 128] = piece.astype(BF16)


def _proj_kernel(x_ref, gin_ref, wt_hbm, gq_ref, gkv_ref, wuq_ref, wukv_ref, cos_ref, sin_ref,
                 q_ref, k_ref, v_ref, qs_ref, z_ref, ks_ref, vs_ref, wt_ref, stage_ref, sem_ref):
    @pl.when(_first_grid_step())
    def _():
        _load_weight_bf16(wt_hbm, wt_ref, stage_ref, sem_ref)

    h = _rms(x_ref[0], gin_ref[...]).astype(BF16)
    cos_t = cos_ref[...]
    sin_t = sin_ref[...]

    def proj(lo, hi):
        return _dot_nt(h, wt_ref[lo:hi, :])

    pa = proj(_R_CQ, _R_KPE)
    cq = _rms(pa[:, :MLA_QLORA], gq_ref[...]).astype(BF16)
    ckv = _rms(pa[:, MLA_QLORA:], gkv_ref[...]).astype(BF16)

    qs_ref[0, :, :SWA_WIDTH] = proj(_R_QSWA, _R_KSWA).astype(BF16)
    qs_ref[0, :, SWA_WIDTH:] = proj(_R_QMEM, _R_ZMEM).astype(BF16)
    z_ref[0, :, :MLA_WIDTH] = proj(_R_ZMLA, _R_QSWA).astype(BF16)
    z_ref[0, :, MLA_WIDTH:MLA_WIDTH + SWA_WIDTH] = proj(_R_ZSWA, _R_QMEM).astype(BF16)
    z_ref[0, :, MLA_WIDTH + SWA_WIDTH:] = proj(_R_ZMEM, IN_WIDTH).astype(BF16)
    kvs = proj(_R_KSWA, _R_ZSWA)
    _store_swa_kv(ks_ref, kvs[:, :SWA_KV_WIDTH])
    _store_swa_kv(vs_ref, kvs[:, SWA_KV_WIDTH:])
    pe = proj(_R_KPE, _R_ZMLA)
    kpe = _rope128(jnp.concatenate([pe, pe], axis=1), cos_t, sin_t).astype(BF16)

    qall = _dot(cq, wuq_ref[...]) * _MLA_Q_SCALE
    kv = _dot(ckv, wukv_ref[...])
    for hd in range(MLA_HEADS):
        oq = hd * (MLA_NOPE + MLA_ROPE)
        q_pe = qall[:, oq + MLA_NOPE:oq + MLA_NOPE + MLA_ROPE]
        q_ref[0, hd, :, :MLA_NOPE] = qall[:, oq:oq + MLA_NOPE].astype(BF16)
        q_ref[0, hd, :, MLA_NOPE:] = _rope128(
            jnp.concatenate([q_pe, q_pe], axis=1), cos_t, sin_t).astype(BF16)
        o = hd * (MLA_NOPE + MLA_V)
        k_ref[0, hd, :, :MLA_NOPE] = kv[:, o:o + MLA_NOPE].astype(BF16)
        k_ref[0, hd, :, MLA_NOPE:] = kpe
        v_ref[0, hd] = kv[:, o + MLA_NOPE:o + MLA_QK_PAD].astype(BF16)


def _proj(x, gin, wt, gq, gkv, wuq, wukv, cos_t, sin_t, tm):
    B, S, _ = x.shape
    const = lambda b, i: (0, 0)
    row = lambda b, i: (b, i, 0)
    single = pl.Buffered(1)

    def out(width):
        return jax.ShapeDtypeStruct((B, S, width), BF16)

    head_spec = pl.BlockSpec((1, MLA_HEADS, tm, MLA_QK_PAD), lambda b, i: (b, 0, i, 0))
    head_shape = jax.ShapeDtypeStruct((B, MLA_HEADS, S, MLA_QK_PAD), BF16)

    return pl.pallas_call(
        _proj_kernel,
        grid=(B, S // tm),
        in_specs=[
            pl.BlockSpec((1, tm, D_MODEL), row),
            pl.BlockSpec((1, D_MODEL), const),
            pl.BlockSpec(memory_space=pl.ANY),
            pl.BlockSpec((1, MLA_QLORA), const),
            pl.BlockSpec((1, MLA_KVLORA), const),
            pl.BlockSpec((MLA_QLORA, MLA_HEADS * (MLA_NOPE + MLA_ROPE)), const, pipeline_mode=single),
            pl.BlockSpec((MLA_KVLORA, MLA_HEADS * (MLA_NOPE + MLA_V)), const, pipeline_mode=single),
            pl.BlockSpec((tm, 128), lambda b, i: (i, 0)),
            pl.BlockSpec((tm, 128), lambda b, i: (i, 0)),
        ],
        out_specs=[
            head_spec, head_spec,
            pl.BlockSpec((1, MLA_HEADS, tm, MLA_V), lambda b, i: (b, 0, i, 0)),
            pl.BlockSpec((1, tm, SWA_WIDTH + MEM_WIDTH), row),
            pl.BlockSpec((1, tm, MIX_WIDTH), row),
            pl.BlockSpec((1, tm, _SWA_KV_PAD), row),
            pl.BlockSpec((1, tm, _SWA_KV_PAD), row),
        ],
        out_shape=[
            head_shape, head_shape, jax.ShapeDtypeStruct((B, MLA_HEADS, S, MLA_V), BF16),
            out(SWA_WIDTH + MEM_WIDTH), out(MIX_WIDTH), out(_SWA_KV_PAD), out(_SWA_KV_PAD),
        ],
        scratch_shapes=[
            pltpu.VMEM((IN_WIDTH, D_MODEL), BF16),
            pltpu.VMEM((2, _W_IN_CHUNK, D_MODEL), F32),
            pltpu.SemaphoreType.DMA((2,)),
        ],
        compiler_params=_params(("arbitrary", "arbitrary")),
        name="proj",
    )(x, gin, wt, gq, gkv, wuq, wukv, cos_t, sin_t)


def _mla_kernel(q_ref, k_ref, v_ref, o_ref, acc_ref, m_ref, s_ref, cm_ref, *, t, g, nq):
    qi0 = pl.program_id(2) * nq
    qi_end = qi0 + nq - 1
    acc_ref[...] = jnp.zeros(acc_ref.shape, F32)
    m_ref[...] = jnp.full(m_ref.shape, NEG, F32)
    ones = jnp.ones((t, MLA_V), BF16)

    def row_max(s):
        return jnp.broadcast_to(jnp.max(s, axis=-1, keepdims=True), (t, 128))

    def scores(qi, c, slot, hd):
        rows = pl.multiple_of((qi - qi0) * t, t)
        keys = pl.multiple_of(c * t, t)
        s = _dot_nt(q_ref[0, hd, pl.ds(rows, t), :], k_ref[0, hd, pl.ds(keys, t), :])
        s_ref[slot, hd] = s
        cm_ref[slot, hd] = row_max(s)

    def update(c, slot, hd, mask):
        keys = pl.multiple_of(c * t, t)
        s = s_ref[slot, hd]
        if mask is not None:
            s = jnp.where(mask, s, NEG)
            m_cur = row_max(s)
        else:
            m_cur = cm_ref[slot, hd]
        m_old = m_ref[hd]
        m_new = jnp.maximum(m_old, m_cur)
        alpha = jnp.exp2(m_old - m_new)
        p = jnp.exp2(s - _lane_tile(m_new, t // 128)).astype(BF16)
        v_ones = jnp.concatenate([v_ref[0, hd, pl.ds(keys, t), :], ones], axis=1)
        acc_ref[hd] = _lane_tile(alpha, MLA_QK_PAD // 128) * acc_ref[hd] + _dot(p, v_ones)
        m_ref[hd] = m_new

    def finish_tile(qi, hd):
        rows = pl.multiple_of((qi - qi0) * t, t)
        acc = acc_ref[hd]
        o_ref[0, pl.ds(rows, t), hd * MLA_V:(hd + 1) * MLA_V] = (
            acc[:, :MLA_V] / acc[:, MLA_V:]).astype(BF16)
        acc_ref[hd] = jnp.zeros(acc.shape, F32)
        m_ref[hd] = jnp.full((t, 128), NEG, F32)

    def following(qi, c):
        is_last = c == qi
        return jnp.where(is_last, jnp.minimum(qi + 1, qi_end), qi), jnp.where(is_last, 0, c + 1)

    def step(qi, c, slot, diagonal):
        qn, cn = following(qi, c)
        mask = None
        if diagonal:
            q_chunk = _chunk_of(qi * t + lax.broadcasted_iota(jnp.int32, (t, t), 0))
            k_chunk = _chunk_of(c * t + lax.broadcasted_iota(jnp.int32, (t, t), 1))
            mask = k_chunk <= q_chunk
        for hd in range(g):
            scores(qn, cn, 1 - slot, hd)
            update(c, slot, hd, mask)
            if diagonal:
                finish_tile(qi, hd)

    for hd in range(g):
        scores(qi0, 0, 0, hd)

    def body(_, pair):
        first, second = pair, following(*pair)
        diag0 = first[1] == first[0]
        diag1 = second[1] == second[0]
        for d0, d1, cond in ((True, False, diag0), (False, True, diag1),
                             (False, False, jnp.logical_not(jnp.logical_or(diag0, diag1)))):
            @pl.when(cond)
            def _(d0=d0, d1=d1):
                step(*first, 0, d0)
                step(*second, 1, d1)
        return following(*second)

    n_steps = nq * qi0 + nq * (nq + 1) // 2
    lax.fori_loop(0, n_steps // 2, body, (qi0, 0))


def _mla(q, k, v, t, g, nq):
    B, _, S, _ = q.shape
    assert S % (nq * t) == 0 and nq % 2 == 0 and (nq * (nq + 1) // 2) % 2 == 0
    return pl.pallas_call(
        functools.partial(_mla_kernel, t=t, g=g, nq=nq),
        grid=(B, MLA_HEADS // g, S // (nq * t)),
        in_specs=[
            pl.BlockSpec((1, g, nq * t, MLA_QK_PAD), lambda b, h, i: (b, h, i, 0)),
            pl.BlockSpec((1, g, S, MLA_QK_PAD), lambda b, h, i: (b, h, 0, 0)),
            pl.BlockSpec((1, g, S, MLA_V), lambda b, h, i: (b, h, 0, 0)),
        ],
        out_specs=pl.BlockSpec((1, nq * t, g * MLA_V), lambda b, h, i: (b, i, h)),
        out_shape=jax.ShapeDtypeStruct((B, S, MLA_WIDTH), BF16),
        scratch_shapes=[
            pltpu.VMEM((g, t, MLA_QK_PAD), F32),
            pltpu.VMEM((g, t, 128), F32),
            pltpu.VMEM((2, g, t, t), F32),
            pltpu.VMEM((2, g, t, 128), F32),
        ],
        compiler_params=_params(("arbitrary", "arbitrary", "arbitrary")),
        name="mla",
    )(q, k, v)


_REL_SPAN = 4 * SWA_BLOCK


_N_OFFSETS = 3 * SWA_BLOCK - 1


def _bias_kernel(table_ref, bucket_ref, o_ref):
    head = lax.broadcasted_iota(jnp.int32, (SWA_HEADS, 128), 0)
    lane = lax.broadcasted_iota(jnp.int32, (SWA_HEADS, _REL_SPAN), 1)

    def lookup(c, per_offset):
        b = bucket_ref[c]
        col = jnp.zeros((SWA_HEADS, 128), F32)
        for hd in range(SWA_HEADS):
            col = jnp.where(head == hd, table_ref[b, hd], col)
        return jnp.where(lane == c, _lane_tile(col, _REL_SPAN // 128), per_offset)

    per_offset_all = lax.fori_loop(0, _N_OFFSETS, lookup, jnp.zeros((SWA_HEADS, _REL_SPAN), F32))
    for hd in range(SWA_HEADS):
        per_offset = per_offset_all[hd:hd + 1, :]
        rows = jnp.broadcast_to(per_offset, (SWA_BLOCK, _REL_SPAN))
        band = pltpu.roll(rows, _REL_SPAN - (SWA_BLOCK - 1), 1, stride=1, stride_axis=0)
        kvh, within = divmod(hd, SWA_GROUP)
        par, half = within % 2, within // 2
        o_ref[kvh, par, half * SWA_BLOCK:(half + 1) * SWA_BLOCK, :] = band[:, :2 * SWA_BLOCK]


def _bias(rel_table, bucket):
    return pl.pallas_call(
        _bias_kernel,
        in_specs=[
            pl.BlockSpec(memory_space=pltpu.SMEM),
            pl.BlockSpec(memory_space=pltpu.SMEM),
        ],
        out_specs=pl.BlockSpec(memory_space=pltpu.VMEM),
        out_shape=jax.ShapeDtypeStruct((SWA_KV_HEADS, 2, 2 * SWA_BLOCK, 2 * SWA_BLOCK), F32),
        name="t5bias",
    )(rel_table, bucket)


def _memkv_kernel(mem_ref, g_ref, w_ref, k_ref, v_ref):
    mn = _rms(mem_ref[0], g_ref[...]).astype(BF16)
    kv = _dot(mn, w_ref[...])
    k_ref[0] = kv[:, :MEM_WIDTH].astype(BF16)
    v_ref[0] = kv[:, MEM_WIDTH:].astype(BF16)


def _memkv(mem, g, w):
    B, M, _ = mem.shape
    const = lambda b: (0, 0)
    row = lambda b: (b, 0, 0)
    return pl.pallas_call(
        _memkv_kernel,
        grid=(B,),
        in_specs=[
            pl.BlockSpec((1, M, D_MODEL), row),
            pl.BlockSpec((1, D_MODEL), const),
            pl.BlockSpec((D_MODEL, 2 * MEM_WIDTH), const),
        ],
        out_specs=[pl.BlockSpec((1, M, MEM_WIDTH), row)] * 2,
        out_shape=[jax.ShapeDtypeStruct((B, M, MEM_WIDTH), BF16)] * 2,
        compiler_params=_params(("arbitrary",)),
        name="memkv",
    )(mem, g, w)


def _swa_into(y_ref, sink_ref, q_ref, kp_ref, kc_ref, vp_ref, vc_ref, bias_ref, kb_ref, vb_ref, nsub,
              after_unit):
    t = pl.program_id(1)
    scale = SWA_HEAD_DIM ** -0.5
    sb = SWA_BLOCK
    kb_ref[:sb] = kp_ref[0]
    kb_ref[sb:] = kc_ref[0]
    vb_ref[:sb] = vp_ref[0]
    vb_ref[sb:] = vc_ref[0]

    row = lax.broadcasted_iota(jnp.int32, (2 * sb, 2 * sb), 0)
    q_chunk = _chunk_of(row & (sb - 1))
    b_chunk = _chunk_of(lax.broadcasted_iota(jnp.int32, (2 * sb, 2 * sb), 1))
    valid_any = jnp.logical_and(b_chunk >= q_chunk, b_chunk <= q_chunk + WINDOW_CHUNKS)
    first_lo = jnp.where(t > 0, 0, 2)
    valid_first = jnp.logical_and(valid_any, b_chunk >= first_lo)
    upper = lax.broadcasted_iota(jnp.int32, (2 * sb, 1), 0) < sb

    def geometry(u):
        r, kvh = divmod(u, SWA_KV_HEADS)
        rows = slice(r * sb, (r + 1) * sb)
        band = slice(r * sb, (r + 2) * sb)
        pair0 = slice(2 * kvh * 128, (2 * kvh + 1) * 128)
        pair1 = slice((2 * kvh + 1) * 128, (2 * kvh + 2) * 128)
        return r, kvh, rows, band, pair0, pair1

    def scores(u):
        r, kvh, rows, band, pair0, pair1 = geometry(u)
        q = jnp.concatenate([q_ref[0, rows, pair0], q_ref[0, rows, pair1]], axis=0)
        return [_dot_nt(q, kb_ref[band, (2 * kvh + par) * 128:(2 * kvh + par + 1) * 128])
                for par in range(2)]

    def attend(u, unit_scores):
        r, kvh, rows, band, pair0, pair1 = geometry(u)
        valid = valid_first if r == 0 else valid_any
        o = None
        for par, s in enumerate(unit_scores):
            kcols = slice((2 * kvh + par) * 128, (2 * kvh + par + 1) * 128)
            hd = SWA_GROUP * kvh + par
            s = jnp.where(valid, s * scale + bias_ref[kvh, par], NEG)
            sink = jnp.where(upper, sink_ref[hd], sink_ref[hd + 2])
            m = jnp.maximum(jnp.max(s, axis=-1, keepdims=True), sink)
            p = jnp.exp(s - m)
            den = jnp.sum(p, axis=-1, keepdims=True) + jnp.exp(sink - m)
            pv = _dot((p * (1.0 / den)).astype(BF16), vb_ref[band, kcols])
            o = pv if o is None else o + pv
        y_ref[rows, pair0] = o[:sb]
        y_ref[rows, pair1] = o[sb:]

    n_units = nsub * SWA_KV_HEADS
    pending = scores(0)
    for u in range(n_units):
        upcoming = scores(u + 1) if u + 1 < n_units else None
        attend(u, pending)
        after_unit(u)
        pending = upcoming


def _mem_into(y_ref, q_ref, k_ref, v_ref, after_head):
    scale = MEM_HEAD_DIM ** -0.5

    def head_cols(hd):
        return slice(hd * MEM_HEAD_DIM, (hd + 1) * MEM_HEAD_DIM)

    def scores(hd):
        return _dot_nt(q_ref[0, :, head_cols(hd)], k_ref[0, :, head_cols(hd)])

    pending = scores(0)
    for hd in range(MEM_HEADS):
        upcoming = scores(hd + 1) if hd + 1 < MEM_HEADS else None
        s = pending * scale
        m = jnp.max(s, axis=-1, keepdims=True)
        p = jnp.exp(s - m)
        inv = 1.0 / jnp.sum(p, axis=-1, keepdims=True)
        y_ref[:, head_cols(hd)] = _dot((p * inv).astype(BF16), v_ref[0, :, head_cols(hd)])
        after_head(hd)
        pending = upcoming


def _tail_kernel(sink_ref, x_ref, ya_ref, za_ref, zb_ref, zc_ref, qs_ref, qm_ref,
                 kp_ref, kc_ref, vp_ref, vc_ref, bias_ref, km_ref, vm_ref, w_hbm, g_ref,
                 o_ref, kb_ref, vb_ref, yb_ref, yc_ref, yacc_ref, w_ref, stage_ref, sem_ref, *, nsub):
    @pl.when(_first_grid_step())
    def _():
        _load_weight_bf16(w_hbm, w_ref, stage_ref, sem_ref)

    def gated(y, z_ref):
        h = 0.5 * z_ref[0].astype(F32)
        return (y * (h + h * jnp.tanh(h))).astype(BF16)

    swa_lo, mem_lo = MLA_WIDTH, MLA_WIDTH + SWA_WIDTH

    g_mla = gated(ya_ref[0].astype(F32), za_ref)
    mla_chunk = D_MODEL // (nsub * SWA_KV_HEADS)

    def project_mla(u):
        cols = slice(u * mla_chunk, (u + 1) * mla_chunk)
        yacc_ref[:, cols] = _dot(g_mla, w_ref[:swa_lo, cols])

    _swa_into(yb_ref, sink_ref, qs_ref, kp_ref, kc_ref, vp_ref, vc_ref, bias_ref, kb_ref, vb_ref, nsub,
              project_mla)

    g_swa = gated(yb_ref[...], zb_ref)
    swa_chunk = D_MODEL // MEM_HEADS

    def project_swa(u):
        cols = slice(u * swa_chunk, (u + 1) * swa_chunk)
        yacc_ref[:, cols] += _dot(g_swa, w_ref[swa_lo:mem_lo, cols])

    _mem_into(yc_ref, qm_ref, km_ref, vm_ref, project_swa)

    g_mem = gated(yc_ref[...], zc_ref)
    sum_sq = jnp.zeros((g_mem.shape[0], 1), F32)
    chunks = [slice(j * swa_chunk, (j + 1) * swa_chunk) for j in range(D_MODEL // swa_chunk)]
    for cols in chunks:
        r = x_ref[0, :, cols] + yacc_ref[:, cols] + _dot(g_mem, w_ref[mem_lo:, cols])
        o_ref[0, :, cols] = r
        sum_sq = sum_sq + jnp.sum(r * r, axis=-1, keepdims=True)
    inv = lax.rsqrt(sum_sq * (1.0 / D_MODEL) + EPS)
    for cols in chunks:
        o_ref[0, :, cols] = (o_ref[0, :, cols] * inv) * g_ref[:, cols]


def _tail(sinks, x, y_mla, z, qs, ks, vs, bias, kmem, vmem, w, g, nsub):
    B, S, _ = x.shape
    tm = nsub * SWA_BLOCK
    M = kmem.shape[1]
    row = lambda b, i: (b, i, 0)
    prev = lambda b, i: (b, jnp.maximum(i * nsub - 1, 0), 0)
    col = lambda c: (lambda b, i: (b, i, c))
    return pl.pallas_call(
        functools.partial(_tail_kernel, nsub=nsub),
        grid=(B, S // tm),
        in_specs=[
            pl.BlockSpec(memory_space=pltpu.SMEM),
            pl.BlockSpec((1, tm, D_MODEL), row),
            pl.BlockSpec((1, tm, MLA_WIDTH), row),
            pl.BlockSpec((1, tm, MLA_WIDTH), col(0)),
            pl.BlockSpec((1, tm, SWA_WIDTH), col(MLA_WIDTH // SWA_WIDTH)),
            pl.BlockSpec((1, tm, MEM_WIDTH), col((MLA_WIDTH + SWA_WIDTH) // MEM_WIDTH)),
            pl.BlockSpec((1, tm, SWA_WIDTH), col(0)),
            pl.BlockSpec((1, tm, MEM_WIDTH), col(1)),
            pl.BlockSpec((1, SWA_BLOCK, _SWA_KV_PAD), prev),
            pl.BlockSpec((1, tm, _SWA_KV_PAD), row),
            pl.BlockSpec((1, SWA_BLOCK, _SWA_KV_PAD), prev),
            pl.BlockSpec((1, tm, _SWA_KV_PAD), row),
            pl.BlockSpec((SWA_KV_HEADS, 2, 2 * SWA_BLOCK, 2 * SWA_BLOCK), lambda b, i: (0, 0, 0, 0)),
            pl.BlockSpec((1, M, MEM_WIDTH), lambda b, i: (b, 0, 0)),
            pl.BlockSpec((1, M, MEM_WIDTH), lambda b, i: (b, 0, 0)),
            pl.BlockSpec(memory_space=pl.ANY),
            pl.BlockSpec((1, D_MODEL), lambda b, i: (0, 0)),
        ],
        out_specs=pl.BlockSpec((1, tm, D_MODEL), row),
        out_shape=jax.ShapeDtypeStruct((B, S, D_MODEL), F32),
        scratch_shapes=[
            pltpu.VMEM((tm + SWA_BLOCK, _SWA_KV_PAD), BF16),
            pltpu.VMEM((tm + SWA_BLOCK, _SWA_KV_PAD), BF16),
            pltpu.VMEM((tm, SWA_WIDTH), F32),
            pltpu.VMEM((tm, MEM_WIDTH), F32),
            pltpu.VMEM((tm, D_MODEL), F32),
            pltpu.VMEM((MIX_WIDTH, D_MODEL), BF16),
            pltpu.VMEM((2, _W_OUT_CHUNK, D_MODEL), F32),
            pltpu.SemaphoreType.DMA((2,)),
        ],
        compiler_params=_params(("arbitrary", "arbitrary")),
        name="tail",
    )(sinks, x, y_mla, z, z, z, qs, qs, ks, ks, vs, vs, bias, kmem, vmem, w, g)


def _rope_tables(seq):
    inv = 1.0 / (ROPE_THETA ** (jnp.arange(0, MLA_ROPE, 2, dtype=F32) / MLA_ROPE))
    ang = jnp.arange(seq, dtype=F32)[:, None] * inv[None, :]
    cos, sin = jnp.cos(ang), jnp.sin(ang)
    zero = jnp.zeros_like(cos)
    return (jnp.concatenate([cos, cos, zero, zero], axis=-1),
            jnp.concatenate([-sin, sin, zero, zero], axis=-1))


def _t5_bucket(rel):
    nb = N_BUCKETS // 2
    max_exact = nb // 2
    bucket = jnp.where(rel > 0, nb, 0)
    n = jnp.abs(rel)
    nf = jnp.maximum(n, 1).astype(F32)
    large = max_exact + (jnp.log(nf / max_exact) / math.log(MAX_DISTANCE / max_exact)
                         * (nb - max_exact)).astype(jnp.int32)
    large = jnp.minimum(large, nb - 1)
    return bucket + jnp.where(n < max_exact, n, large)


def kernel(x, mem, norm_in, w_in, norm_q, norm_kv, w_uq, w_ukv, attn_sinks, rel_bias,
           norm_mem, w_mem_kv, w_out, norm_final):
    B, S, _ = x.shape
    assert norm_in.shape[0] == 1, "single-layer trunk"

    wt = w_in[0].T
    cos_t, sin_t = _rope_tables(S)
    q, k, v, qs, z, ks, vs = _proj(x, norm_in, wt, norm_q, norm_kv, w_uq[0].astype(BF16),
                                   w_ukv[0].astype(BF16), cos_t, sin_t, tm=512)

    y_mla = _mla(q, k, v, t=512, g=4, nq=4)

    bucket = _t5_bucket(jnp.arange(_REL_SPAN) - (2 * SWA_BLOCK - 1)).astype(jnp.int32)
    bias = _bias(rel_bias, bucket)
    kmem, vmem = _memkv(mem, norm_mem, w_mem_kv[0].astype(BF16))
    return _tail(attn_sinks[0], x, y_mla, z, qs, ks, vs, bias, kmem, vmem,
                 w_out[0], norm_final[None, :], nsub=4)
```

```python
import functools
import math

import jax
import jax.numpy as jnp
from jax import lax
from jax.experimental import pallas as pl
from jax.experimental.pallas import tpu as pltpu

D_MODEL = 2048
CHUNK = 64
N_MEM = 256
EPS = 1e-6
NEG = -1e30

MLA_HEADS = 8
MLA_NOPE = 128
MLA_ROPE = 64
MLA_V = 128
MLA_QLORA = 512
MLA_KVLORA = 256
MLA_WIDTH = MLA_HEADS * MLA_V
MLA_QK_PAD = 256
ROPE_THETA = 10000.0

SWA_HEADS = 8
SWA_KV_HEADS = 2
SWA_GROUP = SWA_HEADS // SWA_KV_HEADS
SWA_HEAD_DIM = 64
SWA_WIDTH = SWA_HEADS * SWA_HEAD_DIM
SWA_KV_WIDTH = SWA_KV_HEADS * SWA_HEAD_DIM
WINDOW_CHUNKS = 2
SWA_BLOCK = 128

MEM_HEADS = 4
MEM_HEAD_DIM = 128
MEM_WIDTH = MEM_HEADS * MEM_HEAD_DIM

MIX_WIDTH = MLA_WIDTH + SWA_WIDTH + MEM_WIDTH

N_BUCKETS = 32
MAX_DISTANCE = 128

BF16 = jnp.bfloat16
F32 = jnp.float32

_R_CQ = 0
_R_CKV = _R_CQ + MLA_QLORA
_R_KPE = _R_CKV + MLA_KVLORA
_R_ZMLA = _R_KPE + MLA_ROPE
_R_QSWA = _R_ZMLA + MLA_WIDTH
_R_KSWA = _R_QSWA + SWA_WIDTH
_R_VSWA = _R_KSWA + SWA_KV_WIDTH
_R_ZSWA = _R_VSWA + SWA_KV_WIDTH
_R_QMEM = _R_ZSWA + SWA_WIDTH
_R_ZMEM = _R_QMEM + MEM_WIDTH
IN_WIDTH = _R_ZMEM + MEM_WIDTH

_MLA_Q_SCALE = (MLA_NOPE + MLA_ROPE) ** -0.5 * math.log2(math.e)

_SWA_KV_PAD = 2 * SWA_KV_HEADS * 128

_W_IN_CHUNK = 208
_W_OUT_CHUNK = 256

_VMEM_LIMIT = 56 * 1024 * 1024


def _params(sem, flags=None):
    return pltpu.CompilerParams(dimension_semantics=sem, vmem_limit_bytes=_VMEM_LIMIT, flags=flags)


def _rms(v, g):
    return (v * lax.rsqrt(jnp.mean(v * v, axis=-1, keepdims=True) + EPS)) * g


def _dot(a, b):
    return jnp.dot(a, b, preferred_element_type=F32)


def _dot_nt(a, b):
    return lax.dot_general(a, b, (((1,), (1,)), ((), ())), preferred_element_type=F32)


class _WeightLoader:
    def __init__(self, w_hbm, w_ref, stage_ref, sem_ref):
        self.w_hbm, self.w_ref, self.stage_ref, self.sem_ref = w_hbm, w_ref, stage_ref, sem_ref
        self.rows = stage_ref.shape[1]
        self.n_chunks = w_ref.shape[0] // self.rows
        self.done = 0
        for c in range(min(2, self.n_chunks)):
            self._copy(c).start()

    def _copy(self, c):
        rows = self.rows
        return pltpu.make_async_copy(self.w_hbm.at[pl.ds(c * rows, rows), :],
                                     self.stage_ref.at[c % 2], self.sem_ref.at[c % 2])

    def need(self, hi):
        rows = self.rows
        while self.done * rows < hi:
            c = self.done
            self._copy(c).wait()
            self.w_ref[c * rows:(c + 1) * rows, :] = self.stage_ref[c % 2].astype(BF16)
            if c + 2 < self.n_chunks:
                self._copy(c + 2).start()
            self.done += 1


def _run_first_step_streaming(body, w_hbm, w_ref, stage_ref, sem_ref):
    first = jnp.logical_and(pl.program_id(0) == 0, pl.program_id(1) == 0)

    @pl.when(first)
    def _():
        loader = _WeightLoader(w_hbm, w_ref, stage_ref, sem_ref)
        body(loader.need)
        loader.need(w_ref.shape[0])

    @pl.when(jnp.logical_not(first))
    def _():
        body(lambda hi: None)


def _lane_tile(v, n):
    return jnp.concatenate([v] * n, axis=1)


def _chunk_of(pos):
    return jnp.right_shift(pos, CHUNK.bit_length() - 1)


def _rope128(v, cos_t, sin_t):
    return v * cos_t + pltpu.roll(v, MLA_ROPE // 2, 1) * sin_t


def _store_swa_kv(ref, kv):
    low = lax.broadcasted_iota(jnp.int32, kv.shape, 1) < SWA_HEAD_DIM
    swapped = pltpu.roll(kv, SWA_HEAD_DIM, 1)
    pieces = (jnp.where(low, kv, 0.0), jnp.where(low, 0.0, swapped),
              jnp.where(low, swapped, 0.0), jnp.where(low, 0.0, kv))
    for i, piece in enumerate(pieces):
        ref[0, :, i * 128:(i + 1) * 128] = piece.astype(BF16)


def _proj_kernel(x_ref, gin_ref, wt_hbm, gq_ref, gkv_ref, wuq_ref, wukv_ref, cos_ref, sin_ref,
                 q_ref, k_ref, v_ref, qs_ref, z_ref, ks_ref, vs_ref, wt_ref, stage_ref, sem_ref):
    def body(need):
        h = _rms(x_ref[0], gin_ref[...]).astype(BF16)
        cos_t = cos_ref[...]
        sin_t = sin_ref[...]

        def proj(lo, hi):
            need(hi)
            return _dot_nt(h, wt_ref[lo:hi, :])

        pa = proj(_R_CQ, _R_KPE)
        cq = _rms(pa[:, :MLA_QLORA], gq_ref[...]).astype(BF16)
        ckv = _rms(pa[:, MLA_QLORA:], gkv_ref[...]).astype(BF16)
        pe = proj(_R_KPE, _R_ZMLA)
        kpe = _rope128(jnp.concatenate([pe, pe], axis=1), cos_t, sin_t).astype(BF16)

        z_ref[0, :, :MLA_WIDTH] = proj(_R_ZMLA, _R_QSWA).astype(BF16)
        qs_ref[0, :, :SWA_WIDTH] = proj(_R_QSWA, _R_KSWA).astype(BF16)
        kvs = proj(_R_KSWA, _R_ZSWA)
        _store_swa_kv(ks_ref, kvs[:, :SWA_KV_WIDTH])
        _store_swa_kv(vs_ref, kvs[:, SWA_KV_WIDTH:])
        z_ref[0, :, MLA_WIDTH:MLA_WIDTH + SWA_WIDTH] = proj(_R_ZSWA, _R_QMEM).astype(BF16)
        qs_ref[0, :, SWA_WIDTH:] = proj(_R_QMEM, _R_ZMEM).astype(BF16)
        z_ref[0, :, MLA_WIDTH + SWA_WIDTH:] = proj(_R_ZMEM, IN_WIDTH).astype(BF16)

        qall = _dot(cq, wuq_ref[...]) * _MLA_Q_SCALE
        kv = _dot(ckv, wukv_ref[...])
        for hd in range(MLA_HEADS):
            oq = hd * (MLA_NOPE + MLA_ROPE)
            q_pe = qall[:, oq + MLA_NOPE:oq + MLA_NOPE + MLA_ROPE]
            q_ref[0, hd, :, :MLA_NOPE] = qall[:, oq:oq + MLA_NOPE].astype(BF16)
            q_ref[0, hd, :, MLA_NOPE:] = _rope128(
                jnp.concatenate([q_pe, q_pe], axis=1), cos_t, sin_t).astype(BF16)
            o = hd * (MLA_NOPE + MLA_V)
            k_ref[0, hd, :, :MLA_NOPE] = kv[:, o:o + MLA_NOPE].astype(BF16)
            k_ref[0, hd, :, MLA_NOPE:] = kpe
            v_ref[0, hd] = kv[:, o + MLA_NOPE:o + MLA_QK_PAD].astype(BF16)

    _run_first_step_streaming(body, wt_hbm, wt_ref, stage_ref, sem_ref)


def _proj(x, gin, wt, gq, gkv, wuq, wukv, cos_t, sin_t, tm):
    B, S, _ = x.shape
    const = lambda b, i: (0, 0)
    row = lambda b, i: (b, i, 0)
    single = pl.Buffered(1)

    def out(width):
        return jax.ShapeDtypeStruct((B, S, width), BF16)

    head_spec = pl.BlockSpec((1, MLA_HEADS, tm, MLA_QK_PAD), lambda b, i: (b, 0, i, 0))
    head_shape = jax.ShapeDtypeStruct((B, MLA_HEADS, S, MLA_QK_PAD), BF16)

    return pl.pallas_call(
        _proj_kernel,
        grid=(B, S // tm),
        in_specs=[
            pl.BlockSpec((1, tm, D_MODEL), row),
            pl.BlockSpec((1, D_MODEL), const),
            pl.BlockSpec(memory_space=pl.ANY),
            pl.BlockSpec((1, MLA_QLORA), const),
            pl.BlockSpec((1, MLA_KVLORA), const),
            pl.BlockSpec((MLA_QLORA, MLA_HEADS * (MLA_NOPE + MLA_ROPE)), const, pipeline_mode=single),
            pl.BlockSpec((MLA_KVLORA, MLA_HEADS * (MLA_NOPE + MLA_V)), const, pipeline_mode=single),
            pl.BlockSpec((tm, 128), lambda b, i: (i, 0)),
            pl.BlockSpec((tm, 128), lambda b, i: (i, 0)),
        ],
        out_specs=[
            head_spec, head_spec,
            pl.BlockSpec((1, MLA_HEADS, tm, MLA_V), lambda b, i: (b, 0, i, 0)),
            pl.BlockSpec((1, tm, SWA_WIDTH + MEM_WIDTH), row),
            pl.BlockSpec((1, tm, MIX_WIDTH), row),
            pl.BlockSpec((1, tm, _SWA_KV_PAD), row),
            pl.BlockSpec((1, tm, _SWA_KV_PAD), row),
        ],
        out_shape=[
            head_shape, head_shape, jax.ShapeDtypeStruct((B, MLA_HEADS, S, MLA_V), BF16),
            out(SWA_WIDTH + MEM_WIDTH), out(MIX_WIDTH), out(_SWA_KV_PAD), out(_SWA_KV_PAD),
        ],
        scratch_shapes=[
            pltpu.VMEM((IN_WIDTH, D_MODEL), BF16),
            pltpu.VMEM((2, _W_IN_CHUNK, D_MODEL), F32),
            pltpu.SemaphoreType.DMA((2,)),
        ],
        compiler_params=_params(("arbitrary", "arbitrary")),
        name="proj",
    )(x, gin, wt, gq, gkv, wuq, wukv, cos_t, sin_t)


def _mla_kernel(q_ref, k_ref, v_ref, o_ref, acc_ref, m_ref, s_ref, cm_ref, *, t, g, nq):
    qi0 = pl.program_id(2) * nq
    qi_end = qi0 + nq - 1
    acc_ref[...] = jnp.zeros(acc_ref.shape, F32)
    m_ref[...] = jnp.full(m_ref.shape, NEG, F32)
    ones = jnp.ones((t, MLA_V), BF16)

    def row_max(s):
        return jnp.broadcast_to(jnp.max(s, axis=-1, keepdims=True), (t, 128))

    def scores(qi, c, slot, hd):
        rows = pl.multiple_of((qi - qi0) * t, t)
        keys = pl.multiple_of(c * t, t)
        s = _dot_nt(q_ref[0, hd, pl.ds(rows, t), :], k_ref[0, hd, pl.ds(keys, t), :])
        s_ref[slot, hd] = s
        cm_ref[slot, hd] = row_max(s)

    def update(c, slot, hd, mask):
        keys = pl.multiple_of(c * t, t)
        s = s_ref[slot, hd]
        if mask is not None:
            s = jnp.where(mask, s, NEG)
            m_cur = row_max(s)
        else:
            m_cur = cm_ref[slot, hd]
        m_old = m_ref[hd]
        m_new = jnp.maximum(m_old, m_cur)
        alpha = jnp.exp2(m_old - m_new)
        p = jnp.exp2(s - _lane_tile(m_new, t // 128)).astype(BF16)
        v_ones = jnp.concatenate([v_ref[0, hd, pl.ds(keys, t), :], ones], axis=1)
        acc_ref[hd] = _lane_tile(alpha, MLA_QK_PAD // 128) * acc_ref[hd] + _dot(p, v_ones)
        m_ref[hd] = m_new

    def finish_tile(qi, hd):
        rows = pl.multiple_of((qi - qi0) * t, t)
        acc = acc_ref[hd]
        o_ref[0, pl.ds(rows, t), hd * MLA_V:(hd + 1) * MLA_V] = (
            acc[:, :MLA_V] / acc[:, MLA_V:]).astype(BF16)
        acc_ref[hd] = jnp.zeros(acc.shape, F32)
        m_ref[hd] = jnp.full((t, 128), NEG, F32)

    def following(qi, c):
        is_last = c == qi
        return jnp.where(is_last, jnp.minimum(qi + 1, qi_end), qi), jnp.where(is_last, 0, c + 1)

    def step(qi, c, slot, diagonal):
        qn, cn = following(qi, c)
        mask = None
        if diagonal:
            q_chunk = _chunk_of(qi * t + lax.broadcasted_iota(jnp.int32, (t, t), 0))
            k_chunk = _chunk_of(c * t + lax.broadcasted_iota(jnp.int32, (t, t), 1))
            mask = k_chunk <= q_chunk
        for hd in range(g):
            scores(qn, cn, 1 - slot, hd)
            update(c, slot, hd, mask)
            if diagonal:
                finish_tile(qi, hd)

    for hd in range(g):
        scores(qi0, 0, 0, hd)

    def body(_, pair):
        first, second = pair, following(*pair)
        diag0 = first[1] == first[0]
        diag1 = second[1] == second[0]
        for d0, d1, cond in ((True, False, diag0), (False, True, diag1),
                             (False, False, jnp.logical_not(jnp.logical_or(diag0, diag1)))):
            @pl.when(cond)
            def _(d0=d0, d1=d1):
                step(*first, 0, d0)
                step(*second, 1, d1)
        return following(*second)

    n_steps = nq * qi0 + nq * (nq + 1) // 2
    lax.fori_loop(0, n_steps // 2, body, (qi0, 0))


def _mla(q, k, v, t, g, nq):
    B, _, S, _ = q.shape
    assert S % (nq * t) == 0 and nq % 2 == 0 and (nq * (nq + 1) // 2) % 2 == 0
    return pl.pallas_call(
        functools.partial(_mla_kernel, t=t, g=g, nq=nq),
        grid=(B, MLA_HEADS // g, S // (nq * t)),
        in_specs=[
            pl.BlockSpec((1, g, nq * t, MLA_QK_PAD), lambda b, h, i: (b, h, i, 0)),
            pl.BlockSpec((1, g, S, MLA_QK_PAD), lambda b, h, i: (b, h, 0, 0)),
            pl.BlockSpec((1, g, S, MLA_V), lambda b, h, i: (b, h, 0, 0)),
        ],
        out_specs=pl.BlockSpec((1, nq * t, g * MLA_V), lambda b, h, i: (b, i, h)),
        out_shape=jax.ShapeDtypeStruct((B, S, MLA_WIDTH), BF16),
        scratch_shapes=[
            pltpu.VMEM((g, t, MLA_QK_PAD), F32),
            pltpu.VMEM((g, t, 128), F32),
            pltpu.VMEM((2, g, t, t), F32),
            pltpu.VMEM((2, g, t, 128), F32),
        ],
        compiler_params=_params(("arbitrary", "arbitrary", "arbitrary")),
        name="mla",
    )(q, k, v)


_REL_SPAN = 4 * SWA_BLOCK


_N_OFFSETS = 3 * SWA_BLOCK - 1


def _bias_kernel(table_ref, bucket_ref, o_ref):
    head = lax.broadcasted_iota(jnp.int32, (SWA_HEADS, 128), 0)
    lane = lax.broadcasted_iota(jnp.int32, (SWA_HEADS, _REL_SPAN), 1)

    def lookup(c, per_offset):
        b = bucket_ref[c]
        col = jnp.zeros((SWA_HEADS, 128), F32)
        for hd in range(SWA_HEADS):
            col = jnp.where(head == hd, table_ref[b, hd], col)
        return jnp.where(lane == c, _lane_tile(col, _REL_SPAN // 128), per_offset)

    per_offset_all = lax.fori_loop(0, _N_OFFSETS, lookup, jnp.zeros((SWA_HEADS, _REL_SPAN), F32))
    for hd in range(SWA_HEADS):
        per_offset = per_offset_all[hd:hd + 1, :]
        rows = jnp.broadcast_to(per_offset, (SWA_BLOCK, _REL_SPAN))
        band = pltpu.roll(rows, _REL_SPAN - (SWA_BLOCK - 1), 1, stride=1, stride_axis=0)
        kvh, within = divmod(hd, SWA_GROUP)
        par, half = within % 2, within // 2
        o_ref[kvh, par, half * SWA_BLOCK:(half + 1) * SWA_BLOCK, :] = band[:, :2 * SWA_BLOCK]


def _bias(rel_table, bucket):
    return pl.pallas_call(
        _bias_kernel,
        in_specs=[
            pl.BlockSpec(memory_space=pltpu.SMEM),
            pl.BlockSpec(memory_space=pltpu.SMEM),
        ],
        out_specs=pl.BlockSpec(memory_space=pltpu.VMEM),
        out_shape=jax.ShapeDtypeStruct((SWA_KV_HEADS, 2, 2 * SWA_BLOCK, 2 * SWA_BLOCK), F32),
        name="t5bias",
    )(rel_table, bucket)


def _memkv_kernel(mem_ref, g_ref, w_ref, k_ref, v_ref):
    mn = _rms(mem_ref[0], g_ref[...]).astype(BF16)
    kv = _dot(mn, w_ref[...])
    k_ref[0] = kv[:, :MEM_WIDTH].astype(BF16)
    v_ref[0] = kv[:, MEM_WIDTH:].astype(BF16)


def _memkv(mem, g, w):
    B, M, _ = mem.shape
    const = lambda b: (0, 0)
    row = lambda b: (b, 0, 0)
    return pl.pallas_call(
        _memkv_kernel,
        grid=(B,),
        in_specs=[
            pl.BlockSpec((1, M, D_MODEL), row),
            pl.BlockSpec((1, D_MODEL), const),
            pl.BlockSpec((D_MODEL, 2 * MEM_WIDTH), const),
        ],
        out_specs=[pl.BlockSpec((1, M, MEM_WIDTH), row)] * 2,
        out_shape=[jax.ShapeDtypeStruct((B, M, MEM_WIDTH), BF16)] * 2,
        compiler_params=_params(("arbitrary",)),
        name="memkv",
    )(mem, g, w)


def _swa_into(y_ref, sink_ref, q_ref, bias_ref, kb_ref, vb_ref, blocks, beside_unit):
    t = pl.program_id(1)
    scale = SWA_HEAD_DIM ** -0.5
    sb = SWA_BLOCK
    row = lax.broadcasted_iota(jnp.int32, (2 * sb, 2 * sb), 0)
    q_chunk = _chunk_of(row & (sb - 1))
    b_chunk = _chunk_of(lax.broadcasted_iota(jnp.int32, (2 * sb, 2 * sb), 1))
    valid_any = jnp.logical_and(b_chunk >= q_chunk, b_chunk <= q_chunk + WINDOW_CHUNKS)
    first_lo = jnp.where(t > 0, 0, 2)
    valid_first = jnp.logical_and(valid_any, b_chunk >= first_lo)
    upper = lax.broadcasted_iota(jnp.int32, (2 * sb, 1), 0) < sb

    def geometry(u):
        r, kvh = blocks[u // SWA_KV_HEADS], u % SWA_KV_HEADS
        rows = slice(r * sb, (r + 1) * sb)
        band = slice(r * sb, (r + 2) * sb)
        pair0 = slice(2 * kvh * 128, (2 * kvh + 1) * 128)
        pair1 = slice((2 * kvh + 1) * 128, (2 * kvh + 2) * 128)
        return r, kvh, rows, band, pair0, pair1

    def scores(u):
        r, kvh, rows, band, pair0, pair1 = geometry(u)
        q = jnp.concatenate([q_ref[0, rows, pair0], q_ref[0, rows, pair1]], axis=0)
        return [_dot_nt(q, kb_ref[band, (2 * kvh + par) * 128:(2 * kvh + par + 1) * 128])
                for par in range(2)]

    def attend(u, unit_scores):
        r, kvh, rows, band, pair0, pair1 = geometry(u)
        valid = valid_first if r == 0 else valid_any
        o = None
        for par, s in enumerate(unit_scores):
            kcols = slice((2 * kvh + par) * 128, (2 * kvh + par + 1) * 128)
            hd = SWA_GROUP * kvh + par
            s = jnp.where(valid, s * scale + bias_ref[kvh, par], NEG)
            sink = jnp.where(upper, sink_ref[hd], sink_ref[hd + 2])
            m = jnp.maximum(jnp.max(s, axis=-1, keepdims=True), sink)
            p = jnp.exp(s - m)
            den = jnp.sum(p, axis=-1, keepdims=True) + jnp.exp(sink - m)
            pv = _dot((p * (1.0 / den)).astype(BF16), vb_ref[band, kcols])
            o = pv if o is None else o + pv
        y_ref[rows, pair0] = o[:sb]
        y_ref[rows, pair1] = o[sb:]

    n_units = len(blocks) * SWA_KV_HEADS
    pending = scores(0)
    for u in range(n_units):
        upcoming = scores(u + 1) if u + 1 < n_units else None
        beside_unit(u)
        attend(u, pending)
        pending = upcoming


def _mem_into(y_ref, q_ref, k_ref, v_ref, after_head):
    scale = MEM_HEAD_DIM ** -0.5

    def head_cols(hd):
        return slice(hd * MEM_HEAD_DIM, (hd + 1) * MEM_HEAD_DIM)

    def scores(hd):
        return _dot_nt(q_ref[0, :, head_cols(hd)], k_ref[0, :, head_cols(hd)])

    pending = scores(0)
    for hd in range(MEM_HEADS):
        upcoming = scores(hd + 1) if hd + 1 < MEM_HEADS else None
        s = pending * scale
        m = jnp.max(s, axis=-1, keepdims=True)
        p = jnp.exp(s - m)
        inv = 1.0 / jnp.sum(p, axis=-1, keepdims=True)
        y_ref[:, head_cols(hd)] = _dot((p * inv).astype(BF16), v_ref[0, :, head_cols(hd)])
        after_head(hd)
        pending = upcoming


def _tail_kernel(sink_ref, x_ref, ya_ref, za_ref, zb_ref, zc_ref, qs_ref, qm_ref,
                 kp_ref, kc_ref, vp_ref, vc_ref, bias_ref, km_ref, vm_ref, w_hbm, g_ref,
                 o_ref, kb_ref, vb_ref, yb_ref, yc_ref, yacc_ref, w_ref, stage_ref, sem_ref, *, nsub):
    def gated(y, z):
        h = 0.5 * z.astype(F32)
        return (y * (h + h * jnp.tanh(h))).astype(BF16)

    def body(need):
        kb_ref[:SWA_BLOCK] = kp_ref[0]
        kb_ref[SWA_BLOCK:] = kc_ref[0]
        vb_ref[:SWA_BLOCK] = vp_ref[0]
        vb_ref[SWA_BLOCK:] = vc_ref[0]
        swa_lo, mem_lo = MLA_WIDTH, MLA_WIDTH + SWA_WIDTH

        g_mla = gated(ya_ref[0].astype(F32), za_ref[0])
        mla_chunk = D_MODEL // (nsub * SWA_KV_HEADS)

        def project_mla(u):
            need(swa_lo)
            cols = slice(u * mla_chunk, (u + 1) * mla_chunk)
            yacc_ref[:, cols] = _dot(g_mla, w_ref[:swa_lo, cols])

        _swa_into(yb_ref, sink_ref, qs_ref, bias_ref, kb_ref, vb_ref, list(range(nsub)), project_mla)

        g_swa = gated(yb_ref[...], zb_ref[0])
        swa_chunk = D_MODEL // MEM_HEADS

        def project_swa(u):
            need(mem_lo)
            cols = slice(u * swa_chunk, (u + 1) * swa_chunk)
            yacc_ref[:, cols] += _dot(g_swa, w_ref[swa_lo:mem_lo, cols])

        _mem_into(yc_ref, qm_ref, km_ref, vm_ref, project_swa)

        need(MIX_WIDTH)
        g_mem = gated(yc_ref[...], zc_ref[0])
        sum_sq = jnp.zeros((g_mem.shape[0], 1), F32)
        chunks = [slice(j * swa_chunk, (j + 1) * swa_chunk) for j in range(D_MODEL // swa_chunk)]
        for cols in chunks:
            r = x_ref[0, :, cols] + yacc_ref[:, cols] + _dot(g_mem, w_ref[mem_lo:, cols])
            o_ref[0, :, cols] = r
            sum_sq = sum_sq + jnp.sum(r * r, axis=-1, keepdims=True)
        inv = lax.rsqrt(sum_sq * (1.0 / D_MODEL) + EPS)
        for cols in chunks:
            o_ref[0, :, cols] = (o_ref[0, :, cols] * inv) * g_ref[:, cols]

    _run_first_step_streaming(body, w_hbm, w_ref, stage_ref, sem_ref)


def _tail(sinks, x, y_mla, z, qs, ks, vs, bias, kmem, vmem, w, g, nsub):
    B, S, _ = x.shape
    tm = nsub * SWA_BLOCK
    M = kmem.shape[1]
    row = lambda b, i: (b, i, 0)
    prev = lambda b, i: (b, jnp.maximum(i * nsub - 1, 0), 0)
    col = lambda c: (lambda b, i: (b, i, c))
    return pl.pallas_call(
        functools.partial(_tail_kernel, nsub=nsub),
        grid=(B, S // tm),
        in_specs=[
            pl.BlockSpec(memory_space=pltpu.SMEM),
            pl.BlockSpec((1, tm, D_MODEL), row),
            pl.BlockSpec((1, tm, MLA_WIDTH), row),
            pl.BlockSpec((1, tm, MLA_WIDTH), col(0)),
            pl.BlockSpec((1, tm, SWA_WIDTH), col(MLA_WIDTH // SWA_WIDTH)),
            pl.BlockSpec((1, tm, MEM_WIDTH), col((MLA_WIDTH + SWA_WIDTH) // MEM_WIDTH)),
            pl.BlockSpec((1, tm, SWA_WIDTH), col(0)),
            pl.BlockSpec((1, tm, MEM_WIDTH), col(1)),
            pl.BlockSpec((1, SWA_BLOCK, _SWA_KV_PAD), prev),
            pl.BlockSpec((1, tm, _SWA_KV_PAD), row),
            pl.BlockSpec((1, SWA_BLOCK, _SWA_KV_PAD), prev),
            pl.BlockSpec((1, tm, _SWA_KV_PAD), row),
            pl.BlockSpec((SWA_KV_HEADS, 2, 2 * SWA_BLOCK, 2 * SWA_BLOCK), lambda b, i: (0, 0, 0, 0)),
            pl.BlockSpec((1, M, MEM_WIDTH), lambda b, i: (b, 0, 0)),
            pl.BlockSpec((1, M, MEM_WIDTH), lambda b, i: (b, 0, 0)),
            pl.BlockSpec(memory_space=pl.ANY),
            pl.BlockSpec((1, D_MODEL), lambda b, i: (0, 0)),
        ],
        out_specs=pl.BlockSpec((1, tm, D_MODEL), row),
        out_shape=jax.ShapeDtypeStruct((B, S, D_MODEL), F32),
        scratch_shapes=[
            pltpu.VMEM((tm + SWA_BLOCK, _SWA_KV_PAD), BF16),
            pltpu.VMEM((tm + SWA_BLOCK, _SWA_KV_PAD), BF16),
            pltpu.VMEM((tm, SWA_WIDTH), F32),
            pltpu.VMEM((tm, MEM_WIDTH), F32),
            pltpu.VMEM((tm, D_MODEL), F32),
            pltpu.VMEM((MIX_WIDTH, D_MODEL), BF16),
            pltpu.VMEM((2, _W_OUT_CHUNK, D_MODEL), F32),
            pltpu.SemaphoreType.DMA((2,)),
        ],
        compiler_params=_params(("arbitrary", "arbitrary")),
        name="tail",
    )(sinks, x, y_mla, z, z, z, qs, qs, ks, ks, vs, vs, bias, kmem, vmem, w, g)


def _rope_tables(seq):
    inv = 1.0 / (ROPE_THETA ** (jnp.arange(0, MLA_ROPE, 2, dtype=F32) / MLA_ROPE))
    ang = jnp.arange(seq, dtype=F32)[:, None] * inv[None, :]
    cos, sin = jnp.cos(ang), jnp.sin(ang)
    zero = jnp.zeros_like(cos)
    return (jnp.concatenate([cos, cos, zero, zero], axis=-1),
            jnp.concatenate([-sin, sin, zero, zero], axis=-1))


def _t5_bucket(rel):
    nb = N_BUCKETS // 2
    max_exact = nb // 2
    bucket = jnp.where(rel > 0, nb, 0)
    n = jnp.abs(rel)
    nf = jnp.maximum(n, 1).astype(F32)
    large = max_exact + (jnp.log(nf / max_exact) / math.log(MAX_DISTANCE / max_exact)
                         * (nb - max_exact)).astype(jnp.int32)
    large = jnp.minimum(large, nb - 1)
    return bucket + jnp.where(n < max_exact, n, large)


def kernel(x, mem, norm_in, w_in, norm_q, norm_kv, w_uq, w_ukv, attn_sinks, rel_bias,
           norm_mem, w_mem_kv, w_out, norm_final):
    B, S, _ = x.shape
    assert norm_in.shape[0] == 1, "single-layer trunk"

    wt = w_in[0].T
    cos_t, sin_t = _rope_tables(S)
    q, k, v, qs, z, ks, vs = _proj(x, norm_in, wt, norm_q, norm_kv, w_uq[0].astype(BF16),
                                   w_ukv[0].astype(BF16), cos_t, sin_t, tm=512)

    y_mla = _mla(q, k, v, t=512, g=4, nq=4)

    bucket = _t5_bucket(jnp.arange(_REL_SPAN) - (2 * SWA_BLOCK - 1)).astype(jnp.int32)
    bias = _bias(rel_bias, bucket)
    kmem, vmem = _memkv(mem, norm_mem, w_mem_kv[0].astype(BF16))
    return _tail(attn_sinks[0], x, y_mla, z, qs, ks, vs, bias, kmem, vmem,
                 w_out[0], norm_final[None, :], nsub=4)
```

```python
import functools
import math

import jax
import jax.numpy as jnp
from jax import lax
from jax.experimental import pallas as pl
from jax.experimental.pallas import tpu as pltpu

D_MODEL = 2048
CHUNK = 64
N_MEM = 256
EPS = 1e-6
NEG = -1e30

MLA_HEADS = 8
MLA_NOPE = 128
MLA_ROPE = 64
MLA_V = 128
MLA_QLORA = 512
MLA_KVLORA = 256
MLA_WIDTH = MLA_HEADS * MLA_V
MLA_QK_PAD = 256
ROPE_THETA = 10000.0

SWA_HEADS = 8
SWA_KV_HEADS = 2
SWA_GROUP = SWA_HEADS // SWA_KV_HEADS
SWA_HEAD_DIM = 64
SWA_WIDTH = SWA_HEADS * SWA_HEAD_DIM
SWA_KV_WIDTH = SWA_KV_HEADS * SWA_HEAD_DIM
WINDOW_CHUNKS = 2
SWA_BLOCK = 128

MEM_HEADS = 4
MEM_HEAD_DIM = 128
MEM_WIDTH = MEM_HEADS * MEM_HEAD_DIM

MIX_WIDTH = MLA_WIDTH + SWA_WIDTH + MEM_WIDTH

N_BUCKETS = 32
MAX_DISTANCE = 128

BF16 = jnp.bfloat16
F32 = jnp.float32

_R_CQ = 0
_R_CKV = _R_CQ + MLA_QLORA
_R_KPE = _R_CKV + MLA_KVLORA
_R_ZMLA = _R_KPE + MLA_ROPE
_R_QSWA = _R_ZMLA + MLA_WIDTH
_R_KSWA = _R_QSWA + SWA_WIDTH
_R_VSWA = _R_KSWA + SWA_KV_WIDTH
_R_ZSWA = _R_VSWA + SWA_KV_WIDTH
_R_QMEM = _R_ZSWA + SWA_WIDTH
_R_ZMEM = _R_QMEM + MEM_WIDTH
IN_WIDTH = _R_ZMEM + MEM_WIDTH

_MLA_Q_SCALE = (MLA_NOPE + MLA_ROPE) ** -0.5 * math.log2(math.e)

_SWA_KV_PAD = 2 * SWA_KV_HEADS * 128

_W_IN_CHUNK = 208
_W_OUT_CHUNK = 256

_VMEM_LIMIT = 56 * 1024 * 1024


def _params(sem):
    return pltpu.CompilerParams(dimension_semantics=sem, vmem_limit_bytes=_VMEM_LIMIT)


def _rms(v, g):
    return (v * lax.rsqrt(jnp.mean(v * v, axis=-1, keepdims=True) + EPS)) * g


def _dot(a, b):
    return jnp.dot(a, b, preferred_element_type=F32)


def _dot_nt(a, b):
    return lax.dot_general(a, b, (((1,), (1,)), ((), ())), preferred_element_type=F32)


def _load_weight_bf16(w_hbm, w_ref, stage_ref, sem_ref):
    rows = stage_ref.shape[1]
    n_chunks = w_ref.shape[0] // rows

    def chunk_copy(c):
        return pltpu.make_async_copy(w_hbm.at[pl.ds(c * rows, rows), :], stage_ref.at[c % 2],
                                     sem_ref.at[c % 2])

    chunk_copy(0).start()
    for c in range(n_chunks):
        if c + 1 < n_chunks:
            chunk_copy(c + 1).start()
        chunk_copy(c).wait()
        w_ref[c * rows:(c + 1) * rows, :] = stage_ref[c % 2].astype(BF16)


def _first_grid_step():
    return jnp.logical_and(pl.program_id(0) == 0, pl.program_id(1) == 0)


def _lane_tile(v, n):
    return jnp.concatenate([v] * n, axis=1)


def _chunk_of(pos):
    return jnp.right_shift(pos, CHUNK.bit_length() - 1)


def _rope128(v, cos_t, sin_t):
    return v * cos_t + pltpu.roll(v, MLA_ROPE // 2, 1) * sin_t


def _store_swa_kv(ref, kv):
    low = lax.broadcasted_iota(jnp.int32, kv.shape, 1) < SWA_HEAD_DIM
    swapped = pltpu.roll(kv, SWA_HEAD_DIM, 1)
    pieces = (jnp.where(low, kv, 0.0), jnp.where(low, 0.0, swapped),
              jnp.where(low, swapped, 0.0), jnp.where(low, 0.0, kv))
    for i, piece in enumerate(pieces):
        ref[0, :, i * 128:(i + 1) * 128] = piece.astype(BF16)


def _proj_kernel(x_ref, gin_ref, wt_hbm, gq_ref, gkv_ref, wuq_ref, wukv_ref, cos_ref, sin_ref,
                 q_ref, k_ref, v_ref, qs_ref, z_ref, ks_ref, vs_ref, wt_ref, stage_ref, sem_ref):
    @pl.when(_first_grid_step())
    def _():
        _load_weight_bf16(wt_hbm, wt_ref, stage_ref, sem_ref)

    h = _rms(x_ref[0], gin_ref[...]).astype(BF16)
    cos_t = cos_ref[...]
    sin_t = sin_ref[...]

    def proj(lo, hi):
        return _dot_nt(h, wt_ref[lo:hi, :])

    pa = proj(_R_CQ, _R_KPE)
    cq = _rms(pa[:, :MLA_QLORA], gq_ref[...]).astype(BF16)
    ckv = _rms(pa[:, MLA_QLORA:], gkv_ref[...]).astype(BF16)

    qs_ref[0, :, :SWA_WIDTH] = proj(_R_QSWA, _R_KSWA).astype(BF16)
    qs_ref[0, :, SWA_WIDTH:] = proj(_R_QMEM, _R_ZMEM).astype(BF16)
    z_ref[0, :, :MLA_WIDTH] = proj(_R_ZMLA, _R_QSWA).astype(BF16)
    z_ref[0, :, MLA_WIDTH:MLA_WIDTH + SWA_WIDTH] = proj(_R_ZSWA, _R_QMEM).astype(BF16)
    z_ref[0, :, MLA_WIDTH + SWA_WIDTH:] = proj(_R_ZMEM, IN_WIDTH).astype(BF16)
    kvs = proj(_R_KSWA, _R_ZSWA)
    _store_swa_kv(ks_ref, kvs[:, :SWA_KV_WIDTH])
    _store_swa_kv(vs_ref, kvs[:, SWA_KV_WIDTH:])
    pe = proj(_R_KPE, _R_ZMLA)
    kpe = _rope128(jnp.concatenate([pe, pe], axis=1), cos_t, sin_t).astype(BF16)

    qall = _dot(cq, wuq_ref[...]) * _MLA_Q_SCALE
    kv = _dot(ckv, wukv_ref[...])
    for hd in range(MLA_HEADS):
        oq = hd * (MLA_NOPE + MLA_ROPE)
        q_pe = qall[:, oq + MLA_NOPE:oq + MLA_NOPE + MLA_ROPE]
        q_ref[0, hd, :, :MLA_NOPE] = qall[:, oq:oq + MLA_NOPE].astype(BF16)
        q_ref[0, hd, :, MLA_NOPE:] = _rope128(
            jnp.concatenate([q_pe, q_pe], axis=1), cos_t, sin_t).astype(BF16)
        o = hd * (MLA_NOPE + MLA_V)
        k_ref[0, hd, :, :MLA_NOPE] = kv[:, o:o + MLA_NOPE].astype(BF16)
        k_ref[0, hd, :, MLA_NOPE:] = kpe
        v_ref[0, hd] = kv[:, o + MLA_NOPE:o + MLA_QK_PAD].astype(BF16)


def _proj(x, gin, wt, gq, gkv, wuq, wukv, cos_t, sin_t, tm):
    B, S, _ = x.shape
    const = lambda b, i: (0, 0)
    row = lambda b, i: (b, i, 0)
    single = pl.Buffered(1)

    def out(width):
        return jax.ShapeDtypeStruct((B, S, width), BF16)

    head_spec = pl.BlockSpec((1, MLA_HEADS, tm, MLA_QK_PAD), lambda b, i: (b, 0, i, 0))
    head_shape = jax.ShapeDtypeStruct((B, MLA_HEADS, S, MLA_QK_PAD), BF16)

    return pl.pallas_call(
        _proj_kernel,
        grid=(B, S // tm),
        in_specs=[
            pl.BlockSpec((1, tm, D_MODEL), row),
            pl.BlockSpec((1, D_MODEL), const),
            pl.BlockSpec(memory_space=pl.ANY),
            pl.BlockSpec((1, MLA_QLORA), const),
            pl.BlockSpec((1, MLA_KVLORA), const),
            pl.BlockSpec((MLA_QLORA, MLA_HEADS * (MLA_NOPE + MLA_ROPE)), const, pipeline_mode=single),
            pl.BlockSpec((MLA_KVLORA, MLA_HEADS * (MLA_NOPE + MLA_V)), const, pipeline_mode=single),
            pl.BlockSpec((tm, 128), lambda b, i: (i, 0)),
            pl.BlockSpec((tm, 128), lambda b, i: (i, 0)),
        ],
        out_specs=[
            head_spec, head_spec,
            pl.BlockSpec((1, MLA_HEADS, tm, MLA_V), lambda b, i: (b, 0, i, 0)),
            pl.BlockSpec((1, tm, SWA_WIDTH + MEM_WIDTH), row),
            pl.BlockSpec((1, tm, MIX_WIDTH), row),
            pl.BlockSpec((1, tm, _SWA_KV_PAD), row),
            pl.BlockSpec((1, tm, _SWA_KV_PAD), row),
        ],
        out_shape=[
            head_shape, head_shape, jax.ShapeDtypeStruct((B, MLA_HEADS, S, MLA_V), BF16),
            out(SWA_WIDTH + MEM_WIDTH), out(MIX_WIDTH), out(_SWA_KV_PAD), out(_SWA_KV_PAD),
        ],
        scratch_shapes=[
            pltpu.VMEM((IN_WIDTH, D_MODEL), BF16),
            pltpu.VMEM((2, _W_IN_CHUNK, D_MODEL), F32),
            pltpu.SemaphoreType.DMA((2,)),
        ],
        compiler_params=_params(("arbitrary", "arbitrary")),
        name="proj",
    )(x, gin, wt, gq, gkv, wuq, wukv, cos_t, sin_t)


def _mla_kernel(q_ref, k_ref, v_ref, o_ref, acc_ref, m_ref, s_ref, cm_ref, *, t, g, nq):
    qi0 = pl.program_id(2) * nq
    qi_end = qi0 + nq - 1
    acc_ref[...] = jnp.zeros(acc_ref.shape, F32)
    m_ref[...] = jnp.full(m_ref.shape, NEG, F32)
    ones = jnp.ones((t, MLA_V), BF16)

    def row_max(s):
        return jnp.broadcast_to(jnp.max(s, axis=-1, keepdims=True), (t, 128))

    def diagonal_mask(qi, c):
        q_chunk = _chunk_of(qi * t + lax.broadcasted_iota(jnp.int32, (t, t), 0))
        k_chunk = _chunk_of(c * t + lax.broadcasted_iota(jnp.int32, (t, t), 1))
        return k_chunk <= q_chunk

    def scores(qi, c, slot, hd, mask=None):
        rows = pl.multiple_of((qi - qi0) * t, t)
        keys = pl.multiple_of(c * t, t)
        s = _dot_nt(q_ref[0, hd, pl.ds(rows, t), :], k_ref[0, hd, pl.ds(keys, t), :])
        if mask is not None:
            s = jnp.where(mask, s, NEG)
        s_ref[slot, hd] = s
        cm_ref[slot, hd] = row_max(s)

    def update(c, slot, hd, mask):
        keys = pl.multiple_of(c * t, t)
        s = s_ref[slot, hd]
        if mask is not None:
            s = jnp.where(mask, s, NEG)
            m_cur = row_max(s)
        else:
            m_cur = cm_ref[slot, hd]
        m_old = m_ref[hd]
        m_new = jnp.maximum(m_old, m_cur)
        alpha = jnp.exp2(m_old - m_new)
        p = jnp.exp2(s - _lane_tile(m_new, t // 128)).astype(BF16)
        v_ones = jnp.concatenate([v_ref[0, hd, pl.ds(keys, t), :], ones], axis=1)
        acc_ref[hd] = _lane_tile(alpha, MLA_QK_PAD // 128) * acc_ref[hd] + _dot(p, v_ones)
        m_ref[hd] = m_new

    def finish_tile(qi, hd):
        rows = pl.multiple_of((qi - qi0) * t, t)
        acc = acc_ref[hd]
        o_ref[0, pl.ds(rows, t), hd * MLA_V:(hd + 1) * MLA_V] = (
            acc[:, :MLA_V] / acc[:, MLA_V:]).astype(BF16)
        acc_ref[hd] = jnp.zeros(acc.shape, F32)
        m_ref[hd] = jnp.full((t, 128), NEG, F32)

    def following(qi, c):
        is_last = c == qi
        return jnp.where(is_last, jnp.minimum(qi + 1, qi_end), qi), jnp.where(is_last, 0, c + 1)

    def step(qi, c, slot, diagonal, premasked=False, mask_next=False):
        qn, cn = following(qi, c)
        mask = diagonal_mask(qi, c) if diagonal and not premasked else None
        next_mask = diagonal_mask(qn, cn) if mask_next else None
        for hd in range(g):
            scores(qn, cn, 1 - slot, hd, next_mask)
            update(c, slot, hd, mask)
            if diagonal:
                finish_tile(qi, hd)

    for hd in range(g):
        scores(qi0, 0, 0, hd)

    def body(_, pair):
        first, second = pair, following(*pair)
        diag0 = first[1] == first[0]
        diag1 = second[1] == second[0]
        for d0, d1, cond in ((True, False, diag0), (False, True, diag1),
                             (False, False, jnp.logical_not(jnp.logical_or(diag0, diag1)))):
            @pl.when(cond)
            def _(d0=d0, d1=d1):
                step(*first, 0, d0, mask_next=d1)
                step(*second, 1, d1, premasked=d1)
        return following(*second)

    n_steps = nq * qi0 + nq * (nq + 1) // 2
    lax.fori_loop(0, n_steps // 2, body, (qi0, 0))


def _mla(q, k, v, t, g, nq):
    B, _, S, _ = q.shape
    assert S % (nq * t) == 0 and nq % 2 == 0 and (nq * (nq + 1) // 2) % 2 == 0
    return pl.pallas_call(
        functools.partial(_mla_kernel, t=t, g=g, nq=nq),
        grid=(B, MLA_HEADS // g, S // (nq * t)),
        in_specs=[
            pl.BlockSpec((1, g, nq * t, MLA_QK_PAD), lambda b, h, i: (b, h, i, 0)),
            pl.BlockSpec((1, g, S, MLA_QK_PAD), lambda b, h, i: (b, h, 0, 0)),
            pl.BlockSpec((1, g, S, MLA_V), lambda b, h, i: (b, h, 0, 0)),
        ],
        out_specs=pl.BlockSpec((1, nq * t, g * MLA_V), lambda b, h, i: (b, i, h)),
        out_shape=jax.ShapeDtypeStruct((B, S, MLA_WIDTH), BF16),
        scratch_shapes=[
            pltpu.VMEM((g, t, MLA_QK_PAD), F32),
            pltpu.VMEM((g, t, 128), F32),
            pltpu.VMEM((2, g, t, t), F32),
            pltpu.VMEM((2, g, t, 128), F32),
        ],
        compiler_params=_params(("arbitrary", "arbitrary", "arbitrary")),
        name="mla",
    )(q, k, v)


_REL_SPAN = 4 * SWA_BLOCK
_N_OFFSETS = 3 * SWA_BLOCK - 1


def _bias_kernel(table_ref, bucket_ref, o_ref):
    head = lax.broadcasted_iota(jnp.int32, (SWA_HEADS, 128), 0)
    lane = lax.broadcasted_iota(jnp.int32, (SWA_HEADS, _REL_SPAN), 1)

    def lookup(c, per_offset):
        b = bucket_ref[c]
        col = jnp.zeros((SWA_HEADS, 128), F32)
        for hd in range(SWA_HEADS):
            col = jnp.where(head == hd, table_ref[b, hd], col)
        return jnp.where(lane == c, _lane_tile(col, _REL_SPAN // 128), per_offset)

    per_offset_all = lax.fori_loop(0, _N_OFFSETS, lookup, jnp.zeros((SWA_HEADS, _REL_SPAN), F32))
    for hd in range(SWA_HEADS):
        per_offset = per_offset_all[hd:hd + 1, :]
        rows = jnp.broadcast_to(per_offset, (SWA_BLOCK, _REL_SPAN))
        band = pltpu.roll(rows, _REL_SPAN - (SWA_BLOCK - 1), 1, stride=1, stride_axis=0)
        kvh, within = divmod(hd, SWA_GROUP)
        par, half = within % 2, within // 2
        o_ref[kvh, par, half * SWA_BLOCK:(half + 1) * SWA_BLOCK, :] = band[:, :2 * SWA_BLOCK]


def _bias(rel_table, bucket):
    return pl.pallas_call(
        _bias_kernel,
        in_specs=[
            pl.BlockSpec(memory_space=pltpu.SMEM),
            pl.BlockSpec(memory_space=pltpu.SMEM),
        ],
        out_specs=pl.BlockSpec(memory_space=pltpu.VMEM),
        out_shape=jax.ShapeDtypeStruct((SWA_KV_HEADS, 2, 2 * SWA_BLOCK, 2 * SWA_BLOCK), F32),
        name="t5bias",
    )(rel_table, bucket)


def _memkv_kernel(mem_ref, g_ref, w_ref, k_ref, v_ref):
    mn = _rms(mem_ref[0], g_ref[...]).astype(BF16)
    kv = _dot(mn, w_ref[...])
    k_ref[0] = kv[:, :MEM_WIDTH].astype(BF16)
    v_ref[0] = kv[:, MEM_WIDTH:].astype(BF16)


def _memkv(mem, g, w):
    B, M, _ = mem.shape
    const = lambda b: (0, 0)
    row = lambda b: (b, 0, 0)
    return pl.pallas_call(
        _memkv_kernel,
        grid=(B,),
        in_specs=[
            pl.BlockSpec((1, M, D_MODEL), row),
            pl.BlockSpec((1, D_MODEL), const),
            pl.BlockSpec((D_MODEL, 2 * MEM_WIDTH), const),
        ],
        out_specs=[pl.BlockSpec((1, M, MEM_WIDTH), row)] * 2,
        out_shape=[jax.ShapeDtypeStruct((B, M, MEM_WIDTH), BF16)] * 2,
        compiler_params=_params(("arbitrary",)),
        name="memkv",
    )(mem, g, w)


def _swa_into(y_ref, sink_ref, q_ref, kp_ref, kc_ref, vp_ref, vc_ref, bias_ref, kb_ref, vb_ref, nsub,
              beside_unit):
    t = pl.program_id(1)
    scale = SWA_HEAD_DIM ** -0.5
    sb = SWA_BLOCK
    kb_ref[:sb] = kp_ref[0]
    kb_ref[sb:] = kc_ref[0]
    vb_ref[:sb] = vp_ref[0]
    vb_ref[sb:] = vc_ref[0]

    row = lax.broadcasted_iota(jnp.int32, (2 * sb, 2 * sb), 0)
    q_chunk = _chunk_of(row & (sb - 1))
    b_chunk = _chunk_of(lax.broadcasted_iota(jnp.int32, (2 * sb, 2 * sb), 1))
    valid_any = jnp.logical_and(b_chunk >= q_chunk, b_chunk <= q_chunk + WINDOW_CHUNKS)
    first_lo = jnp.where(t > 0, 0, 2)
    valid_first = jnp.logical_and(valid_any, b_chunk >= first_lo)
    upper = lax.broadcasted_iota(jnp.int32, (2 * sb, 1), 0) < sb

    def geometry(u):
        r, kvh = divmod(u, SWA_KV_HEADS)
        rows = slice(r * sb, (r + 1) * sb)
        band = slice(r * sb, (r + 2) * sb)
        pair0 = slice(2 * kvh * 128, (2 * kvh + 1) * 128)
        pair1 = slice((2 * kvh + 1) * 128, (2 * kvh + 2) * 128)
        return r, kvh, rows, band, pair0, pair1

    def scores(u):
        r, kvh, rows, band, pair0, pair1 = geometry(u)
        q = jnp.concatenate([q_ref[0, rows, pair0], q_ref[0, rows, pair1]], axis=0)
        return [_dot_nt(q, kb_ref[band, (2 * kvh + par) * 128:(2 * kvh + par + 1) * 128])
                for par in range(2)]

    def attend(u, unit_scores):
        r, kvh, rows, band, pair0, pair1 = geometry(u)
        valid = valid_first if r == 0 else valid_any
        o = None
        for par, s in enumerate(unit_scores):
            kcols = slice((2 * kvh + par) * 128, (2 * kvh + par + 1) * 128)
            hd = SWA_GROUP * kvh + par
            s = jnp.where(valid, s * scale + bias_ref[kvh, par], NEG)
            sink = jnp.where(upper, sink_ref[hd], sink_ref[hd + 2])
            m = jnp.maximum(jnp.max(s, axis=-1, keepdims=True), sink)
            p = jnp.exp(s - m)
            den = jnp.sum(p, axis=-1, keepdims=True) + jnp.exp(sink - m)
            pv = _dot((p * (1.0 / den)).astype(BF16), vb_ref[band, kcols])
            o = pv if o is None else o + pv
        y_ref[rows, pair0] = o[:sb]
        y_ref[rows, pair1] = o[sb:]

    n_units = nsub * SWA_KV_HEADS
    pending = scores(0)
    for u in range(n_units):
        upcoming = scores(u + 1) if u + 1 < n_units else None
        beside_unit(u)
        attend(u, pending)
        pending = upcoming


def _mem_into(y_ref, q_ref, k_ref, v_ref, after_head):
    scale = MEM_HEAD_DIM ** -0.5

    def head_cols(hd):
        return slice(hd * MEM_HEAD_DIM, (hd + 1) * MEM_HEAD_DIM)

    def scores(hd):
        return _dot_nt(q_ref[0, :, head_cols(hd)], k_ref[0, :, head_cols(hd)])

    pending = scores(0)
    for hd in range(MEM_HEADS):
        upcoming = scores(hd + 1) if hd + 1 < MEM_HEADS else None
        s = pending * scale
        m = jnp.max(s, axis=-1, keepdims=True)
        p = jnp.exp(s - m)
        inv = 1.0 / jnp.sum(p, axis=-1, keepdims=True)
        y_ref[:, head_cols(hd)] = _dot((p * inv).astype(BF16), v_ref[0, :, head_cols(hd)])
        after_head(hd)
        pending = upcoming


def _tail_kernel(sink_ref, x_ref, ya_ref, za_ref, zb_ref, zc_ref, qs_ref, qm_ref,
                 kp_ref, kc_ref, vp_ref, vc_ref, bias_ref, km_ref, vm_ref, w_hbm, g_ref,
                 o_ref, kb_ref, vb_ref, yb_ref, yc_ref, yacc_ref, w_ref, stage_ref, sem_ref, *, nsub):
    @pl.when(_first_grid_step())
    def _():
        _load_weight_bf16(w_hbm, w_ref, stage_ref, sem_ref)

    def gated(y, z_ref):
        h = 0.5 * z_ref[0].astype(F32)
        return (y * (h + h * jnp.tanh(h))).astype(BF16)

    swa_lo, mem_lo = MLA_WIDTH, MLA_WIDTH + SWA_WIDTH

    g_mla = gated(ya_ref[0].astype(F32), za_ref)
    mla_chunk = D_MODEL // (nsub * SWA_KV_HEADS)

    def project_mla(u):
        cols = slice(u * mla_chunk, (u + 1) * mla_chunk)
        yacc_ref[:, cols] = _dot(g_mla, w_ref[:swa_lo, cols])

    _swa_into(yb_ref, sink_ref, qs_ref, kp_ref, kc_ref, vp_ref, vc_ref, bias_ref, kb_ref, vb_ref, nsub,
              project_mla)

    g_swa = gated(yb_ref[...], zb_ref)
    swa_chunk = D_MODEL // MEM_HEADS

    def project_swa(u):
        cols = slice(u * swa_chunk, (u + 1) * swa_chunk)
        yacc_ref[:, cols] += _dot(g_swa, w_ref[swa_lo:mem_lo, cols])

    _mem_into(yc_ref, qm_ref, km_ref, vm_ref, project_swa)

    g_mem = gated(yc_ref[...], zc_ref)
    sum_sq = jnp.zeros((g_mem.shape[0], 1), F32)
    chunks = [slice(j * swa_chunk, (j + 1) * swa_chunk) for j in range(D_MODEL // swa_chunk)]
    for cols in chunks:
        r = x_ref[0, :, cols] + yacc_ref[:, cols] + _dot(g_mem, w_ref[mem_lo:, cols])
        o_ref[0, :, cols] = r
        sum_sq = sum_sq + jnp.sum(r * r, axis=-1, keepdims=True)
    inv = lax.rsqrt(sum_sq * (1.0 / D_MODEL) + EPS)
    for cols in chunks:
        o_ref[0, :, cols] = (o_ref[0, :, cols] * inv) * g_ref[:, cols]


def _tail(sinks, x, y_mla, z, qs, ks, vs, bias, kmem, vmem, w, g, nsub):
    B, S, _ = x.shape
    tm = nsub * SWA_BLOCK
    M = kmem.shape[1]
    row = lambda b, i: (b, i, 0)
    prev = lambda b, i: (b, jnp.maximum(i * nsub - 1, 0), 0)
    col = lambda c: (lambda b, i: (b, i, c))
    return pl.pallas_call(
        functools.partial(_tail_kernel, nsub=nsub),
        grid=(B, S // tm),
        in_specs=[
            pl.BlockSpec(memory_space=pltpu.SMEM),
            pl.BlockSpec((1, tm, D_MODEL), row),
            pl.BlockSpec((1, tm, MLA_WIDTH), row),
            pl.BlockSpec((1, tm, MLA_WIDTH), col(0)),
            pl.BlockSpec((1, tm, SWA_WIDTH), col(MLA_WIDTH // SWA_WIDTH)),
            pl.BlockSpec((1, tm, MEM_WIDTH), col((MLA_WIDTH + SWA_WIDTH) // MEM_WIDTH)),
            pl.BlockSpec((1, tm, SWA_WIDTH), col(0)),
            pl.BlockSpec((1, tm, MEM_WIDTH), col(1)),
            pl.BlockSpec((1, SWA_BLOCK, _SWA_KV_PAD), prev),
            pl.BlockSpec((1, tm, _SWA_KV_PAD), row),
            pl.BlockSpec((1, SWA_BLOCK, _SWA_KV_PAD), prev),
            pl.BlockSpec((1, tm, _SWA_KV_PAD), row),
            pl.BlockSpec((SWA_KV_HEADS, 2, 2 * SWA_BLOCK, 2 * SWA_BLOCK), lambda b, i: (0, 0, 0, 0)),
            pl.BlockSpec((1, M, MEM_WIDTH), lambda b, i: (b, 0, 0)),
            pl.BlockSpec((1, M, MEM_WIDTH), lambda b, i: (b, 0, 0)),
            pl.BlockSpec(memory_space=pl.ANY),
            pl.BlockSpec((1, D_MODEL), lambda b, i: (0, 0)),
        ],
        out_specs=pl.BlockSpec((1, tm, D_MODEL), row),
        out_shape=jax.ShapeDtypeStruct((B, S, D_MODEL), F32),
        scratch_shapes=[
            pltpu.VMEM((tm + SWA_BLOCK, _SWA_KV_PAD), BF16),
            pltpu.VMEM((tm + SWA_BLOCK, _SWA_KV_PAD), BF16),
            pltpu.VMEM((tm, SWA_WIDTH), F32),
            pltpu.VMEM((tm, MEM_WIDTH), F32),
            pltpu.VMEM((tm, D_MODEL), F32),
            pltpu.VMEM((MIX_WIDTH, D_MODEL), BF16),
            pltpu.VMEM((2, _W_OUT_CHUNK, D_MODEL), F32),
            pltpu.SemaphoreType.DMA((2,)),
        ],
        compiler_params=_params(("arbitrary", "arbitrary")),
        name="tail",
    )(sinks, x, y_mla, z, z, z, qs, qs, ks, ks, vs, vs, bias, kmem, vmem, w, g)


def _rope_tables(seq):
    inv = 1.0 / (ROPE_THETA ** (jnp.arange(0, MLA_ROPE, 2, dtype=F32) / MLA_ROPE))
    ang = jnp.arange(seq, dtype=F32)[:, None] * inv[None, :]
    cos, sin = jnp.cos(ang), jnp.sin(ang)
    zero = jnp.zeros_like(cos)
    return (jnp.concatenate([cos, cos, zero, zero], axis=-1),
            jnp.concatenate([-sin, sin, zero, zero], axis=-1))


def _t5_bucket(rel):
    nb = N_BUCKETS // 2
    max_exact = nb // 2
    bucket = jnp.where(rel > 0, nb, 0)
    n = jnp.abs(rel)
    nf = jnp.maximum(n, 1).astype(F32)
    large = max_exact + (jnp.log(nf / max_exact) / math.log(MAX_DISTANCE / max_exact)
                         * (nb - max_exact)).astype(jnp.int32)
    large = jnp.minimum(large, nb - 1)
    return bucket + jnp.where(n < max_exact, n, large)


def kernel(x, mem, norm_in, w_in, norm_q, norm_kv, w_uq, w_ukv, attn_sinks, rel_bias,
           norm_mem, w_mem_kv, w_out, norm_final):
    B, S, _ = x.shape
    assert norm_in.shape[0] == 1, "single-layer trunk"

    wt = w_in[0].T
    cos_t, sin_t = _rope_tables(S)
    q, k, v, qs, z, ks, vs = _proj(x, norm_in, wt, norm_q, norm_kv, w_uq[0].astype(BF16),
                                   w_ukv[0].astype(BF16), cos_t, sin_t, tm=512)

    y_mla = _mla(q, k, v, t=512, g=4, nq=4)

    bucket = _t5_bucket(jnp.arange(_REL_SPAN) - (2 * SWA_BLOCK - 1)).astype(jnp.int32)
    bias = _bias(rel_bias, bucket)
    kmem, vmem = _memkv(mem, norm_mem, w_mem_kv[0].astype(BF16))
    return _tail(attn_sinks[0], x, y_mla, z, qs, ks, vs, bias, kmem, vmem,
                 w_out[0], norm_final[None, :], nsub=4)
```

```python
import functools
import math

import jax
import jax.numpy as jnp
from jax import lax
from jax.experimental import pallas as pl
from jax.experimental.pallas import tpu as pltpu

D_MODEL = 2048
CHUNK = 64
N_MEM = 256
EPS = 1e-6
NEG = -1e30

MLA_HEADS = 8
MLA_NOPE = 128
MLA_ROPE = 64
MLA_V = 128
MLA_QLORA = 512
MLA_KVLORA = 256
MLA_WIDTH = MLA_HEADS * MLA_V
MLA_QK_PAD = 256
ROPE_THETA = 10000.0

SWA_HEADS = 8
SWA_KV_HEADS = 2
SWA_GROUP = SWA_HEADS // SWA_KV_HEADS
SWA_HEAD_DIM = 64
SWA_WIDTH = SWA_HEADS * SWA_HEAD_DIM
SWA_KV_WIDTH = SWA_KV_HEADS * SWA_HEAD_DIM
WINDOW_CHUNKS = 2
SWA_BLOCK = 128

MEM_HEADS = 4
MEM_HEAD_DIM = 128
MEM_WIDTH = MEM_HEADS * MEM_HEAD_DIM

MIX_WIDTH = MLA_WIDTH + SWA_WIDTH + MEM_WIDTH

N_BUCKETS = 32
MAX_DISTANCE = 128

BF16 = jnp.bfloat16
F32 = jnp.float32

_R_CQ = 0
_R_CKV = _R_CQ + MLA_QLORA
_R_KPE = _R_CKV + MLA_KVLORA
_R_ZMLA = _R_KPE + MLA_ROPE
_R_QSWA = _R_ZMLA + MLA_WIDTH
_R_KSWA = _R_QSWA + SWA_WIDTH
_R_VSWA = _R_KSWA + SWA_KV_WIDTH
_R_ZSWA = _R_VSWA + SWA_KV_WIDTH
_R_QMEM = _R_ZSWA + SWA_WIDTH
_R_ZMEM = _R_QMEM + MEM_WIDTH
IN_WIDTH = _R_ZMEM + MEM_WIDTH

_MLA_Q_SCALE = (MLA_NOPE + MLA_ROPE) ** -0.5 * math.log2(math.e)

_SWA_KV_PAD = 2 * SWA_KV_HEADS * 128

_W_IN_CHUNK = 208
_W_OUT_CHUNK = 256

_V7X_VMEM_BYTES = 64 * 1024 * 1024
_VMEM_LIMIT = _V7X_VMEM_BYTES - 8 * 1024 * 1024


def _params(sem):
    return pltpu.CompilerParams(dimension_semantics=sem, vmem_limit_bytes=_VMEM_LIMIT)


def _rms(v, g):
    return (v * lax.rsqrt(jnp.mean(v * v, axis=-1, keepdims=True) + EPS)) * g


def _dot(a, b):
    return jnp.dot(a, b, preferred_element_type=F32)


def _dot_nt(a, b):
    return lax.dot_general(a, b, (((1,), (1,)), ((), ())), preferred_element_type=F32)


def _load_weight_bf16(w_hbm, w_ref, stage_ref, sem_ref):
    rows = stage_ref.shape[1]
    n_chunks = w_ref.shape[0] // rows

    def chunk_copy(c):
        return pltpu.make_async_copy(w_hbm.at[pl.ds(c * rows, rows), :], stage_ref.at[c % 2],
                                     sem_ref.at[c % 2])

    chunk_copy(0).start()
    for c in range(n_chunks):
        if c + 1 < n_chunks:
            chunk_copy(c + 1).start()
        chunk_copy(c).wait()
        w_ref[c * rows:(c + 1) * rows, :] = stage_ref[c % 2].astype(BF16)


def _first_grid_step():
    return jnp.logical_and(pl.program_id(0) == 0, pl.program_id(1) == 0)


def _lane_tile(v, n):
    return jnp.concatenate([v] * n, axis=1)


def _chunk_of(pos):
    return jnp.right_shift(pos, CHUNK.bit_length() - 1)


def _rope128(v, cos_t, sin_t):
    return v * cos_t + pltpu.roll(v, MLA_ROPE // 2, 1) * sin_t


def _store_swa_kv(ref, kv):
    low = lax.broadcasted_iota(jnp.int32, kv.shape, 1) < SWA_HEAD_DIM
    swapped = pltpu.roll(kv, SWA_HEAD_DIM, 1)
    pieces = (jnp.where(low, kv, 0.0), jnp.where(low, 0.0, swapped),
              jnp.where(low, swapped, 0.0), jnp.where(low, 0.0, kv))
    for i, piece in enumerate(pieces):
        ref[0, :, i * 128:(i + 1) * 128] = piece.astype(BF16)


def _proj_kernel(x_ref, gin_ref, wt_hbm, gq_ref, gkv_ref, wuq_ref, wukv_ref, cos_ref, sin_ref,
                 q_ref, k_ref, v_ref, qs_ref, z_ref, ks_ref, vs_ref, wt_ref, stage_ref, sem_ref):
    @pl.when(_first_grid_step())
    def _():
        _load_weight_bf16(wt_hbm, wt_ref, stage_ref, sem_ref)

    h = _rms(x_ref[0], gin_ref[...]).astype(BF16)
    cos_t = cos_ref[...]
    sin_t = sin_ref[...]

    def proj(lo, hi):
        return _dot_nt(h, wt_ref[lo:hi, :])

    pa = proj(_R_CQ, _R_KPE)
    cq = _rms(pa[:, :MLA_QLORA], gq_ref[...]).astype(BF16)
    ckv = _rms(pa[:, MLA_QLORA:], gkv_ref[...]).astype(BF16)

    qs_ref[0, :, :SWA_WIDTH] = proj(_R_QSWA, _R_KSWA).astype(BF16)
    qs_ref[0, :, SWA_WIDTH:] = proj(_R_QMEM, _R_ZMEM).astype(BF16)
    z_ref[0, :, :MLA_WIDTH] = proj(_R_ZMLA, _R_QSWA).astype(BF16)
    z_ref[0, :, MLA_WIDTH:MLA_WIDTH + SWA_WIDTH] = proj(_R_ZSWA, _R_QMEM).astype(BF16)
    z_ref[0, :, MLA_WIDTH + SWA_WIDTH:] = proj(_R_ZMEM, IN_WIDTH).astype(BF16)
    kvs = proj(_R_KSWA, _R_ZSWA)
    _store_swa_kv(ks_ref, kvs[:, :SWA_KV_WIDTH])
    _store_swa_kv(vs_ref, kvs[:, SWA_KV_WIDTH:])
    pe = proj(_R_KPE, _R_ZMLA)
    kpe = _rope128(jnp.concatenate([pe, pe], axis=1), cos_t, sin_t).astype(BF16)

    qall = _dot(cq, wuq_ref[...]) * _MLA_Q_SCALE
    kv = _dot(ckv, wukv_ref[...])
    for hd in range(MLA_HEADS):
        oq = hd * (MLA_NOPE + MLA_ROPE)
        q_pe = qall[:, oq + MLA_NOPE:oq + MLA_NOPE + MLA_ROPE]
        q_ref[0, hd, :, :MLA_NOPE] = qall[:, oq:oq + MLA_NOPE].astype(BF16)
        q_ref[0, hd, :, MLA_NOPE:] = _rope128(
            jnp.concatenate([q_pe, q_pe], axis=1), cos_t, sin_t).astype(BF16)
        o = hd * (MLA_NOPE + MLA_V)
        k_ref[0, hd, :, :MLA_NOPE] = kv[:, o:o + MLA_NOPE].astype(BF16)
        k_ref[0, hd, :, MLA_NOPE:] = kpe
        v_ref[0, hd] = kv[:, o + MLA_NOPE:o + MLA_QK_PAD].astype(BF16)


def _proj(x, gin, wt, gq, gkv, wuq, wukv, cos_t, sin_t, tm):
    B, S, _ = x.shape
    const = lambda b, i: (0, 0)
    row = lambda b, i: (b, i, 0)
    single = pl.Buffered(1)

    def out(width):
        return jax.ShapeDtypeStruct((B, S, width), BF16)

    head_spec = pl.BlockSpec((1, MLA_HEADS, tm, MLA_QK_PAD), lambda b, i: (b, 0, i, 0))
    head_shape = jax.ShapeDtypeStruct((B, MLA_HEADS, S, MLA_QK_PAD), BF16)

    return pl.pallas_call(
        _proj_kernel,
        grid=(B, S // tm),
        in_specs=[
            pl.BlockSpec((1, tm, D_MODEL), row),
            pl.BlockSpec((1, D_MODEL), const),
            pl.BlockSpec(memory_space=pl.ANY),
            pl.BlockSpec((1, MLA_QLORA), const),
            pl.BlockSpec((1, MLA_KVLORA), const),
            pl.BlockSpec((MLA_QLORA, MLA_HEADS * (MLA_NOPE + MLA_ROPE)), const, pipeline_mode=single),
            pl.BlockSpec((MLA_KVLORA, MLA_HEADS * (MLA_NOPE + MLA_V)), const, pipeline_mode=single),
            pl.BlockSpec((tm, 128), lambda b, i: (i, 0)),
            pl.BlockSpec((tm, 128), lambda b, i: (i, 0)),
        ],
        out_specs=[
            head_spec, head_spec,
            pl.BlockSpec((1, MLA_HEADS, tm, MLA_V), lambda b, i: (b, 0, i, 0)),
            pl.BlockSpec((1, tm, SWA_WIDTH + MEM_WIDTH), row),
            pl.BlockSpec((1, tm, MIX_WIDTH), row),
            pl.BlockSpec((1, tm, _SWA_KV_PAD), row),
            pl.BlockSpec((1, tm, _SWA_KV_PAD), row),
        ],
        out_shape=[
            head_shape, head_shape, jax.ShapeDtypeStruct((B, MLA_HEADS, S, MLA_V), BF16),
            out(SWA_WIDTH + MEM_WIDTH), out(MIX_WIDTH), out(_SWA_KV_PAD), out(_SWA_KV_PAD),
        ],
        scratch_shapes=[
            pltpu.VMEM((IN_WIDTH, D_MODEL), BF16),
            pltpu.VMEM((2, _W_IN_CHUNK, D_MODEL), F32),
            pltpu.SemaphoreType.DMA((2,)),
        ],
        compiler_params=_params(("arbitrary", "arbitrary")),
        name="proj",
    )(x, gin, wt, gq, gkv, wuq, wukv, cos_t, sin_t)


def _mla_kernel(q_ref, k_ref, v_ref, o_ref, acc_ref, m_ref, s_ref, cm_ref, *, t, g, nq):
    qi0 = pl.program_id(2) * nq
    qi_end = qi0 + nq - 1
    acc_ref[...] = jnp.zeros(acc_ref.shape, F32)
    m_ref[...] = jnp.full(m_ref.shape, NEG, F32)
    ones = jnp.ones((t, MLA_V), BF16)

    def row_max(s):
        return jnp.broadcast_to(jnp.max(s, axis=-1, keepdims=True), (t, 128))

    def diagonal_mask(qi, c):
        q_chunk = _chunk_of(qi * t + lax.broadcasted_iota(jnp.int32, (t, t), 0))
        k_chunk = _chunk_of(c * t + lax.broadcasted_iota(jnp.int32, (t, t), 1))
        return k_chunk <= q_chunk

    def scores(qi, c, slot, hd, mask=None):
        rows = pl.multiple_of((qi - qi0) * t, t)
        keys = pl.multiple_of(c * t, t)
        s = _dot_nt(q_ref[0, hd, pl.ds(rows, t), :], k_ref[0, hd, pl.ds(keys, t), :])
        if mask is not None:
            s = jnp.where(mask, s, NEG)
        s_ref[slot, hd] = s
        cm_ref[slot, hd] = row_max(s)

    def update(c, slot, hd, mask):
        keys = pl.multiple_of(c * t, t)
        s = s_ref[slot, hd]
        if mask is not None:
            s = jnp.where(mask, s, NEG)
            m_cur = row_max(s)
        else:
            m_cur = cm_ref[slot, hd]
        m_old = m_ref[hd]
        m_new = jnp.maximum(m_old, m_cur)
        alpha = jnp.exp2(m_old - m_new)
        p = jnp.exp2(s - _lane_tile(m_new, t // 128)).astype(BF16)
        v_ones = jnp.concatenate([v_ref[0, hd, pl.ds(keys, t), :], ones], axis=1)
        acc_ref[hd] = _lane_tile(alpha, MLA_QK_PAD // 128) * acc_ref[hd] + _dot(p, v_ones)
        m_ref[hd] = m_new

    def finish_tile(qi, hd):
        rows = pl.multiple_of((qi - qi0) * t, t)
        acc = acc_ref[hd]
        o_ref[0, pl.ds(rows, t), hd * MLA_V:(hd + 1) * MLA_V] = (
            acc[:, :MLA_V] / acc[:, MLA_V:]).astype(BF16)
        acc_ref[hd] = jnp.zeros(acc.shape, F32)
        m_ref[hd] = jnp.full((t, 128), NEG, F32)

    def following(qi, c):
        is_last = c == qi
        return jnp.where(is_last, jnp.minimum(qi + 1, qi_end), qi), jnp.where(is_last, 0, c + 1)

    def step(qi, c, slot, diagonal, premasked=False, mask_next=False):
        qn, cn = following(qi, c)
        mask = diagonal_mask(qi, c) if diagonal and not premasked else None
        next_mask = diagonal_mask(qn, cn) if mask_next else None
        for hd in range(g):
            scores(qn, cn, 1 - slot, hd, next_mask)
            update(c, slot, hd, mask)
            if diagonal:
                finish_tile(qi, hd)

    for hd in range(g):
        scores(qi0, 0, 0, hd)

    def body(_, pair):
        first, second = pair, following(*pair)
        diag0 = first[1] == first[0]
        diag1 = second[1] == second[0]
        for d0, d1, cond in ((True, False, diag0), (False, True, diag1),
                             (False, False, jnp.logical_not(jnp.logical_or(diag0, diag1)))):
            @pl.when(cond)
            def _(d0=d0, d1=d1):
                step(*first, 0, d0, mask_next=d1)
                step(*second, 1, d1, premasked=d1)
        return following(*second)

    n_steps = nq * qi0 + nq * (nq + 1) // 2
    lax.fori_loop(0, n_steps // 2, body, (qi0, 0))


def _mla(q, k, v, t, g, nq):
    B, _, S, _ = q.shape
    assert S % (nq * t) == 0 and nq % 2 == 0 and (nq * (nq + 1) // 2) % 2 == 0
    return pl.pallas_call(
        functools.partial(_mla_kernel, t=t, g=g, nq=nq),
        grid=(B, MLA_HEADS // g, S // (nq * t)),
        in_specs=[
            pl.BlockSpec((1, g, nq * t, MLA_QK_PAD), lambda b, h, i: (b, h, i, 0)),
            pl.BlockSpec((1, g, S, MLA_QK_PAD), lambda b, h, i: (b, h, 0, 0)),
            pl.BlockSpec((1, g, S, MLA_V), lambda b, h, i: (b, h, 0, 0)),
        ],
        out_specs=pl.BlockSpec((1, nq * t, g * MLA_V), lambda b, h, i: (b, i, h)),
        out_shape=jax.ShapeDtypeStruct((B, S, MLA_WIDTH), BF16),
        scratch_shapes=[
            pltpu.VMEM((g, t, MLA_QK_PAD), F32),
            pltpu.VMEM((g, t, 128), F32),
            pltpu.VMEM((2, g, t, t), F32),
            pltpu.VMEM((2, g, t, 128), F32),
        ],
        compiler_params=_params(("arbitrary", "arbitrary", "arbitrary")),
        name="mla",
    )(q, k, v)


_REL_SPAN = 4 * SWA_BLOCK
_N_OFFSETS = 3 * SWA_BLOCK - 1


def _bias_kernel(table_ref, bucket_ref, o_ref):
    head = lax.broadcasted_iota(jnp.int32, (SWA_HEADS, 128), 0)
    lane = lax.broadcasted_iota(jnp.int32, (SWA_HEADS, _REL_SPAN), 1)

    def lookup(c, per_offset):
        b = bucket_ref[c]
        col = jnp.zeros((SWA_HEADS, 128), F32)
        for hd in range(SWA_HEADS):
            col = jnp.where(head == hd, table_ref[b, hd], col)
        return jnp.where(lane == c, _lane_tile(col, _REL_SPAN // 128), per_offset)

    per_offset_all = lax.fori_loop(0, _N_OFFSETS, lookup, jnp.zeros((SWA_HEADS, _REL_SPAN), F32))
    for hd in range(SWA_HEADS):
        per_offset = per_offset_all[hd:hd + 1, :]
        rows = jnp.broadcast_to(per_offset, (SWA_BLOCK, _REL_SPAN))
        band = pltpu.roll(rows, _REL_SPAN - (SWA_BLOCK - 1), 1, stride=1, stride_axis=0)
        kvh, within = divmod(hd, SWA_GROUP)
        par, half = within % 2, within // 2
        o_ref[kvh, par, half * SWA_BLOCK:(half + 1) * SWA_BLOCK, :] = band[:, :2 * SWA_BLOCK]


def _bias(rel_table, bucket):
    return pl.pallas_call(
        _bias_kernel,
        in_specs=[
            pl.BlockSpec(memory_space=pltpu.SMEM),
            pl.BlockSpec(memory_space=pltpu.SMEM),
        ],
        out_specs=pl.BlockSpec(memory_space=pltpu.VMEM),
        out_shape=jax.ShapeDtypeStruct((SWA_KV_HEADS, 2, 2 * SWA_BLOCK, 2 * SWA_BLOCK), F32),
        name="t5bias",
    )(rel_table, bucket)


def _memkv_kernel(mem_ref, g_ref, w_ref, k_ref, v_ref):
    mn = _rms(mem_ref[0], g_ref[...]).astype(BF16)
    kv = _dot(mn, w_ref[...])
    k_ref[0] = kv[:, :MEM_WIDTH].astype(BF16)
    v_ref[0] = kv[:, MEM_WIDTH:].astype(BF16)


def _memkv(mem, g, w):
    B, M, _ = mem.shape
    const = lambda b: (0, 0)
    row = lambda b: (b, 0, 0)
    return pl.pallas_call(
        _memkv_kernel,
        grid=(B,),
        in_specs=[
            pl.BlockSpec((1, M, D_MODEL), row),
            pl.BlockSpec((1, D_MODEL), const),
            pl.BlockSpec((D_MODEL, 2 * MEM_WIDTH), const),
        ],
        out_specs=[pl.BlockSpec((1, M, MEM_WIDTH), row)] * 2,
        out_shape=[jax.ShapeDtypeStruct((B, M, MEM_WIDTH), BF16)] * 2,
        compiler_params=_params(("arbitrary",)),
        name="memkv",
    )(mem, g, w)


def _swa_into(y_ref, sink_ref, q_ref, kp_ref, kc_ref, vp_ref, vc_ref, bias_ref, kb_ref, vb_ref, nsub,
              beside_unit):
    t = pl.program_id(1)
    scale = SWA_HEAD_DIM ** -0.5
    sb = SWA_BLOCK
    kb_ref[:sb] = kp_ref[0]
    kb_ref[sb:] = kc_ref[0]
    vb_ref[:sb] = vp_ref[0]
    vb_ref[sb:] = vc_ref[0]

    row = lax.broadcasted_iota(jnp.int32, (2 * sb, 2 * sb), 0)
    q_chunk = _chunk_of(row & (sb - 1))
    b_chunk = _chunk_of(lax.broadcasted_iota(jnp.int32, (2 * sb, 2 * sb), 1))
    valid_any = jnp.logical_and(b_chunk >= q_chunk, b_chunk <= q_chunk + WINDOW_CHUNKS)
    first_lo = jnp.where(t > 0, 0, 2)
    valid_first = jnp.logical_and(valid_any, b_chunk >= first_lo)
    upper = lax.broadcasted_iota(jnp.int32, (2 * sb, 1), 0) < sb

    def geometry(u):
        r, kvh = divmod(u, SWA_KV_HEADS)
        rows = slice(r * sb, (r + 1) * sb)
        band = slice(r * sb, (r + 2) * sb)
        pair0 = slice(2 * kvh * 128, (2 * kvh + 1) * 128)
        pair1 = slice((2 * kvh + 1) * 128, (2 * kvh + 2) * 128)
        return r, kvh, rows, band, pair0, pair1

    def scores(u):
        r, kvh, rows, band, pair0, pair1 = geometry(u)
        q = jnp.concatenate([q_ref[0, rows, pair0], q_ref[0, rows, pair1]], axis=0)
        return [_dot_nt(q, kb_ref[band, (2 * kvh + par) * 128:(2 * kvh + par + 1) * 128])
                for par in range(2)]

    def attend(u, unit_scores):
        r, kvh, rows, band, pair0, pair1 = geometry(u)
        valid = valid_first if r == 0 else valid_any
        o = None
        for par, s in enumerate(unit_scores):
            kcols = slice((2 * kvh + par) * 128, (2 * kvh + par + 1) * 128)
            hd = SWA_GROUP * kvh + par
            s = jnp.where(valid, s * scale + bias_ref[kvh, par], NEG)
            sink = jnp.where(upper, sink_ref[hd], sink_ref[hd + 2])
            m = jnp.maximum(jnp.max(s, axis=-1, keepdims=True), sink)
            p = jnp.exp(s - m)
            den = jnp.sum(p, axis=-1, keepdims=True) + jnp.exp(sink - m)
            pv = _dot((p * (1.0 / den)).astype(BF16), vb_ref[band, kcols])
            o = pv if o is None else o + pv
        y_ref[rows, pair0] = o[:sb]
        y_ref[rows, pair1] = o[sb:]

    n_units = nsub * SWA_KV_HEADS
    pending = scores(0)
    for u in range(n_units):
        upcoming = scores(u + 1) if u + 1 < n_units else None
        beside_unit(u)
        attend(u, pending)
        pending = upcoming


def _mem_into(y_ref, q_ref, k_ref, v_ref, after_head):
    scale = MEM_HEAD_DIM ** -0.5

    def head_cols(hd):
        return slice(hd * MEM_HEAD_DIM, (hd + 1) * MEM_HEAD_DIM)

    def scores(hd):
        return _dot_nt(q_ref[0, :, head_cols(hd)], k_ref[0, :, head_cols(hd)])

    pending = scores(0)
    for hd in range(MEM_HEADS):
        upcoming = scores(hd + 1) if hd + 1 < MEM_HEADS else None
        s = pending * scale
        m = jnp.max(s, axis=-1, keepdims=True)
        p = jnp.exp(s - m)
        inv = 1.0 / jnp.sum(p, axis=-1, keepdims=True)
        y_ref[:, head_cols(hd)] = _dot((p * inv).astype(BF16), v_ref[0, :, head_cols(hd)])
        after_head(hd)
        pending = upcoming


def _tail_kernel(sink_ref, x_ref, ya_ref, za_ref, zb_ref, zc_ref, qs_ref, qm_ref,
                 kp_ref, kc_ref, vp_ref, vc_ref, bias_ref, km_ref, vm_ref, w_hbm, g_ref,
                 o_ref, kb_ref, vb_ref, yb_ref, yc_ref, yacc_ref, w_ref, stage_ref, sem_ref, *, nsub):
    @pl.when(_first_grid_step())
    def _():
        _load_weight_bf16(w_hbm, w_ref, stage_ref, sem_ref)

    def gated(y, z_ref):
        h = 0.5 * z_ref[0].astype(F32)
        return (y * (h + h * jnp.tanh(h))).astype(BF16)

    swa_lo, mem_lo = MLA_WIDTH, MLA_WIDTH + SWA_WIDTH

    g_mla = gated(ya_ref[0].astype(F32), za_ref)
    mla_chunk = D_MODEL // (nsub * SWA_KV_HEADS)

    def project_mla(u):
        cols = slice(u * mla_chunk, (u + 1) * mla_chunk)
        yacc_ref[:, cols] = _dot(g_mla, w_ref[:swa_lo, cols])

    _swa_into(yb_ref, sink_ref, qs_ref, kp_ref, kc_ref, vp_ref, vc_ref, bias_ref, kb_ref, vb_ref, nsub,
              project_mla)

    g_swa = gated(yb_ref[...], zb_ref)
    swa_chunk = D_MODEL // MEM_HEADS

    def project_swa(u):
        cols = slice(u * swa_chunk, (u + 1) * swa_chunk)
        yacc_ref[:, cols] += _dot(g_swa, w_ref[swa_lo:mem_lo, cols])

    _mem_into(yc_ref, qm_ref, km_ref, vm_ref, project_swa)

    g_mem = gated(yc_ref[...], zc_ref)
    sum_sq = jnp.zeros((g_mem.shape[0], 1), F32)
    chunks = [slice(j * swa_chunk, (j + 1) * swa_chunk) for j in range(D_MODEL // swa_chunk)]
    for cols in chunks:
        r = x_ref[0, :, cols] + yacc_ref[:, cols] + _dot(g_mem, w_ref[mem_lo:, cols])
        o_ref[0, :, cols] = r
        sum_sq = sum_sq + jnp.sum(r * r, axis=-1, keepdims=True)
    inv = lax.rsqrt(sum_sq * (1.0 / D_MODEL) + EPS)
    for cols in chunks:
        o_ref[0, :, cols] = (o_ref[0, :, cols] * inv) * g_ref[:, cols]


def _tail(sinks, x, y_mla, z, qs, ks, vs, bias, kmem, vmem, w, g, nsub):
    B, S, _ = x.shape
    tm = nsub * SWA_BLOCK
    M = kmem.shape[1]
    row = lambda b, i: (b, i, 0)
    prev = lambda b, i: (b, jnp.maximum(i * nsub - 1, 0), 0)
    col = lambda c: (lambda b, i: (b, i, c))
    return pl.pallas_call(
        functools.partial(_tail_kernel, nsub=nsub),
        grid=(B, S // tm),
        in_specs=[
            pl.BlockSpec(memory_space=pltpu.SMEM),
            pl.BlockSpec((1, tm, D_MODEL), row),
            pl.BlockSpec((1, tm, MLA_WIDTH), row),
            pl.BlockSpec((1, tm, MLA_WIDTH), col(0)),
            pl.BlockSpec((1, tm, SWA_WIDTH), col(MLA_WIDTH // SWA_WIDTH)),
            pl.BlockSpec((1, tm, MEM_WIDTH), col((MLA_WIDTH + SWA_WIDTH) // MEM_WIDTH)),
            pl.BlockSpec((1, tm, SWA_WIDTH), col(0)),
            pl.BlockSpec((1, tm, MEM_WIDTH), col(1)),
            pl.BlockSpec((1, SWA_BLOCK, _SWA_KV_PAD), prev),
            pl.BlockSpec((1, tm, _SWA_KV_PAD), row),
            pl.BlockSpec((1, SWA_BLOCK, _SWA_KV_PAD), prev),
            pl.BlockSpec((1, tm, _SWA_KV_PAD), row),
            pl.BlockSpec((SWA_KV_HEADS, 2, 2 * SWA_BLOCK, 2 * SWA_BLOCK), lambda b, i: (0, 0, 0, 0)),
            pl.BlockSpec((1, M, MEM_WIDTH), lambda b, i: (b, 0, 0)),
            pl.BlockSpec((1, M, MEM_WIDTH), lambda b, i: (b, 0, 0)),
            pl.BlockSpec(memory_space=pl.ANY),
            pl.BlockSpec((1, D_MODEL), lambda b, i: (0, 0)),
        ],
        out_specs=pl.BlockSpec((1, tm, D_MODEL), row),
        out_shape=jax.ShapeDtypeStruct((B, S, D_MODEL), F32),
        scratch_shapes=[
            pltpu.VMEM((tm + SWA_BLOCK, _SWA_KV_PAD), BF16),
            pltpu.VMEM((tm + SWA_BLOCK, _SWA_KV_PAD), BF16),
            pltpu.VMEM((tm, SWA_WIDTH), F32),
            pltpu.VMEM((tm, MEM_WIDTH), F32),
            pltpu.VMEM((tm, D_MODEL), F32),
            pltpu.VMEM((MIX_WIDTH, D_MODEL), BF16),
            pltpu.VMEM((2, _W_OUT_CHUNK, D_MODEL), F32),
            pltpu.SemaphoreType.DMA((2,)),
        ],
        compiler_params=_params(("arbitrary", "arbitrary")),
        name="tail",
    )(sinks, x, y_mla, z, z, z, qs, qs, ks, ks, vs, vs, bias, kmem, vmem, w, g)


def _rope_tables(seq):
    inv = 1.0 / (ROPE_THETA ** (jnp.arange(0, MLA_ROPE, 2, dtype=F32) / MLA_ROPE))
    step = 1 << ((seq - 1).bit_length() // 2)
    assert seq % step == 0
    fine = jnp.arange(step, dtype=F32)[:, None] * inv[None, :]
    coarse = (jnp.arange(seq // step, dtype=F32) * step)[:, None] * inv[None, :]
    cf, sf = jnp.cos(fine)[None], jnp.sin(fine)[None]
    cc, sc = jnp.cos(coarse)[:, None], jnp.sin(coarse)[:, None]
    cos = (cc * cf - sc * sf).reshape(seq, -1)
    sin = (sc * cf + cc * sf).reshape(seq, -1)
    zero = jnp.zeros_like(cos)
    return (jnp.concatenate([cos, cos, zero, zero], axis=-1),
            jnp.concatenate([-sin, sin, zero, zero], axis=-1))


def _t5_bucket(rel):
    nb = N_BUCKETS // 2
    max_exact = nb // 2
    bucket = jnp.where(rel > 0, nb, 0)
    n = jnp.abs(rel)
    nf = jnp.maximum(n, 1).astype(F32)
    large = max_exact + (jnp.log(nf / max_exact) / math.log(MAX_DISTANCE / max_exact)
                         * (nb - max_exact)).astype(jnp.int32)
    large = jnp.minimum(large, nb - 1)
    return bucket + jnp.where(n < max_exact, n, large)


def kernel(x, mem, norm_in, w_in, norm_q, norm_kv, w_uq, w_ukv, attn_sinks, rel_bias,
           norm_mem, w_mem_kv, w_out, norm_final):
    B, S, _ = x.shape
    assert norm_in.shape[0] == 1, "single-layer trunk"

    wt = w_in[0].T
    cos_t, sin_t = _rope_tables(S)
    q, k, v, qs, z, ks, vs = _proj(x, norm_in, wt, norm_q, norm_kv, w_uq[0].astype(BF16),
                                   w_ukv[0].astype(BF16), cos_t, sin_t, tm=512)

    y_mla = _mla(q, k, v, t=512, g=4, nq=4)

    bucket = _t5_bucket(jnp.arange(_REL_SPAN) - (2 * SWA_BLOCK - 1)).astype(jnp.int32)
    bias = _bias(rel_bias, bucket)
    kmem, vmem = _memkv(mem, norm_mem, w_mem_kv[0].astype(BF16))
    return _tail(attn_sinks[0], x, y_mla, z, qs, ks, vs, bias, kmem, vmem,
                 w_out[0], norm_final[None, :], nsub=4)
```

```python
import functools
import math

import jax
import jax.numpy as jnp
from jax import lax
from jax.experimental import pallas as pl
from jax.experimental.pallas import tpu as pltpu

D_MODEL = 2048
CHUNK = 64
EPS = 1e-6
NEG = -1e30

MLA_HEADS = 8
MLA_NOPE = 128
MLA_ROPE = 64
MLA_V = 128
MLA_QLORA = 512
MLA_KVLORA = 256
MLA_WIDTH = MLA_HEADS * MLA_V
MLA_QK_PAD = 256
ROPE_THETA = 10000.0

SWA_HEADS = 8
SWA_KV_HEADS = 2
SWA_GROUP = SWA_HEADS // SWA_KV_HEADS
SWA_HEAD_DIM = 64
SWA_WIDTH = SWA_HEADS * SWA_HEAD_DIM
SWA_KV_WIDTH = SWA_KV_HEADS * SWA_HEAD_DIM
WINDOW_CHUNKS = 2
SWA_BLOCK = 128

MEM_HEADS = 4
MEM_HEAD_DIM = 128
MEM_WIDTH = MEM_HEADS * MEM_HEAD_DIM

MIX_WIDTH = MLA_WIDTH + SWA_WIDTH + MEM_WIDTH

N_BUCKETS = 32
MAX_DISTANCE = 128

BF16 = jnp.bfloat16
F32 = jnp.float32

_R_CQ = 0
_R_CKV = _R_CQ + MLA_QLORA
_R_KPE = _R_CKV + MLA_KVLORA
_R_ZMLA = _R_KPE + MLA_ROPE
_R_QSWA = _R_ZMLA + MLA_WIDTH
_R_KSWA = _R_QSWA + SWA_WIDTH
_R_VSWA = _R_KSWA + SWA_KV_WIDTH
_R_ZSWA = _R_VSWA + SWA_KV_WIDTH
_R_QMEM = _R_ZSWA + SWA_WIDTH
_R_ZMEM = _R_QMEM + MEM_WIDTH
IN_WIDTH = _R_ZMEM + MEM_WIDTH

_MLA_Q_SCALE = (MLA_NOPE + MLA_ROPE) ** -0.5 * math.log2(math.e)

_SWA_KV_PAD = 2 * SWA_KV_HEADS * 128

_W_IN_CHUNK = 208
_W_OUT_CHUNK = 256

_V7X_VMEM_BYTES = 64 * 1024 * 1024
_VMEM_LIMIT = _V7X_VMEM_BYTES - 8 * 1024 * 1024

_PROJ_ROWS = 512
_MLA_TILE = 512
_MLA_GROUP = 4
_MLA_TILES_PER_STEP = 4
_TAIL_BLOCKS = 4


def _params(sem):
    return pltpu.CompilerParams(dimension_semantics=sem, vmem_limit_bytes=_VMEM_LIMIT)


def _rms(v, g):
    return (v * lax.rsqrt(jnp.mean(v * v, axis=-1, keepdims=True) + EPS)) * g


def _dot(a, b):
    return jnp.dot(a, b, preferred_element_type=F32)


def _dot_nt(a, b):
    return lax.dot_general(a, b, (((1,), (1,)), ((), ())), preferred_element_type=F32)


def _load_weight_bf16(w_hbm, w_ref, stage_ref, sem_ref):
    rows = stage_ref.shape[1]
    n_chunks = w_ref.shape[0] // rows

    def chunk_copy(c):
        return pltpu.make_async_copy(w_hbm.at[pl.ds(c * rows, rows), :], stage_ref.at[c % 2],
                                     sem_ref.at[c % 2])

    chunk_copy(0).start()
    for c in range(n_chunks):
        if c + 1 < n_chunks:
            chunk_copy(c + 1).start()
        chunk_copy(c).wait()
        w_ref[c * rows:(c + 1) * rows, :] = stage_ref[c % 2].astype(BF16)


def _first_grid_step():
    return jnp.logical_and(pl.program_id(0) == 0, pl.program_id(1) == 0)


def _lane_tile(v, n):
    return jnp.concatenate([v] * n, axis=1)


def _chunk_of(pos):
    return jnp.right_shift(pos, CHUNK.bit_length() - 1)


def _rope128(v, cos_t, sin_t):
    return v * cos_t + pltpu.roll(v, MLA_ROPE // 2, 1) * sin_t


def _store_swa_kv(ref, kv):
    low = lax.broadcasted_iota(jnp.int32, kv.shape, 1) < SWA_HEAD_DIM
    swapped = pltpu.roll(kv, SWA_HEAD_DIM, 1)
    pieces = (jnp.where(low, kv, 0.0), jnp.where(low, 0.0, swapped),
              jnp.where(low, swapped, 0.0), jnp.where(low, 0.0, kv))
    for i, piece in enumerate(pieces):
        ref[0, :, i * 128:(i + 1) * 128] = piece.astype(BF16)


def _proj_kernel(x_ref, gin_ref, wt_hbm, gq_ref, gkv_ref, wuq_ref, wukv_ref, cos_ref, sin_ref,
                 q_ref, k_ref, v_ref, qs_ref, z_ref, ks_ref, vs_ref, wt_ref, stage_ref, sem_ref):
    @pl.when(_first_grid_step())
    def _():
        _load_weight_bf16(wt_hbm, wt_ref, stage_ref, sem_ref)

    h = _rms(x_ref[0], gin_ref[...]).astype(BF16)
    cos_t = cos_ref[...]
    sin_t = sin_ref[...]

    def proj(lo, hi):
        return _dot_nt(h, wt_ref[lo:hi, :])

    pa = proj(_R_CQ, _R_KPE)
    cq = _rms(pa[:, :MLA_QLORA], gq_ref[...]).astype(BF16)
    ckv = _rms(pa[:, MLA_QLORA:], gkv_ref[...]).astype(BF16)

    qs_ref[0, :, :SWA_WIDTH] = proj(_R_QSWA, _R_KSWA).astype(BF16)
    qs_ref[0, :, SWA_WIDTH:] = proj(_R_QMEM, _R_ZMEM).astype(BF16)
    z_ref[0, :, :MLA_WIDTH] = proj(_R_ZMLA, _R_QSWA).astype(BF16)
    z_ref[0, :, MLA_WIDTH:MLA_WIDTH + SWA_WIDTH] = proj(_R_ZSWA, _R_QMEM).astype(BF16)
    z_ref[0, :, MLA_WIDTH + SWA_WIDTH:] = proj(_R_ZMEM, IN_WIDTH).astype(BF16)
    kvs = proj(_R_KSWA, _R_ZSWA)
    _store_swa_kv(ks_ref, kvs[:, :SWA_KV_WIDTH])
    _store_swa_kv(vs_ref, kvs[:, SWA_KV_WIDTH:])
    pe = proj(_R_KPE, _R_ZMLA)
    kpe = _rope128(jnp.concatenate([pe, pe], axis=1), cos_t, sin_t).astype(BF16)

    qall = _dot(cq, wuq_ref[...]) * _MLA_Q_SCALE
    kv = _dot(ckv, wukv_ref[...])
    for hd in range(MLA_HEADS):
        oq = hd * (MLA_NOPE + MLA_ROPE)
        q_pe = qall[:, oq + MLA_NOPE:oq + MLA_NOPE + MLA_ROPE]
        q_ref[0, hd, :, :MLA_NOPE] = qall[:, oq:oq + MLA_NOPE].astype(BF16)
        q_ref[0, hd, :, MLA_NOPE:] = _rope128(
            jnp.concatenate([q_pe, q_pe], axis=1), cos_t, sin_t).astype(BF16)
        o = hd * (MLA_NOPE + MLA_V)
        k_ref[0, hd, :, :MLA_NOPE] = kv[:, o:o + MLA_NOPE].astype(BF16)
        k_ref[0, hd, :, MLA_NOPE:] = kpe
        v_ref[0, hd] = kv[:, o + MLA_NOPE:o + MLA_QK_PAD].astype(BF16)


def _proj(x, gin, wt, gq, gkv, wuq, wukv, cos_t, sin_t, tm):
    B, S, _ = x.shape
    const = lambda b, i: (0, 0)
    row = lambda b, i: (b, i, 0)
    single = pl.Buffered(1)

    def out(width):
        return jax.ShapeDtypeStruct((B, S, width), BF16)

    head_spec = pl.BlockSpec((1, MLA_HEADS, tm, MLA_QK_PAD), lambda b, i: (b, 0, i, 0))
    head_shape = jax.ShapeDtypeStruct((B, MLA_HEADS, S, MLA_QK_PAD), BF16)

    return pl.pallas_call(
        _proj_kernel,
        grid=(B, S // tm),
        in_specs=[
            pl.BlockSpec((1, tm, D_MODEL), row),
            pl.BlockSpec((1, D_MODEL), const),
            pl.BlockSpec(memory_space=pl.ANY),
            pl.BlockSpec((1, MLA_QLORA), const),
            pl.BlockSpec((1, MLA_KVLORA), const),
            pl.BlockSpec((MLA_QLORA, MLA_HEADS * (MLA_NOPE + MLA_ROPE)), const, pipeline_mode=single),
            pl.BlockSpec((MLA_KVLORA, MLA_HEADS * (MLA_NOPE + MLA_V)), const, pipeline_mode=single),
            pl.BlockSpec((tm, 128), lambda b, i: (i, 0)),
            pl.BlockSpec((tm, 128), lambda b, i: (i, 0)),
        ],
        out_specs=[
            head_spec, head_spec,
            pl.BlockSpec((1, MLA_HEADS, tm, MLA_V), lambda b, i: (b, 0, i, 0)),
            pl.BlockSpec((1, tm, SWA_WIDTH + MEM_WIDTH), row),
            pl.BlockSpec((1, tm, MIX_WIDTH), row),
            pl.BlockSpec((1, tm, _SWA_KV_PAD), row),
            pl.BlockSpec((1, tm, _SWA_KV_PAD), row),
        ],
        out_shape=[
            head_shape, head_shape, jax.ShapeDtypeStruct((B, MLA_HEADS, S, MLA_V), BF16),
            out(SWA_WIDTH + MEM_WIDTH), out(MIX_WIDTH), out(_SWA_KV_PAD), out(_SWA_KV_PAD),
        ],
        scratch_shapes=[
            pltpu.VMEM((IN_WIDTH, D_MODEL), BF16),
            pltpu.VMEM((2, _W_IN_CHUNK, D_MODEL), F32),
            pltpu.SemaphoreType.DMA((2,)),
        ],
        compiler_params=_params(("arbitrary", "arbitrary")),
        name="proj",
    )(x, gin, wt, gq, gkv, wuq, wukv, cos_t, sin_t)


def _mla_kernel(q_ref, k_ref, v_ref, o_ref, acc_ref, m_ref, s_ref, cm_ref, *, t, g, nq):
    qi0 = pl.program_id(2) * nq
    qi_end = qi0 + nq - 1
    acc_ref[...] = jnp.zeros(acc_ref.shape, F32)
    m_ref[...] = jnp.full(m_ref.shape, NEG, F32)
    ones = jnp.ones((t, MLA_V), BF16)

    def row_max(s):
        return jnp.broadcast_to(jnp.max(s, axis=-1, keepdims=True), (t, 128))

    def diagonal_mask():
        q_chunk = _chunk_of(lax.broadcasted_iota(jnp.int32, (t, t), 0))
        k_chunk = _chunk_of(lax.broadcasted_iota(jnp.int32, (t, t), 1))
        return k_chunk <= q_chunk

    def scores(qi, c, slot, hd, mask=None):
        rows = pl.multiple_of((qi - qi0) * t, t)
        keys = pl.multiple_of(c * t, t)
        s = _dot_nt(q_ref[0, hd, pl.ds(rows, t), :], k_ref[0, hd, pl.ds(keys, t), :])
        if mask is not None:
            s = jnp.where(mask, s, NEG)
        s_ref[slot, hd] = s
        cm_ref[slot, hd] = row_max(s)

    def update(c, slot, hd, mask):
        keys = pl.multiple_of(c * t, t)
        s = s_ref[slot, hd]
        if mask is not None:
            s = jnp.where(mask, s, NEG)
            m_cur = row_max(s)
        else:
            m_cur = cm_ref[slot, hd]
        m_old = m_ref[hd]
        m_new = jnp.maximum(m_old, m_cur)
        alpha = jnp.exp2(m_old - m_new)
        p = jnp.exp2(s - _lane_tile(m_new, t // 128)).astype(BF16)
        v_ones = jnp.concatenate([v_ref[0, hd, pl.ds(keys, t), :], ones], axis=1)
        acc_ref[hd] = _lane_tile(alpha, MLA_QK_PAD // 128) * acc_ref[hd] + _dot(p, v_ones)
        m_ref[hd] = m_new

    def finish_tile(qi, hd):
        rows = pl.multiple_of((qi - qi0) * t, t)
        acc = acc_ref[hd]
        o_ref[0, pl.ds(rows, t), hd * MLA_V:(hd + 1) * MLA_V] = (
            acc[:, :MLA_V] / acc[:, MLA_V:]).astype(BF16)
        acc_ref[hd] = jnp.zeros(acc.shape, F32)
        m_ref[hd] = jnp.full((t, 128), NEG, F32)

    def following(qi, c):
        is_last = c == qi
        return jnp.where(is_last, jnp.minimum(qi + 1, qi_end), qi), jnp.where(is_last, 0, c + 1)

    def step(qi, c, slot, diagonal, premasked=False, mask_next=False):
        qn, cn = following(qi, c)
        mask = diagonal_mask() if diagonal and not premasked else None
        next_mask = diagonal_mask() if mask_next else None
        for hd in range(g):
            scores(qn, cn, 1 - slot, hd, next_mask)
            update(c, slot, hd, mask)
            if diagonal:
                finish_tile(qi, hd)

    for hd in range(g):
        scores(qi0, 0, 0, hd)

    def body(_, pair):
        first, second = pair, following(*pair)
        diag0 = first[1] == first[0]
        diag1 = second[1] == second[0]
        for d0, d1, cond in ((True, False, diag0), (False, True, diag1),
                             (False, False, jnp.logical_not(jnp.logical_or(diag0, diag1)))):
            @pl.when(cond)
            def _(d0=d0, d1=d1):
                step(*first, 0, d0, mask_next=d1)
                step(*second, 1, d1, premasked=d1)
        return following(*second)

    n_steps = nq * qi0 + nq * (nq + 1) // 2
    lax.fori_loop(0, n_steps // 2, body, (qi0, 0))


def _mla(q, k, v, t, g, nq):
    B, _, S, _ = q.shape
    assert S % (nq * t) == 0 and nq % 2 == 0 and (nq * (nq + 1) // 2) % 2 == 0
    return pl.pallas_call(
        functools.partial(_mla_kernel, t=t, g=g, nq=nq),
        grid=(B, MLA_HEADS // g, S // (nq * t)),
        in_specs=[
            pl.BlockSpec((1, g, nq * t, MLA_QK_PAD), lambda b, h, i: (b, h, i, 0)),
            pl.BlockSpec((1, g, S, MLA_QK_PAD), lambda b, h, i: (b, h, 0, 0)),
            pl.BlockSpec((1, g, S, MLA_V), lambda b, h, i: (b, h, 0, 0)),
        ],
        out_specs=pl.BlockSpec((1, nq * t, g * MLA_V), lambda b, h, i: (b, i, h)),
        out_shape=jax.ShapeDtypeStruct((B, S, MLA_WIDTH), BF16),
        scratch_shapes=[
            pltpu.VMEM((g, t, MLA_QK_PAD), F32),
            pltpu.VMEM((g, t, 128), F32),
            pltpu.VMEM((2, g, t, t), F32),
            pltpu.VMEM((2, g, t, 128), F32),
        ],
        compiler_params=_params(("arbitrary", "arbitrary", "arbitrary")),
        name="mla",
    )(q, k, v)


_REL_SPAN = 4 * SWA_BLOCK
_N_OFFSETS = 3 * SWA_BLOCK - 1


def _bias_kernel(table_ref, bucket_ref, o_ref):
    head = lax.broadcasted_iota(jnp.int32, (SWA_HEADS, 128), 0)
    lane = lax.broadcasted_iota(jnp.int32, (SWA_HEADS, _REL_SPAN), 1)

    def lookup(c, per_offset):
        b = bucket_ref[c]
        col = jnp.zeros((SWA_HEADS, 128), F32)
        for hd in range(SWA_HEADS):
            col = jnp.where(head == hd, table_ref[b, hd], col)
        return jnp.where(lane == c, _lane_tile(col, _REL_SPAN // 128), per_offset)

    per_offset_all = lax.fori_loop(0, _N_OFFSETS, lookup, jnp.zeros((SWA_HEADS, _REL_SPAN), F32))
    for hd in range(SWA_HEADS):
        per_offset = per_offset_all[hd:hd + 1, :]
        rows = jnp.broadcast_to(per_offset, (SWA_BLOCK, _REL_SPAN))
        band = pltpu.roll(rows, _REL_SPAN - (SWA_BLOCK - 1), 1, stride=1, stride_axis=0)
        kvh, within = divmod(hd, SWA_GROUP)
        par, half = within % 2, within // 2
        o_ref[kvh, par, half * SWA_BLOCK:(half + 1) * SWA_BLOCK, :] = band[:, :2 * SWA_BLOCK]


def _bias(rel_table, bucket):
    return pl.pallas_call(
        _bias_kernel,
        in_specs=[
            pl.BlockSpec(memory_space=pltpu.SMEM),
            pl.BlockSpec(memory_space=pltpu.SMEM),
        ],
        out_specs=pl.BlockSpec(memory_space=pltpu.VMEM),
        out_shape=jax.ShapeDtypeStruct((SWA_KV_HEADS, 2, 2 * SWA_BLOCK, 2 * SWA_BLOCK), F32),
        name="t5bias",
    )(rel_table, bucket)


def _memkv_kernel(mem_ref, g_ref, w_ref, k_ref, v_ref):
    mn = _rms(mem_ref[0], g_ref[...]).astype(BF16)
    kv = _dot(mn, w_ref[...].astype(BF16))
    k_ref[0] = kv[:, :MEM_WIDTH].astype(BF16)
    v_ref[0] = kv[:, MEM_WIDTH:].astype(BF16)


def _memkv(mem, g, w):
    B, M, _ = mem.shape
    const = lambda b: (0, 0)
    row = lambda b: (b, 0, 0)
    return pl.pallas_call(
        _memkv_kernel,
        grid=(B,),
        in_specs=[
            pl.BlockSpec((1, M, D_MODEL), row),
            pl.BlockSpec((1, D_MODEL), const),
            pl.BlockSpec((D_MODEL, 2 * MEM_WIDTH), const),
        ],
        out_specs=[pl.BlockSpec((1, M, MEM_WIDTH), row)] * 2,
        out_shape=[jax.ShapeDtypeStruct((B, M, MEM_WIDTH), BF16)] * 2,
        compiler_params=_params(("arbitrary",)),
        name="memkv",
    )(mem, g, w)


def _swa_into(y_ref, sink_ref, q_ref, kp_ref, kc_ref, vp_ref, vc_ref, bias_ref, kb_ref, vb_ref, nsub,
              beside_unit):
    t = pl.program_id(1)
    scale = SWA_HEAD_DIM ** -0.5
    sb = SWA_BLOCK
    kb_ref[:sb] = kp_ref[0]
    kb_ref[sb:] = kc_ref[0]
    vb_ref[:sb] = vp_ref[0]
    vb_ref[sb:] = vc_ref[0]

    row = lax.broadcasted_iota(jnp.int32, (2 * sb, 2 * sb), 0)
    q_chunk = _chunk_of(row & (sb - 1))
    b_chunk = _chunk_of(lax.broadcasted_iota(jnp.int32, (2 * sb, 2 * sb), 1))
    valid_any = jnp.logical_and(b_chunk >= q_chunk, b_chunk <= q_chunk + WINDOW_CHUNKS)
    first_lo = jnp.where(t > 0, 0, 2)
    valid_first = jnp.logical_and(valid_any, b_chunk >= first_lo)
    upper = lax.broadcasted_iota(jnp.int32, (2 * sb, 1), 0) < sb

    def geometry(u):
        r, kvh = divmod(u, SWA_KV_HEADS)
        rows = slice(r * sb, (r + 1) * sb)
        band = slice(r * sb, (r + 2) * sb)
        pair0 = slice(2 * kvh * 128, (2 * kvh + 1) * 128)
        pair1 = slice((2 * kvh + 1) * 128, (2 * kvh + 2) * 128)
        return r, kvh, rows, band, pair0, pair1

    def scores(u):
        r, kvh, rows, band, pair0, pair1 = geometry(u)
        q = jnp.concatenate([q_ref[0, rows, pair0], q_ref[0, rows, pair1]], axis=0)
        return [_dot_nt(q, kb_ref[band, (2 * kvh + par) * 128:(2 * kvh + par + 1) * 128])
                for par in range(2)]

    def attend(u, unit_scores):
        r, kvh, rows, band, pair0, pair1 = geometry(u)
        valid = valid_first if r == 0 else valid_any
        o = None
        for par, s in enumerate(unit_scores):
            kcols = slice((2 * kvh + par) * 128, (2 * kvh + par + 1) * 128)
            hd = SWA_GROUP * kvh + par
            s = jnp.where(valid, s * scale + bias_ref[kvh, par], NEG)
            sink = jnp.where(upper, sink_ref[hd], sink_ref[hd + 2])
            m = jnp.maximum(jnp.max(s, axis=-1, keepdims=True), sink)
            p = jnp.exp(s - m)
            den = jnp.sum(p, axis=-1, keepdims=True) + jnp.exp(sink - m)
            pv = _dot((p * (1.0 / den)).astype(BF16), vb_ref[band, kcols])
            o = pv if o is None else o + pv
        y_ref[rows, pair0] = o[:sb]
        y_ref[rows, pair1] = o[sb:]

    n_units = nsub * SWA_KV_HEADS
    pending = scores(0)
    for u in range(n_units):
        upcoming = scores(u + 1) if u + 1 < n_units else None
        beside_unit(u)
        attend(u, pending)
        pending = upcoming


def _mem_into(y_ref, q_ref, k_ref, v_ref, after_head):
    scale = MEM_HEAD_DIM ** -0.5

    def head_cols(hd):
        return slice(hd * MEM_HEAD_DIM, (hd + 1) * MEM_HEAD_DIM)

    def scores(hd):
        return _dot_nt(q_ref[0, :, head_cols(hd)], k_ref[0, :, head_cols(hd)])

    pending = scores(0)
    for hd in range(MEM_HEADS):
        upcoming = scores(hd + 1) if hd + 1 < MEM_HEADS else None
        s = pending * scale
        m = jnp.max(s, axis=-1, keepdims=True)
        p = jnp.exp(s - m)
        inv = 1.0 / jnp.sum(p, axis=-1, keepdims=True)
        y_ref[:, head_cols(hd)] = _dot((p * inv).astype(BF16), v_ref[0, :, head_cols(hd)])
        after_head(hd)
        pending = upcoming


def _tail_kernel(sink_ref, x_ref, ya_ref, za_ref, zb_ref, zc_ref, qs_ref, qm_ref,
                 kp_ref, kc_ref, vp_ref, vc_ref, bias_ref, km_ref, vm_ref, w_hbm, g_ref,
                 o_ref, kb_ref, vb_ref, yb_ref, yc_ref, yacc_ref, w_ref, stage_ref, sem_ref, *, nsub):
    @pl.when(_first_grid_step())
    def _():
        _load_weight_bf16(w_hbm, w_ref, stage_ref, sem_ref)

    def gated(y, z_ref):
        h = 0.5 * z_ref[0].astype(F32)
        return (y * (h + h * jnp.tanh(h))).astype(BF16)

    swa_lo, mem_lo = MLA_WIDTH, MLA_WIDTH + SWA_WIDTH

    g_mla = gated(ya_ref[0].astype(F32), za_ref)
    mla_chunk = D_MODEL // (nsub * SWA_KV_HEADS)

    def project_mla(u):
        cols = slice(u * mla_chunk, (u + 1) * mla_chunk)
        yacc_ref[:, cols] = _dot(g_mla, w_ref[:swa_lo, cols])

    _swa_into(yb_ref, sink_ref, qs_ref, kp_ref, kc_ref, vp_ref, vc_ref, bias_ref, kb_ref, vb_ref, nsub,
              project_mla)

    g_swa = gated(yb_ref[...], zb_ref)
    swa_chunk = D_MODEL // MEM_HEADS

    def project_swa(u):
        cols = slice(u * swa_chunk, (u + 1) * swa_chunk)
        yacc_ref[:, cols] += _dot(g_swa, w_ref[swa_lo:mem_lo, cols])

    _mem_into(yc_ref, qm_ref, km_ref, vm_ref, project_swa)

    g_mem = gated(yc_ref[...], zc_ref)
    sum_sq = jnp.zeros((g_mem.shape[0], 1), F32)
    chunks = [slice(j * swa_chunk, (j + 1) * swa_chunk) for j in range(D_MODEL // swa_chunk)]
    for cols in chunks:
        r = x_ref[0, :, cols] + yacc_ref[:, cols] + _dot(g_mem, w_ref[mem_lo:, cols])
        o_ref[0, :, cols] = r
        sum_sq = sum_sq + jnp.sum(r * r, axis=-1, keepdims=True)
    inv = lax.rsqrt(sum_sq * (1.0 / D_MODEL) + EPS)
    for cols in chunks:
        o_ref[0, :, cols] = (o_ref[0, :, cols] * inv) * g_ref[:, cols]


def _tail(sinks, x, y_mla, z, qs, ks, vs, bias, kmem, vmem, w, g, nsub):
    B, S, _ = x.shape
    tm = nsub * SWA_BLOCK
    M = kmem.shape[1]
    row = lambda b, i: (b, i, 0)
    prev = lambda b, i: (b, jnp.maximum(i * nsub - 1, 0), 0)
    col = lambda c: (lambda b, i: (b, i, c))
    return pl.pallas_call(
        functools.partial(_tail_kernel, nsub=nsub),
        grid=(B, S // tm),
        in_specs=[
            pl.BlockSpec(memory_space=pltpu.SMEM),
            pl.BlockSpec((1, tm, D_MODEL), row),
            pl.BlockSpec((1, tm, MLA_WIDTH), row),
            pl.BlockSpec((1, tm, MLA_WIDTH), col(0)),
            pl.BlockSpec((1, tm, SWA_WIDTH), col(MLA_WIDTH // SWA_WIDTH)),
            pl.BlockSpec((1, tm, MEM_WIDTH), col((MLA_WIDTH + SWA_WIDTH) // MEM_WIDTH)),
            pl.BlockSpec((1, tm, SWA_WIDTH), col(0)),
            pl.BlockSpec((1, tm, MEM_WIDTH), col(1)),
            pl.BlockSpec((1, SWA_BLOCK, _SWA_KV_PAD), prev),
            pl.BlockSpec((1, tm, _SWA_KV_PAD), row),
            pl.BlockSpec((1, SWA_BLOCK, _SWA_KV_PAD), prev),
            pl.BlockSpec((1, tm, _SWA_KV_PAD), row),
            pl.BlockSpec((SWA_KV_HEADS, 2, 2 * SWA_BLOCK, 2 * SWA_BLOCK), lambda b, i: (0, 0, 0, 0)),
            pl.BlockSpec((1, M, MEM_WIDTH), lambda b, i: (b, 0, 0)),
            pl.BlockSpec((1, M, MEM_WIDTH), lambda b, i: (b, 0, 0)),
            pl.BlockSpec(memory_space=pl.ANY),
            pl.BlockSpec((1, D_MODEL), lambda b, i: (0, 0)),
        ],
        out_specs=pl.BlockSpec((1, tm, D_MODEL), row),
        out_shape=jax.ShapeDtypeStruct((B, S, D_MODEL), F32),
        scratch_shapes=[
            pltpu.VMEM((tm + SWA_BLOCK, _SWA_KV_PAD), BF16),
            pltpu.VMEM((tm + SWA_BLOCK, _SWA_KV_PAD), BF16),
            pltpu.VMEM((tm, SWA_WIDTH), F32),
            pltpu.VMEM((tm, MEM_WIDTH), F32),
            pltpu.VMEM((tm, D_MODEL), F32),
            pltpu.VMEM((MIX_WIDTH, D_MODEL), BF16),
            pltpu.VMEM((2, _W_OUT_CHUNK, D_MODEL), F32),
            pltpu.SemaphoreType.DMA((2,)),
        ],
        compiler_params=_params(("arbitrary", "arbitrary")),
        name="tail",
    )(sinks, x, y_mla, z, z, z, qs, qs, ks, ks, vs, vs, bias, kmem, vmem, w, g)


def _rope_tables(seq):
    inv = 1.0 / (ROPE_THETA ** (jnp.arange(0, MLA_ROPE, 2, dtype=F32) / MLA_ROPE))
    step = 1 << ((seq - 1).bit_length() // 2)
    assert seq % step == 0
    fine = jnp.arange(step, dtype=F32)[:, None] * inv[None, :]
    coarse = (jnp.arange(seq // step, dtype=F32) * step)[:, None] * inv[None, :]
    cf, sf = jnp.cos(fine)[None], jnp.sin(fine)[None]
    cc, sc = jnp.cos(coarse)[:, None], jnp.sin(coarse)[:, None]
    cos = (cc * cf - sc * sf).reshape(seq, -1)
    sin = (sc * cf + cc * sf).reshape(seq, -1)
    zero = jnp.zeros_like(cos)
    return (jnp.concatenate([cos, cos, zero, zero], axis=-1),
            jnp.concatenate([-sin, sin, zero, zero], axis=-1))


def _t5_bucket(rel):
    nb = N_BUCKETS // 2
    max_exact = nb // 2
    bucket = jnp.where(rel > 0, nb, 0)
    n = jnp.abs(rel)
    nf = jnp.maximum(n, 1).astype(F32)
    large = max_exact + (jnp.log(nf / max_exact) / math.log(MAX_DISTANCE / max_exact)
                         * (nb - max_exact)).astype(jnp.int32)
    large = jnp.minimum(large, nb - 1)
    return bucket + jnp.where(n < max_exact, n, large)


def kernel(x, mem, norm_in, w_in, norm_q, norm_kv, w_uq, w_ukv, attn_sinks, rel_bias,
           norm_mem, w_mem_kv, w_out, norm_final):
    B, S, _ = x.shape
    assert norm_in.shape[0] == 1, "single-layer trunk"
    assert x.shape[2] == D_MODEL and w_in.shape[1:] == (D_MODEL, IN_WIDTH)
    assert S % _PROJ_ROWS == 0 and S % (_TAIL_BLOCKS * SWA_BLOCK) == 0

    wt = w_in[0].T
    cos_t, sin_t = _rope_tables(S)
    q, k, v, qs, z, ks, vs = _proj(x, norm_in, wt, norm_q, norm_kv, w_uq[0].astype(BF16),
                                   w_ukv[0].astype(BF16), cos_t, sin_t, tm=_PROJ_ROWS)

    y_mla = _mla(q, k, v, t=_MLA_TILE, g=_MLA_GROUP, nq=_MLA_TILES_PER_STEP)

    bucket = _t5_bucket(jnp.arange(_REL_SPAN) - (2 * SWA_BLOCK - 1)).astype(jnp.int32)
    bias = _bias(rel_bias, bucket)
    kmem, vmem = _memkv(mem, norm_mem, w_mem_kv[0])
    return _tail(attn_sinks[0], x, y_mla, z, qs, ks, vs, bias, kmem, vmem,
                 w_out[0], norm_final[None, :], nsub=_TAIL_BLOCKS)
```

```python
import functools
import math

import jax
import jax.numpy as jnp
from jax import lax
from jax.experimental import pallas as pl
from jax.experimental.pallas import tpu as pltpu

D_MODEL = 2048
CHUNK = 64
EPS = 1e-6
NEG = -1e30

MLA_HEADS = 8
MLA_NOPE = 128
MLA_ROPE = 64
MLA_V = 128
MLA_QLORA = 512
MLA_KVLORA = 256
MLA_WIDTH = MLA_HEADS * MLA_V
MLA_QK_PAD = 256
ROPE_THETA = 10000.0

SWA_HEADS = 8
SWA_KV_HEADS = 2
SWA_GROUP = SWA_HEADS // SWA_KV_HEADS
SWA_HEAD_DIM = 64
SWA_WIDTH = SWA_HEADS * SWA_HEAD_DIM
SWA_KV_WIDTH = SWA_KV_HEADS * SWA_HEAD_DIM
WINDOW_CHUNKS = 2
SWA_BLOCK = 128

MEM_HEADS = 4
MEM_HEAD_DIM = 128
MEM_WIDTH = MEM_HEADS * MEM_HEAD_DIM

MIX_WIDTH = MLA_WIDTH + SWA_WIDTH + MEM_WIDTH

N_BUCKETS = 32
MAX_DISTANCE = 128

BF16 = jnp.bfloat16
F32 = jnp.float32

_R_CQ = 0
_R_CKV = _R_CQ + MLA_QLORA
_R_KPE = _R_CKV + MLA_KVLORA
_R_ZMLA = _R_KPE + MLA_ROPE
_R_QSWA = _R_ZMLA + MLA_WIDTH
_R_KSWA = _R_QSWA + SWA_WIDTH
_R_VSWA = _R_KSWA + SWA_KV_WIDTH
_R_ZSWA = _R_VSWA + SWA_KV_WIDTH
_R_QMEM = _R_ZSWA + SWA_WIDTH
_R_ZMEM = _R_QMEM + MEM_WIDTH
IN_WIDTH = _R_ZMEM + MEM_WIDTH

_MLA_Q_SCALE = (MLA_NOPE + MLA_ROPE) ** -0.5 * math.log2(math.e)

_SWA_KV_PAD = 2 * SWA_KV_HEADS * 128

_W_IN_CHUNK = 208
_W_OUT_CHUNK = 256

_V7X_VMEM_BYTES = 64 * 1024 * 1024
_VMEM_LIMIT = _V7X_VMEM_BYTES - 8 * 1024 * 1024

_PROJ_ROWS = 512
_MLA_TILE = 512
_MLA_GROUP = 4
_MLA_TILES_PER_STEP = 4
_TAIL_BLOCKS = 4


def _params(sem):
    return pltpu.CompilerParams(dimension_semantics=sem, vmem_limit_bytes=_VMEM_LIMIT)


def _rms(v, g):
    return (v * lax.rsqrt(jnp.mean(v * v, axis=-1, keepdims=True) + EPS)) * g


def _dot(a, b):
    return jnp.dot(a, b, preferred_element_type=F32)


def _dot_nt(a, b):
    return lax.dot_general(a, b, (((1,), (1,)), ((), ())), preferred_element_type=F32)


def _load_weight_bf16(w_hbm, w_ref, stage_ref, sem_ref):
    rows = stage_ref.shape[1]
    n_chunks = w_ref.shape[0] // rows

    def chunk_copy(c):
        return pltpu.make_async_copy(w_hbm.at[pl.ds(c * rows, rows), :], stage_ref.at[c % 2],
                                     sem_ref.at[c % 2])

    chunk_copy(0).start()
    for c in range(n_chunks):
        if c + 1 < n_chunks:
            chunk_copy(c + 1).start()
        chunk_copy(c).wait()
        w_ref[c * rows:(c + 1) * rows, :] = stage_ref[c % 2].astype(BF16)


def _first_grid_step():
    return jnp.logical_and(pl.program_id(0) == 0, pl.program_id(1) == 0)


def _lane_tile(v, n):
    return jnp.concatenate([v] * n, axis=1)


def _chunk_of(pos):
    return jnp.right_shift(pos, CHUNK.bit_length() - 1)


def _rope128(v, cos_t, sin_t):
    return v * cos_t + pltpu.roll(v, MLA_ROPE // 2, 1) * sin_t


def _store_swa_kv(ref, kv):
    low = lax.broadcasted_iota(jnp.int32, kv.shape, 1) < SWA_HEAD_DIM
    swapped = pltpu.roll(kv, SWA_HEAD_DIM, 1)
    pieces = (jnp.where(low, kv, 0.0), jnp.where(low, 0.0, swapped),
              jnp.where(low, swapped, 0.0), jnp.where(low, 0.0, kv))
    for i, piece in enumerate(pieces):
        ref[0, :, i * 128:(i + 1) * 128] = piece.astype(BF16)


def _proj_kernel(x_ref, gin_ref, wt_hbm, gq_ref, gkv_ref, wuq_ref, wukv_ref, cos_ref, sin_ref,
                 q_ref, k_ref, v_ref, qs_ref, z_ref, ks_ref, vs_ref, wt_ref, stage_ref, sem_ref):
    @pl.when(_first_grid_step())
    def _():
        _load_weight_bf16(wt_hbm, wt_ref, stage_ref, sem_ref)

    h = _rms(x_ref[0], gin_ref[...]).astype(BF16)
    cos_t = cos_ref[...]
    sin_t = sin_ref[...]

    def proj(lo, hi):
        return _dot_nt(h, wt_ref[lo:hi, :])

    pa = proj(_R_CQ, _R_KPE)
    cq = _rms(pa[:, :MLA_QLORA], gq_ref[...]).astype(BF16)
    ckv = _rms(pa[:, MLA_QLORA:], gkv_ref[...]).astype(BF16)

    qs_ref[0, :, :SWA_WIDTH] = proj(_R_QSWA, _R_KSWA).astype(BF16)
    qs_ref[0, :, SWA_WIDTH:] = proj(_R_QMEM, _R_ZMEM).astype(BF16)
    z_ref[0, :, :MLA_WIDTH] = proj(_R_ZMLA, _R_QSWA).astype(BF16)
    z_ref[0, :, MLA_WIDTH:MLA_WIDTH + SWA_WIDTH] = proj(_R_ZSWA, _R_QMEM).astype(BF16)
    z_ref[0, :, MLA_WIDTH + SWA_WIDTH:] = proj(_R_ZMEM, IN_WIDTH).astype(BF16)
    kvs = proj(_R_KSWA, _R_ZSWA)
    _store_swa_kv(ks_ref, kvs[:, :SWA_KV_WIDTH])
    _store_swa_kv(vs_ref, kvs[:, SWA_KV_WIDTH:])
    pe = proj(_R_KPE, _R_ZMLA)
    kpe = _rope128(jnp.concatenate([pe, pe], axis=1), cos_t, sin_t).astype(BF16)

    qall = _dot(cq, wuq_ref[...]) * _MLA_Q_SCALE
    kv = _dot(ckv, wukv_ref[...])
    for hd in range(MLA_HEADS):
        oq = hd * (MLA_NOPE + MLA_ROPE)
        q_pe = qall[:, oq + MLA_NOPE:oq + MLA_NOPE + MLA_ROPE]
        q_ref[0, hd, :, :MLA_NOPE] = qall[:, oq:oq + MLA_NOPE].astype(BF16)
        q_ref[0, hd, :, MLA_NOPE:] = _rope128(
            jnp.concatenate([q_pe, q_pe], axis=1), cos_t, sin_t).astype(BF16)
        o = hd * (MLA_NOPE + MLA_V)
        k_ref[0, hd, :, :MLA_NOPE] = kv[:, o:o + MLA_NOPE].astype(BF16)
        k_ref[0, hd, :, MLA_NOPE:] = kpe
        v_ref[0, hd] = kv[:, o + MLA_NOPE:o + MLA_QK_PAD].astype(BF16)


def _proj(x, gin, wt, gq, gkv, wuq, wukv, cos_t, sin_t, tm):
    B, S, _ = x.shape
    const = lambda b, i: (0, 0)
    row = lambda b, i: (b, i, 0)
    single = pl.Buffered(1)

    def out(width):
        return jax.ShapeDtypeStruct((B, S, width), BF16)

    head_spec = pl.BlockSpec((1, MLA_HEADS, tm, MLA_QK_PAD), lambda b, i: (b, 0, i, 0))
    head_shape = jax.ShapeDtypeStruct((B, MLA_HEADS, S, MLA_QK_PAD), BF16)

    return pl.pallas_call(
        _proj_kernel,
        grid=(B, S // tm),
        in_specs=[
            pl.BlockSpec((1, tm, D_MODEL), row),
            pl.BlockSpec((1, D_MODEL), const),
            pl.BlockSpec(memory_space=pl.ANY),
            pl.BlockSpec((1, MLA_QLORA), const),
            pl.BlockSpec((1, MLA_KVLORA), const),
            pl.BlockSpec((MLA_QLORA, MLA_HEADS * (MLA_NOPE + MLA_ROPE)), const, pipeline_mode=single),
            pl.BlockSpec((MLA_KVLORA, MLA_HEADS * (MLA_NOPE + MLA_V)), const, pipeline_mode=single),
            pl.BlockSpec((tm, 128), lambda b, i: (i, 0)),
            pl.BlockSpec((tm, 128), lambda b, i: (i, 0)),
        ],
        out_specs=[
            head_spec, head_spec,
            pl.BlockSpec((1, MLA_HEADS, tm, MLA_V), lambda b, i: (b, 0, i, 0)),
            pl.BlockSpec((1, tm, SWA_WIDTH + MEM_WIDTH), row),
            pl.BlockSpec((1, tm, MIX_WIDTH), row),
            pl.BlockSpec((1, tm, _SWA_KV_PAD), row),
            pl.BlockSpec((1, tm, _SWA_KV_PAD), row),
        ],
        out_shape=[
            head_shape, head_shape, jax.ShapeDtypeStruct((B, MLA_HEADS, S, MLA_V), BF16),
            out(SWA_WIDTH + MEM_WIDTH), out(MIX_WIDTH), out(_SWA_KV_PAD), out(_SWA_KV_PAD),
        ],
        scratch_shapes=[
            pltpu.VMEM((IN_WIDTH, D_MODEL), BF16),
            pltpu.VMEM((2, _W_IN_CHUNK, D_MODEL), F32),
            pltpu.SemaphoreType.DMA((2,)),
        ],
        compiler_params=_params(("arbitrary", "arbitrary")),
        name="proj",
    )(x, gin, wt, gq, gkv, wuq, wukv, cos_t, sin_t)


def _mla_kernel(q_ref, k_ref, v_ref, o_ref, acc_ref, m_ref, s_ref, cm_ref, *, t, g, nq):
    qi0 = pl.program_id(2) * nq
    qi_end = qi0 + nq - 1
    acc_ref[...] = jnp.zeros(acc_ref.shape, F32)
    m_ref[...] = jnp.full(m_ref.shape, NEG, F32)
    ones = jnp.ones((t, MLA_V), BF16)
    row_halves = ((0, t // 2, t // 2), (t // 2, t // 2, t))

    def row_max(s):
        return jnp.broadcast_to(jnp.max(s, axis=-1, keepdims=True), (s.shape[0], 128))

    def diagonal_mask():
        q_chunk = _chunk_of(lax.broadcasted_iota(jnp.int32, (t, t), 0))
        k_chunk = _chunk_of(lax.broadcasted_iota(jnp.int32, (t, t), 1))
        return k_chunk <= q_chunk

    def scores(qi, c, slot, hd, mask=None):
        rows = pl.multiple_of((qi - qi0) * t, t)
        keys = pl.multiple_of(c * t, t)
        for r0, nr, nk in (row_halves if mask is not None else ((0, t, t),)):
            s = _dot_nt(q_ref[0, hd, pl.ds(pl.multiple_of(rows + r0, nr), nr), :],
                        k_ref[0, hd, pl.ds(keys, nk), :])
            if mask is not None:
                s = jnp.where(mask[r0:r0 + nr, :nk], s, NEG)
            s_ref[slot, hd, r0:r0 + nr, :nk] = s
            cm_ref[slot, hd, r0:r0 + nr] = row_max(s)

    def update(c, slot, hd, diagonal, mask):
        keys = pl.multiple_of(c * t, t)
        for r0, nr, nk in (row_halves if diagonal else ((0, t, t),)):
            rows = slice(r0, r0 + nr)
            s = s_ref[slot, hd, rows, :nk]
            if mask is not None:
                s = jnp.where(mask[rows, :nk], s, NEG)
                m_cur = row_max(s)
            else:
                m_cur = cm_ref[slot, hd, rows]
            m_old = m_ref[hd, rows]
            m_new = jnp.maximum(m_old, m_cur)
            alpha = jnp.exp2(m_old - m_new)
            p = jnp.exp2(s - _lane_tile(m_new, nk // 128)).astype(BF16)
            v_ones = jnp.concatenate([v_ref[0, hd, pl.ds(keys, nk), :], ones[:nk]], axis=1)
            acc_ref[hd, rows] = (_lane_tile(alpha, MLA_QK_PAD // 128) * acc_ref[hd, rows]
                                 + _dot(p, v_ones))
            m_ref[hd, rows] = m_new

    def finish_tile(qi, hd):
        rows = pl.multiple_of((qi - qi0) * t, t)
        acc = acc_ref[hd]
        o_ref[0, pl.ds(rows, t), hd * MLA_V:(hd + 1) * MLA_V] = (
            acc[:, :MLA_V] / acc[:, MLA_V:]).astype(BF16)
        acc_ref[hd] = jnp.zeros(acc.shape, F32)
        m_ref[hd] = jnp.full((t, 128), NEG, F32)

    def following(qi, c):
        is_last = c == qi
        return jnp.where(is_last, jnp.minimum(qi + 1, qi_end), qi), jnp.where(is_last, 0, c + 1)

    def step(qi, c, slot, diagonal, premasked=False, mask_next=False):
        qn, cn = following(qi, c)
        mask = diagonal_mask() if diagonal and not premasked else None
        next_mask = diagonal_mask() if mask_next else None
        for hd in range(g):
            scores(qn, cn, 1 - slot, hd, next_mask)
            update(c, slot, hd, diagonal, mask)
            if diagonal:
                finish_tile(qi, hd)

    for hd in range(g):
        scores(qi0, 0, 0, hd)

    def body(_, pair):
        first, second = pair, following(*pair)
        diag0 = first[1] == first[0]
        diag1 = second[1] == second[0]
        for d0, d1, cond in ((True, False, diag0), (False, True, diag1),
                             (False, False, jnp.logical_not(jnp.logical_or(diag0, diag1)))):
            @pl.when(cond)
            def _(d0=d0, d1=d1):
                step(*first, 0, d0, mask_next=d1)
                step(*second, 1, d1, premasked=d1)
        return following(*second)

    n_steps = nq * qi0 + nq * (nq + 1) // 2
    lax.fori_loop(0, n_steps // 2, body, (qi0, 0))


def _mla(q, k, v, t, g, nq):
    B, _, S, _ = q.shape
    assert S % (nq * t) == 0 and nq % 2 == 0 and (nq * (nq + 1) // 2) % 2 == 0
    return pl.pallas_call(
        functools.partial(_mla_kernel, t=t, g=g, nq=nq),
        grid=(B, MLA_HEADS // g, S // (nq * t)),
        in_specs=[
            pl.BlockSpec((1, g, nq * t, MLA_QK_PAD), lambda b, h, i: (b, h, i, 0)),
            pl.BlockSpec((1, g, S, MLA_QK_PAD), lambda b, h, i: (b, h, 0, 0)),
            pl.BlockSpec((1, g, S, MLA_V), lambda b, h, i: (b, h, 0, 0)),
        ],
        out_specs=pl.BlockSpec((1, nq * t, g * MLA_V), lambda b, h, i: (b, i, h)),
        out_shape=jax.ShapeDtypeStruct((B, S, MLA_WIDTH), BF16),
        scratch_shapes=[
            pltpu.VMEM((g, t, MLA_QK_PAD), F32),
            pltpu.VMEM((g, t, 128), F32),
            pltpu.VMEM((2, g, t, t), F32),
            pltpu.VMEM((2, g, t, 128), F32),
        ],
        compiler_params=_params(("arbitrary", "arbitrary", "arbitrary")),
        name="mla",
    )(q, k, v)


_REL_SPAN = 4 * SWA_BLOCK
_N_OFFSETS = 3 * SWA_BLOCK - 1


def _bias_kernel(table_ref, bucket_ref, o_ref):
    head = lax.broadcasted_iota(jnp.int32, (SWA_HEADS, 128), 0)
    lane = lax.broadcasted_iota(jnp.int32, (SWA_HEADS, _REL_SPAN), 1)

    def lookup(c, per_offset):
        b = bucket_ref[c]
        col = jnp.zeros((SWA_HEADS, 128), F32)
        for hd in range(SWA_HEADS):
            col = jnp.where(head == hd, table_ref[b, hd], col)
        return jnp.where(lane == c, _lane_tile(col, _REL_SPAN // 128), per_offset)

    per_offset_all = lax.fori_loop(0, _N_OFFSETS, lookup, jnp.zeros((SWA_HEADS, _REL_SPAN), F32))
    for hd in range(SWA_HEADS):
        per_offset = per_offset_all[hd:hd + 1, :]
        rows = jnp.broadcast_to(per_offset, (SWA_BLOCK, _REL_SPAN))
        band = pltpu.roll(rows, _REL_SPAN - (SWA_BLOCK - 1), 1, stride=1, stride_axis=0)
        kvh, within = divmod(hd, SWA_GROUP)
        par, half = within % 2, within // 2
        o_ref[kvh, par, half * SWA_BLOCK:(half + 1) * SWA_BLOCK, :] = band[:, :2 * SWA_BLOCK]


def _bias(rel_table, bucket):
    return pl.pallas_call(
        _bias_kernel,
        in_specs=[
            pl.BlockSpec(memory_space=pltpu.SMEM),
            pl.BlockSpec(memory_space=pltpu.SMEM),
        ],
        out_specs=pl.BlockSpec(memory_space=pltpu.VMEM),
        out_shape=jax.ShapeDtypeStruct((SWA_KV_HEADS, 2, 2 * SWA_BLOCK, 2 * SWA_BLOCK), F32),
        name="t5bias",
    )(rel_table, bucket)


def _memkv_kernel(mem_ref, g_ref, w_ref, k_ref, v_ref):
    mn = _rms(mem_ref[0], g_ref[...]).astype(BF16)
    kv = _dot(mn, w_ref[...].astype(BF16))
    k_ref[0] = kv[:, :MEM_WIDTH].astype(BF16)
    v_ref[0] = kv[:, MEM_WIDTH:].astype(BF16)


def _memkv(mem, g, w):
    B, M, _ = mem.shape
    const = lambda b: (0, 0)
    row = lambda b: (b, 0, 0)
    return pl.pallas_call(
        _memkv_kernel,
        grid=(B,),
        in_specs=[
            pl.BlockSpec((1, M, D_MODEL), row),
            pl.BlockSpec((1, D_MODEL), const),
            pl.BlockSpec((D_MODEL, 2 * MEM_WIDTH), const),
        ],
        out_specs=[pl.BlockSpec((1, M, MEM_WIDTH), row)] * 2,
        out_shape=[jax.ShapeDtypeStruct((B, M, MEM_WIDTH), BF16)] * 2,
        compiler_params=_params(("arbitrary",)),
        name="memkv",
    )(mem, g, w)


def _swa_into(y_ref, sink_ref, q_ref, kp_ref, kc_ref, vp_ref, vc_ref, bias_ref, kb_ref, vb_ref, nsub,
              beside_unit):
    t = pl.program_id(1)
    scale = SWA_HEAD_DIM ** -0.5
    sb = SWA_BLOCK
    kb_ref[:sb] = kp_ref[0]
    kb_ref[sb:] = kc_ref[0]
    vb_ref[:sb] = vp_ref[0]
    vb_ref[sb:] = vc_ref[0]

    row = lax.broadcasted_iota(jnp.int32, (2 * sb, 2 * sb), 0)
    q_chunk = _chunk_of(row & (sb - 1))
    b_chunk = _chunk_of(lax.broadcasted_iota(jnp.int32, (2 * sb, 2 * sb), 1))
    valid_any = jnp.logical_and(b_chunk >= q_chunk, b_chunk <= q_chunk + WINDOW_CHUNKS)
    first_lo = jnp.where(t > 0, 0, 2)
    valid_first = jnp.logical_and(valid_any, b_chunk >= first_lo)
    upper = lax.broadcasted_iota(jnp.int32, (2 * sb, 1), 0) < sb

    def geometry(u):
        r, kvh = divmod(u, SWA_KV_HEADS)
        rows = slice(r * sb, (r + 1) * sb)
        band = slice(r * sb, (r + 2) * sb)
        pair0 = slice(2 * kvh * 128, (2 * kvh + 1) * 128)
        pair1 = slice((2 * kvh + 1) * 128, (2 * kvh + 2) * 128)
        return r, kvh, rows, band, pair0, pair1

    def scores(u):
        r, kvh, rows, band, pair0, pair1 = geometry(u)
        q = jnp.concatenate([q_ref[0, rows, pair0], q_ref[0, rows, pair1]], axis=0)
        return [_dot_nt(q, kb_ref[band, (2 * kvh + par) * 128:(2 * kvh + par + 1) * 128])
                for par in range(2)]

    def attend(u, unit_scores):
        r, kvh, rows, band, pair0, pair1 = geometry(u)
        valid = valid_first if r == 0 else valid_any
        o = None
        for par, s in enumerate(unit_scores):
            kcols = slice((2 * kvh + par) * 128, (2 * kvh + par + 1) * 128)
            hd = SWA_GROUP * kvh + par
            s = jnp.where(valid, s * scale + bias_ref[kvh, par], NEG)
            sink = jnp.where(upper, sink_ref[hd], sink_ref[hd + 2])
            m = jnp.maximum(jnp.max(s, axis=-1, keepdims=True), sink)
            p = jnp.exp(s - m)
            den = jnp.sum(p, axis=-1, keepdims=True) + jnp.exp(sink - m)
            pv = _dot((p * (1.0 / den)).astype(BF16), vb_ref[band, kcols])
            o = pv if o is None else o + pv
        y_ref[rows, pair0] = o[:sb]
        y_ref[rows, pair1] = o[sb:]

    n_units = nsub * SWA_KV_HEADS
    pending = scores(0)
    for u in range(n_units):
        upcoming = scores(u + 1) if u + 1 < n_units else None
        beside_unit(u)
        attend(u, pending)
        pending = upcoming


def _mem_into(y_ref, q_ref, k_ref, v_ref, after_head):
    scale = MEM_HEAD_DIM ** -0.5

    def head_cols(hd):
        return slice(hd * MEM_HEAD_DIM, (hd + 1) * MEM_HEAD_DIM)

    def scores(hd):
        return _dot_nt(q_ref[0, :, head_cols(hd)], k_ref[0, :, head_cols(hd)])

    pending = scores(0)
    for hd in range(MEM_HEADS):
        upcoming = scores(hd + 1) if hd + 1 < MEM_HEADS else None
        s = pending * scale
        m = jnp.max(s, axis=-1, keepdims=True)
        p = jnp.exp(s - m)
        inv = 1.0 / jnp.sum(p, axis=-1, keepdims=True)
        y_ref[:, head_cols(hd)] = _dot((p * inv).astype(BF16), v_ref[0, :, head_cols(hd)])
        after_head(hd)
        pending = upcoming


def _tail_kernel(sink_ref, x_ref, ya_ref, za_ref, zb_ref, zc_ref, qs_ref, qm_ref,
                 kp_ref, kc_ref, vp_ref, vc_ref, bias_ref, km_ref, vm_ref, w_hbm, g_ref,
                 o_ref, kb_ref, vb_ref, yb_ref, yc_ref, yacc_ref, w_ref, stage_ref, sem_ref, *, nsub):
    @pl.when(_first_grid_step())
    def _():
        _load_weight_bf16(w_hbm, w_ref, stage_ref, sem_ref)

    def gated(y, z_ref):
        h = 0.5 * z_ref[0].astype(F32)
        return (y * (h + h * jnp.tanh(h))).astype(BF16)

    swa_lo, mem_lo = MLA_WIDTH, MLA_WIDTH + SWA_WIDTH

    g_mla = gated(ya_ref[0].astype(F32), za_ref)
    mla_chunk = D_MODEL // (nsub * SWA_KV_HEADS)

    def project_mla(u):
        cols = slice(u * mla_chunk, (u + 1) * mla_chunk)
        yacc_ref[:, cols] = _dot(g_mla, w_ref[:swa_lo, cols])

    _swa_into(yb_ref, sink_ref, qs_ref, kp_ref, kc_ref, vp_ref, vc_ref, bias_ref, kb_ref, vb_ref, nsub,
              project_mla)

    g_swa = gated(yb_ref[...], zb_ref)
    swa_chunk = D_MODEL // MEM_HEADS

    def project_swa(u):
        cols = slice(u * swa_chunk, (u + 1) * swa_chunk)
        yacc_ref[:, cols] += _dot(g_swa, w_ref[swa_lo:mem_lo, cols])

    _mem_into(yc_ref, qm_ref, km_ref, vm_ref, project_swa)

    g_mem = gated(yc_ref[...], zc_ref)
    sum_sq = jnp.zeros((g_mem.shape[0], 1), F32)
    chunks = [slice(j * swa_chunk, (j + 1) * swa_chunk) for j in range(D_MODEL // swa_chunk)]
    for cols in chunks:
        r = x_ref[0, :, cols] + yacc_ref[:, cols] + _dot(g_mem, w_ref[mem_lo:, cols])
        o_ref[0, :, cols] = r
        sum_sq = sum_sq + jnp.sum(r * r, axis=-1, keepdims=True)
    inv = lax.rsqrt(sum_sq * (1.0 / D_MODEL) + EPS)
    for cols in chunks:
        o_ref[0, :, cols] = (o_ref[0, :, cols] * inv) * g_ref[:, cols]


def _tail(sinks, x, y_mla, z, qs, ks, vs, bias, kmem, vmem, w, g, nsub):
    B, S, _ = x.shape
    tm = nsub * SWA_BLOCK
    M = kmem.shape[1]
    row = lambda b, i: (b, i, 0)
    prev = lambda b, i: (b, jnp.maximum(i * nsub - 1, 0), 0)
    col = lambda c: (lambda b, i: (b, i, c))
    return pl.pallas_call(
        functools.partial(_tail_kernel, nsub=nsub),
        grid=(B, S // tm),
        in_specs=[
            pl.BlockSpec(memory_space=pltpu.SMEM),
            pl.BlockSpec((1, tm, D_MODEL), row),
            pl.BlockSpec((1, tm, MLA_WIDTH), row),
            pl.BlockSpec((1, tm, MLA_WIDTH), col(0)),
            pl.BlockSpec((1, tm, SWA_WIDTH), col(MLA_WIDTH // SWA_WIDTH)),
            pl.BlockSpec((1, tm, MEM_WIDTH), col((MLA_WIDTH + SWA_WIDTH) // MEM_WIDTH)),
            pl.BlockSpec((1, tm, SWA_WIDTH), col(0)),
            pl.BlockSpec((1, tm, MEM_WIDTH), col(1)),
            pl.BlockSpec((1, SWA_BLOCK, _SWA_KV_PAD), prev),
            pl.BlockSpec((1, tm, _SWA_KV_PAD), row),
            pl.BlockSpec((1, SWA_BLOCK, _SWA_KV_PAD), prev),
            pl.BlockSpec((1, tm, _SWA_KV_PAD), row),
            pl.BlockSpec((SWA_KV_HEADS, 2, 2 * SWA_BLOCK, 2 * SWA_BLOCK), lambda b, i: (0, 0, 0, 0)),
            pl.BlockSpec((1, M, MEM_WIDTH), lambda b, i: (b, 0, 0)),
            pl.BlockSpec((1, M, MEM_WIDTH), lambda b, i: (b, 0, 0)),
            pl.BlockSpec(memory_space=pl.ANY),
            pl.BlockSpec((1, D_MODEL), lambda b, i: (0, 0)),
        ],
        out_specs=pl.BlockSpec((1, tm, D_MODEL), row),
        out_shape=jax.ShapeDtypeStruct((B, S, D_MODEL), F32),
        scratch_shapes=[
            pltpu.VMEM((tm + SWA_BLOCK, _SWA_KV_PAD), BF16),
            pltpu.VMEM((tm + SWA_BLOCK, _SWA_KV_PAD), BF16),
            pltpu.VMEM((tm, SWA_WIDTH), F32),
            pltpu.VMEM((tm, MEM_WIDTH), F32),
            pltpu.VMEM((tm, D_MODEL), F32),
            pltpu.VMEM((MIX_WIDTH, D_MODEL), BF16),
            pltpu.VMEM((2, _W_OUT_CHUNK, D_MODEL), F32),
            pltpu.SemaphoreType.DMA((2,)),
        ],
        compiler_params=_params(("arbitrary", "arbitrary")),
        name="tail",
    )(sinks, x, y_mla, z, z, z, qs, qs, ks, ks, vs, vs, bias, kmem, vmem, w, g)


def _rope_tables(seq):
    inv = 1.0 / (ROPE_THETA ** (jnp.arange(0, MLA_ROPE, 2, dtype=F32) / MLA_ROPE))
    step = 1 << ((seq - 1).bit_length() // 2)
    assert seq % step == 0
    fine = jnp.arange(step, dtype=F32)[:, None] * inv[None, :]
    coarse = (jnp.arange(seq // step, dtype=F32) * step)[:, None] * inv[None, :]
    cf, sf = jnp.cos(fine)[None], jnp.sin(fine)[None]
    cc, sc = jnp.cos(coarse)[:, None], jnp.sin(coarse)[:, None]
    cos = (cc * cf - sc * sf).reshape(seq, -1)
    sin = (sc * cf + cc * sf).reshape(seq, -1)
    zero = jnp.zeros_like(cos)
    return (jnp.concatenate([cos, cos, zero, zero], axis=-1),
            jnp.concatenate([-sin, sin, zero, zero], axis=-1))


def _t5_bucket(rel):
    nb = N_BUCKETS // 2
    max_exact = nb // 2
    bucket = jnp.where(rel > 0, nb, 0)
    n = jnp.abs(rel)
    nf = jnp.maximum(n, 1).astype(F32)
    large = max_exact + (jnp.log(nf / max_exact) / math.log(MAX_DISTANCE / max_exact)
                         * (nb - max_exact)).astype(jnp.int32)
    large = jnp.minimum(large, nb - 1)
    return bucket + jnp.where(n < max_exact, n, large)


def kernel(x, mem, norm_in, w_in, norm_q, norm_kv, w_uq, w_ukv, attn_sinks, rel_bias,
           norm_mem, w_mem_kv, w_out, norm_final):
    B, S, _ = x.shape
    assert norm_in.shape[0] == 1, "single-layer trunk"
    assert x.shape[2] == D_MODEL and w_in.shape[1:] == (D_MODEL, IN_WIDTH)
    assert S % _PROJ_ROWS == 0 and S % (_TAIL_BLOCKS * SWA_BLOCK) == 0

    wt = w_in[0].T
    cos_t, sin_t = _rope_tables(S)
    q, k, v, qs, z, ks, vs = _proj(x, norm_in, wt, norm_q, norm_kv, w_uq[0].astype(BF16),
                                   w_ukv[0].astype(BF16), cos_t, sin_t, tm=_PROJ_ROWS)

    y_mla = _mla(q, k, v, t=_MLA_TILE, g=_MLA_GROUP, nq=_MLA_TILES_PER_STEP)

    bucket = _t5_bucket(jnp.arange(_REL_SPAN) - (2 * SWA_BLOCK - 1)).astype(jnp.int32)
    bias = _bias(rel_bias, bucket)
    kmem, vmem = _memkv(mem, norm_mem, w_mem_kv[0])
    return _tail(attn_sinks[0], x, y_mla, z, qs, ks, vs, bias, kmem, vmem,
                 w_out[0], norm_final[None, :], nsub=_TAIL_BLOCKS)
```

```python
import functools
import math

import jax
import jax.numpy as jnp
from jax import lax
from jax.experimental import pallas as pl
from jax.experimental.pallas import tpu as pltpu

D_MODEL = 2048
CHUNK = 64
EPS = 1e-6
NEG = -1e30

MLA_HEADS = 8
MLA_NOPE = 128
MLA_ROPE = 64
MLA_V = 128
MLA_QLORA = 512
MLA_KVLORA = 256
MLA_WIDTH = MLA_HEADS * MLA_V
MLA_QK_PAD = 256
ROPE_THETA = 10000.0

SWA_HEADS = 8
SWA_KV_HEADS = 2
SWA_GROUP = SWA_HEADS // SWA_KV_HEADS
SWA_HEAD_DIM = 64
SWA_WIDTH = SWA_HEADS * SWA_HEAD_DIM
SWA_KV_WIDTH = SWA_KV_HEADS * SWA_HEAD_DIM
WINDOW_CHUNKS = 2
SWA_BLOCK = 128

MEM_HEADS = 4
MEM_HEAD_DIM = 128
MEM_WIDTH = MEM_HEADS * MEM_HEAD_DIM

MIX_WIDTH = MLA_WIDTH + SWA_WIDTH + MEM_WIDTH

N_BUCKETS = 32
MAX_DISTANCE = 128

BF16 = jnp.bfloat16
F32 = jnp.float32

_R_CQ = 0
_R_CKV = _R_CQ + MLA_QLORA
_R_KPE = _R_CKV + MLA_KVLORA
_R_ZMLA = _R_KPE + MLA_ROPE
_R_QSWA = _R_ZMLA + MLA_WIDTH
_R_KSWA = _R_QSWA + SWA_WIDTH
_R_VSWA = _R_KSWA + SWA_KV_WIDTH
_R_ZSWA = _R_VSWA + SWA_KV_WIDTH
_R_QMEM = _R_ZSWA + SWA_WIDTH
_R_ZMEM = _R_QMEM + MEM_WIDTH
IN_WIDTH = _R_ZMEM + MEM_WIDTH

_MLA_Q_SCALE = (MLA_NOPE + MLA_ROPE) ** -0.5 * math.log2(math.e)

_SWA_KV_PAD = 2 * SWA_KV_HEADS * 128

_W_IN_CHUNK = 208
_W_OUT_CHUNK = 256

_V7X_VMEM_BYTES = 64 * 1024 * 1024
_VMEM_LIMIT = _V7X_VMEM_BYTES - 8 * 1024 * 1024

_PROJ_ROWS = 512
_MLA_TILE = 512
_MLA_GROUP = 4
_MLA_TILES_PER_STEP = 4
_TAIL_BLOCKS = 4


def _params(sem):
    return pltpu.CompilerParams(dimension_semantics=sem, vmem_limit_bytes=_VMEM_LIMIT)


def _rms(v, g):
    return (v * lax.rsqrt(jnp.mean(v * v, axis=-1, keepdims=True) + EPS)) * g


def _dot(a, b):
    return jnp.dot(a, b, preferred_element_type=F32)


def _dot_nt(a, b):
    return lax.dot_general(a, b, (((1,), (1,)), ((), ())), preferred_element_type=F32)


def _load_weight_bf16(w_hbm, w_ref, stage_ref, sem_ref):
    rows = stage_ref.shape[1]
    n_chunks = w_ref.shape[0] // rows

    def chunk_copy(c):
        return pltpu.make_async_copy(w_hbm.at[pl.ds(c * rows, rows), :], stage_ref.at[c % 2],
                                     sem_ref.at[c % 2])

    chunk_copy(0).start()
    for c in range(n_chunks):
        if c + 1 < n_chunks:
            chunk_copy(c + 1).start()
        chunk_copy(c).wait()
        w_ref[c * rows:(c + 1) * rows, :] = stage_ref[c % 2].astype(BF16)


def _first_grid_step():
    return jnp.logical_and(pl.program_id(0) == 0, pl.program_id(1) == 0)


def _lane_tile(v, n):
    return jnp.concatenate([v] * n, axis=1)


def _chunk_of(pos):
    return jnp.right_shift(pos, CHUNK.bit_length() - 1)


def _rope128(v, cos_t, sin_t):
    return v * cos_t + pltpu.roll(v, MLA_ROPE // 2, 1) * sin_t


def _store_swa_kv(ref, kv):
    low = lax.broadcasted_iota(jnp.int32, kv.shape, 1) < SWA_HEAD_DIM
    swapped = pltpu.roll(kv, SWA_HEAD_DIM, 1)
    pieces = (jnp.where(low, kv, 0.0), jnp.where(low, 0.0, swapped),
              jnp.where(low, swapped, 0.0), jnp.where(low, 0.0, kv))
    for i, piece in enumerate(pieces):
        ref[0, :, i * 128:(i + 1) * 128] = piece.astype(BF16)


def _proj_kernel(x_ref, gin_ref, wt_hbm, gq_ref, gkv_ref, wuq_ref, wukv_ref, cos_ref, sin_ref,
                 q_ref, k_ref, v_ref, qs_ref, z_ref, ks_ref, vs_ref, wt_ref, stage_ref, sem_ref):
    @pl.when(_first_grid_step())
    def _():
        _load_weight_bf16(wt_hbm, wt_ref, stage_ref, sem_ref)

    h = _rms(x_ref[0], gin_ref[...]).astype(BF16)
    cos_t = cos_ref[...]
    sin_t = sin_ref[...]

    def proj(lo, hi):
        return _dot_nt(h, wt_ref[lo:hi, :])

    pa = proj(_R_CQ, _R_KPE)
    cq = _rms(pa[:, :MLA_QLORA], gq_ref[...]).astype(BF16)
    ckv = _rms(pa[:, MLA_QLORA:], gkv_ref[...]).astype(BF16)

    qs_ref[0, :, :SWA_WIDTH] = proj(_R_QSWA, _R_KSWA).astype(BF16)
    qs_ref[0, :, SWA_WIDTH:] = proj(_R_QMEM, _R_ZMEM).astype(BF16)
    z_ref[0, :, :MLA_WIDTH] = proj(_R_ZMLA, _R_QSWA).astype(BF16)
    z_ref[0, :, MLA_WIDTH:MLA_WIDTH + SWA_WIDTH] = proj(_R_ZSWA, _R_QMEM).astype(BF16)
    z_ref[0, :, MLA_WIDTH + SWA_WIDTH:] = proj(_R_ZMEM, IN_WIDTH).astype(BF16)
    kvs = proj(_R_KSWA, _R_ZSWA)
    _store_swa_kv(ks_ref, kvs[:, :SWA_KV_WIDTH])
    _store_swa_kv(vs_ref, kvs[:, SWA_KV_WIDTH:])
    pe = proj(_R_KPE, _R_ZMLA)
    kpe = _rope128(jnp.concatenate([pe, pe], axis=1), cos_t, sin_t).astype(BF16)

    qall = _dot(cq, wuq_ref[...]) * _MLA_Q_SCALE
    kv = _dot(ckv, wukv_ref[...])
    for hd in range(MLA_HEADS):
        oq = hd * (MLA_NOPE + MLA_ROPE)
        q_pe = qall[:, oq + MLA_NOPE:oq + MLA_NOPE + MLA_ROPE]
        q_ref[0, hd, :, :MLA_NOPE] = qall[:, oq:oq + MLA_NOPE].astype(BF16)
        q_ref[0, hd, :, MLA_NOPE:] = _rope128(
            jnp.concatenate([q_pe, q_pe], axis=1), cos_t, sin_t).astype(BF16)
        o = hd * (MLA_NOPE + MLA_V)
        k_ref[0, hd, :, :MLA_NOPE] = kv[:, o:o + MLA_NOPE].astype(BF16)
        k_ref[0, hd, :, MLA_NOPE:] = kpe
        v_ref[0, hd] = kv[:, o + MLA_NOPE:o + MLA_QK_PAD].astype(BF16)


def _proj(x, gin, wt, gq, gkv, wuq, wukv, cos_t, sin_t, tm):
    B, S, _ = x.shape
    const = lambda b, i: (0, 0)
    row = lambda b, i: (b, i, 0)
    single = pl.Buffered(1)

    def out(width):
        return jax.ShapeDtypeStruct((B, S, width), BF16)

    head_spec = pl.BlockSpec((1, MLA_HEADS, tm, MLA_QK_PAD), lambda b, i: (b, 0, i, 0))
    head_shape = jax.ShapeDtypeStruct((B, MLA_HEADS, S, MLA_QK_PAD), BF16)

    return pl.pallas_call(
        _proj_kernel,
        grid=(B, S // tm),
        in_specs=[
            pl.BlockSpec((1, tm, D_MODEL), row),
            pl.BlockSpec((1, D_MODEL), const),
            pl.BlockSpec(memory_space=pl.ANY),
            pl.BlockSpec((1, MLA_QLORA), const),
            pl.BlockSpec((1, MLA_KVLORA), const),
            pl.BlockSpec((MLA_QLORA, MLA_HEADS * (MLA_NOPE + MLA_ROPE)), const, pipeline_mode=single),
            pl.BlockSpec((MLA_KVLORA, MLA_HEADS * (MLA_NOPE + MLA_V)), const, pipeline_mode=single),
            pl.BlockSpec((tm, 128), lambda b, i: (i, 0)),
            pl.BlockSpec((tm, 128), lambda b, i: (i, 0)),
        ],
        out_specs=[
            head_spec, head_spec,
            pl.BlockSpec((1, MLA_HEADS, tm, MLA_V), lambda b, i: (b, 0, i, 0)),
            pl.BlockSpec((1, tm, SWA_WIDTH + MEM_WIDTH), row),
            pl.BlockSpec((1, tm, MIX_WIDTH), row),
            pl.BlockSpec((1, tm, _SWA_KV_PAD), row),
            pl.BlockSpec((1, tm, _SWA_KV_PAD), row),
        ],
        out_shape=[
            head_shape, head_shape, jax.ShapeDtypeStruct((B, MLA_HEADS, S, MLA_V), BF16),
            out(SWA_WIDTH + MEM_WIDTH), out(MIX_WIDTH), out(_SWA_KV_PAD), out(_SWA_KV_PAD),
        ],
        scratch_shapes=[
            pltpu.VMEM((IN_WIDTH, D_MODEL), BF16),
            pltpu.VMEM((2, _W_IN_CHUNK, D_MODEL), F32),
            pltpu.SemaphoreType.DMA((2,)),
        ],
        compiler_params=_params(("arbitrary", "arbitrary")),
        name="proj",
    )(x, gin, wt, gq, gkv, wuq, wukv, cos_t, sin_t)


def _mla_kernel(q_ref, k_ref, v_ref, o_ref, acc_ref, m_ref, s_ref, cm_ref, *, t, g, nq):
    qi0 = pl.program_id(2) * nq
    qi_end = qi0 + nq - 1
    acc_ref[...] = jnp.zeros(acc_ref.shape, F32)
    m_ref[...] = jnp.full(m_ref.shape, NEG, F32)
    ones = jnp.ones((t, MLA_V), BF16)
    row_halves = ((0, t // 2, t // 2), (t // 2, t // 2, t))

    def row_max(s):
        return jnp.broadcast_to(jnp.max(s, axis=-1, keepdims=True), (s.shape[0], 128))

    def diagonal_mask():
        q_chunk = _chunk_of(lax.broadcasted_iota(jnp.int32, (t, t), 0))
        k_chunk = _chunk_of(lax.broadcasted_iota(jnp.int32, (t, t), 1))
        return k_chunk <= q_chunk

    def scores(qi, c, slot, hd, mask=None):
        rows = pl.multiple_of((qi - qi0) * t, t)
        keys = pl.multiple_of(c * t, t)
        for r0, nr, nk in (row_halves if mask is not None else ((0, t, t),)):
            s = _dot_nt(q_ref[0, hd, pl.ds(pl.multiple_of(rows + r0, nr), nr), :],
                        k_ref[0, hd, pl.ds(keys, nk), :])
            if mask is not None:
                s = jnp.where(mask[r0:r0 + nr, :nk], s, NEG)
            s_ref[slot, hd, r0:r0 + nr, :nk] = s
            cm_ref[slot, hd, r0:r0 + nr] = row_max(s)

    def update(c, slot, hd, diagonal, mask):
        keys = pl.multiple_of(c * t, t)
        for r0, nr, nk in (row_halves if diagonal else ((0, t, t),)):
            rows = slice(r0, r0 + nr)
            s = s_ref[slot, hd, rows, :nk]
            if mask is not None:
                s = jnp.where(mask[rows, :nk], s, NEG)
                m_cur = row_max(s)
            else:
                m_cur = cm_ref[slot, hd, rows]
            m_old = m_ref[hd, rows]
            m_new = jnp.maximum(m_old, m_cur)
            alpha = jnp.exp2(m_old - m_new)
            p = jnp.exp2(s - _lane_tile(m_new, nk // 128)).astype(BF16)
            v_ones = jnp.concatenate([v_ref[0, hd, pl.ds(keys, nk), :], ones[:nk]], axis=1)
            acc_ref[hd, rows] = (_lane_tile(alpha, MLA_QK_PAD // 128) * acc_ref[hd, rows]
                                 + _dot(p, v_ones))
            m_ref[hd, rows] = m_new

    def finish_tile(qi, hd):
        rows = pl.multiple_of((qi - qi0) * t, t)
        acc = acc_ref[hd]
        o_ref[0, pl.ds(rows, t), hd * MLA_V:(hd + 1) * MLA_V] = (
            acc[:, :MLA_V] / acc[:, MLA_V:]).astype(BF16)
        acc_ref[hd] = jnp.zeros(acc.shape, F32)
        m_ref[hd] = jnp.full((t, 128), NEG, F32)

    def following(qi, c):
        is_last = c == qi
        return jnp.where(is_last, jnp.minimum(qi + 1, qi_end), qi), jnp.where(is_last, 0, c + 1)

    def step(qi, c, slot, diagonal, premasked=False, mask_next=False):
        qn, cn = following(qi, c)
        mask = diagonal_mask() if diagonal and not premasked else None
        next_mask = diagonal_mask() if mask_next else None
        for hd in range(g):
            scores(qn, cn, 1 - slot, hd, next_mask)
            update(c, slot, hd, diagonal, mask)
            if diagonal:
                finish_tile(qi, hd)

    for hd in range(g):
        scores(qi0, 0, 0, hd)

    def body(_, pair):
        first, second = pair, following(*pair)
        diag0 = first[1] == first[0]
        diag1 = second[1] == second[0]
        for d0, d1, cond in ((True, False, diag0), (False, True, diag1),
                             (False, False, jnp.logical_not(jnp.logical_or(diag0, diag1)))):
            @pl.when(cond)
            def _(d0=d0, d1=d1):
                step(*first, 0, d0, mask_next=d1)
                step(*second, 1, d1, premasked=d1)
        return following(*second)

    n_steps = nq * qi0 + nq * (nq + 1) // 2
    lax.fori_loop(0, n_steps // 2, body, (qi0, 0))


def _mla(q, k, v, t, g, nq):
    B, _, S, _ = q.shape
    assert S % (nq * t) == 0 and nq % 2 == 0 and (nq * (nq + 1) // 2) % 2 == 0
    return pl.pallas_call(
        functools.partial(_mla_kernel, t=t, g=g, nq=nq),
        grid=(B, MLA_HEADS // g, S // (nq * t)),
        in_specs=[
            pl.BlockSpec((1, g, nq * t, MLA_QK_PAD), lambda b, h, i: (b, h, i, 0)),
            pl.BlockSpec((1, g, S, MLA_QK_PAD), lambda b, h, i: (b, h, 0, 0)),
            pl.BlockSpec((1, g, S, MLA_V), lambda b, h, i: (b, h, 0, 0)),
        ],
        out_specs=pl.BlockSpec((1, nq * t, g * MLA_V), lambda b, h, i: (b, i, h)),
        out_shape=jax.ShapeDtypeStruct((B, S, MLA_WIDTH), BF16),
        scratch_shapes=[
            pltpu.VMEM((g, t, MLA_QK_PAD), F32),
            pltpu.VMEM((g, t, 128), F32),
            pltpu.VMEM((2, g, t, t), F32),
            pltpu.VMEM((2, g, t, 128), F32),
        ],
        compiler_params=_params(("arbitrary", "arbitrary", "arbitrary")),
        name="mla",
    )(q, k, v)


_REL_SPAN = 4 * SWA_BLOCK
_N_OFFSETS = 3 * SWA_BLOCK - 1


def _bias_kernel(table_ref, bucket_ref, o_ref, spread_ref):
    for b in range(N_BUCKETS):
        spread_ref[b] = jnp.concatenate(
            [jnp.full((1, 128), table_ref[b, hd], F32) for hd in range(SWA_HEADS)], axis=0)
    lane = lax.broadcasted_iota(jnp.int32, (SWA_HEADS, _REL_SPAN), 1)

    def lookup(c, per_offset):
        col = spread_ref[bucket_ref[c]]
        return jnp.where(lane == c, _lane_tile(col, _REL_SPAN // 128), per_offset)

    per_offset_all = lax.fori_loop(0, _N_OFFSETS, lookup, jnp.zeros((SWA_HEADS, _REL_SPAN), F32))
    for hd in range(SWA_HEADS):
        per_offset = per_offset_all[hd:hd + 1, :]
        rows = jnp.broadcast_to(per_offset, (SWA_BLOCK, _REL_SPAN))
        band = pltpu.roll(rows, _REL_SPAN - (SWA_BLOCK - 1), 1, stride=1, stride_axis=0)
        kvh, within = divmod(hd, SWA_GROUP)
        par, half = within % 2, within // 2
        o_ref[kvh, par, half * SWA_BLOCK:(half + 1) * SWA_BLOCK, :] = band[:, :2 * SWA_BLOCK]


def _bias(rel_table, bucket):
    return pl.pallas_call(
        _bias_kernel,
        in_specs=[
            pl.BlockSpec(memory_space=pltpu.SMEM),
            pl.BlockSpec(memory_space=pltpu.SMEM),
        ],
        out_specs=pl.BlockSpec(memory_space=pltpu.VMEM),
        out_shape=jax.ShapeDtypeStruct((SWA_KV_HEADS, 2, 2 * SWA_BLOCK, 2 * SWA_BLOCK), F32),
        scratch_shapes=[pltpu.VMEM((N_BUCKETS, SWA_HEADS, 128), F32)],
        name="t5bias",
    )(rel_table, bucket)


def _memkv_kernel(mem_ref, g_ref, w_ref, k_ref, v_ref):
    mn = _rms(mem_ref[0], g_ref[...]).astype(BF16)
    kv = _dot(mn, w_ref[...].astype(BF16))
    k_ref[0] = kv[:, :MEM_WIDTH].astype(BF16)
    v_ref[0] = kv[:, MEM_WIDTH:].astype(BF16)


def _memkv(mem, g, w):
    B, M, _ = mem.shape
    const = lambda b: (0, 0)
    row = lambda b: (b, 0, 0)
    return pl.pallas_call(
        _memkv_kernel,
        grid=(B,),
        in_specs=[
            pl.BlockSpec((1, M, D_MODEL), row),
            pl.BlockSpec((1, D_MODEL), const),
            pl.BlockSpec((D_MODEL, 2 * MEM_WIDTH), const),
        ],
        out_specs=[pl.BlockSpec((1, M, MEM_WIDTH), row)] * 2,
        out_shape=[jax.ShapeDtypeStruct((B, M, MEM_WIDTH), BF16)] * 2,
        compiler_params=_params(("arbitrary",)),
        name="memkv",
    )(mem, g, w)


def _swa_into(y_ref, sink_ref, q_ref, kp_ref, kc_ref, vp_ref, vc_ref, bias_ref, kb_ref, vb_ref, nsub,
              beside_unit):
    t = pl.program_id(1)
    scale = SWA_HEAD_DIM ** -0.5
    sb = SWA_BLOCK
    kb_ref[:sb] = kp_ref[0]
    kb_ref[sb:] = kc_ref[0]
    vb_ref[:sb] = vp_ref[0]
    vb_ref[sb:] = vc_ref[0]

    row = lax.broadcasted_iota(jnp.int32, (2 * sb, 2 * sb), 0)
    q_chunk = _chunk_of(row & (sb - 1))
    b_chunk = _chunk_of(lax.broadcasted_iota(jnp.int32, (2 * sb, 2 * sb), 1))
    valid_any = jnp.logical_and(b_chunk >= q_chunk, b_chunk <= q_chunk + WINDOW_CHUNKS)
    first_lo = jnp.where(t > 0, 0, 2)
    valid_first = jnp.logical_and(valid_any, b_chunk >= first_lo)
    upper = lax.broadcasted_iota(jnp.int32, (2 * sb, 1), 0) < sb

    def geometry(u):
        r, kvh = divmod(u, SWA_KV_HEADS)
        rows = slice(r * sb, (r + 1) * sb)
        band = slice(r * sb, (r + 2) * sb)
        pair0 = slice(2 * kvh * 128, (2 * kvh + 1) * 128)
        pair1 = slice((2 * kvh + 1) * 128, (2 * kvh + 2) * 128)
        return r, kvh, rows, band, pair0, pair1

    def scores(u):
        r, kvh, rows, band, pair0, pair1 = geometry(u)
        q = jnp.concatenate([q_ref[0, rows, pair0], q_ref[0, rows, pair1]], axis=0)
        return [_dot_nt(q, kb_ref[band, (2 * kvh + par) * 128:(2 * kvh + par + 1) * 128])
                for par in range(2)]

    def attend(u, unit_scores):
        r, kvh, rows, band, pair0, pair1 = geometry(u)
        valid = valid_first if r == 0 else valid_any
        o = None
        for par, s in enumerate(unit_scores):
            kcols = slice((2 * kvh + par) * 128, (2 * kvh + par + 1) * 128)
            hd = SWA_GROUP * kvh + par
            s = jnp.where(valid, s * scale + bias_ref[kvh, par], NEG)
            sink = jnp.where(upper, sink_ref[hd], sink_ref[hd + 2])
            m = jnp.maximum(jnp.max(s, axis=-1, keepdims=True), sink)
            p = jnp.exp(s - m)
            den = jnp.sum(p, axis=-1, keepdims=True) + jnp.exp(sink - m)
            pv = _dot((p * (1.0 / den)).astype(BF16), vb_ref[band, kcols])
            o = pv if o is None else o + pv
        y_ref[rows, pair0] = o[:sb]
        y_ref[rows, pair1] = o[sb:]

    n_units = nsub * SWA_KV_HEADS
    pending = scores(0)
    for u in range(n_units):
        upcoming = scores(u + 1) if u + 1 < n_units else None
        beside_unit(u)
        attend(u, pending)
        pending = upcoming


def _mem_into(y_ref, q_ref, k_ref, v_ref, after_head):
    scale = MEM_HEAD_DIM ** -0.5

    def head_cols(hd):
        return slice(hd * MEM_HEAD_DIM, (hd + 1) * MEM_HEAD_DIM)

    def scores(hd):
        return _dot_nt(q_ref[0, :, head_cols(hd)], k_ref[0, :, head_cols(hd)])

    pending = scores(0)
    for hd in range(MEM_HEADS):
        upcoming = scores(hd + 1) if hd + 1 < MEM_HEADS else None
        s = pending * scale
        m = jnp.max(s, axis=-1, keepdims=True)
        p = jnp.exp(s - m)
        inv = 1.0 / jnp.sum(p, axis=-1, keepdims=True)
        y_ref[:, head_cols(hd)] = _dot((p * inv).astype(BF16), v_ref[0, :, head_cols(hd)])
        after_head(hd)
        pending = upcoming


def _tail_kernel(sink_ref, x_ref, ya_ref, za_ref, zb_ref, zc_ref, qs_ref, qm_ref,
                 kp_ref, kc_ref, vp_ref, vc_ref, bias_ref, km_ref, vm_ref, w_hbm, g_ref,
                 o_ref, kb_ref, vb_ref, yb_ref, yc_ref, yacc_ref, w_ref, stage_ref, sem_ref, *, nsub):
    @pl.when(_first_grid_step())
    def _():
        _load_weight_bf16(w_hbm, w_ref, stage_ref, sem_ref)

    def gated(y, z_ref):
        h = 0.5 * z_ref[0].astype(F32)
        return (y * (h + h * jnp.tanh(h))).astype(BF16)

    swa_lo, mem_lo = MLA_WIDTH, MLA_WIDTH + SWA_WIDTH

    g_mla = gated(ya_ref[0].astype(F32), za_ref)
    mla_chunk = D_MODEL // (nsub * SWA_KV_HEADS)

    def project_mla(u):
        cols = slice(u * mla_chunk, (u + 1) * mla_chunk)
        yacc_ref[:, cols] = _dot(g_mla, w_ref[:swa_lo, cols])

    _swa_into(yb_ref, sink_ref, qs_ref, kp_ref, kc_ref, vp_ref, vc_ref, bias_ref, kb_ref, vb_ref, nsub,
              project_mla)

    g_swa = gated(yb_ref[...], zb_ref)
    swa_chunk = D_MODEL // MEM_HEADS

    def project_swa(u):
        cols = slice(u * swa_chunk, (u + 1) * swa_chunk)
        yacc_ref[:, cols] += _dot(g_swa, w_ref[swa_lo:mem_lo, cols])

    _mem_into(yc_ref, qm_ref, km_ref, vm_ref, project_swa)

    g_mem = gated(yc_ref[...], zc_ref)
    sum_sq = jnp.zeros((g_mem.shape[0], 1), F32)
    chunks = [slice(j * swa_chunk, (j + 1) * swa_chunk) for j in range(D_MODEL // swa_chunk)]
    for cols in chunks:
        r = x_ref[0, :, cols] + yacc_ref[:, cols] + _dot(g_mem, w_ref[mem_lo:, cols])
        o_ref[0, :, cols] = r
        sum_sq = sum_sq + jnp.sum(r * r, axis=-1, keepdims=True)
    inv = lax.rsqrt(sum_sq * (1.0 / D_MODEL) + EPS)
    for cols in chunks:
        o_ref[0, :, cols] = (o_ref[0, :, cols] * inv) * g_ref[:, cols]


def _tail(sinks, x, y_mla, z, qs, ks, vs, bias, kmem, vmem, w, g, nsub):
    B, S, _ = x.shape
    tm = nsub * SWA_BLOCK
    M = kmem.shape[1]
    row = lambda b, i: (b, i, 0)
    prev = lambda b, i: (b, jnp.maximum(i * nsub - 1, 0), 0)
    col = lambda c: (lambda b, i: (b, i, c))
    return pl.pallas_call(
        functools.partial(_tail_kernel, nsub=nsub),
        grid=(B, S // tm),
        in_specs=[
            pl.BlockSpec(memory_space=pltpu.SMEM),
            pl.BlockSpec((1, tm, D_MODEL), row),
            pl.BlockSpec((1, tm, MLA_WIDTH), row),
            pl.BlockSpec((1, tm, MLA_WIDTH), col(0)),
            pl.BlockSpec((1, tm, SWA_WIDTH), col(MLA_WIDTH // SWA_WIDTH)),
            pl.BlockSpec((1, tm, MEM_WIDTH), col((MLA_WIDTH + SWA_WIDTH) // MEM_WIDTH)),
            pl.BlockSpec((1, tm, SWA_WIDTH), col(0)),
            pl.BlockSpec((1, tm, MEM_WIDTH), col(1)),
            pl.BlockSpec((1, SWA_BLOCK, _SWA_KV_PAD), prev),
            pl.BlockSpec((1, tm, _SWA_KV_PAD), row),
            pl.BlockSpec((1, SWA_BLOCK, _SWA_KV_PAD), prev),
            pl.BlockSpec((1, tm, _SWA_KV_PAD), row),
            pl.BlockSpec((SWA_KV_HEADS, 2, 2 * SWA_BLOCK, 2 * SWA_BLOCK), lambda b, i: (0, 0, 0, 0)),
            pl.BlockSpec((1, M, MEM_WIDTH), lambda b, i: (b, 0, 0)),
            pl.BlockSpec((1, M, MEM_WIDTH), lambda b, i: (b, 0, 0)),
            pl.BlockSpec(memory_space=pl.ANY),
            pl.BlockSpec((1, D_MODEL), lambda b, i: (0, 0)),
        ],
        out_specs=pl.BlockSpec((1, tm, D_MODEL), row),
        out_shape=jax.ShapeDtypeStruct((B, S, D_MODEL), F32),
        scratch_shapes=[
            pltpu.VMEM((tm + SWA_BLOCK, _SWA_KV_PAD), BF16),
            pltpu.VMEM((tm + SWA_BLOCK, _SWA_KV_PAD), BF16),
            pltpu.VMEM((tm, SWA_WIDTH), F32),
            pltpu.VMEM((tm, MEM_WIDTH), F32),
            pltpu.VMEM((tm, D_MODEL), F32),
            pltpu.VMEM((MIX_WIDTH, D_MODEL), BF16),
            pltpu.VMEM((2, _W_OUT_CHUNK, D_MODEL), F32),
            pltpu.SemaphoreType.DMA((2,)),
        ],
        compiler_params=_params(("arbitrary", "arbitrary")),
        name="tail",
    )(sinks, x, y_mla, z, z, z, qs, qs, ks, ks, vs, vs, bias, kmem, vmem, w, g)


def _rope_tables(seq):
    inv = 1.0 / (ROPE_THETA ** (jnp.arange(0, MLA_ROPE, 2, dtype=F32) / MLA_ROPE))
    step = 1 << ((seq - 1).bit_length() // 2)
    assert seq % step == 0
    fine = jnp.arange(step, dtype=F32)[:, None] * inv[None, :]
    coarse = (jnp.arange(seq // step, dtype=F32) * step)[:, None] * inv[None, :]
    cf, sf = jnp.cos(fine)[None], jnp.sin(fine)[None]
    cc, sc = jnp.cos(coarse)[:, None], jnp.sin(coarse)[:, None]
    cos = (cc * cf - sc * sf).reshape(seq, -1)
    sin = (sc * cf + cc * sf).reshape(seq, -1)
    zero = jnp.zeros_like(cos)
    return (jnp.concatenate([cos, cos, zero, zero], axis=-1),
            jnp.concatenate([-sin, sin, zero, zero], axis=-1))


def _t5_bucket(rel):
    nb = N_BUCKETS // 2
    max_exact = nb // 2
    bucket = jnp.where(rel > 0, nb, 0)
    n = jnp.abs(rel)
    nf = jnp.maximum(n, 1).astype(F32)
    large = max_exact + (jnp.log(nf / max_exact) / math.log(MAX_DISTANCE / max_exact)
                         * (nb - max_exact)).astype(jnp.int32)
    large = jnp.minimum(large, nb - 1)
    return bucket + jnp.where(n < max_exact, n, large)


def kernel(x, mem, norm_in, w_in, norm_q, norm_kv, w_uq, w_ukv, attn_sinks, rel_bias,
           norm_mem, w_mem_kv, w_out, norm_final):
    B, S, _ = x.shape
    assert norm_in.shape[0] == 1, "single-layer trunk"
    assert x.shape[2] == D_MODEL and w_in.shape[1:] == (D_MODEL, IN_WIDTH)
    assert S % _PROJ_ROWS == 0 and S % (_TAIL_BLOCKS * SWA_BLOCK) == 0

    wt = w_in[0].T
    cos_t, sin_t = _rope_tables(S)
    q, k, v, qs, z, ks, vs = _proj(x, norm_in, wt, norm_q, norm_kv, w_uq[0].astype(BF16),
                                   w_ukv[0].astype(BF16), cos_t, sin_t, tm=_PROJ_ROWS)

    y_mla = _mla(q, k, v, t=_MLA_TILE, g=_MLA_GROUP, nq=_MLA_TILES_PER_STEP)

    bucket = _t5_bucket(jnp.arange(_REL_SPAN) - (2 * SWA_BLOCK - 1)).astype(jnp.int32)
    bias = _bias(rel_bias, bucket)
    kmem, vmem = _memkv(mem, norm_mem, w_mem_kv[0])
    return _tail(attn_sinks[0], x, y_mla, z, qs, ks, vs, bias, kmem, vmem,
                 w_out[0], norm_final[None, :], nsub=_TAIL_BLOCKS)
```

```python
import functools
import math

import jax
import jax.numpy as jnp
from jax import lax
from jax.experimental import pallas as pl
from jax.experimental.pallas import tpu as pltpu

D_MODEL = 2048
CHUNK = 64
EPS = 1e-6
NEG = -1e30

MLA_HEADS = 8
MLA_NOPE = 128
MLA_ROPE = 64
MLA_V = 128
MLA_QLORA = 512
MLA_KVLORA = 256
MLA_WIDTH = MLA_HEADS * MLA_V
MLA_QK_PAD = 256
ROPE_THETA = 10000.0

SWA_HEADS = 8
SWA_KV_HEADS = 2
SWA_GROUP = SWA_HEADS // SWA_KV_HEADS
SWA_HEAD_DIM = 64
SWA_WIDTH = SWA_HEADS * SWA_HEAD_DIM
SWA_KV_WIDTH = SWA_KV_HEADS * SWA_HEAD_DIM
WINDOW_CHUNKS = 2
SWA_BLOCK = 128

MEM_HEADS = 4
MEM_HEAD_DIM = 128
MEM_WIDTH = MEM_HEADS * MEM_HEAD_DIM

MIX_WIDTH = MLA_WIDTH + SWA_WIDTH + MEM_WIDTH

N_BUCKETS = 32
MAX_DISTANCE = 128

BF16 = jnp.bfloat16
F32 = jnp.float32

_R_CQ = 0
_R_CKV = _R_CQ + MLA_QLORA
_R_KPE = _R_CKV + MLA_KVLORA
_R_ZMLA = _R_KPE + MLA_ROPE
_R_QSWA = _R_ZMLA + MLA_WIDTH
_R_KSWA = _R_QSWA + SWA_WIDTH
_R_VSWA = _R_KSWA + SWA_KV_WIDTH
_R_ZSWA = _R_VSWA + SWA_KV_WIDTH
_R_QMEM = _R_ZSWA + SWA_WIDTH
_R_ZMEM = _R_QMEM + MEM_WIDTH
IN_WIDTH = _R_ZMEM + MEM_WIDTH

_MLA_Q_SCALE = (MLA_NOPE + MLA_ROPE) ** -0.5 * math.log2(math.e)

_SWA_KV_PAD = 2 * SWA_KV_HEADS * 128

_W_IN_CHUNK = 208
_W_OUT_CHUNK = 256

_V7X_VMEM_BYTES = 64 * 1024 * 1024
_VMEM_LIMIT = _V7X_VMEM_BYTES - 8 * 1024 * 1024

_PROJ_ROWS = 512
_MLA_TILE = 512
_MLA_GROUP = 4
_MLA_TILES_PER_STEP = 4
_TAIL_BLOCKS = 4


def _params(sem):
    return pltpu.CompilerParams(dimension_semantics=sem, vmem_limit_bytes=_VMEM_LIMIT)


def _rms(v, g):
    return (v * lax.rsqrt(jnp.mean(v * v, axis=-1, keepdims=True) + EPS)) * g


def _dot(a, b):
    return jnp.dot(a, b, preferred_element_type=F32)


def _dot_nt(a, b):
    return lax.dot_general(a, b, (((1,), (1,)), ((), ())), preferred_element_type=F32)


def _load_weight_bf16(w_hbm, w_ref, stage_ref, sem_ref):
    rows = stage_ref.shape[1]
    n_chunks = w_ref.shape[0] // rows

    def chunk_copy(c):
        return pltpu.make_async_copy(w_hbm.at[pl.ds(c * rows, rows), :], stage_ref.at[c % 2],
                                     sem_ref.at[c % 2])

    chunk_copy(0).start()
    for c in range(n_chunks):
        if c + 1 < n_chunks:
            chunk_copy(c + 1).start()
        chunk_copy(c).wait()
        w_ref[c * rows:(c + 1) * rows, :] = stage_ref[c % 2].astype(BF16)


def _first_grid_step():
    return jnp.logical_and(pl.program_id(0) == 0, pl.program_id(1) == 0)


def _lane_tile(v, n):
    return jnp.concatenate([v] * n, axis=1)


def _chunk_of(pos):
    return jnp.right_shift(pos, CHUNK.bit_length() - 1)


def _rope128(v, cos_t, sin_t):
    return v * cos_t + pltpu.roll(v, MLA_ROPE // 2, 1) * sin_t


def _store_swa_kv(ref, kv):
    low = lax.broadcasted_iota(jnp.int32, kv.shape, 1) < SWA_HEAD_DIM
    swapped = pltpu.roll(kv, SWA_HEAD_DIM, 1)
    pieces = (jnp.where(low, kv, 0.0), jnp.where(low, 0.0, swapped),
              jnp.where(low, swapped, 0.0), jnp.where(low, 0.0, kv))
    for i, piece in enumerate(pieces):
        ref[0, :, i * 128:(i + 1) * 128] = piece.astype(BF16)


def _proj_kernel(x_ref, gin_ref, wt_hbm, gq_ref, gkv_ref, wuq_ref, wukv_ref, cos_ref, sin_ref,
                 q_ref, k_ref, v_ref, qs_ref, z_ref, ks_ref, vs_ref, wt_ref, stage_ref, sem_ref):
    @pl.when(_first_grid_step())
    def _():
        _load_weight_bf16(wt_hbm, wt_ref, stage_ref, sem_ref)

    h = _rms(x_ref[0], gin_ref[...]).astype(BF16)
    cos_t = cos_ref[...]
    sin_t = sin_ref[...]

    def proj(lo, hi):
        return _dot_nt(h, wt_ref[lo:hi, :])

    pa = proj(_R_CQ, _R_KPE)
    cq = _rms(pa[:, :MLA_QLORA], gq_ref[...]).astype(BF16)
    ckv = _rms(pa[:, MLA_QLORA:], gkv_ref[...]).astype(BF16)

    qs_ref[0, :, :SWA_WIDTH] = proj(_R_QSWA, _R_KSWA).astype(BF16)
    qs_ref[0, :, SWA_WIDTH:] = proj(_R_QMEM, _R_ZMEM).astype(BF16)
    z_ref[0, :, :MLA_WIDTH] = proj(_R_ZMLA, _R_QSWA).astype(BF16)
    z_ref[0, :, MLA_WIDTH:MLA_WIDTH + SWA_WIDTH] = proj(_R_ZSWA, _R_QMEM).astype(BF16)
    z_ref[0, :, MLA_WIDTH + SWA_WIDTH:] = proj(_R_ZMEM, IN_WIDTH).astype(BF16)
    kvs = proj(_R_KSWA, _R_ZSWA)
    _store_swa_kv(ks_ref, kvs[:, :SWA_KV_WIDTH])
    _store_swa_kv(vs_ref, kvs[:, SWA_KV_WIDTH:])
    pe = proj(_R_KPE, _R_ZMLA)
    kpe = _rope128(jnp.concatenate([pe, pe], axis=1), cos_t, sin_t).astype(BF16)

    qall = _dot(cq, wuq_ref[...]) * _MLA_Q_SCALE
    kv = _dot(ckv, wukv_ref[...])
    for hd in range(MLA_HEADS):
        oq = hd * (MLA_NOPE + MLA_ROPE)
        q_pe = qall[:, oq + MLA_NOPE:oq + MLA_NOPE + MLA_ROPE]
        q_ref[0, hd, :, :MLA_NOPE] = qall[:, oq:oq + MLA_NOPE].astype(BF16)
        q_ref[0, hd, :, MLA_NOPE:] = _rope128(
            jnp.concatenate([q_pe, q_pe], axis=1), cos_t, sin_t).astype(BF16)
        o = hd * (MLA_NOPE + MLA_V)
        k_ref[0, hd, :, :MLA_NOPE] = kv[:, o:o + MLA_NOPE].astype(BF16)
        k_ref[0, hd, :, MLA_NOPE:] = kpe
        v_ref[0, hd] = kv[:, o + MLA_NOPE:o + MLA_QK_PAD].astype(BF16)


def _proj(x, gin, wt, gq, gkv, wuq, wukv, cos_t, sin_t, tm):
    B, S, _ = x.shape
    const = lambda b, i: (0, 0)
    row = lambda b, i: (b, i, 0)
    single = pl.Buffered(1)

    def out(width):
        return jax.ShapeDtypeStruct((B, S, width), BF16)

    head_spec = pl.BlockSpec((1, MLA_HEADS, tm, MLA_QK_PAD), lambda b, i: (b, 0, i, 0))
    head_shape = jax.ShapeDtypeStruct((B, MLA_HEADS, S, MLA_QK_PAD), BF16)

    return pl.pallas_call(
        _proj_kernel,
        grid=(B, S // tm),
        in_specs=[
            pl.BlockSpec((1, tm, D_MODEL), row),
            pl.BlockSpec((1, D_MODEL), const),
            pl.BlockSpec(memory_space=pl.ANY),
            pl.BlockSpec((1, MLA_QLORA), const),
            pl.BlockSpec((1, MLA_KVLORA), const),
            pl.BlockSpec((MLA_QLORA, MLA_HEADS * (MLA_NOPE + MLA_ROPE)), const, pipeline_mode=single),
            pl.BlockSpec((MLA_KVLORA, MLA_HEADS * (MLA_NOPE + MLA_V)), const, pipeline_mode=single),
            pl.BlockSpec((tm, 128), lambda b, i: (i, 0)),
            pl.BlockSpec((tm, 128), lambda b, i: (i, 0)),
        ],
        out_specs=[
            head_spec, head_spec,
            pl.BlockSpec((1, MLA_HEADS, tm, MLA_V), lambda b, i: (b, 0, i, 0)),
            pl.BlockSpec((1, tm, SWA_WIDTH + MEM_WIDTH), row),
            pl.BlockSpec((1, tm, MIX_WIDTH), row),
            pl.BlockSpec((1, tm, _SWA_KV_PAD), row),
            pl.BlockSpec((1, tm, _SWA_KV_PAD), row),
        ],
        out_shape=[
            head_shape, head_shape, jax.ShapeDtypeStruct((B, MLA_HEADS, S, MLA_V), BF16),
            out(SWA_WIDTH + MEM_WIDTH), out(MIX_WIDTH), out(_SWA_KV_PAD), out(_SWA_KV_PAD),
        ],
        scratch_shapes=[
            pltpu.VMEM((IN_WIDTH, D_MODEL), BF16),
            pltpu.VMEM((2, _W_IN_CHUNK, D_MODEL), F32),
            pltpu.SemaphoreType.DMA((2,)),
        ],
        compiler_params=_params(("arbitrary", "arbitrary")),
        name="proj",
    )(x, gin, wt, gq, gkv, wuq, wukv, cos_t, sin_t)


def _mla_kernel(q_ref, k_ref, v_ref, o_ref, acc_ref, m_ref, s_ref, cm_ref, *, t, g, nq):
    qi0 = pl.program_id(2) * nq
    qi_end = qi0 + nq - 1
    acc_ref[...] = jnp.zeros(acc_ref.shape, F32)
    m_ref[...] = jnp.full(m_ref.shape, NEG, F32)
    ones = jnp.ones((t, MLA_V), BF16)
    row_halves = ((0, t // 2, t // 2), (t // 2, t // 2, t))

    def row_max(s):
        return jnp.broadcast_to(jnp.max(s, axis=-1, keepdims=True), (s.shape[0], 128))

    def diagonal_mask():
        q_chunk = _chunk_of(lax.broadcasted_iota(jnp.int32, (t, t), 0))
        k_chunk = _chunk_of(lax.broadcasted_iota(jnp.int32, (t, t), 1))
        return k_chunk <= q_chunk

    def scores(qi, c, slot, hd, mask=None):
        rows = pl.multiple_of((qi - qi0) * t, t)
        keys = pl.multiple_of(c * t, t)
        for r0, nr, nk in (row_halves if mask is not None else ((0, t, t),)):
            s = _dot_nt(q_ref[0, hd, pl.ds(pl.multiple_of(rows + r0, nr), nr), :],
                        k_ref[0, hd, pl.ds(keys, nk), :])
            if mask is not None:
                s = jnp.where(mask[r0:r0 + nr, :nk], s, NEG)
            s_ref[slot, hd, r0:r0 + nr, :nk] = s
            cm_ref[slot, hd, r0:r0 + nr] = row_max(s)

    def update(c, slot, hd, diagonal, mask):
        keys = pl.multiple_of(c * t, t)
        for r0, nr, nk in (row_halves if diagonal else ((0, t, t),)):
            rows = slice(r0, r0 + nr)
            s = s_ref[slot, hd, rows, :nk]
            if mask is not None:
                s = jnp.where(mask[rows, :nk], s, NEG)
                m_cur = row_max(s)
            else:
                m_cur = cm_ref[slot, hd, rows]
            m_old = m_ref[hd, rows]
            m_new = jnp.maximum(m_old, m_cur)
            alpha = jnp.exp2(m_old - m_new)
            p = jnp.exp2(s - _lane_tile(m_new, nk // 128)).astype(BF16)
            v_ones = jnp.concatenate([v_ref[0, hd, pl.ds(keys, nk), :], ones[:nk]], axis=1)
            acc_ref[hd, rows] = (_lane_tile(alpha, MLA_QK_PAD // 128) * acc_ref[hd, rows]
                                 + _dot(p, v_ones))
            m_ref[hd, rows] = m_new

    def finish_tile(qi, hd):
        rows = pl.multiple_of((qi - qi0) * t, t)
        acc = acc_ref[hd]
        o_ref[0, pl.ds(rows, t), hd * MLA_V:(hd + 1) * MLA_V] = (
            acc[:, :MLA_V] / acc[:, MLA_V:]).astype(BF16)
        acc_ref[hd] = jnp.zeros(acc.shape, F32)
        m_ref[hd] = jnp.full((t, 128), NEG, F32)

    def following(qi, c):
        is_last = c == qi
        return jnp.where(is_last, jnp.minimum(qi + 1, qi_end), qi), jnp.where(is_last, 0, c + 1)

    def step(qi, c, slot, diagonal, premasked=False, mask_next=False):
        qn, cn = following(qi, c)
        mask = diagonal_mask() if diagonal and not premasked else None
        next_mask = diagonal_mask() if mask_next else None
        for hd in range(g):
            scores(qn, cn, 1 - slot, hd, next_mask)
            update(c, slot, hd, diagonal, mask)
            if diagonal:
                finish_tile(qi, hd)

    for hd in range(g):
        scores(qi0, 0, 0, hd)

    def body(_, pair):
        first, second = pair, following(*pair)
        diag0 = first[1] == first[0]
        diag1 = second[1] == second[0]
        for d0, d1, cond in ((True, False, diag0), (False, True, diag1),
                             (False, False, jnp.logical_not(jnp.logical_or(diag0, diag1)))):
            @pl.when(cond)
            def _(d0=d0, d1=d1):
                step(*first, 0, d0, mask_next=d1)
                step(*second, 1, d1, premasked=d1)
        return following(*second)

    n_steps = nq * qi0 + nq * (nq + 1) // 2
    lax.fori_loop(0, n_steps // 2, body, (qi0, 0))


def _mla(q, k, v, t, g, nq):
    B, _, S, _ = q.shape
    assert S % (nq * t) == 0 and nq % 2 == 0 and (nq * (nq + 1) // 2) % 2 == 0
    return pl.pallas_call(
        functools.partial(_mla_kernel, t=t, g=g, nq=nq),
        grid=(B, MLA_HEADS // g, S // (nq * t)),
        in_specs=[
            pl.BlockSpec((1, g, nq * t, MLA_QK_PAD), lambda b, h, i: (b, h, i, 0)),
            pl.BlockSpec((1, g, S, MLA_QK_PAD), lambda b, h, i: (b, h, 0, 0)),
            pl.BlockSpec((1, g, S, MLA_V), lambda b, h, i: (b, h, 0, 0)),
        ],
        out_specs=pl.BlockSpec((1, nq * t, g * MLA_V), lambda b, h, i: (b, i, h)),
        out_shape=jax.ShapeDtypeStruct((B, S, MLA_WIDTH), BF16),
        scratch_shapes=[
            pltpu.VMEM((g, t, MLA_QK_PAD), F32),
            pltpu.VMEM((g, t, 128), F32),
            pltpu.VMEM((2, g, t, t), F32),
            pltpu.VMEM((2, g, t, 128), F32),
        ],
        compiler_params=_params(("arbitrary", "arbitrary", "arbitrary")),
        name="mla",
    )(q, k, v)


_REL_SPAN = 4 * SWA_BLOCK
_N_OFFSETS = 3 * SWA_BLOCK - 1


def _bias_memkv_kernel(table_ref, bucket_ref, mem_hbm, g_ref, w_hbm, o_ref, k_ref, v_ref,
                       spread_ref, mem_ref, w_ref, sem_ref):
    fetches = [pltpu.make_async_copy(mem_hbm, mem_ref, sem_ref.at[0]),
               pltpu.make_async_copy(w_hbm, w_ref, sem_ref.at[1])]
    for fetch in fetches:
        fetch.start()

    for b in range(N_BUCKETS):
        spread_ref[b] = jnp.concatenate(
            [jnp.full((1, 128), table_ref[b, hd], F32) for hd in range(SWA_HEADS)], axis=0)
    lane = lax.broadcasted_iota(jnp.int32, (SWA_HEADS, _REL_SPAN), 1)

    def lookup(c, per_offset):
        col = spread_ref[bucket_ref[c]]
        return jnp.where(lane == c, _lane_tile(col, _REL_SPAN // 128), per_offset)

    per_offset_all = lax.fori_loop(0, _N_OFFSETS, lookup, jnp.zeros((SWA_HEADS, _REL_SPAN), F32))
    for hd in range(SWA_HEADS):
        per_offset = per_offset_all[hd:hd + 1, :]
        rows = jnp.broadcast_to(per_offset, (SWA_BLOCK, _REL_SPAN))
        band = pltpu.roll(rows, _REL_SPAN - (SWA_BLOCK - 1), 1, stride=1, stride_axis=0)
        kvh, within = divmod(hd, SWA_GROUP)
        par, half = within % 2, within // 2
        o_ref[kvh, par, half * SWA_BLOCK:(half + 1) * SWA_BLOCK, :] = band[:, :2 * SWA_BLOCK]

    for fetch in fetches:
        fetch.wait()
    w = w_ref[...].astype(BF16)
    for b in range(mem_ref.shape[0]):
        kv = _dot(_rms(mem_ref[b], g_ref[...]).astype(BF16), w)
        k_ref[b] = kv[:, :MEM_WIDTH].astype(BF16)
        v_ref[b] = kv[:, MEM_WIDTH:].astype(BF16)


def _bias_memkv(rel_table, bucket, mem, g, w):
    B, M, _ = mem.shape
    smem = pl.BlockSpec(memory_space=pltpu.SMEM)
    vmem = pl.BlockSpec(memory_space=pltpu.VMEM)
    hbm = pl.BlockSpec(memory_space=pl.ANY)
    return pl.pallas_call(
        _bias_memkv_kernel,
        in_specs=[smem, smem, hbm, vmem, hbm],
        out_specs=[vmem] * 3,
        out_shape=[jax.ShapeDtypeStruct((SWA_KV_HEADS, 2, 2 * SWA_BLOCK, 2 * SWA_BLOCK), F32),
                   jax.ShapeDtypeStruct((B, M, MEM_WIDTH), BF16),
                   jax.ShapeDtypeStruct((B, M, MEM_WIDTH), BF16)],
        scratch_shapes=[
            pltpu.VMEM((N_BUCKETS, SWA_HEADS, 128), F32),
            pltpu.VMEM(mem.shape, F32),
            pltpu.VMEM(w.shape, F32),
            pltpu.SemaphoreType.DMA((2,)),
        ],
        compiler_params=pltpu.CompilerParams(vmem_limit_bytes=_VMEM_LIMIT),
        name="t5bias_memkv",
    )(rel_table, bucket, mem, g, w)


def _swa_into(y_ref, sink_ref, q_ref, kp_ref, kc_ref, vp_ref, vc_ref, bias_ref, kb_ref, vb_ref, nsub,
              beside_unit):
    t = pl.program_id(1)
    scale = SWA_HEAD_DIM ** -0.5
    sb = SWA_BLOCK
    kb_ref[:sb] = kp_ref[0]
    kb_ref[sb:] = kc_ref[0]
    vb_ref[:sb] = vp_ref[0]
    vb_ref[sb:] = vc_ref[0]

    row = lax.broadcasted_iota(jnp.int32, (2 * sb, 2 * sb), 0)
    q_chunk = _chunk_of(row & (sb - 1))
    b_chunk = _chunk_of(lax.broadcasted_iota(jnp.int32, (2 * sb, 2 * sb), 1))
    valid_any = jnp.logical_and(b_chunk >= q_chunk, b_chunk <= q_chunk + WINDOW_CHUNKS)
    first_lo = jnp.where(t > 0, 0, 2)
    valid_first = jnp.logical_and(valid_any, b_chunk >= first_lo)
    upper = lax.broadcasted_iota(jnp.int32, (2 * sb, 1), 0) < sb

    def geometry(u):
        r, kvh = divmod(u, SWA_KV_HEADS)
        rows = slice(r * sb, (r + 1) * sb)
        band = slice(r * sb, (r + 2) * sb)
        pair0 = slice(2 * kvh * 128, (2 * kvh + 1) * 128)
        pair1 = slice((2 * kvh + 1) * 128, (2 * kvh + 2) * 128)
        return r, kvh, rows, band, pair0, pair1

    def scores(u):
        r, kvh, rows, band, pair0, pair1 = geometry(u)
        q = jnp.concatenate([q_ref[0, rows, pair0], q_ref[0, rows, pair1]], axis=0)
        return [_dot_nt(q, kb_ref[band, (2 * kvh + par) * 128:(2 * kvh + par + 1) * 128])
                for par in range(2)]

    def attend(u, unit_scores):
        r, kvh, rows, band, pair0, pair1 = geometry(u)
        valid = valid_first if r == 0 else valid_any
        o = None
        for par, s in enumerate(unit_scores):
            kcols = slice((2 * kvh + par) * 128, (2 * kvh + par + 1) * 128)
            hd = SWA_GROUP * kvh + par
            s = jnp.where(valid, s * scale + bias_ref[kvh, par], NEG)
            sink = jnp.where(upper, sink_ref[hd], sink_ref[hd + 2])
            m = jnp.maximum(jnp.max(s, axis=-1, keepdims=True), sink)
            p = jnp.exp(s - m)
            den = jnp.sum(p, axis=-1, keepdims=True) + jnp.exp(sink - m)
            pv = _dot((p * (1.0 / den)).astype(BF16), vb_ref[band, kcols])
            o = pv if o is None else o + pv
        y_ref[rows, pair0] = o[:sb]
        y_ref[rows, pair1] = o[sb:]

    n_units = nsub * SWA_KV_HEADS
    pending = scores(0)
    for u in range(n_units):
        upcoming = scores(u + 1) if u + 1 < n_units else None
        beside_unit(u)
        attend(u, pending)
        pending = upcoming


def _mem_into(y_ref, q_ref, k_ref, v_ref, after_head):
    scale = MEM_HEAD_DIM ** -0.5

    def head_cols(hd):
        return slice(hd * MEM_HEAD_DIM, (hd + 1) * MEM_HEAD_DIM)

    def scores(hd):
        return _dot_nt(q_ref[0, :, head_cols(hd)], k_ref[0, :, head_cols(hd)])

    pending = scores(0)
    for hd in range(MEM_HEADS):
        upcoming = scores(hd + 1) if hd + 1 < MEM_HEADS else None
        s = pending * scale
        m = jnp.max(s, axis=-1, keepdims=True)
        p = jnp.exp(s - m)
        inv = 1.0 / jnp.sum(p, axis=-1, keepdims=True)
        y_ref[:, head_cols(hd)] = _dot((p * inv).astype(BF16), v_ref[0, :, head_cols(hd)])
        after_head(hd)
        pending = upcoming


def _tail_kernel(sink_ref, x_ref, ya_ref, za_ref, zb_ref, zc_ref, qs_ref, qm_ref,
                 kp_ref, kc_ref, vp_ref, vc_ref, bias_ref, km_ref, vm_ref, w_hbm, g_ref,
                 o_ref, kb_ref, vb_ref, yb_ref, yc_ref, yacc_ref, w_ref, stage_ref, sem_ref, *, nsub):
    @pl.when(_first_grid_step())
    def _():
        _load_weight_bf16(w_hbm, w_ref, stage_ref, sem_ref)

    def gated(y, z_ref):
        h = 0.5 * z_ref[0].astype(F32)
        return (y * (h + h * jnp.tanh(h))).astype(BF16)

    swa_lo, mem_lo = MLA_WIDTH, MLA_WIDTH + SWA_WIDTH

    g_mla = gated(ya_ref[0].astype(F32), za_ref)
    mla_chunk = D_MODEL // (nsub * SWA_KV_HEADS)

    def project_mla(u):
        cols = slice(u * mla_chunk, (u + 1) * mla_chunk)
        yacc_ref[:, cols] = _dot(g_mla, w_ref[:swa_lo, cols])

    _swa_into(yb_ref, sink_ref, qs_ref, kp_ref, kc_ref, vp_ref, vc_ref, bias_ref, kb_ref, vb_ref, nsub,
              project_mla)

    g_swa = gated(yb_ref[...], zb_ref)
    swa_chunk = D_MODEL // MEM_HEADS

    def project_swa(u):
        cols = slice(u * swa_chunk, (u + 1) * swa_chunk)
        yacc_ref[:, cols] += _dot(g_swa, w_ref[swa_lo:mem_lo, cols])

    _mem_into(yc_ref, qm_ref, km_ref, vm_ref, project_swa)

    g_mem = gated(yc_ref[...], zc_ref)
    sum_sq = jnp.zeros((g_mem.shape[0], 1), F32)
    chunks = [slice(j * swa_chunk, (j + 1) * swa_chunk) for j in range(D_MODEL // swa_chunk)]
    for cols in chunks:
        r = x_ref[0, :, cols] + yacc_ref[:, cols] + _dot(g_mem, w_ref[mem_lo:, cols])
        o_ref[0, :, cols] = r
        sum_sq = sum_sq + jnp.sum(r * r, axis=-1, keepdims=True)
    inv = lax.rsqrt(sum_sq * (1.0 / D_MODEL) + EPS)
    for cols in chunks:
        o_ref[0, :, cols] = (o_ref[0, :, cols] * inv) * g_ref[:, cols]


def _tail(sinks, x, y_mla, z, qs, ks, vs, bias, kmem, vmem, w, g, nsub):
    B, S, _ = x.shape
    tm = nsub * SWA_BLOCK
    M = kmem.shape[1]
    row = lambda b, i: (b, i, 0)
    prev = lambda b, i: (b, jnp.maximum(i * nsub - 1, 0), 0)
    col = lambda c: (lambda b, i: (b, i, c))
    return pl.pallas_call(
        functools.partial(_tail_kernel, nsub=nsub),
        grid=(B, S // tm),
        in_specs=[
            pl.BlockSpec(memory_space=pltpu.SMEM),
            pl.BlockSpec((1, tm, D_MODEL), row),
            pl.BlockSpec((1, tm, MLA_WIDTH), row),
            pl.BlockSpec((1, tm, MLA_WIDTH), col(0)),
            pl.BlockSpec((1, tm, SWA_WIDTH), col(MLA_WIDTH // SWA_WIDTH)),
            pl.BlockSpec((1, tm, MEM_WIDTH), col((MLA_WIDTH + SWA_WIDTH) // MEM_WIDTH)),
            pl.BlockSpec((1, tm, SWA_WIDTH), col(0)),
            pl.BlockSpec((1, tm, MEM_WIDTH), col(1)),
            pl.BlockSpec((1, SWA_BLOCK, _SWA_KV_PAD), prev),
            pl.BlockSpec((1, tm, _SWA_KV_PAD), row),
            pl.BlockSpec((1, SWA_BLOCK, _SWA_KV_PAD), prev),
            pl.BlockSpec((1, tm, _SWA_KV_PAD), row),
            pl.BlockSpec((SWA_KV_HEADS, 2, 2 * SWA_BLOCK, 2 * SWA_BLOCK), lambda b, i: (0, 0, 0, 0)),
            pl.BlockSpec((1, M, MEM_WIDTH), lambda b, i: (b, 0, 0)),
            pl.BlockSpec((1, M, MEM_WIDTH), lambda b, i: (b, 0, 0)),
            pl.BlockSpec(memory_space=pl.ANY),
            pl.BlockSpec((1, D_MODEL), lambda b, i: (0, 0)),
        ],
        out_specs=pl.BlockSpec((1, tm, D_MODEL), row),
        out_shape=jax.ShapeDtypeStruct((B, S, D_MODEL), F32),
        scratch_shapes=[
            pltpu.VMEM((tm + SWA_BLOCK, _SWA_KV_PAD), BF16),
            pltpu.VMEM((tm + SWA_BLOCK, _SWA_KV_PAD), BF16),
            pltpu.VMEM((tm, SWA_WIDTH), F32),
            pltpu.VMEM((tm, MEM_WIDTH), F32),
            pltpu.VMEM((tm, D_MODEL), F32),
            pltpu.VMEM((MIX_WIDTH, D_MODEL), BF16),
            pltpu.VMEM((2, _W_OUT_CHUNK, D_MODEL), F32),
            pltpu.SemaphoreType.DMA((2,)),
        ],
        compiler_params=_params(("arbitrary", "arbitrary")),
        name="tail",
    )(sinks, x, y_mla, z, z, z, qs, qs, ks, ks, vs, vs, bias, kmem, vmem, w, g)


def _rope_tables(seq):
    inv = 1.0 / (ROPE_THETA ** (jnp.arange(0, MLA_ROPE, 2, dtype=F32) / MLA_ROPE))
    step = 1 << ((seq - 1).bit_length() // 2)
    assert seq % step == 0
    fine = jnp.arange(step, dtype=F32)[:, None] * inv[None, :]
    coarse = (jnp.arange(seq // step, dtype=F32) * step)[:, None] * inv[None, :]
    cf, sf = jnp.cos(fine)[None], jnp.sin(fine)[None]
    cc, sc = jnp.cos(coarse)[:, None], jnp.sin(coarse)[:, None]
    cos = (cc * cf - sc * sf).reshape(seq, -1)
    sin = (sc * cf + cc * sf).reshape(seq, -1)
    zero = jnp.zeros_like(cos)
    return (jnp.concatenate([cos, cos, zero, zero], axis=-1),
            jnp.concatenate([-sin, sin, zero, zero], axis=-1))


def _t5_bucket(rel):
    nb = N_BUCKETS // 2
    max_exact = nb // 2
    bucket = jnp.where(rel > 0, nb, 0)
    n = jnp.abs(rel)
    nf = jnp.maximum(n, 1).astype(F32)
    large = max_exact + (jnp.log(nf / max_exact) / math.log(MAX_DISTANCE / max_exact)
                         * (nb - max_exact)).astype(jnp.int32)
    large = jnp.minimum(large, nb - 1)
    return bucket + jnp.where(n < max_exact, n, large)


def kernel(x, mem, norm_in, w_in, norm_q, norm_kv, w_uq, w_ukv, attn_sinks, rel_bias,
           norm_mem, w_mem_kv, w_out, norm_final):
    B, S, _ = x.shape
    assert norm_in.shape[0] == 1, "single-layer trunk"
    assert x.shape[2] == D_MODEL and w_in.shape[1:] == (D_MODEL, IN_WIDTH)
    assert S % _PROJ_ROWS == 0 and S % (_TAIL_BLOCKS * SWA_BLOCK) == 0

    wt = w_in[0].T
    cos_t, sin_t = _rope_tables(S)
    q, k, v, qs, z, ks, vs = _proj(x, norm_in, wt, norm_q, norm_kv, w_uq[0].astype(BF16),
                                   w_ukv[0].astype(BF16), cos_t, sin_t, tm=_PROJ_ROWS)

    y_mla = _mla(q, k, v, t=_MLA_TILE, g=_MLA_GROUP, nq=_MLA_TILES_PER_STEP)

    bucket = _t5_bucket(jnp.arange(_REL_SPAN) - (2 * SWA_BLOCK - 1)).astype(jnp.int32)
    bias, kmem, vmem = _bias_memkv(rel_bias, bucket, mem, norm_mem, w_mem_kv[0])
    return _tail(attn_sinks[0], x, y_mla, z, qs, ks, vs, bias, kmem, vmem,
                 w_out[0], norm_final[None, :], nsub=_TAIL_BLOCKS)
```

```python
import functools
import math

import jax
import jax.numpy as jnp
from jax import lax
from jax.experimental import pallas as pl
from jax.experimental.pallas import tpu as pltpu

D_MODEL = 2048
CHUNK = 64
EPS = 1e-6
NEG = -1e30

MLA_HEADS = 8
MLA_NOPE = 128
MLA_ROPE = 64
MLA_V = 128
MLA_QLORA = 512
MLA_KVLORA = 256
MLA_WIDTH = MLA_HEADS * MLA_V
MLA_QK_PAD = 256
ROPE_THETA = 10000.0

SWA_HEADS = 8
SWA_KV_HEADS = 2
SWA_GROUP = SWA_HEADS // SWA_KV_HEADS
SWA_HEAD_DIM = 64
SWA_WIDTH = SWA_HEADS * SWA_HEAD_DIM
SWA_KV_WIDTH = SWA_KV_HEADS * SWA_HEAD_DIM
WINDOW_CHUNKS = 2
SWA_BLOCK = 128

MEM_HEADS = 4
MEM_HEAD_DIM = 128
MEM_WIDTH = MEM_HEADS * MEM_HEAD_DIM

MIX_WIDTH = MLA_WIDTH + SWA_WIDTH + MEM_WIDTH

N_BUCKETS = 32
MAX_DISTANCE = 128

BF16 = jnp.bfloat16
F32 = jnp.float32

_R_CQ = 0
_R_CKV = _R_CQ + MLA_QLORA
_R_KPE = _R_CKV + MLA_KVLORA
_R_ZMLA = _R_KPE + MLA_ROPE
_R_QSWA = _R_ZMLA + MLA_WIDTH
_R_KSWA = _R_QSWA + SWA_WIDTH
_R_VSWA = _R_KSWA + SWA_KV_WIDTH
_R_ZSWA = _R_VSWA + SWA_KV_WIDTH
_R_QMEM = _R_ZSWA + SWA_WIDTH
_R_ZMEM = _R_QMEM + MEM_WIDTH
IN_WIDTH = _R_ZMEM + MEM_WIDTH

_MLA_Q_SCALE = (MLA_NOPE + MLA_ROPE) ** -0.5 * math.log2(math.e)

_SWA_KV_PAD = 2 * SWA_KV_HEADS * 128

_W_IN_CHUNK = 208
_W_OUT_CHUNK = 256
_STAGE_SLOTS = 4

_V7X_VMEM_BYTES = 64 * 1024 * 1024
_VMEM_LIMIT = _V7X_VMEM_BYTES - 8 * 1024 * 1024

_PROJ_ROWS = 512
_MLA_TILE = 512
_MLA_GROUP = 4
_MLA_TILES_PER_STEP = 4
_TAIL_BLOCKS = 4


def _params(sem):
    return pltpu.CompilerParams(dimension_semantics=sem, vmem_limit_bytes=_VMEM_LIMIT)


def _rms(v, g):
    return (v * lax.rsqrt(jnp.mean(v * v, axis=-1, keepdims=True) + EPS)) * g


def _dot(a, b):
    return jnp.dot(a, b, preferred_element_type=F32)


def _dot_nt(a, b):
    return lax.dot_general(a, b, (((1,), (1,)), ((), ())), preferred_element_type=F32)


def _load_weight_bf16(w_hbm, w_ref, stage_ref, sem_ref):
    slots, rows, _ = stage_ref.shape
    n_chunks = w_ref.shape[0] // rows
    ahead = slots - 1

    def chunk_copy(c):
        return pltpu.make_async_copy(w_hbm.at[pl.ds(c * rows, rows), :], stage_ref.at[c % slots],
                                     sem_ref.at[c % slots])

    for c in range(min(ahead, n_chunks)):
        chunk_copy(c).start()
    for c in range(n_chunks):
        if c + ahead < n_chunks:
            chunk_copy(c + ahead).start()
        chunk_copy(c).wait()
        w_ref[c * rows:(c + 1) * rows, :] = stage_ref[c % slots].astype(BF16)


def _first_grid_step():
    return jnp.logical_and(pl.program_id(0) == 0, pl.program_id(1) == 0)


def _lane_tile(v, n):
    return jnp.concatenate([v] * n, axis=1)


def _chunk_of(pos):
    return jnp.right_shift(pos, CHUNK.bit_length() - 1)


def _rope128(v, cos_t, sin_t):
    return v * cos_t + pltpu.roll(v, MLA_ROPE // 2, 1) * sin_t


def _store_swa_kv(ref, kv):
    low = lax.broadcasted_iota(jnp.int32, kv.shape, 1) < SWA_HEAD_DIM
    swapped = pltpu.roll(kv, SWA_HEAD_DIM, 1)
    pieces = (jnp.where(low, kv, 0.0), jnp.where(low, 0.0, swapped),
              jnp.where(low, swapped, 0.0), jnp.where(low, 0.0, kv))
    for i, piece in enumerate(pieces):
        ref[0, :, i * 128:(i + 1) * 128] = piece.astype(BF16)


def _proj_kernel(x_ref, gin_ref, wt_hbm, gq_ref, gkv_ref, wuq_ref, wukv_ref, cos_ref, sin_ref,
                 q_ref, k_ref, v_ref, qs_ref, z_ref, ks_ref, vs_ref, wt_ref, stage_ref, sem_ref):
    @pl.when(_first_grid_step())
    def _():
        _load_weight_bf16(wt_hbm, wt_ref, stage_ref, sem_ref)

    h = _rms(x_ref[0], gin_ref[...]).astype(BF16)
    cos_t = cos_ref[...]
    sin_t = sin_ref[...]

    def proj(lo, hi):
        return _dot_nt(h, wt_ref[lo:hi, :])

    pa = proj(_R_CQ, _R_KPE)
    cq = _rms(pa[:, :MLA_QLORA], gq_ref[...]).astype(BF16)
    ckv = _rms(pa[:, MLA_QLORA:], gkv_ref[...]).astype(BF16)

    qs_ref[0, :, :SWA_WIDTH] = proj(_R_QSWA, _R_KSWA).astype(BF16)
    qs_ref[0, :, SWA_WIDTH:] = proj(_R_QMEM, _R_ZMEM).astype(BF16)
    z_ref[0, :, :MLA_WIDTH] = proj(_R_ZMLA, _R_QSWA).astype(BF16)
    z_ref[0, :, MLA_WIDTH:MLA_WIDTH + SWA_WIDTH] = proj(_R_ZSWA, _R_QMEM).astype(BF16)
    z_ref[0, :, MLA_WIDTH + SWA_WIDTH:] = proj(_R_ZMEM, IN_WIDTH).astype(BF16)
    kvs = proj(_R_KSWA, _R_ZSWA)
    _store_swa_kv(ks_ref, kvs[:, :SWA_KV_WIDTH])
    _store_swa_kv(vs_ref, kvs[:, SWA_KV_WIDTH:])
    pe = proj(_R_KPE, _R_ZMLA)
    kpe = _rope128(jnp.concatenate([pe, pe], axis=1), cos_t, sin_t).astype(BF16)

    qall = _dot(cq, wuq_ref[...]) * _MLA_Q_SCALE
    kv = _dot(ckv, wukv_ref[...])
    for hd in range(MLA_HEADS):
        oq = hd * (MLA_NOPE + MLA_ROPE)
        q_pe = qall[:, oq + MLA_NOPE:oq + MLA_NOPE + MLA_ROPE]
        q_ref[0, hd, :, :MLA_NOPE] = qall[:, oq:oq + MLA_NOPE].astype(BF16)
        q_ref[0, hd, :, MLA_NOPE:] = _rope128(
            jnp.concatenate([q_pe, q_pe], axis=1), cos_t, sin_t).astype(BF16)
        o = hd * (MLA_NOPE + MLA_V)
        k_ref[0, hd, :, :MLA_NOPE] = kv[:, o:o + MLA_NOPE].astype(BF16)
        k_ref[0, hd, :, MLA_NOPE:] = kpe
        v_ref[0, hd] = kv[:, o + MLA_NOPE:o + MLA_QK_PAD].astype(BF16)


def _proj(x, gin, wt, gq, gkv, wuq, wukv, cos_t, sin_t, tm):
    B, S, _ = x.shape
    const = lambda b, i: (0, 0)
    row = lambda b, i: (b, i, 0)
    single = pl.Buffered(1)

    def out(width):
        return jax.ShapeDtypeStruct((B, S, width), BF16)

    head_spec = pl.BlockSpec((1, MLA_HEADS, tm, MLA_QK_PAD), lambda b, i: (b, 0, i, 0))
    head_shape = jax.ShapeDtypeStruct((B, MLA_HEADS, S, MLA_QK_PAD), BF16)

    return pl.pallas_call(
        _proj_kernel,
        grid=(B, S // tm),
        in_specs=[
            pl.BlockSpec((1, tm, D_MODEL), row),
            pl.BlockSpec((1, D_MODEL), const),
            pl.BlockSpec(memory_space=pl.ANY),
            pl.BlockSpec((1, MLA_QLORA), const),
            pl.BlockSpec((1, MLA_KVLORA), const),
            pl.BlockSpec((MLA_QLORA, MLA_HEADS * (MLA_NOPE + MLA_ROPE)), const, pipeline_mode=single),
            pl.BlockSpec((MLA_KVLORA, MLA_HEADS * (MLA_NOPE + MLA_V)), const, pipeline_mode=single),
            pl.BlockSpec((tm, 128), lambda b, i: (i, 0)),
            pl.BlockSpec((tm, 128), lambda b, i: (i, 0)),
        ],
        out_specs=[
            head_spec, head_spec,
            pl.BlockSpec((1, MLA_HEADS, tm, MLA_V), lambda b, i: (b, 0, i, 0)),
            pl.BlockSpec((1, tm, SWA_WIDTH + MEM_WIDTH), row),
            pl.BlockSpec((1, tm, MIX_WIDTH), row),
            pl.BlockSpec((1, tm, _SWA_KV_PAD), row),
            pl.BlockSpec((1, tm, _SWA_KV_PAD), row),
        ],
        out_shape=[
            head_shape, head_shape, jax.ShapeDtypeStruct((B, MLA_HEADS, S, MLA_V), BF16),
            out(SWA_WIDTH + MEM_WIDTH), out(MIX_WIDTH), out(_SWA_KV_PAD), out(_SWA_KV_PAD),
        ],
        scratch_shapes=[
            pltpu.VMEM((IN_WIDTH, D_MODEL), BF16),
            pltpu.VMEM((_STAGE_SLOTS, _W_IN_CHUNK, D_MODEL), F32),
            pltpu.SemaphoreType.DMA((_STAGE_SLOTS,)),
        ],
        compiler_params=_params(("arbitrary", "arbitrary")),
        name="proj",
    )(x, gin, wt, gq, gkv, wuq, wukv, cos_t, sin_t)


def _mla_kernel(q_ref, k_ref, v_ref, o_ref, acc_ref, m_ref, s_ref, cm_ref, *, t, g, nq):
    qi0 = pl.program_id(2) * nq
    qi_end = qi0 + nq - 1
    acc_ref[...] = jnp.zeros(acc_ref.shape, F32)
    m_ref[...] = jnp.full(m_ref.shape, NEG, F32)
    ones = jnp.ones((t, MLA_V), BF16)
    row_halves = ((0, t // 2, t // 2), (t // 2, t // 2, t))

    def row_max(s):
        return jnp.broadcast_to(jnp.max(s, axis=-1, keepdims=True), (s.shape[0], 128))

    def diagonal_mask():
        q_chunk = _chunk_of(lax.broadcasted_iota(jnp.int32, (t, t), 0))
        k_chunk = _chunk_of(lax.broadcasted_iota(jnp.int32, (t, t), 1))
        return k_chunk <= q_chunk

    def scores(qi, c, slot, hd, mask=None):
        rows = pl.multiple_of((qi - qi0) * t, t)
        keys = pl.multiple_of(c * t, t)
        for r0, nr, nk in (row_halves if mask is not None else ((0, t, t),)):
            s = _dot_nt(q_ref[0, hd, pl.ds(pl.multiple_of(rows + r0, nr), nr), :],
                        k_ref[0, hd, pl.ds(keys, nk), :])
            if mask is not None:
                s = jnp.where(mask[r0:r0 + nr, :nk], s, NEG)
            s_ref[slot, hd, r0:r0 + nr, :nk] = s
            cm_ref[slot, hd, r0:r0 + nr] = row_max(s)

    def update(c, slot, hd, diagonal, mask):
        keys = pl.multiple_of(c * t, t)
        for r0, nr, nk in (row_halves if diagonal else ((0, t, t),)):
            rows = slice(r0, r0 + nr)
            s = s_ref[slot, hd, rows, :nk]
            if mask is not None:
                s = jnp.where(mask[rows, :nk], s, NEG)
                m_cur = row_max(s)
            else:
                m_cur = cm_ref[slot, hd, rows]
            m_old = m_ref[hd, rows]
            m_new = jnp.maximum(m_old, m_cur)
            alpha = jnp.exp2(m_old - m_new)
            p = jnp.exp2(s - _lane_tile(m_new, nk // 128)).astype(BF16)
            v_ones = jnp.concatenate([v_ref[0, hd, pl.ds(keys, nk), :], ones[:nk]], axis=1)
            acc_ref[hd, rows] = (_lane_tile(alpha, MLA_QK_PAD // 128) * acc_ref[hd, rows]
                                 + _dot(p, v_ones))
            m_ref[hd, rows] = m_new

    def finish_tile(qi, hd):
        rows = pl.multiple_of((qi - qi0) * t, t)
        acc = acc_ref[hd]
        o_ref[0, pl.ds(rows, t), hd * MLA_V:(hd + 1) * MLA_V] = (
            acc[:, :MLA_V] / acc[:, MLA_V:]).astype(BF16)
        acc_ref[hd] = jnp.zeros(acc.shape, F32)
        m_ref[hd] = jnp.full((t, 128), NEG, F32)

    def following(qi, c):
        is_last = c == qi
        return jnp.where(is_last, jnp.minimum(qi + 1, qi_end), qi), jnp.where(is_last, 0, c + 1)

    def step(qi, c, slot, diagonal, premasked=False, mask_next=False):
        qn, cn = following(qi, c)
        mask = diagonal_mask() if diagonal and not premasked else None
        next_mask = diagonal_mask() if mask_next else None
        for hd in range(g):
            scores(qn, cn, 1 - slot, hd, next_mask)
            update(c, slot, hd, diagonal, mask)
            if diagonal:
                finish_tile(qi, hd)

    for hd in range(g):
        scores(qi0, 0, 0, hd)

    def body(_, pair):
        first, second = pair, following(*pair)
        diag0 = first[1] == first[0]
        diag1 = second[1] == second[0]
        for d0, d1, cond in ((True, False, diag0), (False, True, diag1),
                             (False, False, jnp.logical_not(jnp.logical_or(diag0, diag1)))):
            @pl.when(cond)
            def _(d0=d0, d1=d1):
                step(*first, 0, d0, mask_next=d1)
                step(*second, 1, d1, premasked=d1)
        return following(*second)

    n_steps = nq * qi0 + nq * (nq + 1) // 2
    lax.fori_loop(0, n_steps // 2, body, (qi0, 0))


def _mla(q, k, v, t, g, nq):
    B, _, S, _ = q.shape
    assert S % (nq * t) == 0 and nq % 2 == 0 and (nq * (nq + 1) // 2) % 2 == 0
    return pl.pallas_call(
        functools.partial(_mla_kernel, t=t, g=g, nq=nq),
        grid=(B, MLA_HEADS // g, S // (nq * t)),
        in_specs=[
            pl.BlockSpec((1, g, nq * t, MLA_QK_PAD), lambda b, h, i: (b, h, i, 0)),
            pl.BlockSpec((1, g, S, MLA_QK_PAD), lambda b, h, i: (b, h, 0, 0)),
            pl.BlockSpec((1, g, S, MLA_V), lambda b, h, i: (b, h, 0, 0)),
        ],
        out_specs=pl.BlockSpec((1, nq * t, g * MLA_V), lambda b, h, i: (b, i, h)),
        out_shape=jax.ShapeDtypeStruct((B, S, MLA_WIDTH), BF16),
        scratch_shapes=[
            pltpu.VMEM((g, t, MLA_QK_PAD), F32),
            pltpu.VMEM((g, t, 128), F32),
            pltpu.VMEM((2, g, t, t), F32),
            pltpu.VMEM((2, g, t, 128), F32),
        ],
        compiler_params=_params(("arbitrary", "arbitrary", "arbitrary")),
        name="mla",
    )(q, k, v)


_REL_SPAN = 4 * SWA_BLOCK
_N_OFFSETS = 3 * SWA_BLOCK - 1


def _bias_memkv_kernel(table_ref, bucket_ref, mem_hbm, g_ref, w_hbm, o_ref, k_ref, v_ref,
                       spread_ref, mem_ref, w_ref, sem_ref):
    fetches = [pltpu.make_async_copy(mem_hbm, mem_ref, sem_ref.at[0]),
               pltpu.make_async_copy(w_hbm, w_ref, sem_ref.at[1])]
    for fetch in fetches:
        fetch.start()

    for b in range(N_BUCKETS):
        spread_ref[b] = jnp.concatenate(
            [jnp.full((1, 128), table_ref[b, hd], F32) for hd in range(SWA_HEADS)], axis=0)
    lane = lax.broadcasted_iota(jnp.int32, (SWA_HEADS, _REL_SPAN), 1)

    def lookup(c, per_offset):
        col = spread_ref[bucket_ref[c]]
        return jnp.where(lane == c, _lane_tile(col, _REL_SPAN // 128), per_offset)

    per_offset_all = lax.fori_loop(0, _N_OFFSETS, lookup, jnp.zeros((SWA_HEADS, _REL_SPAN), F32))
    for hd in range(SWA_HEADS):
        per_offset = per_offset_all[hd:hd + 1, :]
        rows = jnp.broadcast_to(per_offset, (SWA_BLOCK, _REL_SPAN))
        band = pltpu.roll(rows, _REL_SPAN - (SWA_BLOCK - 1), 1, stride=1, stride_axis=0)
        kvh, within = divmod(hd, SWA_GROUP)
        par, half = within % 2, within // 2
        o_ref[kvh, par, half * SWA_BLOCK:(half + 1) * SWA_BLOCK, :] = band[:, :2 * SWA_BLOCK]

    for fetch in fetches:
        fetch.wait()
    w = w_ref[...].astype(BF16)
    for b in range(mem_ref.shape[0]):
        kv = _dot(_rms(mem_ref[b], g_ref[...]).astype(BF16), w)
        k_ref[b] = kv[:, :MEM_WIDTH].astype(BF16)
        v_ref[b] = kv[:, MEM_WIDTH:].astype(BF16)


def _bias_memkv(rel_table, bucket, mem, g, w):
    B, M, _ = mem.shape
    smem = pl.BlockSpec(memory_space=pltpu.SMEM)
    vmem = pl.BlockSpec(memory_space=pltpu.VMEM)
    hbm = pl.BlockSpec(memory_space=pl.ANY)
    return pl.pallas_call(
        _bias_memkv_kernel,
        in_specs=[smem, smem, hbm, vmem, hbm],
        out_specs=[vmem] * 3,
        out_shape=[jax.ShapeDtypeStruct((SWA_KV_HEADS, 2, 2 * SWA_BLOCK, 2 * SWA_BLOCK), F32),
                   jax.ShapeDtypeStruct((B, M, MEM_WIDTH), BF16),
                   jax.ShapeDtypeStruct((B, M, MEM_WIDTH), BF16)],
        scratch_shapes=[
            pltpu.VMEM((N_BUCKETS, SWA_HEADS, 128), F32),
            pltpu.VMEM(mem.shape, F32),
            pltpu.VMEM(w.shape, F32),
            pltpu.SemaphoreType.DMA((2,)),
        ],
        compiler_params=pltpu.CompilerParams(vmem_limit_bytes=_VMEM_LIMIT),
        name="t5bias_memkv",
    )(rel_table, bucket, mem, g, w)


def _swa_into(y_ref, sink_ref, q_ref, kp_ref, kc_ref, vp_ref, vc_ref, bias_ref, kb_ref, vb_ref, nsub,
              beside_unit):
    t = pl.program_id(1)
    scale = SWA_HEAD_DIM ** -0.5
    sb = SWA_BLOCK
    kb_ref[:sb] = kp_ref[0]
    kb_ref[sb:] = kc_ref[0]
    vb_ref[:sb] = vp_ref[0]
    vb_ref[sb:] = vc_ref[0]

    row = lax.broadcasted_iota(jnp.int32, (2 * sb, 2 * sb), 0)
    q_chunk = _chunk_of(row & (sb - 1))
    b_chunk = _chunk_of(lax.broadcasted_iota(jnp.int32, (2 * sb, 2 * sb), 1))
    valid_any = jnp.logical_and(b_chunk >= q_chunk, b_chunk <= q_chunk + WINDOW_CHUNKS)
    first_lo = jnp.where(t > 0, 0, 2)
    valid_first = jnp.logical_and(valid_any, b_chunk >= first_lo)
    upper = lax.broadcasted_iota(jnp.int32, (2 * sb, 1), 0) < sb

    def geometry(u):
        r, kvh = divmod(u, SWA_KV_HEADS)
        rows = slice(r * sb, (r + 1) * sb)
        band = slice(r * sb, (r + 2) * sb)
        pair0 = slice(2 * kvh * 128, (2 * kvh + 1) * 128)
        pair1 = slice((2 * kvh + 1) * 128, (2 * kvh + 2) * 128)
        return r, kvh, rows, band, pair0, pair1

    def scores(u):
        r, kvh, rows, band, pair0, pair1 = geometry(u)
        q = jnp.concatenate([q_ref[0, rows, pair0], q_ref[0, rows, pair1]], axis=0)
        return [_dot_nt(q, kb_ref[band, (2 * kvh + par) * 128:(2 * kvh + par + 1) * 128])
                for par in range(2)]

    def attend(u, unit_scores):
        r, kvh, rows, band, pair0, pair1 = geometry(u)
        valid = valid_first if r == 0 else valid_any
        o = None
        for par, s in enumerate(unit_scores):
            kcols = slice((2 * kvh + par) * 128, (2 * kvh + par + 1) * 128)
            hd = SWA_GROUP * kvh + par
            s = jnp.where(valid, s * scale + bias_ref[kvh, par], NEG)
            sink = jnp.where(upper, sink_ref[hd], sink_ref[hd + 2])
            m = jnp.maximum(jnp.max(s, axis=-1, keepdims=True), sink)
            p = jnp.exp(s - m)
            den = jnp.sum(p, axis=-1, keepdims=True) + jnp.exp(sink - m)
            pv = _dot((p * (1.0 / den)).astype(BF16), vb_ref[band, kcols])
            o = pv if o is None else o + pv
        y_ref[rows, pair0] = o[:sb]
        y_ref[rows, pair1] = o[sb:]

    n_units = nsub * SWA_KV_HEADS
    pending = scores(0)
    for u in range(n_units):
        upcoming = scores(u + 1) if u + 1 < n_units else None
        beside_unit(u)
        attend(u, pending)
        pending = upcoming


def _mem_into(y_ref, q_ref, k_ref, v_ref, after_head):
    scale = MEM_HEAD_DIM ** -0.5

    def head_cols(hd):
        return slice(hd * MEM_HEAD_DIM, (hd + 1) * MEM_HEAD_DIM)

    def scores(hd):
        return _dot_nt(q_ref[0, :, head_cols(hd)], k_ref[0, :, head_cols(hd)])

    pending = scores(0)
    for hd in range(MEM_HEADS):
        upcoming = scores(hd + 1) if hd + 1 < MEM_HEADS else None
        s = pending * scale
        m = jnp.max(s, axis=-1, keepdims=True)
        p = jnp.exp(s - m)
        inv = 1.0 / jnp.sum(p, axis=-1, keepdims=True)
        y_ref[:, head_cols(hd)] = _dot((p * inv).astype(BF16), v_ref[0, :, head_cols(hd)])
        after_head(hd)
        pending = upcoming


def _tail_kernel(sink_ref, x_ref, ya_ref, za_ref, zb_ref, zc_ref, qs_ref, qm_ref,
                 kp_ref, kc_ref, vp_ref, vc_ref, bias_ref, km_ref, vm_ref, w_hbm, g_ref,
                 o_ref, kb_ref, vb_ref, yb_ref, yc_ref, yacc_ref, w_ref, stage_ref, sem_ref, *, nsub):
    @pl.when(_first_grid_step())
    def _():
        _load_weight_bf16(w_hbm, w_ref, stage_ref, sem_ref)

    def gated(y, z_ref):
        h = 0.5 * z_ref[0].astype(F32)
        return (y * (h + h * jnp.tanh(h))).astype(BF16)

    swa_lo, mem_lo = MLA_WIDTH, MLA_WIDTH + SWA_WIDTH

    g_mla = gated(ya_ref[0].astype(F32), za_ref)
    mla_chunk = D_MODEL // (nsub * SWA_KV_HEADS)

    def project_mla(u):
        cols = slice(u * mla_chunk, (u + 1) * mla_chunk)
        yacc_ref[:, cols] = _dot(g_mla, w_ref[:swa_lo, cols])

    _swa_into(yb_ref, sink_ref, qs_ref, kp_ref, kc_ref, vp_ref, vc_ref, bias_ref, kb_ref, vb_ref, nsub,
              project_mla)

    g_swa = gated(yb_ref[...], zb_ref)
    swa_chunk = D_MODEL // MEM_HEADS

    def project_swa(u):
        cols = slice(u * swa_chunk, (u + 1) * swa_chunk)
        yacc_ref[:, cols] += _dot(g_swa, w_ref[swa_lo:mem_lo, cols])

    _mem_into(yc_ref, qm_ref, km_ref, vm_ref, project_swa)

    g_mem = gated(yc_ref[...], zc_ref)
    sum_sq = jnp.zeros((g_mem.shape[0], 1), F32)
    chunks = [slice(j * swa_chunk, (j + 1) * swa_chunk) for j in range(D_MODEL // swa_chunk)]
    for cols in chunks:
        r = x_ref[0, :, cols] + yacc_ref[:, cols] + _dot(g_mem, w_ref[mem_lo:, cols])
        o_ref[0, :, cols] = r
        sum_sq = sum_sq + jnp.sum(r * r, axis=-1, keepdims=True)
    inv = lax.rsqrt(sum_sq * (1.0 / D_MODEL) + EPS)
    for cols in chunks:
        o_ref[0, :, cols] = (o_ref[0, :, cols] * inv) * g_ref[:, cols]


def _tail(sinks, x, y_mla, z, qs, ks, vs, bias, kmem, vmem, w, g, nsub):
    B, S, _ = x.shape
    tm = nsub * SWA_BLOCK
    M = kmem.shape[1]
    row = lambda b, i: (b, i, 0)
    prev = lambda b, i: (b, jnp.maximum(i * nsub - 1, 0), 0)
    col = lambda c: (lambda b, i: (b, i, c))
    return pl.pallas_call(
        functools.partial(_tail_kernel, nsub=nsub),
        grid=(B, S // tm),
        in_specs=[
            pl.BlockSpec(memory_space=pltpu.SMEM),
            pl.BlockSpec((1, tm, D_MODEL), row),
            pl.BlockSpec((1, tm, MLA_WIDTH), row),
            pl.BlockSpec((1, tm, MLA_WIDTH), col(0)),
            pl.BlockSpec((1, tm, SWA_WIDTH), col(MLA_WIDTH // SWA_WIDTH)),
            pl.BlockSpec((1, tm, MEM_WIDTH), col((MLA_WIDTH + SWA_WIDTH) // MEM_WIDTH)),
            pl.BlockSpec((1, tm, SWA_WIDTH), col(0)),
            pl.BlockSpec((1, tm, MEM_WIDTH), col(1)),
            pl.BlockSpec((1, SWA_BLOCK, _SWA_KV_PAD), prev),
            pl.BlockSpec((1, tm, _SWA_KV_PAD), row),
            pl.BlockSpec((1, SWA_BLOCK, _SWA_KV_PAD), prev),
            pl.BlockSpec((1, tm, _SWA_KV_PAD), row),
            pl.BlockSpec((SWA_KV_HEADS, 2, 2 * SWA_BLOCK, 2 * SWA_BLOCK), lambda b, i: (0, 0, 0, 0)),
            pl.BlockSpec((1, M, MEM_WIDTH), lambda b, i: (b, 0, 0)),
            pl.BlockSpec((1, M, MEM_WIDTH), lambda b, i: (b, 0, 0)),
            pl.BlockSpec(memory_space=pl.ANY),
            pl.BlockSpec((1, D_MODEL), lambda b, i: (0, 0)),
        ],
        out_specs=pl.BlockSpec((1, tm, D_MODEL), row),
        out_shape=jax.ShapeDtypeStruct((B, S, D_MODEL), F32),
        scratch_shapes=[
            pltpu.VMEM((tm + SWA_BLOCK, _SWA_KV_PAD), BF16),
            pltpu.VMEM((tm + SWA_BLOCK, _SWA_KV_PAD), BF16),
            pltpu.VMEM((tm, SWA_WIDTH), F32),
            pltpu.VMEM((tm, MEM_WIDTH), F32),
            pltpu.VMEM((tm, D_MODEL), F32),
            pltpu.VMEM((MIX_WIDTH, D_MODEL), BF16),
            pltpu.VMEM((_STAGE_SLOTS, _W_OUT_CHUNK, D_MODEL), F32),
            pltpu.SemaphoreType.DMA((_STAGE_SLOTS,)),
        ],
        compiler_params=_params(("arbitrary", "arbitrary")),
        name="tail",
    )(sinks, x, y_mla, z, z, z, qs, qs, ks, ks, vs, vs, bias, kmem, vmem, w, g)


def _rope_tables(seq):
    inv = 1.0 / (ROPE_THETA ** (jnp.arange(0, MLA_ROPE, 2, dtype=F32) / MLA_ROPE))
    step = 1 << ((seq - 1).bit_length() // 2)
    assert seq % step == 0
    fine = jnp.arange(step, dtype=F32)[:, None] * inv[None, :]
    coarse = (jnp.arange(seq // step, dtype=F32) * step)[:, None] * inv[None, :]
    cf, sf = jnp.cos(fine)[None], jnp.sin(fine)[None]
    cc, sc = jnp.cos(coarse)[:, None], jnp.sin(coarse)[:, None]
    cos = (cc * cf - sc * sf).reshape(seq, -1)
    sin = (sc * cf + cc * sf).reshape(seq, -1)
    zero = jnp.zeros_like(cos)
    return (jnp.concatenate([cos, cos, zero, zero], axis=-1),
            jnp.concatenate([-sin, sin, zero, zero], axis=-1))


def _t5_bucket(rel):
    nb = N_BUCKETS // 2
    max_exact = nb // 2
    bucket = jnp.where(rel > 0, nb, 0)
    n = jnp.abs(rel)
    nf = jnp.maximum(n, 1).astype(F32)
    large = max_exact + (jnp.log(nf / max_exact) / math.log(MAX_DISTANCE / max_exact)
                         * (nb - max_exact)).astype(jnp.int32)
    large = jnp.minimum(large, nb - 1)
    return bucket + jnp.where(n < max_exact, n, large)


def kernel(x, mem, norm_in, w_in, norm_q, norm_kv, w_uq, w_ukv, attn_sinks, rel_bias,
           norm_mem, w_mem_kv, w_out, norm_final):
    B, S, _ = x.shape
    assert norm_in.shape[0] == 1, "single-layer trunk"
    assert x.shape[2] == D_MODEL and w_in.shape[1:] == (D_MODEL, IN_WIDTH)
    assert S % _PROJ_ROWS == 0 and S % (_TAIL_BLOCKS * SWA_BLOCK) == 0

    wt = w_in[0].T
    cos_t, sin_t = _rope_tables(S)
    q, k, v, qs, z, ks, vs = _proj(x, norm_in, wt, norm_q, norm_kv, w_uq[0].astype(BF16),
                                   w_ukv[0].astype(BF16), cos_t, sin_t, tm=_PROJ_ROWS)

    y_mla = _mla(q, k, v, t=_MLA_TILE, g=_MLA_GROUP, nq=_MLA_TILES_PER_STEP)

    bucket = _t5_bucket(jnp.arange(_REL_SPAN) - (2 * SWA_BLOCK - 1)).astype(jnp.int32)
    bias, kmem, vmem = _bias_memkv(rel_bias, bucket, mem, norm_mem, w_mem_kv[0])
    return _tail(attn_sinks[0], x, y_mla, z, qs, ks, vs, bias, kmem, vmem,
                 w_out[0], norm_final[None, :], nsub=_TAIL_BLOCKS)
```

```python
import functools
import math

import jax
import jax.numpy as jnp
from jax import lax
from jax.experimental import pallas as pl
from jax.experimental.pallas import tpu as pltpu

D_MODEL = 2048
CHUNK = 64
EPS = 1e-6
NEG = -1e30

MLA_HEADS = 8
MLA_NOPE = 128
MLA_ROPE = 64
MLA_V = 128
MLA_QLORA = 512
MLA_KVLORA = 256
MLA_WIDTH = MLA_HEADS * MLA_V
MLA_QK_PAD = 256
ROPE_THETA = 10000.0

SWA_HEADS = 8
SWA_KV_HEADS = 2
SWA_GROUP = SWA_HEADS // SWA_KV_HEADS
SWA_HEAD_DIM = 64
SWA_WIDTH = SWA_HEADS * SWA_HEAD_DIM
SWA_KV_WIDTH = SWA_KV_HEADS * SWA_HEAD_DIM
WINDOW_CHUNKS = 2
SWA_BLOCK = 128

MEM_HEADS = 4
MEM_HEAD_DIM = 128
MEM_WIDTH = MEM_HEADS * MEM_HEAD_DIM

MIX_WIDTH = MLA_WIDTH + SWA_WIDTH + MEM_WIDTH

N_BUCKETS = 32
MAX_DISTANCE = 128

BF16 = jnp.bfloat16
F32 = jnp.float32

_R_CQ = 0
_R_CKV = _R_CQ + MLA_QLORA
_R_KPE = _R_CKV + MLA_KVLORA
_R_ZMLA = _R_KPE + MLA_ROPE
_R_QSWA = _R_ZMLA + MLA_WIDTH
_R_KSWA = _R_QSWA + SWA_WIDTH
_R_VSWA = _R_KSWA + SWA_KV_WIDTH
_R_ZSWA = _R_VSWA + SWA_KV_WIDTH
_R_QMEM = _R_ZSWA + SWA_WIDTH
_R_ZMEM = _R_QMEM + MEM_WIDTH
IN_WIDTH = _R_ZMEM + MEM_WIDTH

_MLA_Q_SCALE = (MLA_NOPE + MLA_ROPE) ** -0.5 * math.log2(math.e)

_SWA_KV_PAD = 2 * SWA_KV_HEADS * 128

_W_IN_CHUNK = 80
_W_OUT_CHUNK = 128
_STAGE_SLOTS = 8

_V7X_VMEM_BYTES = 64 * 1024 * 1024
_VMEM_LIMIT = _V7X_VMEM_BYTES - 8 * 1024 * 1024

_PROJ_ROWS = 512
_MLA_TILE = 512
_MLA_GROUP = 4
_MLA_TILES_PER_STEP = 4
_TAIL_BLOCKS = 4


def _params(sem):
    return pltpu.CompilerParams(dimension_semantics=sem, vmem_limit_bytes=_VMEM_LIMIT)


def _rms(v, g):
    return (v * lax.rsqrt(jnp.mean(v * v, axis=-1, keepdims=True) + EPS)) * g


def _dot(a, b):
    return jnp.dot(a, b, preferred_element_type=F32)


def _dot_nt(a, b):
    return lax.dot_general(a, b, (((1,), (1,)), ((), ())), preferred_element_type=F32)


def _load_weight_bf16(w_hbm, w_ref, stage_ref, sem_ref):
    slots, rows, _ = stage_ref.shape
    n_chunks = w_ref.shape[0] // rows
    ahead = slots - 1

    def chunk_copy(c):
        return pltpu.make_async_copy(w_hbm.at[pl.ds(c * rows, rows), :], stage_ref.at[c % slots],
                                     sem_ref.at[c % slots])

    for c in range(min(ahead, n_chunks)):
        chunk_copy(c).start()
    for c in range(n_chunks):
        if c + ahead < n_chunks:
            chunk_copy(c + ahead).start()
        chunk_copy(c).wait()
        w_ref[c * rows:(c + 1) * rows, :] = stage_ref[c % slots].astype(BF16)


def _first_grid_step():
    return jnp.logical_and(pl.program_id(0) == 0, pl.program_id(1) == 0)


def _lane_tile(v, n):
    return jnp.concatenate([v] * n, axis=1)


def _chunk_of(pos):
    return jnp.right_shift(pos, CHUNK.bit_length() - 1)


def _rope128(v, cos_t, sin_t):
    return v * cos_t + pltpu.roll(v, MLA_ROPE // 2, 1) * sin_t


def _store_swa_kv(ref, kv):
    low = lax.broadcasted_iota(jnp.int32, kv.shape, 1) < SWA_HEAD_DIM
    swapped = pltpu.roll(kv, SWA_HEAD_DIM, 1)
    pieces = (jnp.where(low, kv, 0.0), jnp.where(low, 0.0, swapped),
              jnp.where(low, swapped, 0.0), jnp.where(low, 0.0, kv))
    for i, piece in enumerate(pieces):
        ref[0, :, i * 128:(i + 1) * 128] = piece.astype(BF16)


def _proj_kernel(x_ref, gin_ref, wt_hbm, gq_ref, gkv_ref, wuq_ref, wukv_ref, cos_ref, sin_ref,
                 q_ref, k_ref, v_ref, qs_ref, z_ref, ks_ref, vs_ref, wt_ref, stage_ref, sem_ref):
    @pl.when(_first_grid_step())
    def _():
        _load_weight_bf16(wt_hbm, wt_ref, stage_ref, sem_ref)

    h = _rms(x_ref[0], gin_ref[...]).astype(BF16)
    cos_t = cos_ref[...]
    sin_t = sin_ref[...]

    def proj(lo, hi):
        return _dot_nt(h, wt_ref[lo:hi, :])

    pa = proj(_R_CQ, _R_KPE)
    cq = _rms(pa[:, :MLA_QLORA], gq_ref[...]).astype(BF16)
    ckv = _rms(pa[:, MLA_QLORA:], gkv_ref[...]).astype(BF16)

    qs_ref[0, :, :SWA_WIDTH] = proj(_R_QSWA, _R_KSWA).astype(BF16)
    qs_ref[0, :, SWA_WIDTH:] = proj(_R_QMEM, _R_ZMEM).astype(BF16)
    z_ref[0, :, :MLA_WIDTH] = proj(_R_ZMLA, _R_QSWA).astype(BF16)
    z_ref[0, :, MLA_WIDTH:MLA_WIDTH + SWA_WIDTH] = proj(_R_ZSWA, _R_QMEM).astype(BF16)
    z_ref[0, :, MLA_WIDTH + SWA_WIDTH:] = proj(_R_ZMEM, IN_WIDTH).astype(BF16)
    kvs = proj(_R_KSWA, _R_ZSWA)
    _store_swa_kv(ks_ref, kvs[:, :SWA_KV_WIDTH])
    _store_swa_kv(vs_ref, kvs[:, SWA_KV_WIDTH:])
    pe = proj(_R_KPE, _R_ZMLA)
    kpe = _rope128(jnp.concatenate([pe, pe], axis=1), cos_t, sin_t).astype(BF16)

    qall = _dot(cq, wuq_ref[...]) * _MLA_Q_SCALE
    kv = _dot(ckv, wukv_ref[...])
    for hd in range(MLA_HEADS):
        oq = hd * (MLA_NOPE + MLA_ROPE)
        q_pe = qall[:, oq + MLA_NOPE:oq + MLA_NOPE + MLA_ROPE]
        q_ref[0, hd, :, :MLA_NOPE] = qall[:, oq:oq + MLA_NOPE].astype(BF16)
        q_ref[0, hd, :, MLA_NOPE:] = _rope128(
            jnp.concatenate([q_pe, q_pe], axis=1), cos_t, sin_t).astype(BF16)
        o = hd * (MLA_NOPE + MLA_V)
        k_ref[0, hd, :, :MLA_NOPE] = kv[:, o:o + MLA_NOPE].astype(BF16)
        k_ref[0, hd, :, MLA_NOPE:] = kpe
        v_ref[0, hd] = kv[:, o + MLA_NOPE:o + MLA_QK_PAD].astype(BF16)


def _proj(x, gin, wt, gq, gkv, wuq, wukv, cos_t, sin_t, tm):
    B, S, _ = x.shape
    const = lambda b, i: (0, 0)
    row = lambda b, i: (b, i, 0)
    single = pl.Buffered(1)

    def out(width):
        return jax.ShapeDtypeStruct((B, S, width), BF16)

    head_spec = pl.BlockSpec((1, MLA_HEADS, tm, MLA_QK_PAD), lambda b, i: (b, 0, i, 0))
    head_shape = jax.ShapeDtypeStruct((B, MLA_HEADS, S, MLA_QK_PAD), BF16)

    return pl.pallas_call(
        _proj_kernel,
        grid=(B, S // tm),
        in_specs=[
            pl.BlockSpec((1, tm, D_MODEL), row),
            pl.BlockSpec((1, D_MODEL), const),
            pl.BlockSpec(memory_space=pl.ANY),
            pl.BlockSpec((1, MLA_QLORA), const),
            pl.BlockSpec((1, MLA_KVLORA), const),
            pl.BlockSpec((MLA_QLORA, MLA_HEADS * (MLA_NOPE + MLA_ROPE)), const, pipeline_mode=single),
            pl.BlockSpec((MLA_KVLORA, MLA_HEADS * (MLA_NOPE + MLA_V)), const, pipeline_mode=single),
            pl.BlockSpec((tm, 128), lambda b, i: (i, 0)),
            pl.BlockSpec((tm, 128), lambda b, i: (i, 0)),
        ],
        out_specs=[
            head_spec, head_spec,
            pl.BlockSpec((1, MLA_HEADS, tm, MLA_V), lambda b, i: (b, 0, i, 0)),
            pl.BlockSpec((1, tm, SWA_WIDTH + MEM_WIDTH), row),
            pl.BlockSpec((1, tm, MIX_WIDTH), row),
            pl.BlockSpec((1, tm, _SWA_KV_PAD), row),
            pl.BlockSpec((1, tm, _SWA_KV_PAD), row),
        ],
        out_shape=[
            head_shape, head_shape, jax.ShapeDtypeStruct((B, MLA_HEADS, S, MLA_V), BF16),
            out(SWA_WIDTH + MEM_WIDTH), out(MIX_WIDTH), out(_SWA_KV_PAD), out(_SWA_KV_PAD),
        ],
        scratch_shapes=[
            pltpu.VMEM((IN_WIDTH, D_MODEL), BF16),
            pltpu.VMEM((_STAGE_SLOTS, _W_IN_CHUNK, D_MODEL), F32),
            pltpu.SemaphoreType.DMA((_STAGE_SLOTS,)),
        ],
        compiler_params=_params(("arbitrary", "arbitrary")),
        name="proj",
    )(x, gin, wt, gq, gkv, wuq, wukv, cos_t, sin_t)


def _mla_kernel(q_ref, k_ref, v_ref, o_ref, acc_ref, m_ref, s_ref, cm_ref, *, t, g, nq):
    qi0 = pl.program_id(2) * nq
    qi_end = qi0 + nq - 1
    acc_ref[...] = jnp.zeros(acc_ref.shape, F32)
    m_ref[...] = jnp.full(m_ref.shape, NEG, F32)
    ones = jnp.ones((t, MLA_V), BF16)
    row_halves = ((0, t // 2, t // 2), (t // 2, t // 2, t))

    def row_max(s):
        return jnp.broadcast_to(jnp.max(s, axis=-1, keepdims=True), (s.shape[0], 128))

    def diagonal_mask():
        q_chunk = _chunk_of(lax.broadcasted_iota(jnp.int32, (t, t), 0))
        k_chunk = _chunk_of(lax.broadcasted_iota(jnp.int32, (t, t), 1))
        return k_chunk <= q_chunk

    def scores(qi, c, slot, hd, mask=None):
        rows = pl.multiple_of((qi - qi0) * t, t)
        keys = pl.multiple_of(c * t, t)
        for r0, nr, nk in (row_halves if mask is not None else ((0, t, t),)):
            s = _dot_nt(q_ref[0, hd, pl.ds(pl.multiple_of(rows + r0, nr), nr), :],
                        k_ref[0, hd, pl.ds(keys, nk), :])
            if mask is not None:
                s = jnp.where(mask[r0:r0 + nr, :nk], s, NEG)
            s_ref[slot, hd, r0:r0 + nr, :nk] = s
            cm_ref[slot, hd, r0:r0 + nr] = row_max(s)

    def update(c, slot, hd, diagonal, mask):
        keys = pl.multiple_of(c * t, t)
        for r0, nr, nk in (row_halves if diagonal else ((0, t, t),)):
            rows = slice(r0, r0 + nr)
            s = s_ref[slot, hd, rows, :nk]
            if mask is not None:
                s = jnp.where(mask[rows, :nk], s, NEG)
                m_cur = row_max(s)
            else:
                m_cur = cm_ref[slot, hd, rows]
            m_old = m_ref[hd, rows]
            m_new = jnp.maximum(m_old, m_cur)
            alpha = jnp.exp2(m_old - m_new)
            p = jnp.exp2(s - _lane_tile(m_new, nk // 128)).astype(BF16)
            v_ones = jnp.concatenate([v_ref[0, hd, pl.ds(keys, nk), :], ones[:nk]], axis=1)
            acc_ref[hd, rows] = (_lane_tile(alpha, MLA_QK_PAD // 128) * acc_ref[hd, rows]
                                 + _dot(p, v_ones))
            m_ref[hd, rows] = m_new

    def finish_tile(qi, hd):
        rows = pl.multiple_of((qi - qi0) * t, t)
        acc = acc_ref[hd]
        o_ref[0, pl.ds(rows, t), hd * MLA_V:(hd + 1) * MLA_V] = (
            acc[:, :MLA_V] / acc[:, MLA_V:]).astype(BF16)
        acc_ref[hd] = jnp.zeros(acc.shape, F32)
        m_ref[hd] = jnp.full((t, 128), NEG, F32)

    def following(qi, c):
        is_last = c == qi
        return jnp.where(is_last, jnp.minimum(qi + 1, qi_end), qi), jnp.where(is_last, 0, c + 1)

    def step(qi, c, slot, diagonal, premasked=False, mask_next=False):
        qn, cn = following(qi, c)
        mask = diagonal_mask() if diagonal and not premasked else None
        next_mask = diagonal_mask() if mask_next else None
        for hd in range(g):
            scores(qn, cn, 1 - slot, hd, next_mask)
            update(c, slot, hd, diagonal, mask)
            if diagonal:
                finish_tile(qi, hd)

    for hd in range(g):
        scores(qi0, 0, 0, hd)

    def body(_, pair):
        first, second = pair, following(*pair)
        diag0 = first[1] == first[0]
        diag1 = second[1] == second[0]
        for d0, d1, cond in ((True, False, diag0), (False, True, diag1),
                             (False, False, jnp.logical_not(jnp.logical_or(diag0, diag1)))):
            @pl.when(cond)
            def _(d0=d0, d1=d1):
                step(*first, 0, d0, mask_next=d1)
                step(*second, 1, d1, premasked=d1)
        return following(*second)

    n_steps = nq * qi0 + nq * (nq + 1) // 2
    lax.fori_loop(0, n_steps // 2, body, (qi0, 0))


def _mla(q, k, v, t, g, nq):
    B, _, S, _ = q.shape
    assert S % (nq * t) == 0 and nq % 2 == 0 and (nq * (nq + 1) // 2) % 2 == 0
    return pl.pallas_call(
        functools.partial(_mla_kernel, t=t, g=g, nq=nq),
        grid=(B, MLA_HEADS // g, S // (nq * t)),
        in_specs=[
            pl.BlockSpec((1, g, nq * t, MLA_QK_PAD), lambda b, h, i: (b, h, i, 0)),
            pl.BlockSpec((1, g, S, MLA_QK_PAD), lambda b, h, i: (b, h, 0, 0)),
            pl.BlockSpec((1, g, S, MLA_V), lambda b, h, i: (b, h, 0, 0)),
        ],
        out_specs=pl.BlockSpec((1, nq * t, g * MLA_V), lambda b, h, i: (b, i, h)),
        out_shape=jax.ShapeDtypeStruct((B, S, MLA_WIDTH), BF16),
        scratch_shapes=[
            pltpu.VMEM((g, t, MLA_QK_PAD), F32),
            pltpu.VMEM((g, t, 128), F32),
            pltpu.VMEM((2, g, t, t), F32),
            pltpu.VMEM((2, g, t, 128), F32),
        ],
        compiler_params=_params(("arbitrary", "arbitrary", "arbitrary")),
        name="mla",
    )(q, k, v)


_REL_SPAN = 4 * SWA_BLOCK
_N_OFFSETS = 3 * SWA_BLOCK - 1


def _bias_memkv_kernel(table_ref, bucket_ref, mem_hbm, g_ref, w_hbm, o_ref, k_ref, v_ref,
                       spread_ref, mem_ref, w_ref, sem_ref):
    fetches = [pltpu.make_async_copy(mem_hbm, mem_ref, sem_ref.at[0]),
               pltpu.make_async_copy(w_hbm, w_ref, sem_ref.at[1])]
    for fetch in fetches:
        fetch.start()

    for b in range(N_BUCKETS):
        spread_ref[b] = jnp.concatenate(
            [jnp.full((1, 128), table_ref[b, hd], F32) for hd in range(SWA_HEADS)], axis=0)
    lane = lax.broadcasted_iota(jnp.int32, (SWA_HEADS, _REL_SPAN), 1)

    def lookup(c, per_offset):
        col = spread_ref[bucket_ref[c]]
        return jnp.where(lane == c, _lane_tile(col, _REL_SPAN // 128), per_offset)

    per_offset_all = lax.fori_loop(0, _N_OFFSETS, lookup, jnp.zeros((SWA_HEADS, _REL_SPAN), F32))
    for hd in range(SWA_HEADS):
        per_offset = per_offset_all[hd:hd + 1, :]
        rows = jnp.broadcast_to(per_offset, (SWA_BLOCK, _REL_SPAN))
        band = pltpu.roll(rows, _REL_SPAN - (SWA_BLOCK - 1), 1, stride=1, stride_axis=0)
        kvh, within = divmod(hd, SWA_GROUP)
        par, half = within % 2, within // 2
        o_ref[kvh, par, half * SWA_BLOCK:(half + 1) * SWA_BLOCK, :] = band[:, :2 * SWA_BLOCK]

    for fetch in fetches:
        fetch.wait()
    w = w_ref[...].astype(BF16)
    for b in range(mem_ref.shape[0]):
        kv = _dot(_rms(mem_ref[b], g_ref[...]).astype(BF16), w)
        k_ref[b] = kv[:, :MEM_WIDTH].astype(BF16)
        v_ref[b] = kv[:, MEM_WIDTH:].astype(BF16)


def _bias_memkv(rel_table, bucket, mem, g, w):
    B, M, _ = mem.shape
    smem = pl.BlockSpec(memory_space=pltpu.SMEM)
    vmem = pl.BlockSpec(memory_space=pltpu.VMEM)
    hbm = pl.BlockSpec(memory_space=pl.ANY)
    return pl.pallas_call(
        _bias_memkv_kernel,
        in_specs=[smem, smem, hbm, vmem, hbm],
        out_specs=[vmem] * 3,
        out_shape=[jax.ShapeDtypeStruct((SWA_KV_HEADS, 2, 2 * SWA_BLOCK, 2 * SWA_BLOCK), F32),
                   jax.ShapeDtypeStruct((B, M, MEM_WIDTH), BF16),
                   jax.ShapeDtypeStruct((B, M, MEM_WIDTH), BF16)],
        scratch_shapes=[
            pltpu.VMEM((N_BUCKETS, SWA_HEADS, 128), F32),
            pltpu.VMEM(mem.shape, F32),
            pltpu.VMEM(w.shape, F32),
            pltpu.SemaphoreType.DMA((2,)),
        ],
        compiler_params=pltpu.CompilerParams(vmem_limit_bytes=_VMEM_LIMIT),
        name="t5bias_memkv",
    )(rel_table, bucket, mem, g, w)


def _swa_into(y_ref, sink_ref, q_ref, kp_ref, kc_ref, vp_ref, vc_ref, bias_ref, kb_ref, vb_ref, nsub,
              beside_unit):
    t = pl.program_id(1)
    scale = SWA_HEAD_DIM ** -0.5
    sb = SWA_BLOCK
    kb_ref[:sb] = kp_ref[0]
    kb_ref[sb:] = kc_ref[0]
    vb_ref[:sb] = vp_ref[0]
    vb_ref[sb:] = vc_ref[0]

    row = lax.broadcasted_iota(jnp.int32, (2 * sb, 2 * sb), 0)
    q_chunk = _chunk_of(row & (sb - 1))
    b_chunk = _chunk_of(lax.broadcasted_iota(jnp.int32, (2 * sb, 2 * sb), 1))
    valid_any = jnp.logical_and(b_chunk >= q_chunk, b_chunk <= q_chunk + WINDOW_CHUNKS)
    first_lo = jnp.where(t > 0, 0, 2)
    valid_first = jnp.logical_and(valid_any, b_chunk >= first_lo)
    upper = lax.broadcasted_iota(jnp.int32, (2 * sb, 1), 0) < sb

    def geometry(u):
        r, kvh = divmod(u, SWA_KV_HEADS)
        rows = slice(r * sb, (r + 1) * sb)
        band = slice(r * sb, (r + 2) * sb)
        pair0 = slice(2 * kvh * 128, (2 * kvh + 1) * 128)
        pair1 = slice((2 * kvh + 1) * 128, (2 * kvh + 2) * 128)
        return r, kvh, rows, band, pair0, pair1

    def scores(u):
        r, kvh, rows, band, pair0, pair1 = geometry(u)
        q = jnp.concatenate([q_ref[0, rows, pair0], q_ref[0, rows, pair1]], axis=0)
        return [_dot_nt(q, kb_ref[band, (2 * kvh + par) * 128:(2 * kvh + par + 1) * 128])
                for par in range(2)]

    def attend(u, unit_scores):
        r, kvh, rows, band, pair0, pair1 = geometry(u)
        valid = valid_first if r == 0 else valid_any
        o = None
        for par, s in enumerate(unit_scores):
            kcols = slice((2 * kvh + par) * 128, (2 * kvh + par + 1) * 128)
            hd = SWA_GROUP * kvh + par
            s = jnp.where(valid, s * scale + bias_ref[kvh, par], NEG)
            sink = jnp.where(upper, sink_ref[hd], sink_ref[hd + 2])
            m = jnp.maximum(jnp.max(s, axis=-1, keepdims=True), sink)
            p = jnp.exp(s - m)
            den = jnp.sum(p, axis=-1, keepdims=True) + jnp.exp(sink - m)
            pv = _dot((p * (1.0 / den)).astype(BF16), vb_ref[band, kcols])
            o = pv if o is None else o + pv
        y_ref[rows, pair0] = o[:sb]
        y_ref[rows, pair1] = o[sb:]

    n_units = nsub * SWA_KV_HEADS
    pending = scores(0)
    for u in range(n_units):
        upcoming = scores(u + 1) if u + 1 < n_units else None
        beside_unit(u)
        attend(u, pending)
        pending = upcoming


def _mem_into(y_ref, q_ref, k_ref, v_ref, after_head):
    scale = MEM_HEAD_DIM ** -0.5

    def head_cols(hd):
        return slice(hd * MEM_HEAD_DIM, (hd + 1) * MEM_HEAD_DIM)

    def scores(hd):
        return _dot_nt(q_ref[0, :, head_cols(hd)], k_ref[0, :, head_cols(hd)])

    pending = scores(0)
    for hd in range(MEM_HEADS):
        upcoming = scores(hd + 1) if hd + 1 < MEM_HEADS else None
        s = pending * scale
        m = jnp.max(s, axis=-1, keepdims=True)
        p = jnp.exp(s - m)
        inv = 1.0 / jnp.sum(p, axis=-1, keepdims=True)
        y_ref[:, head_cols(hd)] = _dot((p * inv).astype(BF16), v_ref[0, :, head_cols(hd)])
        after_head(hd)
        pending = upcoming


def _tail_kernel(sink_ref, x_ref, ya_ref, za_ref, zb_ref, zc_ref, qs_ref, qm_ref,
                 kp_ref, kc_ref, vp_ref, vc_ref, bias_ref, km_ref, vm_ref, w_hbm, g_ref,
                 o_ref, kb_ref, vb_ref, yb_ref, yc_ref, yacc_ref, w_ref, stage_ref, sem_ref, *, nsub):
    @pl.when(_first_grid_step())
    def _():
        _load_weight_bf16(w_hbm, w_ref, stage_ref, sem_ref)

    def gated(y, z_ref):
        h = 0.5 * z_ref[0].astype(F32)
        return (y * (h + h * jnp.tanh(h))).astype(BF16)

    swa_lo, mem_lo = MLA_WIDTH, MLA_WIDTH + SWA_WIDTH

    g_mla = gated(ya_ref[0].astype(F32), za_ref)
    mla_chunk = D_MODEL // (nsub * SWA_KV_HEADS)

    def project_mla(u):
        cols = slice(u * mla_chunk, (u + 1) * mla_chunk)
        yacc_ref[:, cols] = _dot(g_mla, w_ref[:swa_lo, cols])

    _swa_into(yb_ref, sink_ref, qs_ref, kp_ref, kc_ref, vp_ref, vc_ref, bias_ref, kb_ref, vb_ref, nsub,
              project_mla)

    g_swa = gated(yb_ref[...], zb_ref)
    swa_chunk = D_MODEL // MEM_HEADS

    def project_swa(u):
        cols = slice(u * swa_chunk, (u + 1) * swa_chunk)
        yacc_ref[:, cols] += _dot(g_swa, w_ref[swa_lo:mem_lo, cols])

    _mem_into(yc_ref, qm_ref, km_ref, vm_ref, project_swa)

    g_mem = gated(yc_ref[...], zc_ref)
    sum_sq = jnp.zeros((g_mem.shape[0], 1), F32)
    chunks = [slice(j * swa_chunk, (j + 1) * swa_chunk) for j in range(D_MODEL // swa_chunk)]
    for cols in chunks:
        r = x_ref[0, :, cols] + yacc_ref[:, cols] + _dot(g_mem, w_ref[mem_lo:, cols])
        o_ref[0, :, cols] = r
        sum_sq = sum_sq + jnp.sum(r * r, axis=-1, keepdims=True)
    inv = lax.rsqrt(sum_sq * (1.0 / D_MODEL) + EPS)
    for cols in chunks:
        o_ref[0, :, cols] = (o_ref[0, :, cols] * inv) * g_ref[:, cols]


def _tail(sinks, x, y_mla, z, qs, ks, vs, bias, kmem, vmem, w, g, nsub):
    B, S, _ = x.shape
    tm = nsub * SWA_BLOCK
    M = kmem.shape[1]
    row = lambda b, i: (b, i, 0)
    prev = lambda b, i: (b, jnp.maximum(i * nsub - 1, 0), 0)
    col = lambda c: (lambda b, i: (b, i, c))
    return pl.pallas_call(
        functools.partial(_tail_kernel, nsub=nsub),
        grid=(B, S // tm),
        in_specs=[
            pl.BlockSpec(memory_space=pltpu.SMEM),
            pl.BlockSpec((1, tm, D_MODEL), row),
            pl.BlockSpec((1, tm, MLA_WIDTH), row),
            pl.BlockSpec((1, tm, MLA_WIDTH), col(0)),
            pl.BlockSpec((1, tm, SWA_WIDTH), col(MLA_WIDTH // SWA_WIDTH)),
            pl.BlockSpec((1, tm, MEM_WIDTH), col((MLA_WIDTH + SWA_WIDTH) // MEM_WIDTH)),
            pl.BlockSpec((1, tm, SWA_WIDTH), col(0)),
            pl.BlockSpec((1, tm, MEM_WIDTH), col(1)),
            pl.BlockSpec((1, SWA_BLOCK, _SWA_KV_PAD), prev),
            pl.BlockSpec((1, tm, _SWA_KV_PAD), row),
            pl.BlockSpec((1, SWA_BLOCK, _SWA_KV_PAD), prev),
            pl.BlockSpec((1, tm, _SWA_KV_PAD), row),
            pl.BlockSpec((SWA_KV_HEADS, 2, 2 * SWA_BLOCK, 2 * SWA_BLOCK), lambda b, i: (0, 0, 0, 0)),
            pl.BlockSpec((1, M, MEM_WIDTH), lambda b, i: (b, 0, 0)),
            pl.BlockSpec((1, M, MEM_WIDTH), lambda b, i: (b, 0, 0)),
            pl.BlockSpec(memory_space=pl.ANY),
            pl.BlockSpec((1, D_MODEL), lambda b, i: (0, 0)),
        ],
        out_specs=pl.BlockSpec((1, tm, D_MODEL), row),
        out_shape=jax.ShapeDtypeStruct((B, S, D_MODEL), F32),
        scratch_shapes=[
            pltpu.VMEM((tm + SWA_BLOCK, _SWA_KV_PAD), BF16),
            pltpu.VMEM((tm + SWA_BLOCK, _SWA_KV_PAD), BF16),
            pltpu.VMEM((tm, SWA_WIDTH), F32),
            pltpu.VMEM((tm, MEM_WIDTH), F32),
            pltpu.VMEM((tm, D_MODEL), F32),
            pltpu.VMEM((MIX_WIDTH, D_MODEL), BF16),
            pltpu.VMEM((_STAGE_SLOTS, _W_OUT_CHUNK, D_MODEL), F32),
            pltpu.SemaphoreType.DMA((_STAGE_SLOTS,)),
        ],
        compiler_params=_params(("arbitrary", "arbitrary")),
        name="tail",
    )(sinks, x, y_mla, z, z, z, qs, qs, ks, ks, vs, vs, bias, kmem, vmem, w, g)


def _rope_tables(seq):
    inv = 1.0 / (ROPE_THETA ** (jnp.arange(0, MLA_ROPE, 2, dtype=F32) / MLA_ROPE))
    step = 1 << ((seq - 1).bit_length() // 2)
    assert seq % step == 0
    fine = jnp.arange(step, dtype=F32)[:, None] * inv[None, :]
    coarse = (jnp.arange(seq // step, dtype=F32) * step)[:, None] * inv[None, :]
    cf, sf = jnp.cos(fine)[None], jnp.sin(fine)[None]
    cc, sc = jnp.cos(coarse)[:, None], jnp.sin(coarse)[:, None]
    cos = (cc * cf - sc * sf).reshape(seq, -1)
    sin = (sc * cf + cc * sf).reshape(seq, -1)
    zero = jnp.zeros_like(cos)
    return (jnp.concatenate([cos, cos, zero, zero], axis=-1),
            jnp.concatenate([-sin, sin, zero, zero], axis=-1))


def _t5_bucket(rel):
    nb = N_BUCKETS // 2
    max_exact = nb // 2
    bucket = jnp.where(rel > 0, nb, 0)
    n = jnp.abs(rel)
    nf = jnp.maximum(n, 1).astype(F32)
    large = max_exact + (jnp.log(nf / max_exact) / math.log(MAX_DISTANCE / max_exact)
                         * (nb - max_exact)).astype(jnp.int32)
    large = jnp.minimum(large, nb - 1)
    return bucket + jnp.where(n < max_exact, n, large)


def kernel(x, mem, norm_in, w_in, norm_q, norm_kv, w_uq, w_ukv, attn_sinks, rel_bias,
           norm_mem, w_mem_kv, w_out, norm_final):
    B, S, _ = x.shape
    assert norm_in.shape[0] == 1, "single-layer trunk"
    assert x.shape[2] == D_MODEL and w_in.shape[1:] == (D_MODEL, IN_WIDTH)
    assert S % _PROJ_ROWS == 0 and S % (_TAIL_BLOCKS * SWA_BLOCK) == 0

    wt = w_in[0].T
    cos_t, sin_t = _rope_tables(S)
    q, k, v, qs, z, ks, vs = _proj(x, norm_in, wt, norm_q, norm_kv, w_uq[0].astype(BF16),
                                   w_ukv[0].astype(BF16), cos_t, sin_t, tm=_PROJ_ROWS)

    y_mla = _mla(q, k, v, t=_MLA_TILE, g=_MLA_GROUP, nq=_MLA_TILES_PER_STEP)

    bucket = _t5_bucket(jnp.arange(_REL_SPAN) - (2 * SWA_BLOCK - 1)).astype(jnp.int32)
    bias, kmem, vmem = _bias_memkv(rel_bias, bucket, mem, norm_mem, w_mem_kv[0])
    return _tail(attn_sinks[0], x, y_mla, z, qs, ks, vs, bias, kmem, vmem,
                 w_out[0], norm_final[None, :], nsub=_TAIL_BLOCKS)
```

```python
import functools
import math

import jax
import jax.numpy as jnp
from jax import lax
from jax.experimental import pallas as pl
from jax.experimental.pallas import tpu as pltpu

D_MODEL = 2048
CHUNK = 64
EPS = 1e-6
NEG = -1e30

MLA_HEADS = 8
MLA_NOPE = 128
MLA_ROPE = 64
MLA_V = 128
MLA_QLORA = 512
MLA_KVLORA = 256
MLA_WIDTH = MLA_HEADS * MLA_V
MLA_QK_PAD = 256
ROPE_THETA = 10000.0

SWA_HEADS = 8
SWA_KV_HEADS = 2
SWA_GROUP = SWA_HEADS // SWA_KV_HEADS
SWA_HEAD_DIM = 64
SWA_WIDTH = SWA_HEADS * SWA_HEAD_DIM
SWA_KV_WIDTH = SWA_KV_HEADS * SWA_HEAD_DIM
WINDOW_CHUNKS = 2
SWA_BLOCK = 128

MEM_HEADS = 4
MEM_HEAD_DIM = 128
MEM_WIDTH = MEM_HEADS * MEM_HEAD_DIM

MIX_WIDTH = MLA_WIDTH + SWA_WIDTH + MEM_WIDTH

N_BUCKETS = 32
MAX_DISTANCE = 128

BF16 = jnp.bfloat16
F32 = jnp.float32

_R_CQ = 0
_R_CKV = _R_CQ + MLA_QLORA
_R_KPE = _R_CKV + MLA_KVLORA
_R_ZMLA = _R_KPE + MLA_ROPE
_R_QSWA = _R_ZMLA + MLA_WIDTH
_R_KSWA = _R_QSWA + SWA_WIDTH
_R_VSWA = _R_KSWA + SWA_KV_WIDTH
_R_ZSWA = _R_VSWA + SWA_KV_WIDTH
_R_QMEM = _R_ZSWA + SWA_WIDTH
_R_ZMEM = _R_QMEM + MEM_WIDTH
IN_WIDTH = _R_ZMEM + MEM_WIDTH

_MLA_Q_SCALE = (MLA_NOPE + MLA_ROPE) ** -0.5 * math.log2(math.e)

_SWA_KV_PAD = 2 * SWA_KV_HEADS * 128

_W_IN_CHUNK = 208
_W_OUT_CHUNK = 256
_STAGE_SLOTS = 4

_V7X_VMEM_BYTES = 64 * 1024 * 1024
_VMEM_LIMIT = _V7X_VMEM_BYTES - 8 * 1024 * 1024

_PROJ_ROWS = 512
_MLA_TILE = 512
_MLA_GROUP = 4
_MLA_TILES_PER_STEP = 4
_TAIL_BLOCKS = 4


def _params(sem):
    return pltpu.CompilerParams(dimension_semantics=sem, vmem_limit_bytes=_VMEM_LIMIT)


def _rms(v, g):
    return (v * lax.rsqrt(jnp.mean(v * v, axis=-1, keepdims=True) + EPS)) * g


def _dot(a, b):
    return jnp.dot(a, b, preferred_element_type=F32)


def _dot_nt(a, b):
    return lax.dot_general(a, b, (((1,), (1,)), ((), ())), preferred_element_type=F32)


def _load_weight_bf16(w_hbm, w_ref, stage_ref, sem_ref):
    slots, rows, _ = stage_ref.shape
    n_chunks = w_ref.shape[0] // rows
    ahead = slots - 1

    def chunk_copy(c):
        return pltpu.make_async_copy(w_hbm.at[pl.ds(c * rows, rows), :], stage_ref.at[c % slots],
                                     sem_ref.at[c % slots])

    for c in range(min(ahead, n_chunks)):
        chunk_copy(c).start()
    for c in range(n_chunks):
        if c + ahead < n_chunks:
            chunk_copy(c + ahead).start()
        chunk_copy(c).wait()
        w_ref[c * rows:(c + 1) * rows, :] = stage_ref[c % slots].astype(BF16)


def _first_grid_step():
    return jnp.logical_and(pl.program_id(0) == 0, pl.program_id(1) == 0)


def _lane_tile(v, n):
    return jnp.concatenate([v] * n, axis=1)


def _chunk_of(pos):
    return jnp.right_shift(pos, CHUNK.bit_length() - 1)


def _rope128(v, cos_t, sin_t):
    return v * cos_t + pltpu.roll(v, MLA_ROPE // 2, 1) * sin_t


def _store_swa_kv(ref, kv):
    low = lax.broadcasted_iota(jnp.int32, kv.shape, 1) < SWA_HEAD_DIM
    swapped = pltpu.roll(kv, SWA_HEAD_DIM, 1)
    pieces = (jnp.where(low, kv, 0.0), jnp.where(low, 0.0, swapped),
              jnp.where(low, swapped, 0.0), jnp.where(low, 0.0, kv))
    for i, piece in enumerate(pieces):
        ref[0, :, i * 128:(i + 1) * 128] = piece.astype(BF16)


def _proj_kernel(x_ref, gin_ref, wt_hbm, gq_ref, gkv_ref, wuq_ref, wukv_ref, cos_ref, sin_ref,
                 q_ref, k_ref, v_ref, qs_ref, z_ref, ks_ref, vs_ref, wt_ref, stage_ref, sem_ref):
    @pl.when(_first_grid_step())
    def _():
        _load_weight_bf16(wt_hbm, wt_ref, stage_ref, sem_ref)

    h = _rms(x_ref[0], gin_ref[...]).astype(BF16)
    cos_t = cos_ref[...]
    sin_t = sin_ref[...]

    def proj(lo, hi):
        return _dot_nt(h, wt_ref[lo:hi, :])

    pa = proj(_R_CQ, _R_KPE)
    cq = _rms(pa[:, :MLA_QLORA], gq_ref[...]).astype(BF16)
    ckv = _rms(pa[:, MLA_QLORA:], gkv_ref[...]).astype(BF16)

    qs_ref[0, :, :SWA_WIDTH] = proj(_R_QSWA, _R_KSWA).astype(BF16)
    qs_ref[0, :, SWA_WIDTH:] = proj(_R_QMEM, _R_ZMEM).astype(BF16)
    z_ref[0, :, :MLA_WIDTH] = proj(_R_ZMLA, _R_QSWA).astype(BF16)
    z_ref[0, :, MLA_WIDTH:MLA_WIDTH + SWA_WIDTH] = proj(_R_ZSWA, _R_QMEM).astype(BF16)
    z_ref[0, :, MLA_WIDTH + SWA_WIDTH:] = proj(_R_ZMEM, IN_WIDTH).astype(BF16)
    kvs = proj(_R_KSWA, _R_ZSWA)
    _store_swa_kv(ks_ref, kvs[:, :SWA_KV_WIDTH])
    _store_swa_kv(vs_ref, kvs[:, SWA_KV_WIDTH:])
    pe = proj(_R_KPE, _R_ZMLA)
    kpe = _rope128(jnp.concatenate([pe, pe], axis=1), cos_t, sin_t).astype(BF16)

    qall = _dot(cq, wuq_ref[...]) * _MLA_Q_SCALE
    kv = _dot(ckv, wukv_ref[...])
    for hd in range(MLA_HEADS):
        oq = hd * (MLA_NOPE + MLA_ROPE)
        q_pe = qall[:, oq + MLA_NOPE:oq + MLA_NOPE + MLA_ROPE]
        q_ref[0, hd, :, :MLA_NOPE] = qall[:, oq:oq + MLA_NOPE].astype(BF16)
        q_ref[0, hd, :, MLA_NOPE:] = _rope128(
            jnp.concatenate([q_pe, q_pe], axis=1), cos_t, sin_t).astype(BF16)
        o = hd * (MLA_NOPE + MLA_V)
        k_ref[0, hd, :, :MLA_NOPE] = kv[:, o:o + MLA_NOPE].astype(BF16)
        k_ref[0, hd, :, MLA_NOPE:] = kpe
        v_ref[0, hd] = kv[:, o + MLA_NOPE:o + MLA_QK_PAD].astype(BF16)


def _proj(x, gin, wt, gq, gkv, wuq, wukv, cos_t, sin_t, tm):
    B, S, _ = x.shape
    const = lambda b, i: (0, 0)
    row = lambda b, i: (b, i, 0)
    single = pl.Buffered(1)

    def out(width):
        return jax.ShapeDtypeStruct((B, S, width), BF16)

    head_spec = pl.BlockSpec((1, MLA_HEADS, tm, MLA_QK_PAD), lambda b, i: (b, 0, i, 0))
    head_shape = jax.ShapeDtypeStruct((B, MLA_HEADS, S, MLA_QK_PAD), BF16)

    return pl.pallas_call(
        _proj_kernel,
        grid=(B, S // tm),
        in_specs=[
            pl.BlockSpec((1, tm, D_MODEL), row),
            pl.BlockSpec((1, D_MODEL), const),
            pl.BlockSpec(memory_space=pl.ANY),
            pl.BlockSpec((1, MLA_QLORA), const),
            pl.BlockSpec((1, MLA_KVLORA), const),
            pl.BlockSpec((MLA_QLORA, MLA_HEADS * (MLA_NOPE + MLA_ROPE)), const, pipeline_mode=single),
            pl.BlockSpec((MLA_KVLORA, MLA_HEADS * (MLA_NOPE + MLA_V)), const, pipeline_mode=single),
            pl.BlockSpec((tm, 128), lambda b, i: (i, 0)),
            pl.BlockSpec((tm, 128), lambda b, i: (i, 0)),
        ],
        out_specs=[
            head_spec, head_spec,
            pl.BlockSpec((1, MLA_HEADS, tm, MLA_V), lambda b, i: (b, 0, i, 0)),
            pl.BlockSpec((1, tm, SWA_WIDTH + MEM_WIDTH), row),
            pl.BlockSpec((1, tm, MIX_WIDTH), row),
            pl.BlockSpec((1, tm, _SWA_KV_PAD), row),
            pl.BlockSpec((1, tm, _SWA_KV_PAD), row),
        ],
        out_shape=[
            head_shape, head_shape, jax.ShapeDtypeStruct((B, MLA_HEADS, S, MLA_V), BF16),
            out(SWA_WIDTH + MEM_WIDTH), out(MIX_WIDTH), out(_SWA_KV_PAD), out(_SWA_KV_PAD),
        ],
        scratch_shapes=[
            pltpu.VMEM((IN_WIDTH, D_MODEL), BF16),
            pltpu.VMEM((_STAGE_SLOTS, _W_IN_CHUNK, D_MODEL), F32),
            pltpu.SemaphoreType.DMA((_STAGE_SLOTS,)),
        ],
        compiler_params=_params(("arbitrary", "arbitrary")),
        name="proj",
    )(x, gin, wt, gq, gkv, wuq, wukv, cos_t, sin_t)


def _mla_kernel(q_ref, k_ref, v_ref, o_ref, acc_ref, m_ref, s_ref, cm_ref, *, t, g, nq):
    qi0 = pl.program_id(2) * nq
    qi_end = qi0 + nq - 1
    acc_ref[...] = jnp.zeros(acc_ref.shape, F32)
    m_ref[...] = jnp.full(m_ref.shape, NEG, F32)
    ones = jnp.ones((t, MLA_V), BF16)
    row_halves = ((0, t // 2, t // 2), (t // 2, t // 2, t))

    def row_max(s):
        return jnp.broadcast_to(jnp.max(s, axis=-1, keepdims=True), (s.shape[0], 128))

    def diagonal_mask():
        q_chunk = _chunk_of(lax.broadcasted_iota(jnp.int32, (t, t), 0))
        k_chunk = _chunk_of(lax.broadcasted_iota(jnp.int32, (t, t), 1))
        return k_chunk <= q_chunk

    def scores(qi, c, slot, hd, mask=None):
        rows = pl.multiple_of((qi - qi0) * t, t)
        keys = pl.multiple_of(c * t, t)
        for r0, nr, nk in (row_halves if mask is not None else ((0, t, t),)):
            s = _dot_nt(q_ref[0, hd, pl.ds(pl.multiple_of(rows + r0, nr), nr), :],
                        k_ref[0, hd, pl.ds(keys, nk), :])
            if mask is not None:
                s = jnp.where(mask[r0:r0 + nr, :nk], s, NEG)
            s_ref[slot, hd, r0:r0 + nr, :nk] = s
            cm_ref[slot, hd, r0:r0 + nr] = row_max(s)

    def update(c, slot, hd, diagonal, mask):
        keys = pl.multiple_of(c * t, t)
        for r0, nr, nk in (row_halves if diagonal else ((0, t, t),)):
            rows = slice(r0, r0 + nr)
            s = s_ref[slot, hd, rows, :nk]
            if mask is not None:
                s = jnp.where(mask[rows, :nk], s, NEG)
                m_cur = row_max(s)
            else:
                m_cur = cm_ref[slot, hd, rows]
            m_old = m_ref[hd, rows]
            m_new = jnp.maximum(m_old, m_cur)
            alpha = jnp.exp2(m_old - m_new)
            p = jnp.exp2(s - _lane_tile(m_new, nk // 128)).astype(BF16)
            v_ones = jnp.concatenate([v_ref[0, hd, pl.ds(keys, nk), :], ones[:nk]], axis=1)
            acc_ref[hd, rows] = (_lane_tile(alpha, MLA_QK_PAD // 128) * acc_ref[hd, rows]
                                 + _dot(p, v_ones))
            m_ref[hd, rows] = m_new

    def finish_tile(qi, hd):
        rows = pl.multiple_of((qi - qi0) * t, t)
        acc = acc_ref[hd]
        o_ref[0, pl.ds(rows, t), hd * MLA_V:(hd + 1) * MLA_V] = (
            acc[:, :MLA_V] / acc[:, MLA_V:]).astype(BF16)
        acc_ref[hd] = jnp.zeros(acc.shape, F32)
        m_ref[hd] = jnp.full((t, 128), NEG, F32)

    def following(qi, c):
        is_last = c == qi
        return jnp.where(is_last, jnp.minimum(qi + 1, qi_end), qi), jnp.where(is_last, 0, c + 1)

    def step(qi, c, slot, diagonal, premasked=False, mask_next=False):
        qn, cn = following(qi, c)
        mask = diagonal_mask() if diagonal and not premasked else None
        next_mask = diagonal_mask() if mask_next else None
        for hd in range(g):
            scores(qn, cn, 1 - slot, hd, next_mask)
            update(c, slot, hd, diagonal, mask)
            if diagonal:
                finish_tile(qi, hd)

    for hd in range(g):
        scores(qi0, 0, 0, hd)

    def body(_, pair):
        first, second = pair, following(*pair)
        diag0 = first[1] == first[0]
        diag1 = second[1] == second[0]
        for d0, d1, cond in ((True, False, diag0), (False, True, diag1),
                             (False, False, jnp.logical_not(jnp.logical_or(diag0, diag1)))):
            @pl.when(cond)
            def _(d0=d0, d1=d1):
                step(*first, 0, d0, mask_next=d1)
                step(*second, 1, d1, premasked=d1)
        return following(*second)

    n_steps = nq * qi0 + nq * (nq + 1) // 2
    lax.fori_loop(0, n_steps // 2, body, (qi0, 0))


def _mla(q, k, v, t, g, nq):
    B, _, S, _ = q.shape
    assert S % (nq * t) == 0 and nq % 2 == 0 and (nq * (nq + 1) // 2) % 2 == 0
    return pl.pallas_call(
        functools.partial(_mla_kernel, t=t, g=g, nq=nq),
        grid=(B, MLA_HEADS // g, S // (nq * t)),
        in_specs=[
            pl.BlockSpec((1, g, nq * t, MLA_QK_PAD), lambda b, h, i: (b, h, i, 0)),
            pl.BlockSpec((1, g, S, MLA_QK_PAD), lambda b, h, i: (b, h, 0, 0)),
            pl.BlockSpec((1, g, S, MLA_V), lambda b, h, i: (b, h, 0, 0)),
        ],
        out_specs=pl.BlockSpec((1, nq * t, g * MLA_V), lambda b, h, i: (b, i, h)),
        out_shape=jax.ShapeDtypeStruct((B, S, MLA_WIDTH), BF16),
        scratch_shapes=[
            pltpu.VMEM((g, t, MLA_QK_PAD), F32),
            pltpu.VMEM((g, t, 128), F32),
            pltpu.VMEM((2, g, t, t), F32),
            pltpu.VMEM((2, g, t, 128), F32),
        ],
        compiler_params=_params(("arbitrary", "arbitrary", "arbitrary")),
        name="mla",
    )(q, k, v)


_REL_SPAN = 4 * SWA_BLOCK
_N_OFFSETS = 3 * SWA_BLOCK - 1


def _bias_memkv_kernel(table_ref, bucket_ref, mem_hbm, g_ref, w_hbm, o_ref, k_ref, v_ref,
                       spread_ref, mem_ref, w_ref, sem_ref):
    n_batch = mem_ref.shape[0]
    half = w_ref.shape[0] // 2
    fetches = [pltpu.make_async_copy(mem_hbm.at[b], mem_ref.at[b], sem_ref.at[b]) for b in range(n_batch)]
    fetches += [pltpu.make_async_copy(w_hbm.at[pl.ds(i * half, half), :], w_ref.at[pl.ds(i * half, half), :],
                                      sem_ref.at[n_batch + i]) for i in range(2)]
    for fetch in fetches:
        fetch.start()

    for b in range(N_BUCKETS):
        spread_ref[b] = jnp.concatenate(
            [jnp.full((1, 128), table_ref[b, hd], F32) for hd in range(SWA_HEADS)], axis=0)
    lane = lax.broadcasted_iota(jnp.int32, (SWA_HEADS, _REL_SPAN), 1)

    def lookup(c, per_offset):
        col = spread_ref[bucket_ref[c]]
        return jnp.where(lane == c, _lane_tile(col, _REL_SPAN // 128), per_offset)

    per_offset_all = lax.fori_loop(0, _N_OFFSETS, lookup, jnp.zeros((SWA_HEADS, _REL_SPAN), F32))
    for hd in range(SWA_HEADS):
        per_offset = per_offset_all[hd:hd + 1, :]
        rows = jnp.broadcast_to(per_offset, (SWA_BLOCK, _REL_SPAN))
        band = pltpu.roll(rows, _REL_SPAN - (SWA_BLOCK - 1), 1, stride=1, stride_axis=0)
        kvh, within = divmod(hd, SWA_GROUP)
        par, half = within % 2, within // 2
        o_ref[kvh, par, half * SWA_BLOCK:(half + 1) * SWA_BLOCK, :] = band[:, :2 * SWA_BLOCK]

    for fetch in fetches:
        fetch.wait()
    w = w_ref[...].astype(BF16)
    for b in range(n_batch):
        kv = _dot(_rms(mem_ref[b], g_ref[...]).astype(BF16), w)
        k_ref[b] = kv[:, :MEM_WIDTH].astype(BF16)
        v_ref[b] = kv[:, MEM_WIDTH:].astype(BF16)


def _bias_memkv(rel_table, bucket, mem, g, w):
    B, M, _ = mem.shape
    smem = pl.BlockSpec(memory_space=pltpu.SMEM)
    vmem = pl.BlockSpec(memory_space=pltpu.VMEM)
    hbm = pl.BlockSpec(memory_space=pl.ANY)
    return pl.pallas_call(
        _bias_memkv_kernel,
        in_specs=[smem, smem, hbm, vmem, hbm],
        out_specs=[vmem] * 3,
        out_shape=[jax.ShapeDtypeStruct((SWA_KV_HEADS, 2, 2 * SWA_BLOCK, 2 * SWA_BLOCK), F32),
                   jax.ShapeDtypeStruct((B, M, MEM_WIDTH), BF16),
                   jax.ShapeDtypeStruct((B, M, MEM_WIDTH), BF16)],
        scratch_shapes=[
            pltpu.VMEM((N_BUCKETS, SWA_HEADS, 128), F32),
            pltpu.VMEM(mem.shape, F32),
            pltpu.VMEM(w.shape, F32),
            pltpu.SemaphoreType.DMA((B + 2,)),
        ],
        compiler_params=pltpu.CompilerParams(vmem_limit_bytes=_VMEM_LIMIT),
        name="t5bias_memkv",
    )(rel_table, bucket, mem, g, w)


def _swa_into(y_ref, sink_ref, q_ref, kp_ref, kc_ref, vp_ref, vc_ref, bias_ref, kb_ref, vb_ref, nsub,
              beside_unit):
    t = pl.program_id(1)
    scale = SWA_HEAD_DIM ** -0.5
    sb = SWA_BLOCK
    kb_ref[:sb] = kp_ref[0]
    kb_ref[sb:] = kc_ref[0]
    vb_ref[:sb] = vp_ref[0]
    vb_ref[sb:] = vc_ref[0]

    row = lax.broadcasted_iota(jnp.int32, (2 * sb, 2 * sb), 0)
    q_chunk = _chunk_of(row & (sb - 1))
    b_chunk = _chunk_of(lax.broadcasted_iota(jnp.int32, (2 * sb, 2 * sb), 1))
    valid_any = jnp.logical_and(b_chunk >= q_chunk, b_chunk <= q_chunk + WINDOW_CHUNKS)
    first_lo = jnp.where(t > 0, 0, 2)
    valid_first = jnp.logical_and(valid_any, b_chunk >= first_lo)
    upper = lax.broadcasted_iota(jnp.int32, (2 * sb, 1), 0) < sb

    def geometry(u):
        r, kvh = divmod(u, SWA_KV_HEADS)
        rows = slice(r * sb, (r + 1) * sb)
        band = slice(r * sb, (r + 2) * sb)
        pair0 = slice(2 * kvh * 128, (2 * kvh + 1) * 128)
        pair1 = slice((2 * kvh + 1) * 128, (2 * kvh + 2) * 128)
        return r, kvh, rows, band, pair0, pair1

    def scores(u):
        r, kvh, rows, band, pair0, pair1 = geometry(u)
        q = jnp.concatenate([q_ref[0, rows, pair0], q_ref[0, rows, pair1]], axis=0)
        return [_dot_nt(q, kb_ref[band, (2 * kvh + par) * 128:(2 * kvh + par + 1) * 128])
                for par in range(2)]

    def attend(u, unit_scores):
        r, kvh, rows, band, pair0, pair1 = geometry(u)
        valid = valid_first if r == 0 else valid_any
        o = None
        for par, s in enumerate(unit_scores):
            kcols = slice((2 * kvh + par) * 128, (2 * kvh + par + 1) * 128)
            hd = SWA_GROUP * kvh + par
            s = jnp.where(valid, s * scale + bias_ref[kvh, par], NEG)
            sink = jnp.where(upper, sink_ref[hd], sink_ref[hd + 2])
            m = jnp.maximum(jnp.max(s, axis=-1, keepdims=True), sink)
            p = jnp.exp(s - m)
            den = jnp.sum(p, axis=-1, keepdims=True) + jnp.exp(sink - m)
            pv = _dot((p * (1.0 / den)).astype(BF16), vb_ref[band, kcols])
            o = pv if o is None else o + pv
        y_ref[rows, pair0] = o[:sb]
        y_ref[rows, pair1] = o[sb:]

    n_units = nsub * SWA_KV_HEADS
    pending = scores(0)
    for u in range(n_units):
        upcoming = scores(u + 1) if u + 1 < n_units else None
        beside_unit(u)
        attend(u, pending)
        pending = upcoming


def _mem_into(y_ref, q_ref, k_ref, v_ref, after_head):
    scale = MEM_HEAD_DIM ** -0.5

    def head_cols(hd):
        return slice(hd * MEM_HEAD_DIM, (hd + 1) * MEM_HEAD_DIM)

    def scores(hd):
        return _dot_nt(q_ref[0, :, head_cols(hd)], k_ref[0, :, head_cols(hd)])

    pending = scores(0)
    for hd in range(MEM_HEADS):
        upcoming = scores(hd + 1) if hd + 1 < MEM_HEADS else None
        s = pending * scale
        m = jnp.max(s, axis=-1, keepdims=True)
        p = jnp.exp(s - m)
        inv = 1.0 / jnp.sum(p, axis=-1, keepdims=True)
        y_ref[:, head_cols(hd)] = _dot((p * inv).astype(BF16), v_ref[0, :, head_cols(hd)])
        after_head(hd)
        pending = upcoming


def _tail_kernel(sink_ref, x_ref, ya_ref, za_ref, zb_ref, zc_ref, qs_ref, qm_ref,
                 kp_ref, kc_ref, vp_ref, vc_ref, bias_ref, km_ref, vm_ref, w_hbm, g_ref,
                 o_ref, kb_ref, vb_ref, yb_ref, yc_ref, yacc_ref, w_ref, stage_ref, sem_ref, *, nsub):
    @pl.when(_first_grid_step())
    def _():
        _load_weight_bf16(w_hbm, w_ref, stage_ref, sem_ref)

    def gated(y, z_ref):
        h = 0.5 * z_ref[0].astype(F32)
        return (y * (h + h * jnp.tanh(h))).astype(BF16)

    swa_lo, mem_lo = MLA_WIDTH, MLA_WIDTH + SWA_WIDTH

    g_mla = gated(ya_ref[0].astype(F32), za_ref)
    mla_chunk = D_MODEL // (nsub * SWA_KV_HEADS)

    def project_mla(u):
        cols = slice(u * mla_chunk, (u + 1) * mla_chunk)
        yacc_ref[:, cols] = _dot(g_mla, w_ref[:swa_lo, cols])

    _swa_into(yb_ref, sink_ref, qs_ref, kp_ref, kc_ref, vp_ref, vc_ref, bias_ref, kb_ref, vb_ref, nsub,
              project_mla)

    g_swa = gated(yb_ref[...], zb_ref)
    swa_chunk = D_MODEL // MEM_HEADS

    def project_swa(u):
        cols = slice(u * swa_chunk, (u + 1) * swa_chunk)
        yacc_ref[:, cols] += _dot(g_swa, w_ref[swa_lo:mem_lo, cols])

    _mem_into(yc_ref, qm_ref, km_ref, vm_ref, project_swa)

    g_mem = gated(yc_ref[...], zc_ref)
    sum_sq = jnp.zeros((g_mem.shape[0], 1), F32)
    chunks = [slice(j * swa_chunk, (j + 1) * swa_chunk) for j in range(D_MODEL // swa_chunk)]
    for cols in chunks:
        r = x_ref[0, :, cols] + yacc_ref[:, cols] + _dot(g_mem, w_ref[mem_lo:, cols])
        o_ref[0, :, cols] = r
        sum_sq = sum_sq + jnp.sum(r * r, axis=-1, keepdims=True)
    inv = lax.rsqrt(sum_sq * (1.0 / D_MODEL) + EPS)
    for cols in chunks:
        o_ref[0, :, cols] = (o_ref[0, :, cols] * inv) * g_ref[:, cols]


def _tail(sinks, x, y_mla, z, qs, ks, vs, bias, kmem, vmem, w, g, nsub):
    B, S, _ = x.shape
    tm = nsub * SWA_BLOCK
    M = kmem.shape[1]
    row = lambda b, i: (b, i, 0)
    prev = lambda b, i: (b, jnp.maximum(i * nsub - 1, 0), 0)
    col = lambda c: (lambda b, i: (b, i, c))
    return pl.pallas_call(
        functools.partial(_tail_kernel, nsub=nsub),
        grid=(B, S // tm),
        in_specs=[
            pl.BlockSpec(memory_space=pltpu.SMEM),
            pl.BlockSpec((1, tm, D_MODEL), row),
            pl.BlockSpec((1, tm, MLA_WIDTH), row),
            pl.BlockSpec((1, tm, MLA_WIDTH), col(0)),
            pl.BlockSpec((1, tm, SWA_WIDTH), col(MLA_WIDTH // SWA_WIDTH)),
            pl.BlockSpec((1, tm, MEM_WIDTH), col((MLA_WIDTH + SWA_WIDTH) // MEM_WIDTH)),
            pl.BlockSpec((1, tm, SWA_WIDTH), col(0)),
            pl.BlockSpec((1, tm, MEM_WIDTH), col(1)),
            pl.BlockSpec((1, SWA_BLOCK, _SWA_KV_PAD), prev),
            pl.BlockSpec((1, tm, _SWA_KV_PAD), row),
            pl.BlockSpec((1, SWA_BLOCK, _SWA_KV_PAD), prev),
            pl.BlockSpec((1, tm, _SWA_KV_PAD), row),
            pl.BlockSpec((SWA_KV_HEADS, 2, 2 * SWA_BLOCK, 2 * SWA_BLOCK), lambda b, i: (0, 0, 0, 0)),
            pl.BlockSpec((1, M, MEM_WIDTH), lambda b, i: (b, 0, 0)),
            pl.BlockSpec((1, M, MEM_WIDTH), lambda b, i: (b, 0, 0)),
            pl.BlockSpec(memory_space=pl.ANY),
            pl.BlockSpec((1, D_MODEL), lambda b, i: (0, 0)),
        ],
        out_specs=pl.BlockSpec((1, tm, D_MODEL), row),
        out_shape=jax.ShapeDtypeStruct((B, S, D_MODEL), F32),
        scratch_shapes=[
            pltpu.VMEM((tm + SWA_BLOCK, _SWA_KV_PAD), BF16),
            pltpu.VMEM((tm + SWA_BLOCK, _SWA_KV_PAD), BF16),
            pltpu.VMEM((tm, SWA_WIDTH), F32),
            pltpu.VMEM((tm, MEM_WIDTH), F32),
            pltpu.VMEM((tm, D_MODEL), F32),
            pltpu.VMEM((MIX_WIDTH, D_MODEL), BF16),
            pltpu.VMEM((_STAGE_SLOTS, _W_OUT_CHUNK, D_MODEL), F32),
            pltpu.SemaphoreType.DMA((_STAGE_SLOTS,)),
        ],
        compiler_params=_params(("arbitrary", "arbitrary")),
        name="tail",
    )(sinks, x, y_mla, z, z, z, qs, qs, ks, ks, vs, vs, bias, kmem, vmem, w, g)


def _rope_tables(seq):
    inv = 1.0 / (ROPE_THETA ** (jnp.arange(0, MLA_ROPE, 2, dtype=F32) / MLA_ROPE))
    step = 1 << ((seq - 1).bit_length() // 2)
    assert seq % step == 0
    fine = jnp.arange(step, dtype=F32)[:, None] * inv[None, :]
    coarse = (jnp.arange(seq // step, dtype=F32) * step)[:, None] * inv[None, :]
    cf, sf = jnp.cos(fine)[None], jnp.sin(fine)[None]
    cc, sc = jnp.cos(coarse)[:, None], jnp.sin(coarse)[:, None]
    cos = (cc * cf - sc * sf).reshape(seq, -1)
    sin = (sc * cf + cc * sf).reshape(seq, -1)
    zero = jnp.zeros_like(cos)
    return (jnp.concatenate([cos, cos, zero, zero], axis=-1),
            jnp.concatenate([-sin, sin, zero, zero], axis=-1))


def _t5_bucket(rel):
    nb = N_BUCKETS // 2
    max_exact = nb // 2
    bucket = jnp.where(rel > 0, nb, 0)
    n = jnp.abs(rel)
    nf = jnp.maximum(n, 1).astype(F32)
    large = max_exact + (jnp.log(nf / max_exact) / math.log(MAX_DISTANCE / max_exact)
                         * (nb - max_exact)).astype(jnp.int32)
    large = jnp.minimum(large, nb - 1)
    return bucket + jnp.where(n < max_exact, n, large)


def kernel(x, mem, norm_in, w_in, norm_q, norm_kv, w_uq, w_ukv, attn_sinks, rel_bias,
           norm_mem, w_mem_kv, w_out, norm_final):
    B, S, _ = x.shape
    assert norm_in.shape[0] == 1, "single-layer trunk"
    assert x.shape[2] == D_MODEL and w_in.shape[1:] == (D_MODEL, IN_WIDTH)
    assert S % _PROJ_ROWS == 0 and S % (_TAIL_BLOCKS * SWA_BLOCK) == 0

    wt = w_in[0].T
    cos_t, sin_t = _rope_tables(S)
    q, k, v, qs, z, ks, vs = _proj(x, norm_in, wt, norm_q, norm_kv, w_uq[0].astype(BF16),
                                   w_ukv[0].astype(BF16), cos_t, sin_t, tm=_PROJ_ROWS)

    y_mla = _mla(q, k, v, t=_MLA_TILE, g=_MLA_GROUP, nq=_MLA_TILES_PER_STEP)

    bucket = _t5_bucket(jnp.arange(_REL_SPAN) - (2 * SWA_BLOCK - 1)).astype(jnp.int32)
    bias, kmem, vmem = _bias_memkv(rel_bias, bucket, mem, norm_mem, w_mem_kv[0])
    return _tail(attn_sinks[0], x, y_mla, z, qs, ks, vs, bias, kmem, vmem,
                 w_out[0], norm_final[None, :], nsub=_TAIL_BLOCKS)
```

```python
import functools
import math

import jax
import jax.numpy as jnp
from jax import lax
from jax.experimental import pallas as pl
from jax.experimental.pallas import tpu as pltpu

D_MODEL = 2048
CHUNK = 64
EPS = 1e-6
NEG = -1e30

MLA_HEADS = 8
MLA_NOPE = 128
MLA_ROPE = 64
MLA_V = 128
MLA_QLORA = 512
MLA_KVLORA = 256
MLA_WIDTH = MLA_HEADS * MLA_V
MLA_QK_PAD = 256
ROPE_THETA = 10000.0

SWA_HEADS = 8
SWA_KV_HEADS = 2
SWA_GROUP = SWA_HEADS // SWA_KV_HEADS
SWA_HEAD_DIM = 64
SWA_WIDTH = SWA_HEADS * SWA_HEAD_DIM
SWA_KV_WIDTH = SWA_KV_HEADS * SWA_HEAD_DIM
WINDOW_CHUNKS = 2
SWA_BLOCK = 128

MEM_HEADS = 4
MEM_HEAD_DIM = 128
MEM_WIDTH = MEM_HEADS * MEM_HEAD_DIM

MIX_WIDTH = MLA_WIDTH + SWA_WIDTH + MEM_WIDTH

N_BUCKETS = 32
MAX_DISTANCE = 128

BF16 = jnp.bfloat16
F32 = jnp.float32

_R_CQ = 0
_R_CKV = _R_CQ + MLA_QLORA
_R_KPE = _R_CKV + MLA_KVLORA
_R_ZMLA = _R_KPE + MLA_ROPE
_R_QSWA = _R_ZMLA + MLA_WIDTH
_R_KSWA = _R_QSWA + SWA_WIDTH
_R_VSWA = _R_KSWA + SWA_KV_WIDTH
_R_ZSWA = _R_VSWA + SWA_KV_WIDTH
_R_QMEM = _R_ZSWA + SWA_WIDTH
_R_ZMEM = _R_QMEM + MEM_WIDTH
IN_WIDTH = _R_ZMEM + MEM_WIDTH

_MLA_Q_SCALE = (MLA_NOPE + MLA_ROPE) ** -0.5 * math.log2(math.e)

_SWA_KV_PAD = 2 * SWA_KV_HEADS * 128

_W_IN_CHUNK = 208
_W_OUT_CHUNK = 256
_STAGE_SLOTS = 4

_V7X_VMEM_BYTES = 64 * 1024 * 1024
_VMEM_LIMIT = _V7X_VMEM_BYTES - 8 * 1024 * 1024

_PROJ_ROWS = 512
_MLA_TILE = 512
_MLA_GROUP = 4
_MLA_TILES_PER_STEP = 4
_TAIL_BLOCKS = 4


def _params(sem):
    return pltpu.CompilerParams(dimension_semantics=sem, vmem_limit_bytes=_VMEM_LIMIT)


def _rms(v, g):
    return (v * lax.rsqrt(jnp.mean(v * v, axis=-1, keepdims=True) + EPS)) * g


def _dot(a, b):
    return jnp.dot(a, b, preferred_element_type=F32)


def _dot_nt(a, b):
    return lax.dot_general(a, b, (((1,), (1,)), ((), ())), preferred_element_type=F32)


def _load_weight_bf16(w_hbm, w_ref, stage_ref, sem_ref):
    slots, rows, _ = stage_ref.shape
    n_chunks = w_ref.shape[0] // rows
    ahead = slots - 1

    def chunk_copy(c):
        return pltpu.make_async_copy(w_hbm.at[pl.ds(c * rows, rows), :], stage_ref.at[c % slots],
                                     sem_ref.at[c % slots])

    for c in range(min(ahead, n_chunks)):
        chunk_copy(c).start()
    for c in range(n_chunks):
        if c + ahead < n_chunks:
            chunk_copy(c + ahead).start()
        chunk_copy(c).wait()
        w_ref[c * rows:(c + 1) * rows, :] = stage_ref[c % slots].astype(BF16)


def _first_grid_step():
    return jnp.logical_and(pl.program_id(0) == 0, pl.program_id(1) == 0)


def _lane_tile(v, n):
    return jnp.concatenate([v] * n, axis=1)


def _chunk_of(pos):
    return jnp.right_shift(pos, CHUNK.bit_length() - 1)


def _rope128(v, cos_t, sin_t):
    return v * cos_t + pltpu.roll(v, MLA_ROPE // 2, 1) * sin_t


def _store_swa_kv(ref, kv):
    low = lax.broadcasted_iota(jnp.int32, kv.shape, 1) < SWA_HEAD_DIM
    swapped = pltpu.roll(kv, SWA_HEAD_DIM, 1)
    pieces = (jnp.where(low, kv, 0.0), jnp.where(low, 0.0, swapped),
              jnp.where(low, swapped, 0.0), jnp.where(low, 0.0, kv))
    for i, piece in enumerate(pieces):
        ref[0, :, i * 128:(i + 1) * 128] = piece.astype(BF16)


def _proj_kernel(x_ref, gin_ref, wt_hbm, gq_ref, gkv_ref, wuq_ref, wukv_ref, cos_ref, sin_ref,
                 q_ref, k_ref, v_ref, qs_ref, z_ref, ks_ref, vs_ref, wt_ref, stage_ref, sem_ref):
    @pl.when(_first_grid_step())
    def _():
        _load_weight_bf16(wt_hbm, wt_ref, stage_ref, sem_ref)

    h = _rms(x_ref[0], gin_ref[...]).astype(BF16)
    cos_t = cos_ref[...]
    sin_t = sin_ref[...]

    def proj(lo, hi):
        return _dot_nt(h, wt_ref[lo:hi, :])

    pa = proj(_R_CQ, _R_KPE)
    cq = _rms(pa[:, :MLA_QLORA], gq_ref[...]).astype(BF16)
    ckv = _rms(pa[:, MLA_QLORA:], gkv_ref[...]).astype(BF16)

    qs_ref[0, :, :SWA_WIDTH] = proj(_R_QSWA, _R_KSWA).astype(BF16)
    qs_ref[0, :, SWA_WIDTH:] = proj(_R_QMEM, _R_ZMEM).astype(BF16)
    z_ref[0, :, :MLA_WIDTH] = proj(_R_ZMLA, _R_QSWA).astype(BF16)
    z_ref[0, :, MLA_WIDTH:MLA_WIDTH + SWA_WIDTH] = proj(_R_ZSWA, _R_QMEM).astype(BF16)
    z_ref[0, :, MLA_WIDTH + SWA_WIDTH:] = proj(_R_ZMEM, IN_WIDTH).astype(BF16)
    kvs = proj(_R_KSWA, _R_ZSWA)
    _store_swa_kv(ks_ref, kvs[:, :SWA_KV_WIDTH])
    _store_swa_kv(vs_ref, kvs[:, SWA_KV_WIDTH:])
    pe = proj(_R_KPE, _R_ZMLA)
    kpe = _rope128(jnp.concatenate([pe, pe], axis=1), cos_t, sin_t).astype(BF16)

    qall = _dot(cq, wuq_ref[...]) * _MLA_Q_SCALE
    kv = _dot(ckv, wukv_ref[...])
    for hd in range(MLA_HEADS):
        oq = hd * (MLA_NOPE + MLA_ROPE)
        q_pe = qall[:, oq + MLA_NOPE:oq + MLA_NOPE + MLA_ROPE]
        q_ref[0, hd, :, :MLA_NOPE] = qall[:, oq:oq + MLA_NOPE].astype(BF16)
        q_ref[0, hd, :, MLA_NOPE:] = _rope128(
            jnp.concatenate([q_pe, q_pe], axis=1), cos_t, sin_t).astype(BF16)
        o = hd * (MLA_NOPE + MLA_V)
        k_ref[0, hd, :, :MLA_NOPE] = kv[:, o:o + MLA_NOPE].astype(BF16)
        k_ref[0, hd, :, MLA_NOPE:] = kpe
        v_ref[0, hd] = kv[:, o + MLA_NOPE:o + MLA_QK_PAD].astype(BF16)


def _proj(x, gin, wt, gq, gkv, wuq, wukv, cos_t, sin_t, tm):
    B, S, _ = x.shape
    const = lambda b, i: (0, 0)
    row = lambda b, i: (b, i, 0)
    single = pl.Buffered(1)

    def out(width):
        return jax.ShapeDtypeStruct((B, S, width), BF16)

    head_spec = pl.BlockSpec((1, MLA_HEADS, tm, MLA_QK_PAD), lambda b, i: (b, 0, i, 0))
    head_shape = jax.ShapeDtypeStruct((B, MLA_HEADS, S, MLA_QK_PAD), BF16)

    return pl.pallas_call(
        _proj_kernel,
        grid=(B, S // tm),
        in_specs=[
            pl.BlockSpec((1, tm, D_MODEL), row),
            pl.BlockSpec((1, D_MODEL), const),
            pl.BlockSpec(memory_space=pl.ANY),
            pl.BlockSpec((1, MLA_QLORA), const),
            pl.BlockSpec((1, MLA_KVLORA), const),
            pl.BlockSpec((MLA_QLORA, MLA_HEADS * (MLA_NOPE + MLA_ROPE)), const, pipeline_mode=single),
            pl.BlockSpec((MLA_KVLORA, MLA_HEADS * (MLA_NOPE + MLA_V)), const, pipeline_mode=single),
            pl.BlockSpec((tm, 128), lambda b, i: (i, 0)),
            pl.BlockSpec((tm, 128), lambda b, i: (i, 0)),
        ],
        out_specs=[
            head_spec, head_spec,
            pl.BlockSpec((1, MLA_HEADS, tm, MLA_V), lambda b, i: (b, 0, i, 0)),
            pl.BlockSpec((1, tm, SWA_WIDTH + MEM_WIDTH), row),
            pl.BlockSpec((1, tm, MIX_WIDTH), row),
            pl.BlockSpec((1, tm, _SWA_KV_PAD), row),
            pl.BlockSpec((1, tm, _SWA_KV_PAD), row),
        ],
        out_shape=[
            head_shape, head_shape, jax.ShapeDtypeStruct((B, MLA_HEADS, S, MLA_V), BF16),
            out(SWA_WIDTH + MEM_WIDTH), out(MIX_WIDTH), out(_SWA_KV_PAD), out(_SWA_KV_PAD),
        ],
        scratch_shapes=[
            pltpu.VMEM((IN_WIDTH, D_MODEL), BF16),
            pltpu.VMEM((_STAGE_SLOTS, _W_IN_CHUNK, D_MODEL), F32),
            pltpu.SemaphoreType.DMA((_STAGE_SLOTS,)),
        ],
        compiler_params=_params(("arbitrary", "arbitrary")),
        name="proj",
    )(x, gin, wt, gq, gkv, wuq, wukv, cos_t, sin_t)


def _mla_kernel(q_ref, k_ref, v_ref, o_ref, acc_ref, m_ref, s_ref, cm_ref, *, t, g, nq):
    qi0 = pl.program_id(2) * nq
    qi_end = qi0 + nq - 1
    acc_ref[...] = jnp.zeros(acc_ref.shape, F32)
    m_ref[...] = jnp.full(m_ref.shape, NEG, F32)
    ones = jnp.ones((t, MLA_V), BF16)
    row_halves = ((0, t // 2, t // 2), (t // 2, t // 2, t))

    def row_max(s):
        return jnp.broadcast_to(jnp.max(s, axis=-1, keepdims=True), (s.shape[0], 128))

    def diagonal_mask():
        q_chunk = _chunk_of(lax.broadcasted_iota(jnp.int32, (128, 128), 0))
        k_chunk = _chunk_of(lax.broadcasted_iota(jnp.int32, (128, 128), 1))
        return k_chunk <= q_chunk

    def masked_band(band, mask):
        width = band.shape[1]
        block = jnp.where(mask, band[:, width - 128:], NEG)
        return block if width == 128 else jnp.concatenate([band[:, :width - 128], block], axis=1)

    def scores(qi, c, slot, hd, mask=None):
        rows = pl.multiple_of((qi - qi0) * t, t)
        keys = pl.multiple_of(c * t, t)
        if mask is None:
            s = _dot_nt(q_ref[0, hd, pl.ds(rows, t), :], k_ref[0, hd, pl.ds(keys, t), :])
            s_ref[slot, hd] = s
            cm_ref[slot, hd] = row_max(s)
            return
        for r0, nr, nk in row_halves:
            s = _dot_nt(q_ref[0, hd, pl.ds(pl.multiple_of(rows + r0, nr), nr), :],
                        k_ref[0, hd, pl.ds(keys, nk), :])
            for a in range(0, nr, 128):
                reach = r0 + a + 128
                band = masked_band(s[a:a + 128, :reach], mask)
                s_ref[slot, hd, r0 + a:reach, :reach] = band
                cm_ref[slot, hd, r0 + a:reach] = row_max(band)

    def update(c, slot, hd, diagonal, mask):
        keys = pl.multiple_of(c * t, t)
        if not diagonal:
            m_old = m_ref[hd]
            m_new = jnp.maximum(m_old, cm_ref[slot, hd])
            alpha = jnp.exp2(m_old - m_new)
            p = jnp.exp2(s_ref[slot, hd] - _lane_tile(m_new, t // 128)).astype(BF16)
            v_ones = jnp.concatenate([v_ref[0, hd, pl.ds(keys, t), :], ones], axis=1)
            acc_ref[hd] = _lane_tile(alpha, MLA_QK_PAD // 128) * acc_ref[hd] + _dot(p, v_ones)
            m_ref[hd] = m_new
            return
        for r0, nr, nk in row_halves:
            alphas, probs = [], []
            for a in range(0, nr, 128):
                reach = r0 + a + 128
                band = s_ref[slot, hd, r0 + a:reach, :reach]
                if mask is not None:
                    band = masked_band(band, mask)
                    m_cur = row_max(band)
                else:
                    m_cur = cm_ref[slot, hd, r0 + a:reach]
                m_old = m_ref[hd, r0 + a:reach]
                m_new = jnp.maximum(m_old, m_cur)
                alphas.append(jnp.exp2(m_old - m_new))
                p = jnp.exp2(band - _lane_tile(m_new, reach // 128)).astype(BF16)
                if reach < nk:
                    p = jnp.concatenate([p, jnp.zeros((128, nk - reach), BF16)], axis=1)
                probs.append(p)
                m_ref[hd, r0 + a:reach] = m_new
            rows = slice(r0, r0 + nr)
            v_ones = jnp.concatenate([v_ref[0, hd, pl.ds(keys, nk), :], ones[:nk]], axis=1)
            acc_ref[hd, rows] = (
                _lane_tile(jnp.concatenate(alphas, axis=0), MLA_QK_PAD // 128) * acc_ref[hd, rows]
                + _dot(jnp.concatenate(probs, axis=0), v_ones))

    def finish_tile(qi, hd):
        rows = pl.multiple_of((qi - qi0) * t, t)
        acc = acc_ref[hd]
        o_ref[0, pl.ds(rows, t), hd * MLA_V:(hd + 1) * MLA_V] = (
            acc[:, :MLA_V] / acc[:, MLA_V:]).astype(BF16)
        acc_ref[hd] = jnp.zeros(acc.shape, F32)
        m_ref[hd] = jnp.full((t, 128), NEG, F32)

    def following(qi, c):
        is_last = c == qi
        return jnp.where(is_last, jnp.minimum(qi + 1, qi_end), qi), jnp.where(is_last, 0, c + 1)

    def step(qi, c, slot, diagonal, premasked=False, mask_next=False):
        qn, cn = following(qi, c)
        mask = diagonal_mask() if diagonal and not premasked else None
        next_mask = diagonal_mask() if mask_next else None
        for hd in range(g):
            scores(qn, cn, 1 - slot, hd, next_mask)
            update(c, slot, hd, diagonal, mask)
            if diagonal:
                finish_tile(qi, hd)

    for hd in range(g):
        scores(qi0, 0, 0, hd)

    def body(_, pair):
        first, second = pair, following(*pair)
        diag0 = first[1] == first[0]
        diag1 = second[1] == second[0]
        for d0, d1, cond in ((True, False, diag0), (False, True, diag1),
                             (False, False, jnp.logical_not(jnp.logical_or(diag0, diag1)))):
            @pl.when(cond)
            def _(d0=d0, d1=d1):
                step(*first, 0, d0, mask_next=d1)
                step(*second, 1, d1, premasked=d1)
        return following(*second)

    n_steps = nq * qi0 + nq * (nq + 1) // 2
    lax.fori_loop(0, n_steps // 2, body, (qi0, 0))


def _mla(q, k, v, t, g, nq):
    B, _, S, _ = q.shape
    assert S % (nq * t) == 0 and nq % 2 == 0 and (nq * (nq + 1) // 2) % 2 == 0
    return pl.pallas_call(
        functools.partial(_mla_kernel, t=t, g=g, nq=nq),
        grid=(B, MLA_HEADS // g, S // (nq * t)),
        in_specs=[
            pl.BlockSpec((1, g, nq * t, MLA_QK_PAD), lambda b, h, i: (b, h, i, 0)),
            pl.BlockSpec((1, g, S, MLA_QK_PAD), lambda b, h, i: (b, h, 0, 0)),
            pl.BlockSpec((1, g, S, MLA_V), lambda b, h, i: (b, h, 0, 0)),
        ],
        out_specs=pl.BlockSpec((1, nq * t, g * MLA_V), lambda b, h, i: (b, i, h)),
        out_shape=jax.ShapeDtypeStruct((B, S, MLA_WIDTH), BF16),
        scratch_shapes=[
            pltpu.VMEM((g, t, MLA_QK_PAD), F32),
            pltpu.VMEM((g, t, 128), F32),
            pltpu.VMEM((2, g, t, t), F32),
            pltpu.VMEM((2, g, t, 128), F32),
        ],
        compiler_params=_params(("arbitrary", "arbitrary", "arbitrary")),
        name="mla",
    )(q, k, v)


_REL_SPAN = 4 * SWA_BLOCK
_N_OFFSETS = 3 * SWA_BLOCK - 1


def _bias_memkv_kernel(table_ref, bucket_ref, mem_hbm, g_ref, w_hbm, o_ref, k_ref, v_ref,
                       spread_ref, mem_ref, w_ref, sem_ref):
    fetches = [pltpu.make_async_copy(mem_hbm, mem_ref, sem_ref.at[0]),
               pltpu.make_async_copy(w_hbm, w_ref, sem_ref.at[1])]
    for fetch in fetches:
        fetch.start()

    for b in range(N_BUCKETS):
        spread_ref[b] = jnp.concatenate(
            [jnp.full((1, 128), table_ref[b, hd], F32) for hd in range(SWA_HEADS)], axis=0)
    lane = lax.broadcasted_iota(jnp.int32, (SWA_HEADS, _REL_SPAN), 1)

    def lookup(c, per_offset):
        col = spread_ref[bucket_ref[c]]
        return jnp.where(lane == c, _lane_tile(col, _REL_SPAN // 128), per_offset)

    per_offset_all = lax.fori_loop(0, _N_OFFSETS, lookup, jnp.zeros((SWA_HEADS, _REL_SPAN), F32))
    for hd in range(SWA_HEADS):
        per_offset = per_offset_all[hd:hd + 1, :]
        rows = jnp.broadcast_to(per_offset, (SWA_BLOCK, _REL_SPAN))
        band = pltpu.roll(rows, _REL_SPAN - (SWA_BLOCK - 1), 1, stride=1, stride_axis=0)
        kvh, within = divmod(hd, SWA_GROUP)
        par, half = within % 2, within // 2
        o_ref[kvh, par, half * SWA_BLOCK:(half + 1) * SWA_BLOCK, :] = band[:, :2 * SWA_BLOCK]

    for fetch in fetches:
        fetch.wait()
    w = w_ref[...].astype(BF16)
    for b in range(mem_ref.shape[0]):
        kv = _dot(_rms(mem_ref[b], g_ref[...]).astype(BF16), w)
        k_ref[b] = kv[:, :MEM_WIDTH].astype(BF16)
        v_ref[b] = kv[:, MEM_WIDTH:].astype(BF16)


def _bias_memkv(rel_table, bucket, mem, g, w):
    B, M, _ = mem.shape
    smem = pl.BlockSpec(memory_space=pltpu.SMEM)
    vmem = pl.BlockSpec(memory_space=pltpu.VMEM)
    hbm = pl.BlockSpec(memory_space=pl.ANY)
    return pl.pallas_call(
        _bias_memkv_kernel,
        in_specs=[smem, smem, hbm, vmem, hbm],
        out_specs=[vmem] * 3,
        out_shape=[jax.ShapeDtypeStruct((SWA_KV_HEADS, 2, 2 * SWA_BLOCK, 2 * SWA_BLOCK), F32),
                   jax.ShapeDtypeStruct((B, M, MEM_WIDTH), BF16),
                   jax.ShapeDtypeStruct((B, M, MEM_WIDTH), BF16)],
        scratch_shapes=[
            pltpu.VMEM((N_BUCKETS, SWA_HEADS, 128), F32),
            pltpu.VMEM(mem.shape, F32),
            pltpu.VMEM(w.shape, F32),
            pltpu.SemaphoreType.DMA((2,)),
        ],
        compiler_params=pltpu.CompilerParams(vmem_limit_bytes=_VMEM_LIMIT),
        name="t5bias_memkv",
    )(rel_table, bucket, mem, g, w)


def _swa_into(y_ref, sink_ref, q_ref, kp_ref, kc_ref, vp_ref, vc_ref, bias_ref, kb_ref, vb_ref, nsub,
              beside_unit):
    t = pl.program_id(1)
    scale = SWA_HEAD_DIM ** -0.5
    sb = SWA_BLOCK
    kb_ref[:sb] = kp_ref[0]
    kb_ref[sb:] = kc_ref[0]
    vb_ref[:sb] = vp_ref[0]
    vb_ref[sb:] = vc_ref[0]

    row = lax.broadcasted_iota(jnp.int32, (2 * sb, 2 * sb), 0)
    q_chunk = _chunk_of(row & (sb - 1))
    b_chunk = _chunk_of(lax.broadcasted_iota(jnp.int32, (2 * sb, 2 * sb), 1))
    valid_any = jnp.logical_and(b_chunk >= q_chunk, b_chunk <= q_chunk + WINDOW_CHUNKS)
    first_lo = jnp.where(t > 0, 0, 2)
    valid_first = jnp.logical_and(valid_any, b_chunk >= first_lo)
    upper = lax.broadcasted_iota(jnp.int32, (2 * sb, 1), 0) < sb

    def geometry(u):
        r, kvh = divmod(u, SWA_KV_HEADS)
        rows = slice(r * sb, (r + 1) * sb)
        band = slice(r * sb, (r + 2) * sb)
        pair0 = slice(2 * kvh * 128, (2 * kvh + 1) * 128)
        pair1 = slice((2 * kvh + 1) * 128, (2 * kvh + 2) * 128)
        return r, kvh, rows, band, pair0, pair1

    def scores(u):
        r, kvh, rows, band, pair0, pair1 = geometry(u)
        q = jnp.concatenate([q_ref[0, rows, pair0], q_ref[0, rows, pair1]], axis=0)
        return [_dot_nt(q, kb_ref[band, (2 * kvh + par) * 128:(2 * kvh + par + 1) * 128])
                for par in range(2)]

    def attend(u, unit_scores):
        r, kvh, rows, band, pair0, pair1 = geometry(u)
        valid = valid_first if r == 0 else valid_any
        o = None
        for par, s in enumerate(unit_scores):
            kcols = slice((2 * kvh + par) * 128, (2 * kvh + par + 1) * 128)
            hd = SWA_GROUP * kvh + par
            s = jnp.where(valid, s * scale + bias_ref[kvh, par], NEG)
            sink = jnp.where(upper, sink_ref[hd], sink_ref[hd + 2])
            m = jnp.maximum(jnp.max(s, axis=-1, keepdims=True), sink)
            p = jnp.exp(s - m)
            den = jnp.sum(p, axis=-1, keepdims=True) + jnp.exp(sink - m)
            pv = _dot((p * (1.0 / den)).astype(BF16), vb_ref[band, kcols])
            o = pv if o is None else o + pv
        y_ref[rows, pair0] = o[:sb]
        y_ref[rows, pair1] = o[sb:]

    n_units = nsub * SWA_KV_HEADS
    pending = scores(0)
    for u in range(n_units):
        upcoming = scores(u + 1) if u + 1 < n_units else None
        beside_unit(u)
        attend(u, pending)
        pending = upcoming


def _mem_into(y_ref, q_ref, k_ref, v_ref, after_head):
    scale = MEM_HEAD_DIM ** -0.5

    def head_cols(hd):
        return slice(hd * MEM_HEAD_DIM, (hd + 1) * MEM_HEAD_DIM)

    def scores(hd):
        return _dot_nt(q_ref[0, :, head_cols(hd)], k_ref[0, :, head_cols(hd)])

    pending = scores(0)
    for hd in range(MEM_HEADS):
        upcoming = scores(hd + 1) if hd + 1 < MEM_HEADS else None
        s = pending * scale
        m = jnp.max(s, axis=-1, keepdims=True)
        p = jnp.exp(s - m)
        inv = 1.0 / jnp.sum(p, axis=-1, keepdims=True)
        y_ref[:, head_cols(hd)] = _dot((p * inv).astype(BF16), v_ref[0, :, head_cols(hd)])
        after_head(hd)
        pending = upcoming


def _tail_kernel(sink_ref, x_ref, ya_ref, za_ref, zb_ref, zc_ref, qs_ref, qm_ref,
                 kp_ref, kc_ref, vp_ref, vc_ref, bias_ref, km_ref, vm_ref, w_hbm, g_ref,
                 o_ref, kb_ref, vb_ref, yb_ref, yc_ref, yacc_ref, w_ref, stage_ref, sem_ref, *, nsub):
    @pl.when(_first_grid_step())
    def _():
        _load_weight_bf16(w_hbm, w_ref, stage_ref, sem_ref)

    def gated(y, z_ref):
        h = 0.5 * z_ref[0].astype(F32)
        return (y * (h + h * jnp.tanh(h))).astype(BF16)

    swa_lo, mem_lo = MLA_WIDTH, MLA_WIDTH + SWA_WIDTH

    g_mla = gated(ya_ref[0].astype(F32), za_ref)
    mla_chunk = D_MODEL // (nsub * SWA_KV_HEADS)

    def project_mla(u):
        cols = slice(u * mla_chunk, (u + 1) * mla_chunk)
        yacc_ref[:, cols] = _dot(g_mla, w_ref[:swa_lo, cols])

    _swa_into(yb_ref, sink_ref, qs_ref, kp_ref, kc_ref, vp_ref, vc_ref, bias_ref, kb_ref, vb_ref, nsub,
              project_mla)

    g_swa = gated(yb_ref[...], zb_ref)
    swa_chunk = D_MODEL // MEM_HEADS

    def project_swa(u):
        cols = slice(u * swa_chunk, (u + 1) * swa_chunk)
        yacc_ref[:, cols] += _dot(g_swa, w_ref[swa_lo:mem_lo, cols])

    _mem_into(yc_ref, qm_ref, km_ref, vm_ref, project_swa)

    g_mem = gated(yc_ref[...], zc_ref)
    sum_sq = jnp.zeros((g_mem.shape[0], 1), F32)
    chunks = [slice(j * swa_chunk, (j + 1) * swa_chunk) for j in range(D_MODEL // swa_chunk)]
    for cols in chunks:
        r = x_ref[0, :, cols] + yacc_ref[:, cols] + _dot(g_mem, w_ref[mem_lo:, cols])
        o_ref[0, :, cols] = r
        sum_sq = sum_sq + jnp.sum(r * r, axis=-1, keepdims=True)
    inv = lax.rsqrt(sum_sq * (1.0 / D_MODEL) + EPS)
    for cols in chunks:
        o_ref[0, :, cols] = (o_ref[0, :, cols] * inv) * g_ref[:, cols]


def _tail(sinks, x, y_mla, z, qs, ks, vs, bias, kmem, vmem, w, g, nsub):
    B, S, _ = x.shape
    tm = nsub * SWA_BLOCK
    M = kmem.shape[1]
    row = lambda b, i: (b, i, 0)
    prev = lambda b, i: (b, jnp.maximum(i * nsub - 1, 0), 0)
    col = lambda c: (lambda b, i: (b, i, c))
    return pl.pallas_call(
        functools.partial(_tail_kernel, nsub=nsub),
        grid=(B, S // tm),
        in_specs=[
            pl.BlockSpec(memory_space=pltpu.SMEM),
            pl.BlockSpec((1, tm, D_MODEL), row),
            pl.BlockSpec((1, tm, MLA_WIDTH), row),
            pl.BlockSpec((1, tm, MLA_WIDTH), col(0)),
            pl.BlockSpec((1, tm, SWA_WIDTH), col(MLA_WIDTH // SWA_WIDTH)),
            pl.BlockSpec((1, tm, MEM_WIDTH), col((MLA_WIDTH + SWA_WIDTH) // MEM_WIDTH)),
            pl.BlockSpec((1, tm, SWA_WIDTH), col(0)),
            pl.BlockSpec((1, tm, MEM_WIDTH), col(1)),
            pl.BlockSpec((1, SWA_BLOCK, _SWA_KV_PAD), prev),
            pl.BlockSpec((1, tm, _SWA_KV_PAD), row),
            pl.BlockSpec((1, SWA_BLOCK, _SWA_KV_PAD), prev),
            pl.BlockSpec((1, tm, _SWA_KV_PAD), row),
            pl.BlockSpec((SWA_KV_HEADS, 2, 2 * SWA_BLOCK, 2 * SWA_BLOCK), lambda b, i: (0, 0, 0, 0)),
            pl.BlockSpec((1, M, MEM_WIDTH), lambda b, i: (b, 0, 0)),
            pl.BlockSpec((1, M, MEM_WIDTH), lambda b, i: (b, 0, 0)),
            pl.BlockSpec(memory_space=pl.ANY),
            pl.BlockSpec((1, D_MODEL), lambda b, i: (0, 0)),
        ],
        out_specs=pl.BlockSpec((1, tm, D_MODEL), row),
        out_shape=jax.ShapeDtypeStruct((B, S, D_MODEL), F32),
        scratch_shapes=[
            pltpu.VMEM((tm + SWA_BLOCK, _SWA_KV_PAD), BF16),
            pltpu.VMEM((tm + SWA_BLOCK, _SWA_KV_PAD), BF16),
            pltpu.VMEM((tm, SWA_WIDTH), F32),
            pltpu.VMEM((tm, MEM_WIDTH), F32),
            pltpu.VMEM((tm, D_MODEL), F32),
            pltpu.VMEM((MIX_WIDTH, D_MODEL), BF16),
            pltpu.VMEM((_STAGE_SLOTS, _W_OUT_CHUNK, D_MODEL), F32),
            pltpu.SemaphoreType.DMA((_STAGE_SLOTS,)),
        ],
        compiler_params=_params(("arbitrary", "arbitrary")),
        name="tail",
    )(sinks, x, y_mla, z, z, z, qs, qs, ks, ks, vs, vs, bias, kmem, vmem, w, g)


def _rope_tables(seq):
    inv = 1.0 / (ROPE_THETA ** (jnp.arange(0, MLA_ROPE, 2, dtype=F32) / MLA_ROPE))
    step = 1 << ((seq - 1).bit_length() // 2)
    assert seq % step == 0
    fine = jnp.arange(step, dtype=F32)[:, None] * inv[None, :]
    coarse = (jnp.arange(seq // step, dtype=F32) * step)[:, None] * inv[None, :]
    cf, sf = jnp.cos(fine)[None], jnp.sin(fine)[None]
    cc, sc = jnp.cos(coarse)[:, None], jnp.sin(coarse)[:, None]
    cos = (cc * cf - sc * sf).reshape(seq, -1)
    sin = (sc * cf + cc * sf).reshape(seq, -1)
    zero = jnp.zeros_like(cos)
    return (jnp.concatenate([cos, cos, zero, zero], axis=-1),
            jnp.concatenate([-sin, sin, zero, zero], axis=-1))


def _t5_bucket(rel):
    nb = N_BUCKETS // 2
    max_exact = nb // 2
    bucket = jnp.where(rel > 0, nb, 0)
    n = jnp.abs(rel)
    nf = jnp.maximum(n, 1).astype(F32)
    large = max_exact + (jnp.log(nf / max_exact) / math.log(MAX_DISTANCE / max_exact)
                         * (nb - max_exact)).astype(jnp.int32)
    large = jnp.minimum(large, nb - 1)
    return bucket + jnp.where(n < max_exact, n, large)


def kernel(x, mem, norm_in, w_in, norm_q, norm_kv, w_uq, w_ukv, attn_sinks, rel_bias,
           norm_mem, w_mem_kv, w_out, norm_final):
    B, S, _ = x.shape
    assert norm_in.shape[0] == 1, "single-layer trunk"
    assert x.shape[2] == D_MODEL and w_in.shape[1:] == (D_MODEL, IN_WIDTH)
    assert S % _PROJ_ROWS == 0 and S % (_TAIL_BLOCKS * SWA_BLOCK) == 0

    wt = w_in[0].T
    cos_t, sin_t = _rope_tables(S)
    q, k, v, qs, z, ks, vs = _proj(x, norm_in, wt, norm_q, norm_kv, w_uq[0].astype(BF16),
                                   w_ukv[0].astype(BF16), cos_t, sin_t, tm=_PROJ_ROWS)

    y_mla = _mla(q, k, v, t=_MLA_TILE, g=_MLA_GROUP, nq=_MLA_TILES_PER_STEP)

    bucket = _t5_bucket(jnp.arange(_REL_SPAN) - (2 * SWA_BLOCK - 1)).astype(jnp.int32)
    bias, kmem, vmem = _bias_memkv(rel_bias, bucket, mem, norm_mem, w_mem_kv[0])
    return _tail(attn_sinks[0], x, y_mla, z, qs, ks, vs, bias, kmem, vmem,
                 w_out[0], norm_final[None, :], nsub=_TAIL_BLOCKS)
```

```python
import functools
import math

import jax
import jax.numpy as jnp
from jax import lax
from jax.experimental import pallas as pl
from jax.experimental.pallas import tpu as pltpu

D_MODEL = 2048
CHUNK = 64
EPS = 1e-6
NEG = -1e30

MLA_HEADS = 8
MLA_NOPE = 128
MLA_ROPE = 64
MLA_V = 128
MLA_QLORA = 512
MLA_KVLORA = 256
MLA_WIDTH = MLA_HEADS * MLA_V
MLA_QK_PAD = 256
ROPE_THETA = 10000.0

SWA_HEADS = 8
SWA_KV_HEADS = 2
SWA_GROUP = SWA_HEADS // SWA_KV_HEADS
SWA_HEAD_DIM = 64
SWA_WIDTH = SWA_HEADS * SWA_HEAD_DIM
SWA_KV_WIDTH = SWA_KV_HEADS * SWA_HEAD_DIM
WINDOW_CHUNKS = 2
SWA_BLOCK = 128

MEM_HEADS = 4
MEM_HEAD_DIM = 128
MEM_WIDTH = MEM_HEADS * MEM_HEAD_DIM

MIX_WIDTH = MLA_WIDTH + SWA_WIDTH + MEM_WIDTH

N_BUCKETS = 32
MAX_DISTANCE = 128

BF16 = jnp.bfloat16
F32 = jnp.float32

_R_CQ = 0
_R_CKV = _R_CQ + MLA_QLORA
_R_KPE = _R_CKV + MLA_KVLORA
_R_ZMLA = _R_KPE + MLA_ROPE
_R_QSWA = _R_ZMLA + MLA_WIDTH
_R_KSWA = _R_QSWA + SWA_WIDTH
_R_VSWA = _R_KSWA + SWA_KV_WIDTH
_R_ZSWA = _R_VSWA + SWA_KV_WIDTH
_R_QMEM = _R_ZSWA + SWA_WIDTH
_R_ZMEM = _R_QMEM + MEM_WIDTH
IN_WIDTH = _R_ZMEM + MEM_WIDTH

_MLA_Q_SCALE = (MLA_NOPE + MLA_ROPE) ** -0.5 * math.log2(math.e)

_SWA_KV_PAD = 2 * SWA_KV_HEADS * 128

_W_IN_CHUNK = 208
_W_OUT_CHUNK = 256
_STAGE_SLOTS = 4

_V7X_VMEM_BYTES = 64 * 1024 * 1024
_VMEM_LIMIT = _V7X_VMEM_BYTES - 8 * 1024 * 1024

_PROJ_ROWS = 512
_MLA_TILE = 512
_MLA_GROUP = 4
_MLA_TILES_PER_STEP = 4
_TAIL_BLOCKS = 4


def _params(sem):
    return pltpu.CompilerParams(dimension_semantics=sem, vmem_limit_bytes=_VMEM_LIMIT)


def _rms(v, g):
    return (v * lax.rsqrt(jnp.mean(v * v, axis=-1, keepdims=True) + EPS)) * g


def _dot(a, b):
    return jnp.dot(a, b, preferred_element_type=F32)


def _dot_nt(a, b):
    return lax.dot_general(a, b, (((1,), (1,)), ((), ())), preferred_element_type=F32)


def _load_weight_bf16(w_hbm, w_ref, stage_ref, sem_ref):
    slots, rows, _ = stage_ref.shape
    n_chunks = w_ref.shape[0] // rows
    ahead = slots - 1

    def chunk_copy(c):
        return pltpu.make_async_copy(w_hbm.at[pl.ds(c * rows, rows), :], stage_ref.at[c % slots],
                                     sem_ref.at[c % slots])

    for c in range(min(ahead, n_chunks)):
        chunk_copy(c).start()
    for c in range(n_chunks):
        if c + ahead < n_chunks:
            chunk_copy(c + ahead).start()
        chunk_copy(c).wait()
        w_ref[c * rows:(c + 1) * rows, :] = stage_ref[c % slots].astype(BF16)


def _first_grid_step():
    return jnp.logical_and(pl.program_id(0) == 0, pl.program_id(1) == 0)


def _lane_tile(v, n):
    return jnp.concatenate([v] * n, axis=1)


def _chunk_of(pos):
    return jnp.right_shift(pos, CHUNK.bit_length() - 1)


def _rope128(v, cos_t, sin_t):
    return v * cos_t + pltpu.roll(v, MLA_ROPE // 2, 1) * sin_t


def _store_swa_kv(ref, kv):
    low = lax.broadcasted_iota(jnp.int32, kv.shape, 1) < SWA_HEAD_DIM
    swapped = pltpu.roll(kv, SWA_HEAD_DIM, 1)
    pieces = (jnp.where(low, kv, 0.0), jnp.where(low, 0.0, swapped),
              jnp.where(low, swapped, 0.0), jnp.where(low, 0.0, kv))
    for i, piece in enumerate(pieces):
        ref[0, :, i * 128:(i + 1) * 128] = piece.astype(BF16)


def _proj_kernel(x_ref, gin_ref, wt_hbm, gq_ref, gkv_ref, wuq_ref, wukv_ref, cos_ref, sin_ref,
                 q_ref, k_ref, v_ref, qs_ref, z_ref, ks_ref, vs_ref, wt_ref, stage_ref, sem_ref):
    @pl.when(_first_grid_step())
    def _():
        _load_weight_bf16(wt_hbm, wt_ref, stage_ref, sem_ref)

    h = _rms(x_ref[0], gin_ref[...]).astype(BF16)
    cos_t = cos_ref[...]
    sin_t = sin_ref[...]

    def proj(lo, hi):
        return _dot_nt(h, wt_ref[lo:hi, :])

    pa = proj(_R_CQ, _R_KPE)
    cq = _rms(pa[:, :MLA_QLORA], gq_ref[...]).astype(BF16)
    ckv = _rms(pa[:, MLA_QLORA:], gkv_ref[...]).astype(BF16)

    qs_ref[0, :, :SWA_WIDTH] = proj(_R_QSWA, _R_KSWA).astype(BF16)
    qs_ref[0, :, SWA_WIDTH:] = proj(_R_QMEM, _R_ZMEM).astype(BF16)
    z_ref[0, :, :MLA_WIDTH] = proj(_R_ZMLA, _R_QSWA).astype(BF16)
    z_ref[0, :, MLA_WIDTH:MLA_WIDTH + SWA_WIDTH] = proj(_R_ZSWA, _R_QMEM).astype(BF16)
    z_ref[0, :, MLA_WIDTH + SWA_WIDTH:] = proj(_R_ZMEM, IN_WIDTH).astype(BF16)
    kvs = proj(_R_KSWA, _R_ZSWA)
    _store_swa_kv(ks_ref, kvs[:, :SWA_KV_WIDTH])
    _store_swa_kv(vs_ref, kvs[:, SWA_KV_WIDTH:])
    pe = proj(_R_KPE, _R_ZMLA)
    kpe = _rope128(jnp.concatenate([pe, pe], axis=1), cos_t, sin_t).astype(BF16)

    qall = _dot(cq, wuq_ref[...]) * _MLA_Q_SCALE
    kv = _dot(ckv, wukv_ref[...])
    for hd in range(MLA_HEADS):
        oq = hd * (MLA_NOPE + MLA_ROPE)
        q_pe = qall[:, oq + MLA_NOPE:oq + MLA_NOPE + MLA_ROPE]
        q_ref[0, hd, :, :MLA_NOPE] = qall[:, oq:oq + MLA_NOPE].astype(BF16)
        q_ref[0, hd, :, MLA_NOPE:] = _rope128(
            jnp.concatenate([q_pe, q_pe], axis=1), cos_t, sin_t).astype(BF16)
        o = hd * (MLA_NOPE + MLA_V)
        k_ref[0, hd, :, :MLA_NOPE] = kv[:, o:o + MLA_NOPE].astype(BF16)
        k_ref[0, hd, :, MLA_NOPE:] = kpe
        v_ref[0, hd] = kv[:, o + MLA_NOPE:o + MLA_QK_PAD].astype(BF16)


def _proj(x, gin, wt, gq, gkv, wuq, wukv, cos_t, sin_t, tm):
    B, S, _ = x.shape
    const = lambda b, i: (0, 0)
    row = lambda b, i: (b, i, 0)
    single = pl.Buffered(1)

    def out(width):
        return jax.ShapeDtypeStruct((B, S, width), BF16)

    head_spec = pl.BlockSpec((1, MLA_HEADS, tm, MLA_QK_PAD), lambda b, i: (b, 0, i, 0))
    head_shape = jax.ShapeDtypeStruct((B, MLA_HEADS, S, MLA_QK_PAD), BF16)

    return pl.pallas_call(
        _proj_kernel,
        grid=(B, S // tm),
        in_specs=[
            pl.BlockSpec((1, tm, D_MODEL), row),
            pl.BlockSpec((1, D_MODEL), const),
            pl.BlockSpec(memory_space=pl.ANY),
            pl.BlockSpec((1, MLA_QLORA), const),
            pl.BlockSpec((1, MLA_KVLORA), const),
            pl.BlockSpec((MLA_QLORA, MLA_HEADS * (MLA_NOPE + MLA_ROPE)), const, pipeline_mode=single),
            pl.BlockSpec((MLA_KVLORA, MLA_HEADS * (MLA_NOPE + MLA_V)), const, pipeline_mode=single),
            pl.BlockSpec((tm, 128), lambda b, i: (i, 0)),
            pl.BlockSpec((tm, 128), lambda b, i: (i, 0)),
        ],
        out_specs=[
            head_spec, head_spec,
            pl.BlockSpec((1, MLA_HEADS, tm, MLA_V), lambda b, i: (b, 0, i, 0)),
            pl.BlockSpec((1, tm, SWA_WIDTH + MEM_WIDTH), row),
            pl.BlockSpec((1, tm, MIX_WIDTH), row),
            pl.BlockSpec((1, tm, _SWA_KV_PAD), row),
            pl.BlockSpec((1, tm, _SWA_KV_PAD), row),
        ],
        out_shape=[
            head_shape, head_shape, jax.ShapeDtypeStruct((B, MLA_HEADS, S, MLA_V), BF16),
            out(SWA_WIDTH + MEM_WIDTH), out(MIX_WIDTH), out(_SWA_KV_PAD), out(_SWA_KV_PAD),
        ],
        scratch_shapes=[
            pltpu.VMEM((IN_WIDTH, D_MODEL), BF16),
            pltpu.VMEM((_STAGE_SLOTS, _W_IN_CHUNK, D_MODEL), F32),
            pltpu.SemaphoreType.DMA((_STAGE_SLOTS,)),
        ],
        compiler_params=_params(("arbitrary", "arbitrary")),
        name="proj",
    )(x, gin, wt, gq, gkv, wuq, wukv, cos_t, sin_t)


def _mla_kernel(q_ref, k_ref, v_ref, o_ref, acc_ref, m_ref, s_ref, cm_ref, *, t, g, nq):
    qi0 = pl.program_id(2) * nq
    qi_end = qi0 + nq - 1
    acc_ref[...] = jnp.zeros(acc_ref.shape, F32)
    m_ref[...] = jnp.full(m_ref.shape, NEG, F32)
    ones = jnp.ones((t, MLA_V), BF16)
    row_halves = ((0, t // 2, t // 2), (t // 2, t // 2, t))

    def row_max(s):
        return jnp.broadcast_to(jnp.max(s, axis=-1, keepdims=True), (s.shape[0], 128))

    def diagonal_mask():
        q_chunk = _chunk_of(lax.broadcasted_iota(jnp.int32, (128, 128), 0))
        k_chunk = _chunk_of(lax.broadcasted_iota(jnp.int32, (128, 128), 1))
        return k_chunk <= q_chunk

    def masked_band(band, mask):
        width = band.shape[1]
        block = jnp.where(mask, band[:, width - 128:], NEG)
        return block if width == 128 else jnp.concatenate([band[:, :width - 128], block], axis=1)

    def scores(qi, c, slot, hd, mask=None):
        rows = pl.multiple_of((qi - qi0) * t, t)
        keys = pl.multiple_of(c * t, t)
        if mask is None:
            s = _dot_nt(q_ref[0, hd, pl.ds(rows, t), :], k_ref[0, hd, pl.ds(keys, t), :])
            s_ref[slot, hd] = s
            cm_ref[slot, hd] = row_max(s)
            return
        for r0, nr, nk in row_halves:
            s = _dot_nt(q_ref[0, hd, pl.ds(pl.multiple_of(rows + r0, nr), nr), :],
                        k_ref[0, hd, pl.ds(keys, nk), :])
            for a in range(0, nr, 128):
                reach = r0 + a + 128
                band = masked_band(s[a:a + 128, :reach], mask)
                s_ref[slot, hd, r0 + a:reach, :reach] = band
                cm_ref[slot, hd, r0 + a:reach] = row_max(band)

    def update(c, slot, hd, diagonal, mask):
        keys = pl.multiple_of(c * t, t)
        if not diagonal:
            m_old = m_ref[hd]
            m_new = jnp.maximum(m_old, cm_ref[slot, hd])
            alpha = jnp.exp2(m_old - m_new)
            p = jnp.exp2(s_ref[slot, hd] - _lane_tile(m_new, t // 128)).astype(BF16)
            v_ones = jnp.concatenate([v_ref[0, hd, pl.ds(keys, t), :], ones], axis=1)
            acc_ref[hd] = _lane_tile(alpha, MLA_QK_PAD // 128) * acc_ref[hd] + _dot(p, v_ones)
            m_ref[hd] = m_new
            return
        for r0, nr, nk in row_halves:
            alphas, probs = [], []
            for a in range(0, nr, 128):
                reach = r0 + a + 128
                band = s_ref[slot, hd, r0 + a:reach, :reach]
                if mask is not None:
                    band = masked_band(band, mask)
                    m_cur = row_max(band)
                else:
                    m_cur = cm_ref[slot, hd, r0 + a:reach]
                m_old = m_ref[hd, r0 + a:reach]
                m_new = jnp.maximum(m_old, m_cur)
                alphas.append(jnp.exp2(m_old - m_new))
                p = jnp.exp2(band - _lane_tile(m_new, reach // 128)).astype(BF16)
                if reach < nk:
                    p = jnp.concatenate([p, jnp.zeros((128, nk - reach), BF16)], axis=1)
                probs.append(p)
                m_ref[hd, r0 + a:reach] = m_new
            rows = slice(r0, r0 + nr)
            v_ones = jnp.concatenate([v_ref[0, hd, pl.ds(keys, nk), :], ones[:nk]], axis=1)
            acc_ref[hd, rows] = (
                _lane_tile(jnp.concatenate(alphas, axis=0), MLA_QK_PAD // 128) * acc_ref[hd, rows]
                + _dot(jnp.concatenate(probs, axis=0), v_ones))

    def finish_tile(qi, hd):
        rows = pl.multiple_of((qi - qi0) * t, t)
        acc = acc_ref[hd]
        o_ref[0, pl.ds(rows, t), hd * MLA_V:(hd + 1) * MLA_V] = (
            acc[:, :MLA_V] / acc[:, MLA_V:]).astype(BF16)
        acc_ref[hd] = jnp.zeros(acc.shape, F32)
        m_ref[hd] = jnp.full((t, 128), NEG, F32)

    def following(qi, c):
        is_last = c == qi
        return jnp.where(is_last, jnp.minimum(qi + 1, qi_end), qi), jnp.where(is_last, 0, c + 1)

    def step(qi, c, slot, diagonal, premasked=False, mask_next=False):
        qn, cn = following(qi, c)
        mask = diagonal_mask() if diagonal and not premasked else None
        next_mask = diagonal_mask() if mask_next else None
        for hd in range(g):
            scores(qn, cn, 1 - slot, hd, next_mask)
            update(c, slot, hd, diagonal, mask)
            if diagonal:
                finish_tile(qi, hd)

    for hd in range(g):
        scores(qi0, 0, 0, hd)

    def body(_, pair):
        first, second = pair, following(*pair)
        diag0 = first[1] == first[0]
        diag1 = second[1] == second[0]
        for d0, d1, cond in ((True, False, diag0), (False, True, diag1),
                             (False, False, jnp.logical_not(jnp.logical_or(diag0, diag1)))):
            @pl.when(cond)
            def _(d0=d0, d1=d1):
                step(*first, 0, d0, mask_next=d1)
                step(*second, 1, d1, premasked=d1)
        return following(*second)

    n_steps = nq * qi0 + nq * (nq + 1) // 2
    lax.fori_loop(0, n_steps // 2, body, (qi0, 0))


def _mla(q, k, v, t, g, nq):
    B, _, S, _ = q.shape
    assert S % (nq * t) == 0 and nq % 2 == 0 and (nq * (nq + 1) // 2) % 2 == 0
    return pl.pallas_call(
        functools.partial(_mla_kernel, t=t, g=g, nq=nq),
        grid=(B, MLA_HEADS // g, S // (nq * t)),
        in_specs=[
            pl.BlockSpec((1, g, nq * t, MLA_QK_PAD), lambda b, h, i: (b, h, i, 0)),
            pl.BlockSpec((1, g, S, MLA_QK_PAD), lambda b, h, i: (b, h, 0, 0)),
            pl.BlockSpec((1, g, S, MLA_V), lambda b, h, i: (b, h, 0, 0)),
        ],
        out_specs=pl.BlockSpec((1, nq * t, g * MLA_V), lambda b, h, i: (b, i, h)),
        out_shape=jax.ShapeDtypeStruct((B, S, MLA_WIDTH), BF16),
        scratch_shapes=[
            pltpu.VMEM((g, t, MLA_QK_PAD), F32),
            pltpu.VMEM((g, t, 128), F32),
            pltpu.VMEM((2, g, t, t), F32),
            pltpu.VMEM((2, g, t, 128), F32),
        ],
        compiler_params=_params(("arbitrary", "arbitrary", "arbitrary")),
        name="mla",
    )(q, k, v)


_REL_SPAN = 4 * SWA_BLOCK
_N_OFFSETS = 3 * SWA_BLOCK - 1


def _bias_memkv_kernel(table_ref, bucket_ref, mem_hbm, g_ref, w_hbm, o_ref, k_ref, v_ref,
                       spread_ref, mem_ref, w_ref, sem_ref):
    fetches = [pltpu.make_async_copy(mem_hbm, mem_ref, sem_ref.at[0]),
               pltpu.make_async_copy(w_hbm, w_ref, sem_ref.at[1])]
    for fetch in fetches:
        fetch.start()

    for b in range(N_BUCKETS):
        spread_ref[b] = jnp.concatenate(
            [jnp.full((1, 128), table_ref[b, hd], F32) for hd in range(SWA_HEADS)], axis=0)
    lane = lax.broadcasted_iota(jnp.int32, (SWA_HEADS, _REL_SPAN), 1)

    def lookup(c, per_offset):
        col = spread_ref[bucket_ref[c]]
        return jnp.where(lane == c, _lane_tile(col, _REL_SPAN // 128), per_offset)

    per_offset_all = lax.fori_loop(0, _N_OFFSETS, lookup, jnp.zeros((SWA_HEADS, _REL_SPAN), F32))
    q_chunk = _chunk_of(lax.broadcasted_iota(jnp.int32, (SWA_BLOCK, 2 * SWA_BLOCK), 0))
    b_chunk = _chunk_of(lax.broadcasted_iota(jnp.int32, (SWA_BLOCK, 2 * SWA_BLOCK), 1))
    in_window = jnp.logical_and(b_chunk >= q_chunk, b_chunk <= q_chunk + WINDOW_CHUNKS)
    for hd in range(SWA_HEADS):
        per_offset = per_offset_all[hd:hd + 1, :]
        rows = jnp.broadcast_to(per_offset, (SWA_BLOCK, _REL_SPAN))
        band = pltpu.roll(rows, _REL_SPAN - (SWA_BLOCK - 1), 1, stride=1, stride_axis=0)
        kvh, within = divmod(hd, SWA_GROUP)
        par, half = within % 2, within // 2
        o_ref[kvh, par, half * SWA_BLOCK:(half + 1) * SWA_BLOCK, :] = jnp.where(
            in_window, band[:, :2 * SWA_BLOCK], NEG)

    for fetch in fetches:
        fetch.wait()
    w = w_ref[...].astype(BF16)
    for b in range(mem_ref.shape[0]):
        kv = _dot(_rms(mem_ref[b], g_ref[...]).astype(BF16), w)
        k_ref[b] = kv[:, :MEM_WIDTH].astype(BF16)
        v_ref[b] = kv[:, MEM_WIDTH:].astype(BF16)


def _bias_memkv(rel_table, bucket, mem, g, w):
    B, M, _ = mem.shape
    smem = pl.BlockSpec(memory_space=pltpu.SMEM)
    vmem = pl.BlockSpec(memory_space=pltpu.VMEM)
    hbm = pl.BlockSpec(memory_space=pl.ANY)
    return pl.pallas_call(
        _bias_memkv_kernel,
        in_specs=[smem, smem, hbm, vmem, hbm],
        out_specs=[vmem] * 3,
        out_shape=[jax.ShapeDtypeStruct((SWA_KV_HEADS, 2, 2 * SWA_BLOCK, 2 * SWA_BLOCK), F32),
                   jax.ShapeDtypeStruct((B, M, MEM_WIDTH), BF16),
                   jax.ShapeDtypeStruct((B, M, MEM_WIDTH), BF16)],
        scratch_shapes=[
            pltpu.VMEM((N_BUCKETS, SWA_HEADS, 128), F32),
            pltpu.VMEM(mem.shape, F32),
            pltpu.VMEM(w.shape, F32),
            pltpu.SemaphoreType.DMA((2,)),
        ],
        compiler_params=pltpu.CompilerParams(vmem_limit_bytes=_VMEM_LIMIT),
        name="t5bias_memkv",
    )(rel_table, bucket, mem, g, w)


def _swa_into(y_ref, sink_ref, q_ref, kp_ref, kc_ref, vp_ref, vc_ref, bias_ref, kb_ref, vb_ref, nsub,
              beside_unit):
    t = pl.program_id(1)
    scale = SWA_HEAD_DIM ** -0.5
    sb = SWA_BLOCK
    kb_ref[:sb] = kp_ref[0]
    kb_ref[sb:] = kc_ref[0]
    vb_ref[:sb] = vp_ref[0]
    vb_ref[sb:] = vc_ref[0]

    b_chunk = _chunk_of(lax.broadcasted_iota(jnp.int32, (2 * sb, 2 * sb), 1))
    has_key = b_chunk >= jnp.where(t > 0, 0, 2)
    upper = lax.broadcasted_iota(jnp.int32, (2 * sb, 1), 0) < sb

    def geometry(u):
        r, kvh = divmod(u, SWA_KV_HEADS)
        rows = slice(r * sb, (r + 1) * sb)
        band = slice(r * sb, (r + 2) * sb)
        pair0 = slice(2 * kvh * 128, (2 * kvh + 1) * 128)
        pair1 = slice((2 * kvh + 1) * 128, (2 * kvh + 2) * 128)
        return r, kvh, rows, band, pair0, pair1

    def scores(u):
        r, kvh, rows, band, pair0, pair1 = geometry(u)
        q = jnp.concatenate([q_ref[0, rows, pair0], q_ref[0, rows, pair1]], axis=0)
        return [_dot_nt(q, kb_ref[band, (2 * kvh + par) * 128:(2 * kvh + par + 1) * 128])
                for par in range(2)]

    def attend(u, unit_scores):
        r, kvh, rows, band, pair0, pair1 = geometry(u)
        o = None
        for par, s in enumerate(unit_scores):
            kcols = slice((2 * kvh + par) * 128, (2 * kvh + par + 1) * 128)
            hd = SWA_GROUP * kvh + par
            s = s * scale + bias_ref[kvh, par]
            if r == 0:
                s = jnp.where(has_key, s, NEG)
            sink = jnp.where(upper, sink_ref[hd], sink_ref[hd + 2])
            m = jnp.maximum(jnp.max(s, axis=-1, keepdims=True), sink)
            p = jnp.exp(s - m)
            den = jnp.sum(p, axis=-1, keepdims=True) + jnp.exp(sink - m)
            pv = _dot((p * (1.0 / den)).astype(BF16), vb_ref[band, kcols])
            o = pv if o is None else o + pv
        y_ref[rows, pair0] = o[:sb]
        y_ref[rows, pair1] = o[sb:]

    n_units = nsub * SWA_KV_HEADS
    pending = scores(0)
    for u in range(n_units):
        upcoming = scores(u + 1) if u + 1 < n_units else None
        beside_unit(u)
        attend(u, pending)
        pending = upcoming


def _mem_into(y_ref, q_ref, k_ref, v_ref, after_head):
    scale = MEM_HEAD_DIM ** -0.5

    def head_cols(hd):
        return slice(hd * MEM_HEAD_DIM, (hd + 1) * MEM_HEAD_DIM)

    def scores(hd):
        return _dot_nt(q_ref[0, :, head_cols(hd)], k_ref[0, :, head_cols(hd)])

    pending = scores(0)
    for hd in range(MEM_HEADS):
        upcoming = scores(hd + 1) if hd + 1 < MEM_HEADS else None
        s = pending * scale
        m = jnp.max(s, axis=-1, keepdims=True)
        p = jnp.exp(s - m)
        inv = 1.0 / jnp.sum(p, axis=-1, keepdims=True)
        y_ref[:, head_cols(hd)] = _dot((p * inv).astype(BF16), v_ref[0, :, head_cols(hd)])
        after_head(hd)
        pending = upcoming


def _tail_kernel(sink_ref, x_ref, ya_ref, za_ref, zb_ref, zc_ref, qs_ref, qm_ref,
                 kp_ref, kc_ref, vp_ref, vc_ref, bias_ref, km_ref, vm_ref, w_hbm, g_ref,
                 o_ref, kb_ref, vb_ref, yb_ref, yc_ref, yacc_ref, w_ref, stage_ref, sem_ref, *, nsub):
    @pl.when(_first_grid_step())
    def _():
        _load_weight_bf16(w_hbm, w_ref, stage_ref, sem_ref)

    def gated(y, z_ref):
        h = 0.5 * z_ref[0].astype(F32)
        return (y * (h + h * jnp.tanh(h))).astype(BF16)

    swa_lo, mem_lo = MLA_WIDTH, MLA_WIDTH + SWA_WIDTH

    g_mla = gated(ya_ref[0].astype(F32), za_ref)
    mla_chunk = D_MODEL // (nsub * SWA_KV_HEADS)

    def project_mla(u):
        cols = slice(u * mla_chunk, (u + 1) * mla_chunk)
        yacc_ref[:, cols] = _dot(g_mla, w_ref[:swa_lo, cols])

    _swa_into(yb_ref, sink_ref, qs_ref, kp_ref, kc_ref, vp_ref, vc_ref, bias_ref, kb_ref, vb_ref, nsub,
              project_mla)

    g_swa = gated(yb_ref[...], zb_ref)
    swa_chunk = D_MODEL // MEM_HEADS

    def project_swa(u):
        cols = slice(u * swa_chunk, (u + 1) * swa_chunk)
        yacc_ref[:, cols] += _dot(g_swa, w_ref[swa_lo:mem_lo, cols])

    _mem_into(yc_ref, qm_ref, km_ref, vm_ref, project_swa)

    g_mem = gated(yc_ref[...], zc_ref)
    sum_sq = jnp.zeros((g_mem.shape[0], 1), F32)
    chunks = [slice(j * swa_chunk, (j + 1) * swa_chunk) for j in range(D_MODEL // swa_chunk)]
    for cols in chunks:
        r = x_ref[0, :, cols] + yacc_ref[:, cols] + _dot(g_mem, w_ref[mem_lo:, cols])
        o_ref[0, :, cols] = r
        sum_sq = sum_sq + jnp.sum(r * r, axis=-1, keepdims=True)
    inv = lax.rsqrt(sum_sq * (1.0 / D_MODEL) + EPS)
    for cols in chunks:
        o_ref[0, :, cols] = (o_ref[0, :, cols] * inv) * g_ref[:, cols]


def _tail(sinks, x, y_mla, z, qs, ks, vs, bias, kmem, vmem, w, g, nsub):
    B, S, _ = x.shape
    tm = nsub * SWA_BLOCK
    M = kmem.shape[1]
    row = lambda b, i: (b, i, 0)
    prev = lambda b, i: (b, jnp.maximum(i * nsub - 1, 0), 0)
    col = lambda c: (lambda b, i: (b, i, c))
    return pl.pallas_call(
        functools.partial(_tail_kernel, nsub=nsub),
        grid=(B, S // tm),
        in_specs=[
            pl.BlockSpec(memory_space=pltpu.SMEM),
            pl.BlockSpec((1, tm, D_MODEL), row),
            pl.BlockSpec((1, tm, MLA_WIDTH), row),
            pl.BlockSpec((1, tm, MLA_WIDTH), col(0)),
            pl.BlockSpec((1, tm, SWA_WIDTH), col(MLA_WIDTH // SWA_WIDTH)),
            pl.BlockSpec((1, tm, MEM_WIDTH), col((MLA_WIDTH + SWA_WIDTH) // MEM_WIDTH)),
            pl.BlockSpec((1, tm, SWA_WIDTH), col(0)),
            pl.BlockSpec((1, tm, MEM_WIDTH), col(1)),
            pl.BlockSpec((1, SWA_BLOCK, _SWA_KV_PAD), prev),
            pl.BlockSpec((1, tm, _SWA_KV_PAD), row),
            pl.BlockSpec((1, SWA_BLOCK, _SWA_KV_PAD), prev),
            pl.BlockSpec((1, tm, _SWA_KV_PAD), row),
            pl.BlockSpec((SWA_KV_HEADS, 2, 2 * SWA_BLOCK, 2 * SWA_BLOCK), lambda b, i: (0, 0, 0, 0)),
            pl.BlockSpec((1, M, MEM_WIDTH), lambda b, i: (b, 0, 0)),
            pl.BlockSpec((1, M, MEM_WIDTH), lambda b, i: (b, 0, 0)),
            pl.BlockSpec(memory_space=pl.ANY),
            pl.BlockSpec((1, D_MODEL), lambda b, i: (0, 0)),
        ],
        out_specs=pl.BlockSpec((1, tm, D_MODEL), row),
        out_shape=jax.ShapeDtypeStruct((B, S, D_MODEL), F32),
        scratch_shapes=[
            pltpu.VMEM((tm + SWA_BLOCK, _SWA_KV_PAD), BF16),
            pltpu.VMEM((tm + SWA_BLOCK, _SWA_KV_PAD), BF16),
            pltpu.VMEM((tm, SWA_WIDTH), F32),
            pltpu.VMEM((tm, MEM_WIDTH), F32),
            pltpu.VMEM((tm, D_MODEL), F32),
            pltpu.VMEM((MIX_WIDTH, D_MODEL), BF16),
            pltpu.VMEM((_STAGE_SLOTS, _W_OUT_CHUNK, D_MODEL), F32),
            pltpu.SemaphoreType.DMA((_STAGE_SLOTS,)),
        ],
        compiler_params=_params(("arbitrary", "arbitrary")),
        name="tail",
    )(sinks, x, y_mla, z, z, z, qs, qs, ks, ks, vs, vs, bias, kmem, vmem, w, g)


def _rope_tables(seq):
    inv = 1.0 / (ROPE_THETA ** (jnp.arange(0, MLA_ROPE, 2, dtype=F32) / MLA_ROPE))
    step = 1 << ((seq - 1).bit_length() // 2)
    assert seq % step == 0
    fine = jnp.arange(step, dtype=F32)[:, None] * inv[None, :]
    coarse = (jnp.arange(seq // step, dtype=F32) * step)[:, None] * inv[None, :]
    cf, sf = jnp.cos(fine)[None], jnp.sin(fine)[None]
    cc, sc = jnp.cos(coarse)[:, None], jnp.sin(coarse)[:, None]
    cos = (cc * cf - sc * sf).reshape(seq, -1)
    sin = (sc * cf + cc * sf).reshape(seq, -1)
    zero = jnp.zeros_like(cos)
    return (jnp.concatenate([cos, cos, zero, zero], axis=-1),
            jnp.concatenate([-sin, sin, zero, zero], axis=-1))


def _t5_bucket(rel):
    nb = N_BUCKETS // 2
    max_exact = nb // 2
    bucket = jnp.where(rel > 0, nb, 0)
    n = jnp.abs(rel)
    nf = jnp.maximum(n, 1).astype(F32)
    large = max_exact + (jnp.log(nf / max_exact) / math.log(MAX_DISTANCE / max_exact)
                         * (nb - max_exact)).astype(jnp.int32)
    large = jnp.minimum(large, nb - 1)
    return bucket + jnp.where(n < max_exact, n, large)


def kernel(x, mem, norm_in, w_in, norm_q, norm_kv, w_uq, w_ukv, attn_sinks, rel_bias,
           norm_mem, w_mem_kv, w_out, norm_final):
    B, S, _ = x.shape
    assert norm_in.shape[0] == 1, "single-layer trunk"
    assert x.shape[2] == D_MODEL and w_in.shape[1:] == (D_MODEL, IN_WIDTH)
    assert S % _PROJ_ROWS == 0 and S % (_TAIL_BLOCKS * SWA_BLOCK) == 0

    wt = w_in[0].T
    cos_t, sin_t = _rope_tables(S)
    q, k, v, qs, z, ks, vs = _proj(x, norm_in, wt, norm_q, norm_kv, w_uq[0].astype(BF16),
                                   w_ukv[0].astype(BF16), cos_t, sin_t, tm=_PROJ_ROWS)

    y_mla = _mla(q, k, v, t=_MLA_TILE, g=_MLA_GROUP, nq=_MLA_TILES_PER_STEP)

    bucket = _t5_bucket(jnp.arange(_REL_SPAN) - (2 * SWA_BLOCK - 1)).astype(jnp.int32)
    bias, kmem, vmem = _bias_memkv(rel_bias, bucket, mem, norm_mem, w_mem_kv[0])
    return _tail(attn_sinks[0], x, y_mla, z, qs, ks, vs, bias, kmem, vmem,
                 w_out[0], norm_final[None, :], nsub=_TAIL_BLOCKS)
```

```python
import functools
import math

import jax
import jax.numpy as jnp
from jax import lax
from jax.experimental import pallas as pl
from jax.experimental.pallas import tpu as pltpu

D_MODEL = 2048
CHUNK = 64
EPS = 1e-6
NEG = -1e30

MLA_HEADS = 8
MLA_NOPE = 128
MLA_ROPE = 64
MLA_V = 128
MLA_QLORA = 512
MLA_KVLORA = 256
MLA_WIDTH = MLA_HEADS * MLA_V
MLA_QK_PAD = 256
ROPE_THETA = 10000.0

SWA_HEADS = 8
SWA_KV_HEADS = 2
SWA_GROUP = SWA_HEADS // SWA_KV_HEADS
SWA_HEAD_DIM = 64
SWA_WIDTH = SWA_HEADS * SWA_HEAD_DIM
SWA_KV_WIDTH = SWA_KV_HEADS * SWA_HEAD_DIM
WINDOW_CHUNKS = 2
SWA_BLOCK = 128

MEM_HEADS = 4
MEM_HEAD_DIM = 128
MEM_WIDTH = MEM_HEADS * MEM_HEAD_DIM

MIX_WIDTH = MLA_WIDTH + SWA_WIDTH + MEM_WIDTH

N_BUCKETS = 32
MAX_DISTANCE = 128

BF16 = jnp.bfloat16
F32 = jnp.float32

_R_CQ = 0
_R_CKV = _R_CQ + MLA_QLORA
_R_KPE = _R_CKV + MLA_KVLORA
_R_ZMLA = _R_KPE + MLA_ROPE
_R_QSWA = _R_ZMLA + MLA_WIDTH
_R_KSWA = _R_QSWA + SWA_WIDTH
_R_VSWA = _R_KSWA + SWA_KV_WIDTH
_R_ZSWA = _R_VSWA + SWA_KV_WIDTH
_R_QMEM = _R_ZSWA + SWA_WIDTH
_R_ZMEM = _R_QMEM + MEM_WIDTH
IN_WIDTH = _R_ZMEM + MEM_WIDTH

_MLA_Q_SCALE = (MLA_NOPE + MLA_ROPE) ** -0.5 * math.log2(math.e)

_SWA_KV_PAD = 2 * SWA_KV_HEADS * 128

_W_IN_CHUNK = 208
_W_OUT_CHUNK = 256
_STAGE_SLOTS = 4

_V7X_VMEM_BYTES = 64 * 1024 * 1024
_VMEM_LIMIT = _V7X_VMEM_BYTES - 8 * 1024 * 1024

_PROJ_ROWS = 512
_MLA_TILE = 512
_MLA_GROUP = 4
_MLA_TILES_PER_STEP = 4
_TAIL_BLOCKS = 4


def _params(sem):
    return pltpu.CompilerParams(dimension_semantics=sem, vmem_limit_bytes=_VMEM_LIMIT)


def _rms(v, g):
    return (v * lax.rsqrt(jnp.mean(v * v, axis=-1, keepdims=True) + EPS)) * g


def _dot(a, b):
    return jnp.dot(a, b, preferred_element_type=F32)


def _dot_nt(a, b):
    return lax.dot_general(a, b, (((1,), (1,)), ((), ())), preferred_element_type=F32)


def _load_weight_bf16(w_hbm, w_ref, stage_ref, sem_ref):
    slots, rows, _ = stage_ref.shape
    n_chunks = w_ref.shape[0] // rows
    ahead = slots - 1

    def chunk_copy(c):
        return pltpu.make_async_copy(w_hbm.at[pl.ds(c * rows, rows), :], stage_ref.at[c % slots],
                                     sem_ref.at[c % slots])

    for c in range(min(ahead, n_chunks)):
        chunk_copy(c).start()
    for c in range(n_chunks):
        if c + ahead < n_chunks:
            chunk_copy(c + ahead).start()
        chunk_copy(c).wait()
        w_ref[c * rows:(c + 1) * rows, :] = stage_ref[c % slots].astype(BF16)


def _first_grid_step():
    return jnp.logical_and(pl.program_id(0) == 0, pl.program_id(1) == 0)


def _lane_tile(v, n):
    return jnp.concatenate([v] * n, axis=1)


def _chunk_of(pos):
    return jnp.right_shift(pos, CHUNK.bit_length() - 1)


def _rope128(v, cos_t, sin_t):
    return v * cos_t + pltpu.roll(v, MLA_ROPE // 2, 1) * sin_t


def _store_swa_kv(ref, kv):
    low = lax.broadcasted_iota(jnp.int32, kv.shape, 1) < SWA_HEAD_DIM
    swapped = pltpu.roll(kv, SWA_HEAD_DIM, 1)
    pieces = (jnp.where(low, kv, 0.0), jnp.where(low, 0.0, swapped),
              jnp.where(low, swapped, 0.0), jnp.where(low, 0.0, kv))
    for i, piece in enumerate(pieces):
        ref[0, :, i * 128:(i + 1) * 128] = piece.astype(BF16)


def _proj_kernel(x_ref, gin_ref, wt_hbm, gq_ref, gkv_ref, wuq_ref, wukv_ref, cos_ref, sin_ref,
                 q_ref, k_ref, v_ref, qs_ref, z_ref, ks_ref, vs_ref, wt_ref, stage_ref, sem_ref):
    @pl.when(_first_grid_step())
    def _():
        _load_weight_bf16(wt_hbm, wt_ref, stage_ref, sem_ref)

    h = _rms(x_ref[0], gin_ref[...]).astype(BF16)
    cos_t = cos_ref[...]
    sin_t = sin_ref[...]

    def proj(lo, hi):
        return _dot_nt(h, wt_ref[lo:hi, :])

    pa = proj(_R_CQ, _R_KPE)
    cq = _rms(pa[:, :MLA_QLORA], gq_ref[...]).astype(BF16)
    ckv = _rms(pa[:, MLA_QLORA:], gkv_ref[...]).astype(BF16)

    qs_ref[0, :, :SWA_WIDTH] = proj(_R_QSWA, _R_KSWA).astype(BF16)
    qs_ref[0, :, SWA_WIDTH:] = proj(_R_QMEM, _R_ZMEM).astype(BF16)
    z_ref[0, :, :MLA_WIDTH] = proj(_R_ZMLA, _R_QSWA).astype(BF16)
    z_ref[0, :, MLA_WIDTH:MLA_WIDTH + SWA_WIDTH] = proj(_R_ZSWA, _R_QMEM).astype(BF16)
    z_ref[0, :, MLA_WIDTH + SWA_WIDTH:] = proj(_R_ZMEM, IN_WIDTH).astype(BF16)
    kvs = proj(_R_KSWA, _R_ZSWA)
    _store_swa_kv(ks_ref, kvs[:, :SWA_KV_WIDTH])
    _store_swa_kv(vs_ref, kvs[:, SWA_KV_WIDTH:])
    pe = proj(_R_KPE, _R_ZMLA)
    kpe = _rope128(jnp.concatenate([pe, pe], axis=1), cos_t, sin_t).astype(BF16)

    qall = _dot(cq, wuq_ref[...]) * _MLA_Q_SCALE
    kv = _dot(ckv, wukv_ref[...])
    for hd in range(MLA_HEADS):
        oq = hd * (MLA_NOPE + MLA_ROPE)
        q_pe = qall[:, oq + MLA_NOPE:oq + MLA_NOPE + MLA_ROPE]
        q_ref[0, hd, :, :MLA_NOPE] = qall[:, oq:oq + MLA_NOPE].astype(BF16)
        q_ref[0, hd, :, MLA_NOPE:] = _rope128(
            jnp.concatenate([q_pe, q_pe], axis=1), cos_t, sin_t).astype(BF16)
        o = hd * (MLA_NOPE + MLA_V)
        k_ref[0, hd, :, :MLA_NOPE] = kv[:, o:o + MLA_NOPE].astype(BF16)
        k_ref[0, hd, :, MLA_NOPE:] = kpe
        v_ref[0, hd] = kv[:, o + MLA_NOPE:o + MLA_QK_PAD].astype(BF16)


def _proj(x, gin, wt, gq, gkv, wuq, wukv, cos_t, sin_t, tm):
    B, S, _ = x.shape
    const = lambda b, i: (0, 0)
    row = lambda b, i: (b, i, 0)
    single = pl.Buffered(1)

    def out(width):
        return jax.ShapeDtypeStruct((B, S, width), BF16)

    head_spec = pl.BlockSpec((1, MLA_HEADS, tm, MLA_QK_PAD), lambda b, i: (b, 0, i, 0))
    head_shape = jax.ShapeDtypeStruct((B, MLA_HEADS, S, MLA_QK_PAD), BF16)

    return pl.pallas_call(
        _proj_kernel,
        grid=(B, S // tm),
        in_specs=[
            pl.BlockSpec((1, tm, D_MODEL), row),
            pl.BlockSpec((1, D_MODEL), const),
            pl.BlockSpec(memory_space=pl.ANY),
            pl.BlockSpec((1, MLA_QLORA), const),
            pl.BlockSpec((1, MLA_KVLORA), const),
            pl.BlockSpec((MLA_QLORA, MLA_HEADS * (MLA_NOPE + MLA_ROPE)), const, pipeline_mode=single),
            pl.BlockSpec((MLA_KVLORA, MLA_HEADS * (MLA_NOPE + MLA_V)), const, pipeline_mode=single),
            pl.BlockSpec((tm, 128), lambda b, i: (i, 0)),
            pl.BlockSpec((tm, 128), lambda b, i: (i, 0)),
        ],
        out_specs=[
            head_spec, head_spec,
            pl.BlockSpec((1, MLA_HEADS, tm, MLA_V), lambda b, i: (b, 0, i, 0)),
            pl.BlockSpec((1, tm, SWA_WIDTH + MEM_WIDTH), row),
            pl.BlockSpec((1, tm, MIX_WIDTH), row),
            pl.BlockSpec((1, tm, _SWA_KV_PAD), row),
            pl.BlockSpec((1, tm, _SWA_KV_PAD), row),
        ],
        out_shape=[
            head_shape, head_shape, jax.ShapeDtypeStruct((B, MLA_HEADS, S, MLA_V), BF16),
            out(SWA_WIDTH + MEM_WIDTH), out(MIX_WIDTH), out(_SWA_KV_PAD), out(_SWA_KV_PAD),
        ],
        scratch_shapes=[
            pltpu.VMEM((IN_WIDTH, D_MODEL), BF16),
            pltpu.VMEM((_STAGE_SLOTS, _W_IN_CHUNK, D_MODEL), F32),
            pltpu.SemaphoreType.DMA((_STAGE_SLOTS,)),
        ],
        compiler_params=_params(("arbitrary", "arbitrary")),
        name="proj",
    )(x, gin, wt, gq, gkv, wuq, wukv, cos_t, sin_t)


def _mla_kernel(q_ref, k_ref, v_ref, o_ref, acc_ref, m_ref, s_ref, cm_ref, *, t, g, nq):
    qi0 = pl.program_id(2) * nq
    qi_end = qi0 + nq - 1
    acc_ref[...] = jnp.zeros(acc_ref.shape, F32)
    m_ref[...] = jnp.full(m_ref.shape, NEG, F32)
    ones = jnp.ones((t, MLA_V), BF16)
    row_halves = ((0, t // 2, t // 2), (t // 2, t // 2, t))

    def row_max(s):
        return jnp.broadcast_to(jnp.max(s, axis=-1, keepdims=True), (s.shape[0], 128))

    def diagonal_mask():
        q_chunk = _chunk_of(lax.broadcasted_iota(jnp.int32, (128, 128), 0))
        k_chunk = _chunk_of(lax.broadcasted_iota(jnp.int32, (128, 128), 1))
        return k_chunk <= q_chunk

    def masked_band(band, mask):
        width = band.shape[1]
        block = jnp.where(mask, band[:, width - 128:], NEG)
        return block if width == 128 else jnp.concatenate([band[:, :width - 128], block], axis=1)

    def scores(qi, c, slot, hd, mask=None):
        rows = pl.multiple_of((qi - qi0) * t, t)
        keys = pl.multiple_of(c * t, t)
        if mask is None:
            s = _dot_nt(q_ref[0, hd, pl.ds(rows, t), :], k_ref[0, hd, pl.ds(keys, t), :])
            s_ref[slot, hd] = s
            cm_ref[slot, hd] = row_max(s)
            return
        for r0, nr, nk in row_halves:
            s = _dot_nt(q_ref[0, hd, pl.ds(pl.multiple_of(rows + r0, nr), nr), :],
                        k_ref[0, hd, pl.ds(keys, nk), :])
            for a in range(0, nr, 128):
                reach = r0 + a + 128
                band = masked_band(s[a:a + 128, :reach], mask)
                s_ref[slot, hd, r0 + a:reach, :reach] = band
                cm_ref[slot, hd, r0 + a:reach] = row_max(band)

    def update(c, slot, hd, diagonal, mask):
        keys = pl.multiple_of(c * t, t)
        if not diagonal:
            m_old = m_ref[hd]
            m_new = jnp.maximum(m_old, cm_ref[slot, hd])
            alpha = jnp.exp2(m_old - m_new)
            p = jnp.exp2(s_ref[slot, hd] - _lane_tile(m_new, t // 128)).astype(BF16)
            v_ones = jnp.concatenate([v_ref[0, hd, pl.ds(keys, t), :], ones], axis=1)
            acc_ref[hd] = _lane_tile(alpha, MLA_QK_PAD // 128) * acc_ref[hd] + _dot(p, v_ones)
            m_ref[hd] = m_new
            return
        for r0, nr, nk in row_halves:
            alphas, probs = [], []
            for a in range(0, nr, 128):
                reach = r0 + a + 128
                band = s_ref[slot, hd, r0 + a:reach, :reach]
                if mask is not None:
                    band = masked_band(band, mask)
                    m_cur = row_max(band)
                else:
                    m_cur = cm_ref[slot, hd, r0 + a:reach]
                m_old = m_ref[hd, r0 + a:reach]
                m_new = jnp.maximum(m_old, m_cur)
                alphas.append(jnp.exp2(m_old - m_new))
                p = jnp.exp2(band - _lane_tile(m_new, reach // 128)).astype(BF16)
                if reach < nk:
                    p = jnp.concatenate([p, jnp.zeros((128, nk - reach), BF16)], axis=1)
                probs.append(p)
                m_ref[hd, r0 + a:reach] = m_new
            rows = slice(r0, r0 + nr)
            v_ones = jnp.concatenate([v_ref[0, hd, pl.ds(keys, nk), :], ones[:nk]], axis=1)
            acc_ref[hd, rows] = (
                _lane_tile(jnp.concatenate(alphas, axis=0), MLA_QK_PAD // 128) * acc_ref[hd, rows]
                + _dot(jnp.concatenate(probs, axis=0), v_ones))

    def finish_tile(qi, hd):
        rows = pl.multiple_of((qi - qi0) * t, t)
        acc = acc_ref[hd]
        o_ref[0, pl.ds(rows, t), hd * MLA_V:(hd + 1) * MLA_V] = (
            acc[:, :MLA_V] / acc[:, MLA_V:]).astype(BF16)
        acc_ref[hd] = jnp.zeros(acc.shape, F32)
        m_ref[hd] = jnp.full((t, 128), NEG, F32)

    def following(qi, c):
        is_last = c == qi
        return jnp.where(is_last, jnp.minimum(qi + 1, qi_end), qi), jnp.where(is_last, 0, c + 1)

    def step(qi, c, slot, diagonal, premasked=False, mask_next=False, score_next=True):
        qn, cn = following(qi, c)
        mask = diagonal_mask() if diagonal and not premasked else None
        next_mask = diagonal_mask() if mask_next else None
        for hd in range(g):
            if score_next:
                scores(qn, cn, 1 - slot, hd, next_mask)
            update(c, slot, hd, diagonal, mask)
            if diagonal:
                finish_tile(qi, hd)

    for hd in range(g):
        scores(qi0, 0, 0, hd)

    def body(_, pair):
        first, second = pair, following(*pair)
        diag0 = first[1] == first[0]
        diag1 = second[1] == second[0]
        for d0, d1, cond in ((True, False, diag0), (False, True, diag1),
                             (False, False, jnp.logical_not(jnp.logical_or(diag0, diag1)))):
            @pl.when(cond)
            def _(d0=d0, d1=d1):
                step(*first, 0, d0, mask_next=d1)
                step(*second, 1, d1, premasked=d1)
        return following(*second)

    n_steps = nq * qi0 + nq * (nq + 1) // 2
    first = lax.fori_loop(0, n_steps // 2 - 1, body, (qi0, 0))
    second = following(*first)
    step(*first, 0, False, mask_next=True)
    step(*second, 1, True, premasked=True, score_next=False)


def _mla(q, k, v, t, g, nq):
    B, _, S, _ = q.shape
    assert S % (nq * t) == 0 and nq % 2 == 0 and (nq * (nq + 1) // 2) % 2 == 0
    return pl.pallas_call(
        functools.partial(_mla_kernel, t=t, g=g, nq=nq),
        grid=(B, MLA_HEADS // g, S // (nq * t)),
        in_specs=[
            pl.BlockSpec((1, g, nq * t, MLA_QK_PAD), lambda b, h, i: (b, h, i, 0)),
            pl.BlockSpec((1, g, S, MLA_QK_PAD), lambda b, h, i: (b, h, 0, 0)),
            pl.BlockSpec((1, g, S, MLA_V), lambda b, h, i: (b, h, 0, 0)),
        ],
        out_specs=pl.BlockSpec((1, nq * t, g * MLA_V), lambda b, h, i: (b, i, h)),
        out_shape=jax.ShapeDtypeStruct((B, S, MLA_WIDTH), BF16),
        scratch_shapes=[
            pltpu.VMEM((g, t, MLA_QK_PAD), F32),
            pltpu.VMEM((g, t, 128), F32),
            pltpu.VMEM((2, g, t, t), F32),
            pltpu.VMEM((2, g, t, 128), F32),
        ],
        compiler_params=_params(("arbitrary", "arbitrary", "arbitrary")),
        name="mla",
    )(q, k, v)


_REL_SPAN = 4 * SWA_BLOCK
_N_OFFSETS = 3 * SWA_BLOCK - 1


def _bias_memkv_kernel(table_ref, bucket_ref, mem_hbm, g_ref, w_hbm, o_ref, k_ref, v_ref,
                       spread_ref, mem_ref, w_ref, sem_ref):
    fetches = [pltpu.make_async_copy(mem_hbm, mem_ref, sem_ref.at[0]),
               pltpu.make_async_copy(w_hbm, w_ref, sem_ref.at[1])]
    for fetch in fetches:
        fetch.start()

    for b in range(N_BUCKETS):
        spread_ref[b] = jnp.concatenate(
            [jnp.full((1, 128), table_ref[b, hd], F32) for hd in range(SWA_HEADS)], axis=0)
    lane = lax.broadcasted_iota(jnp.int32, (SWA_HEADS, _REL_SPAN), 1)

    def lookup(c, per_offset):
        col = spread_ref[bucket_ref[c]]
        return jnp.where(lane == c, _lane_tile(col, _REL_SPAN // 128), per_offset)

    per_offset_all = lax.fori_loop(0, _N_OFFSETS, lookup, jnp.zeros((SWA_HEADS, _REL_SPAN), F32))
    q_chunk = _chunk_of(lax.broadcasted_iota(jnp.int32, (SWA_BLOCK, 2 * SWA_BLOCK), 0))
    b_chunk = _chunk_of(lax.broadcasted_iota(jnp.int32, (SWA_BLOCK, 2 * SWA_BLOCK), 1))
    in_window = jnp.logical_and(b_chunk >= q_chunk, b_chunk <= q_chunk + WINDOW_CHUNKS)
    for hd in range(SWA_HEADS):
        per_offset = per_offset_all[hd:hd + 1, :]
        rows = jnp.broadcast_to(per_offset, (SWA_BLOCK, _REL_SPAN))
        band = pltpu.roll(rows, _REL_SPAN - (SWA_BLOCK - 1), 1, stride=1, stride_axis=0)
        kvh, within = divmod(hd, SWA_GROUP)
        par, half = within % 2, within // 2
        o_ref[kvh, par, half * SWA_BLOCK:(half + 1) * SWA_BLOCK, :] = jnp.where(
            in_window, band[:, :2 * SWA_BLOCK], NEG)

    for fetch in fetches:
        fetch.wait()
    w = w_ref[...].astype(BF16)
    for b in range(mem_ref.shape[0]):
        kv = _dot(_rms(mem_ref[b], g_ref[...]).astype(BF16), w)
        k_ref[b] = kv[:, :MEM_WIDTH].astype(BF16)
        v_ref[b] = kv[:, MEM_WIDTH:].astype(BF16)


def _bias_memkv(rel_table, bucket, mem, g, w):
    B, M, _ = mem.shape
    smem = pl.BlockSpec(memory_space=pltpu.SMEM)
    vmem = pl.BlockSpec(memory_space=pltpu.VMEM)
    hbm = pl.BlockSpec(memory_space=pl.ANY)
    return pl.pallas_call(
        _bias_memkv_kernel,
        in_specs=[smem, smem, hbm, vmem, hbm],
        out_specs=[vmem] * 3,
        out_shape=[jax.ShapeDtypeStruct((SWA_KV_HEADS, 2, 2 * SWA_BLOCK, 2 * SWA_BLOCK), F32),
                   jax.ShapeDtypeStruct((B, M, MEM_WIDTH), BF16),
                   jax.ShapeDtypeStruct((B, M, MEM_WIDTH), BF16)],
        scratch_shapes=[
            pltpu.VMEM((N_BUCKETS, SWA_HEADS, 128), F32),
            pltpu.VMEM(mem.shape, F32),
            pltpu.VMEM(w.shape, F32),
            pltpu.SemaphoreType.DMA((2,)),
        ],
        compiler_params=pltpu.CompilerParams(vmem_limit_bytes=_VMEM_LIMIT),
        name="t5bias_memkv",
    )(rel_table, bucket, mem, g, w)


def _swa_into(y_ref, sink_ref, q_ref, kp_ref, kc_ref, vp_ref, vc_ref, bias_ref, kb_ref, vb_ref, nsub,
              beside_unit):
    t = pl.program_id(1)
    scale = SWA_HEAD_DIM ** -0.5
    sb = SWA_BLOCK
    kb_ref[:sb] = kp_ref[0]
    kb_ref[sb:] = kc_ref[0]
    vb_ref[:sb] = vp_ref[0]
    vb_ref[sb:] = vc_ref[0]

    b_chunk = _chunk_of(lax.broadcasted_iota(jnp.int32, (2 * sb, 2 * sb), 1))
    has_key = b_chunk >= jnp.where(t > 0, 0, 2)
    upper = lax.broadcasted_iota(jnp.int32, (2 * sb, 1), 0) < sb

    def geometry(u):
        r, kvh = divmod(u, SWA_KV_HEADS)
        rows = slice(r * sb, (r + 1) * sb)
        band = slice(r * sb, (r + 2) * sb)
        pair0 = slice(2 * kvh * 128, (2 * kvh + 1) * 128)
        pair1 = slice((2 * kvh + 1) * 128, (2 * kvh + 2) * 128)
        return r, kvh, rows, band, pair0, pair1

    def scores(u):
        r, kvh, rows, band, pair0, pair1 = geometry(u)
        q = jnp.concatenate([q_ref[0, rows, pair0], q_ref[0, rows, pair1]], axis=0)
        return [_dot_nt(q, kb_ref[band, (2 * kvh + par) * 128:(2 * kvh + par + 1) * 128])
                for par in range(2)]

    def attend(u, unit_scores):
        r, kvh, rows, band, pair0, pair1 = geometry(u)
        o = None
        for par, s in enumerate(unit_scores):
            kcols = slice((2 * kvh + par) * 128, (2 * kvh + par + 1) * 128)
            hd = SWA_GROUP * kvh + par
            s = s * scale + bias_ref[kvh, par]
            if r == 0:
                s = jnp.where(has_key, s, NEG)
            sink = jnp.where(upper, sink_ref[hd], sink_ref[hd + 2])
            m = jnp.maximum(jnp.max(s, axis=-1, keepdims=True), sink)
            p = jnp.exp(s - m)
            den = jnp.sum(p, axis=-1, keepdims=True) + jnp.exp(sink - m)
            pv = _dot((p * (1.0 / den)).astype(BF16), vb_ref[band, kcols])
            o = pv if o is None else o + pv
        y_ref[rows, pair0] = o[:sb]
        y_ref[rows, pair1] = o[sb:]

    n_units = nsub * SWA_KV_HEADS
    pending = scores(0)
    for u in range(n_units):
        upcoming = scores(u + 1) if u + 1 < n_units else None
        beside_unit(u)
        attend(u, pending)
        pending = upcoming


def _mem_into(y_ref, q_ref, k_ref, v_ref, after_head):
    scale = MEM_HEAD_DIM ** -0.5

    def head_cols(hd):
        return slice(hd * MEM_HEAD_DIM, (hd + 1) * MEM_HEAD_DIM)

    def scores(hd):
        return _dot_nt(q_ref[0, :, head_cols(hd)], k_ref[0, :, head_cols(hd)])

    pending = scores(0)
    for hd in range(MEM_HEADS):
        upcoming = scores(hd + 1) if hd + 1 < MEM_HEADS else None
        s = pending * scale
        m = jnp.max(s, axis=-1, keepdims=True)
        p = jnp.exp(s - m)
        inv = 1.0 / jnp.sum(p, axis=-1, keepdims=True)
        y_ref[:, head_cols(hd)] = _dot((p * inv).astype(BF16), v_ref[0, :, head_cols(hd)])
        after_head(hd)
        pending = upcoming


def _tail_kernel(sink_ref, x_ref, ya_ref, za_ref, zb_ref, zc_ref, qs_ref, qm_ref,
                 kp_ref, kc_ref, vp_ref, vc_ref, bias_ref, km_ref, vm_ref, w_hbm, g_ref,
                 o_ref, kb_ref, vb_ref, yb_ref, yc_ref, yacc_ref, w_ref, stage_ref, sem_ref, *, nsub):
    @pl.when(_first_grid_step())
    def _():
        _load_weight_bf16(w_hbm, w_ref, stage_ref, sem_ref)

    def gated(y, z_ref):
        h = 0.5 * z_ref[0].astype(F32)
        return (y * (h + h * jnp.tanh(h))).astype(BF16)

    swa_lo, mem_lo = MLA_WIDTH, MLA_WIDTH + SWA_WIDTH

    g_mla = gated(ya_ref[0].astype(F32), za_ref)
    mla_chunk = D_MODEL // (nsub * SWA_KV_HEADS)

    def project_mla(u):
        cols = slice(u * mla_chunk, (u + 1) * mla_chunk)
        yacc_ref[:, cols] = _dot(g_mla, w_ref[:swa_lo, cols])

    _swa_into(yb_ref, sink_ref, qs_ref, kp_ref, kc_ref, vp_ref, vc_ref, bias_ref, kb_ref, vb_ref, nsub,
              project_mla)

    g_swa = gated(yb_ref[...], zb_ref)
    swa_chunk = D_MODEL // MEM_HEADS

    def project_swa(u):
        cols = slice(u * swa_chunk, (u + 1) * swa_chunk)
        yacc_ref[:, cols] += _dot(g_swa, w_ref[swa_lo:mem_lo, cols])

    _mem_into(yc_ref, qm_ref, km_ref, vm_ref, project_swa)

    g_mem = gated(yc_ref[...], zc_ref)
    sum_sq = jnp.zeros((g_mem.shape[0], 1), F32)
    chunks = [slice(j * swa_chunk, (j + 1) * swa_chunk) for j in range(D_MODEL // swa_chunk)]
    for cols in chunks:
        r = x_ref[0, :, cols] + yacc_ref[:, cols] + _dot(g_mem, w_ref[mem_lo:, cols])
        o_ref[0, :, cols] = r
        sum_sq = sum_sq + jnp.sum(r * r, axis=-1, keepdims=True)
    inv = lax.rsqrt(sum_sq * (1.0 / D_MODEL) + EPS)
    for cols in chunks:
        o_ref[0, :, cols] = (o_ref[0, :, cols] * inv) * g_ref[:, cols]


def _tail(sinks, x, y_mla, z, qs, ks, vs, bias, kmem, vmem, w, g, nsub):
    B, S, _ = x.shape
    tm = nsub * SWA_BLOCK
    M = kmem.shape[1]
    row = lambda b, i: (b, i, 0)
    prev = lambda b, i: (b, jnp.maximum(i * nsub - 1, 0), 0)
    col = lambda c: (lambda b, i: (b, i, c))
    return pl.pallas_call(
        functools.partial(_tail_kernel, nsub=nsub),
        grid=(B, S // tm),
        in_specs=[
            pl.BlockSpec(memory_space=pltpu.SMEM),
            pl.BlockSpec((1, tm, D_MODEL), row),
            pl.BlockSpec((1, tm, MLA_WIDTH), row),
            pl.BlockSpec((1, tm, MLA_WIDTH), col(0)),
            pl.BlockSpec((1, tm, SWA_WIDTH), col(MLA_WIDTH // SWA_WIDTH)),
            pl.BlockSpec((1, tm, MEM_WIDTH), col((MLA_WIDTH + SWA_WIDTH) // MEM_WIDTH)),
            pl.BlockSpec((1, tm, SWA_WIDTH), col(0)),
            pl.BlockSpec((1, tm, MEM_WIDTH), col(1)),
            pl.BlockSpec((1, SWA_BLOCK, _SWA_KV_PAD), prev),
            pl.BlockSpec((1, tm, _SWA_KV_PAD), row),
            pl.BlockSpec((1, SWA_BLOCK, _SWA_KV_PAD), prev),
            pl.BlockSpec((1, tm, _SWA_KV_PAD), row),
            pl.BlockSpec((SWA_KV_HEADS, 2, 2 * SWA_BLOCK, 2 * SWA_BLOCK), lambda b, i: (0, 0, 0, 0)),
            pl.BlockSpec((1, M, MEM_WIDTH), lambda b, i: (b, 0, 0)),
            pl.BlockSpec((1, M, MEM_WIDTH), lambda b, i: (b, 0, 0)),
            pl.BlockSpec(memory_space=pl.ANY),
            pl.BlockSpec((1, D_MODEL), lambda b, i: (0, 0)),
        ],
        out_specs=pl.BlockSpec((1, tm, D_MODEL), row),
        out_shape=jax.ShapeDtypeStruct((B, S, D_MODEL), F32),
        scratch_shapes=[
            pltpu.VMEM((tm + SWA_BLOCK, _SWA_KV_PAD), BF16),
            pltpu.VMEM((tm + SWA_BLOCK, _SWA_KV_PAD), BF16),
            pltpu.VMEM((tm, SWA_WIDTH), F32),
            pltpu.VMEM((tm, MEM_WIDTH), F32),
            pltpu.VMEM((tm, D_MODEL), F32),
            pltpu.VMEM((MIX_WIDTH, D_MODEL), BF16),
            pltpu.VMEM((_STAGE_SLOTS, _W_OUT_CHUNK, D_MODEL), F32),
            pltpu.SemaphoreType.DMA((_STAGE_SLOTS,)),
        ],
        compiler_params=_params(("arbitrary", "arbitrary")),
        name="tail",
    )(sinks, x, y_mla, z, z, z, qs, qs, ks, ks, vs, vs, bias, kmem, vmem, w, g)


def _rope_tables(seq):
    inv = 1.0 / (ROPE_THETA ** (jnp.arange(0, MLA_ROPE, 2, dtype=F32) / MLA_ROPE))
    step = 1 << ((seq - 1).bit_length() // 2)
    assert seq % step == 0
    fine = jnp.arange(step, dtype=F32)[:, None] * inv[None, :]
    coarse = (jnp.arange(seq // step, dtype=F32) * step)[:, None] * inv[None, :]
    cf, sf = jnp.cos(fine)[None], jnp.sin(fine)[None]
    cc, sc = jnp.cos(coarse)[:, None], jnp.sin(coarse)[:, None]
    cos = (cc * cf - sc * sf).reshape(seq, -1)
    sin = (sc * cf + cc * sf).reshape(seq, -1)
    zero = jnp.zeros_like(cos)
    return (jnp.concatenate([cos, cos, zero, zero], axis=-1),
            jnp.concatenate([-sin, sin, zero, zero], axis=-1))


def _t5_bucket(rel):
    nb = N_BUCKETS // 2
    max_exact = nb // 2
    bucket = jnp.where(rel > 0, nb, 0)
    n = jnp.abs(rel)
    nf = jnp.maximum(n, 1).astype(F32)
    large = max_exact + (jnp.log(nf / max_exact) / math.log(MAX_DISTANCE / max_exact)
                         * (nb - max_exact)).astype(jnp.int32)
    large = jnp.minimum(large, nb - 1)
    return bucket + jnp.where(n < max_exact, n, large)


def kernel(x, mem, norm_in, w_in, norm_q, norm_kv, w_uq, w_ukv, attn_sinks, rel_bias,
           norm_mem, w_mem_kv, w_out, norm_final):
    B, S, _ = x.shape
    assert norm_in.shape[0] == 1, "single-layer trunk"
    assert x.shape[2] == D_MODEL and w_in.shape[1:] == (D_MODEL, IN_WIDTH)
    assert S % _PROJ_ROWS == 0 and S % (_TAIL_BLOCKS * SWA_BLOCK) == 0

    wt = w_in[0].T
    cos_t, sin_t = _rope_tables(S)
    q, k, v, qs, z, ks, vs = _proj(x, norm_in, wt, norm_q, norm_kv, w_uq[0].astype(BF16),
                                   w_ukv[0].astype(BF16), cos_t, sin_t, tm=_PROJ_ROWS)

    y_mla = _mla(q, k, v, t=_MLA_TILE, g=_MLA_GROUP, nq=_MLA_TILES_PER_STEP)

    bucket = _t5_bucket(jnp.arange(_REL_SPAN) - (2 * SWA_BLOCK - 1)).astype(jnp.int32)
    bias, kmem, vmem = _bias_memkv(rel_bias, bucket, mem, norm_mem, w_mem_kv[0])
    return _tail(attn_sinks[0], x, y_mla, z, qs, ks, vs, bias, kmem, vmem,
                 w_out[0], norm_final[None, :], nsub=_TAIL_BLOCKS)
```

```python
import functools
import math

import jax
import jax.numpy as jnp
from jax import lax
from jax.experimental import pallas as pl
from jax.experimental.pallas import tpu as pltpu

D_MODEL = 2048
CHUNK = 64
EPS = 1e-6
NEG = -1e30

MLA_HEADS = 8
MLA_NOPE = 128
MLA_ROPE = 64
MLA_V = 128
MLA_QLORA = 512
MLA_KVLORA = 256
MLA_WIDTH = MLA_HEADS * MLA_V
MLA_QK_PAD = 256
ROPE_THETA = 10000.0

SWA_HEADS = 8
SWA_KV_HEADS = 2
SWA_GROUP = SWA_HEADS // SWA_KV_HEADS
SWA_HEAD_DIM = 64
SWA_WIDTH = SWA_HEADS * SWA_HEAD_DIM
SWA_KV_WIDTH = SWA_KV_HEADS * SWA_HEAD_DIM
WINDOW_CHUNKS = 2
SWA_BLOCK = 128

MEM_HEADS = 4
MEM_HEAD_DIM = 128
MEM_WIDTH = MEM_HEADS * MEM_HEAD_DIM

MIX_WIDTH = MLA_WIDTH + SWA_WIDTH + MEM_WIDTH

N_BUCKETS = 32
MAX_DISTANCE = 128

BF16 = jnp.bfloat16
F32 = jnp.float32

_R_CQ = 0
_R_CKV = _R_CQ + MLA_QLORA
_R_KPE = _R_CKV + MLA_KVLORA
_R_ZMLA = _R_KPE + MLA_ROPE
_R_QSWA = _R_ZMLA + MLA_WIDTH
_R_KSWA = _R_QSWA + SWA_WIDTH
_R_VSWA = _R_KSWA + SWA_KV_WIDTH
_R_ZSWA = _R_VSWA + SWA_KV_WIDTH
_R_QMEM = _R_ZSWA + SWA_WIDTH
_R_ZMEM = _R_QMEM + MEM_WIDTH
IN_WIDTH = _R_ZMEM + MEM_WIDTH

_MLA_Q_SCALE = (MLA_NOPE + MLA_ROPE) ** -0.5 * math.log2(math.e)

_SWA_KV_PAD = 2 * SWA_KV_HEADS * 128

_W_IN_CHUNK = 208
_W_OUT_CHUNK = 256
_STAGE_SLOTS = 4

_V7X_VMEM_BYTES = 64 * 1024 * 1024
_VMEM_LIMIT = _V7X_VMEM_BYTES - 8 * 1024 * 1024

_PROJ_ROWS = 512
_MLA_TILE = 512
_MLA_GROUP = 4
_MLA_TILES_PER_STEP = 4
_TAIL_BLOCKS = 4


def _params(sem):
    return pltpu.CompilerParams(dimension_semantics=sem, vmem_limit_bytes=_VMEM_LIMIT)


def _rms(v, g):
    return (v * lax.rsqrt(jnp.mean(v * v, axis=-1, keepdims=True) + EPS)) * g


def _dot(a, b):
    return jnp.dot(a, b, preferred_element_type=F32)


def _dot_nt(a, b):
    return lax.dot_general(a, b, (((1,), (1,)), ((), ())), preferred_element_type=F32)


def _load_weight_bf16(w_hbm, w_ref, stage_ref, sem_ref):
    slots, rows, _ = stage_ref.shape
    n_chunks = w_ref.shape[0] // rows
    ahead = slots - 1

    def chunk_copy(c):
        return pltpu.make_async_copy(w_hbm.at[pl.ds(c * rows, rows), :], stage_ref.at[c % slots],
                                     sem_ref.at[c % slots])

    for c in range(min(ahead, n_chunks)):
        chunk_copy(c).start()
    for c in range(n_chunks):
        if c + ahead < n_chunks:
            chunk_copy(c + ahead).start()
        chunk_copy(c).wait()
        w_ref[c * rows:(c + 1) * rows, :] = stage_ref[c % slots].astype(BF16)


def _first_grid_step():
    return jnp.logical_and(pl.program_id(0) == 0, pl.program_id(1) == 0)


def _lane_tile(v, n):
    return jnp.concatenate([v] * n, axis=1)


def _chunk_of(pos):
    return jnp.right_shift(pos, CHUNK.bit_length() - 1)


def _rope128(v, cos_t, sin_t):
    return v * cos_t + pltpu.roll(v, MLA_ROPE // 2, 1) * sin_t


def _store_swa_kv(ref, kv):
    low = lax.broadcasted_iota(jnp.int32, kv.shape, 1) < SWA_HEAD_DIM
    swapped = pltpu.roll(kv, SWA_HEAD_DIM, 1)
    pieces = (jnp.where(low, kv, 0.0), jnp.where(low, 0.0, swapped),
              jnp.where(low, swapped, 0.0), jnp.where(low, 0.0, kv))
    for i, piece in enumerate(pieces):
        ref[0, :, i * 128:(i + 1) * 128] = piece.astype(BF16)


def _proj_kernel(x_ref, gin_ref, wt_hbm, gq_ref, gkv_ref, wuq_ref, wukv_ref, cos_ref, sin_ref,
                 q_ref, k_ref, v_ref, qs_ref, z_ref, ks_ref, vs_ref, wt_ref, stage_ref, sem_ref):
    @pl.when(_first_grid_step())
    def _():
        _load_weight_bf16(wt_hbm, wt_ref, stage_ref, sem_ref)

    h = _rms(x_ref[0], gin_ref[...]).astype(BF16)
    cos_t = cos_ref[...]
    sin_t = sin_ref[...]

    def proj(lo, hi):
        return _dot_nt(h, wt_ref[lo:hi, :])

    pa = proj(_R_CQ, _R_KPE)
    cq = _rms(pa[:, :MLA_QLORA], gq_ref[...]).astype(BF16)
    ckv = _rms(pa[:, MLA_QLORA:], gkv_ref[...]).astype(BF16)

    qs_ref[0, :, :SWA_WIDTH] = proj(_R_QSWA, _R_KSWA).astype(BF16)
    qs_ref[0, :, SWA_WIDTH:] = proj(_R_QMEM, _R_ZMEM).astype(BF16)
    z_ref[0, :, :MLA_WIDTH] = proj(_R_ZMLA, _R_QSWA).astype(BF16)
    z_ref[0, :, MLA_WIDTH:MLA_WIDTH + SWA_WIDTH] = proj(_R_ZSWA, _R_QMEM).astype(BF16)
    z_ref[0, :, MLA_WIDTH + SWA_WIDTH:] = proj(_R_ZMEM, IN_WIDTH).astype(BF16)
    kvs = proj(_R_KSWA, _R_ZSWA)
    _store_swa_kv(ks_ref, kvs[:, :SWA_KV_WIDTH])
    _store_swa_kv(vs_ref, kvs[:, SWA_KV_WIDTH:])
    pe = proj(_R_KPE, _R_ZMLA)
    kpe = _rope128(jnp.concatenate([pe, pe], axis=1), cos_t, sin_t).astype(BF16)

    qall = _dot(cq, wuq_ref[...]) * _MLA_Q_SCALE
    kv = _dot(ckv, wukv_ref[...])
    for hd in range(MLA_HEADS):
        oq = hd * (MLA_NOPE + MLA_ROPE)
        q_pe = qall[:, oq + MLA_NOPE:oq + MLA_NOPE + MLA_ROPE]
        q_ref[0, hd, :, :MLA_NOPE] = qall[:, oq:oq + MLA_NOPE].astype(BF16)
        q_ref[0, hd, :, MLA_NOPE:] = _rope128(
            jnp.concatenate([q_pe, q_pe], axis=1), cos_t, sin_t).astype(BF16)
        o = hd * (MLA_NOPE + MLA_V)
        k_ref[0, hd, :, :MLA_NOPE] = kv[:, o:o + MLA_NOPE].astype(BF16)
        k_ref[0, hd, :, MLA_NOPE:] = kpe
        v_ref[0, hd] = kv[:, o + MLA_NOPE:o + MLA_QK_PAD].astype(BF16)


def _proj(x, gin, wt, gq, gkv, wuq, wukv, cos_t, sin_t, tm):
    B, S, _ = x.shape
    const = lambda b, i: (0, 0)
    row = lambda b, i: (b, i, 0)
    single = pl.Buffered(1)

    def out(width):
        return jax.ShapeDtypeStruct((B, S, width), BF16)

    head_spec = pl.BlockSpec((1, MLA_HEADS, tm, MLA_QK_PAD), lambda b, i: (b, 0, i, 0))
    head_shape = jax.ShapeDtypeStruct((B, MLA_HEADS, S, MLA_QK_PAD), BF16)

    return pl.pallas_call(
        _proj_kernel,
        grid=(B, S // tm),
        in_specs=[
            pl.BlockSpec((1, tm, D_MODEL), row),
            pl.BlockSpec((1, D_MODEL), const),
            pl.BlockSpec(memory_space=pl.ANY),
            pl.BlockSpec((1, MLA_QLORA), const),
            pl.BlockSpec((1, MLA_KVLORA), const),
            pl.BlockSpec((MLA_QLORA, MLA_HEADS * (MLA_NOPE + MLA_ROPE)), const, pipeline_mode=single),
            pl.BlockSpec((MLA_KVLORA, MLA_HEADS * (MLA_NOPE + MLA_V)), const, pipeline_mode=single),
            pl.BlockSpec((tm, 128), lambda b, i: (i, 0)),
            pl.BlockSpec((tm, 128), lambda b, i: (i, 0)),
        ],
        out_specs=[
            head_spec, head_spec,
            pl.BlockSpec((1, MLA_HEADS, tm, MLA_V), lambda b, i: (b, 0, i, 0)),
            pl.BlockSpec((1, tm, SWA_WIDTH + MEM_WIDTH), row),
            pl.BlockSpec((1, tm, MIX_WIDTH), row),
            pl.BlockSpec((1, tm, _SWA_KV_PAD), row),
            pl.BlockSpec((1, tm, _SWA_KV_PAD), row),
        ],
        out_shape=[
            head_shape, head_shape, jax.ShapeDtypeStruct((B, MLA_HEADS, S, MLA_V), BF16),
            out(SWA_WIDTH + MEM_WIDTH), out(MIX_WIDTH), out(_SWA_KV_PAD), out(_SWA_KV_PAD),
        ],
        scratch_shapes=[
            pltpu.VMEM((IN_WIDTH, D_MODEL), BF16),
            pltpu.VMEM((_STAGE_SLOTS, _W_IN_CHUNK, D_MODEL), F32),
            pltpu.SemaphoreType.DMA((_STAGE_SLOTS,)),
        ],
        compiler_params=_params(("arbitrary", "arbitrary")),
        name="proj",
    )(x, gin, wt, gq, gkv, wuq, wukv, cos_t, sin_t)


def _mla_kernel(q_ref, k_ref, v_ref, o_ref, acc_ref, m_ref, s_ref, cm_ref, *, t, g, nq):
    qi0 = pl.program_id(2) * nq
    qi_end = qi0 + nq - 1
    acc_ref[...] = jnp.zeros(acc_ref.shape, F32)
    m_ref[...] = jnp.full(m_ref.shape, NEG, F32)
    ones = jnp.ones((t, MLA_V), BF16)
    row_halves = ((0, t // 2, t // 2), (t // 2, t // 2, t))

    def row_max(s):
        return jnp.broadcast_to(jnp.max(s, axis=-1, keepdims=True), (s.shape[0], 128))

    def diagonal_mask():
        q_chunk = _chunk_of(lax.broadcasted_iota(jnp.int32, (128, 128), 0))
        k_chunk = _chunk_of(lax.broadcasted_iota(jnp.int32, (128, 128), 1))
        return k_chunk <= q_chunk

    def masked_band(band, mask):
        width = band.shape[1]
        block = jnp.where(mask, band[:, width - 128:], NEG)
        return block if width == 128 else jnp.concatenate([band[:, :width - 128], block], axis=1)

    def scores(qi, c, slot, hd, mask=None):
        rows = pl.multiple_of((qi - qi0) * t, t)
        keys = pl.multiple_of(c * t, t)
        if mask is None:
            s = _dot_nt(q_ref[0, hd, pl.ds(rows, t), :], k_ref[0, hd, pl.ds(keys, t), :])
            s_ref[slot, hd] = s
            cm_ref[slot, hd] = row_max(s)
            return
        for r0, nr, nk in row_halves:
            s = _dot_nt(q_ref[0, hd, pl.ds(pl.multiple_of(rows + r0, nr), nr), :],
                        k_ref[0, hd, pl.ds(keys, nk), :])
            for a in range(0, nr, 128):
                reach = r0 + a + 128
                band = masked_band(s[a:a + 128, :reach], mask)
                s_ref[slot, hd, r0 + a:reach, :reach] = band
                cm_ref[slot, hd, r0 + a:reach] = row_max(band)

    def update(c, slot, hd, diagonal, mask):
        keys = pl.multiple_of(c * t, t)
        if not diagonal:
            m_old = m_ref[hd]
            m_new = jnp.maximum(m_old, cm_ref[slot, hd])
            alpha = jnp.exp2(m_old - m_new)
            p = jnp.exp2(s_ref[slot, hd] - _lane_tile(m_new, t // 128)).astype(BF16)
            v_ones = jnp.concatenate([v_ref[0, hd, pl.ds(keys, t), :], ones], axis=1)
            acc_ref[hd] = _lane_tile(alpha, MLA_QK_PAD // 128) * acc_ref[hd] + _dot(p, v_ones)
            m_ref[hd] = m_new
            return
        for r0, nr, nk in row_halves:
            alphas, probs = [], []
            for a in range(0, nr, 128):
                reach = r0 + a + 128
                band = s_ref[slot, hd, r0 + a:reach, :reach]
                if mask is not None:
                    band = masked_band(band, mask)
                    m_cur = row_max(band)
                else:
                    m_cur = cm_ref[slot, hd, r0 + a:reach]
                m_old = m_ref[hd, r0 + a:reach]
                m_new = jnp.maximum(m_old, m_cur)
                alphas.append(jnp.exp2(m_old - m_new))
                p = jnp.exp2(band - _lane_tile(m_new, reach // 128)).astype(BF16)
                if reach < nk:
                    p = jnp.concatenate([p, jnp.zeros((128, nk - reach), BF16)], axis=1)
                probs.append(p)
                m_ref[hd, r0 + a:reach] = m_new
            rows = slice(r0, r0 + nr)
            v_ones = jnp.concatenate([v_ref[0, hd, pl.ds(keys, nk), :], ones[:nk]], axis=1)
            acc_ref[hd, rows] = (
                _lane_tile(jnp.concatenate(alphas, axis=0), MLA_QK_PAD // 128) * acc_ref[hd, rows]
                + _dot(jnp.concatenate(probs, axis=0), v_ones))

    def finish_tile(qi, hd):
        rows = pl.multiple_of((qi - qi0) * t, t)
        acc = acc_ref[hd]
        o_ref[0, pl.ds(rows, t), hd * MLA_V:(hd + 1) * MLA_V] = (
            acc[:, :MLA_V] / acc[:, MLA_V:]).astype(BF16)
        m_ref[hd] = jnp.full((t, 128), NEG, F32)

    def following(qi, c):
        is_last = c == qi
        return jnp.where(is_last, jnp.minimum(qi + 1, qi_end), qi), jnp.where(is_last, 0, c + 1)

    def step(qi, c, slot, diagonal, premasked=False, mask_next=False, score_next=True):
        qn, cn = following(qi, c)
        mask = diagonal_mask() if diagonal and not premasked else None
        next_mask = diagonal_mask() if mask_next else None
        for hd in range(g):
            if score_next:
                scores(qn, cn, 1 - slot, hd, next_mask)
            update(c, slot, hd, diagonal, mask)
            if diagonal:
                finish_tile(qi, hd)

    for hd in range(g):
        scores(qi0, 0, 0, hd)

    def body(_, pair):
        first, second = pair, following(*pair)
        diag0 = first[1] == first[0]
        diag1 = second[1] == second[0]
        for d0, d1, cond in ((True, False, diag0), (False, True, diag1),
                             (False, False, jnp.logical_not(jnp.logical_or(diag0, diag1)))):
            @pl.when(cond)
            def _(d0=d0, d1=d1):
                step(*first, 0, d0, mask_next=d1)
                step(*second, 1, d1, premasked=d1)
        return following(*second)

    n_steps = nq * qi0 + nq * (nq + 1) // 2
    first = lax.fori_loop(0, n_steps // 2 - 1, body, (qi0, 0))
    second = following(*first)
    step(*first, 0, False, mask_next=True)
    step(*second, 1, True, premasked=True, score_next=False)


def _mla(q, k, v, t, g, nq):
    B, _, S, _ = q.shape
    assert S % (nq * t) == 0 and nq % 2 == 0 and (nq * (nq + 1) // 2) % 2 == 0
    return pl.pallas_call(
        functools.partial(_mla_kernel, t=t, g=g, nq=nq),
        grid=(B, MLA_HEADS // g, S // (nq * t)),
        in_specs=[
            pl.BlockSpec((1, g, nq * t, MLA_QK_PAD), lambda b, h, i: (b, h, i, 0)),
            pl.BlockSpec((1, g, S, MLA_QK_PAD), lambda b, h, i: (b, h, 0, 0)),
            pl.BlockSpec((1, g, S, MLA_V), lambda b, h, i: (b, h, 0, 0)),
        ],
        out_specs=pl.BlockSpec((1, nq * t, g * MLA_V), lambda b, h, i: (b, i, h)),
        out_shape=jax.ShapeDtypeStruct((B, S, MLA_WIDTH), BF16),
        scratch_shapes=[
            pltpu.VMEM((g, t, MLA_QK_PAD), F32),
            pltpu.VMEM((g, t, 128), F32),
            pltpu.VMEM((2, g, t, t), F32),
            pltpu.VMEM((2, g, t, 128), F32),
        ],
        compiler_params=_params(("arbitrary", "arbitrary", "arbitrary")),
        name="mla",
    )(q, k, v)


_REL_SPAN = 4 * SWA_BLOCK
_N_OFFSETS = 3 * SWA_BLOCK - 1


def _bias_memkv_kernel(table_ref, bucket_ref, mem_hbm, g_ref, w_hbm, o_ref, k_ref, v_ref,
                       spread_ref, mem_ref, w_ref, sem_ref):
    fetches = [pltpu.make_async_copy(mem_hbm, mem_ref, sem_ref.at[0]),
               pltpu.make_async_copy(w_hbm, w_ref, sem_ref.at[1])]
    for fetch in fetches:
        fetch.start()

    for b in range(N_BUCKETS):
        spread_ref[b] = jnp.concatenate(
            [jnp.full((1, 128), table_ref[b, hd], F32) for hd in range(SWA_HEADS)], axis=0)
    lane = lax.broadcasted_iota(jnp.int32, (SWA_HEADS, _REL_SPAN), 1)

    def lookup(c, per_offset):
        col = spread_ref[bucket_ref[c]]
        return jnp.where(lane == c, _lane_tile(col, _REL_SPAN // 128), per_offset)

    per_offset_all = lax.fori_loop(0, _N_OFFSETS, lookup, jnp.zeros((SWA_HEADS, _REL_SPAN), F32))
    q_chunk = _chunk_of(lax.broadcasted_iota(jnp.int32, (SWA_BLOCK, 2 * SWA_BLOCK), 0))
    b_chunk = _chunk_of(lax.broadcasted_iota(jnp.int32, (SWA_BLOCK, 2 * SWA_BLOCK), 1))
    in_window = jnp.logical_and(b_chunk >= q_chunk, b_chunk <= q_chunk + WINDOW_CHUNKS)
    for hd in range(SWA_HEADS):
        per_offset = per_offset_all[hd:hd + 1, :]
        rows = jnp.broadcast_to(per_offset, (SWA_BLOCK, _REL_SPAN))
        band = pltpu.roll(rows, _REL_SPAN - (SWA_BLOCK - 1), 1, stride=1, stride_axis=0)
        kvh, within = divmod(hd, SWA_GROUP)
        par, half = within % 2, within // 2
        o_ref[kvh, par, half * SWA_BLOCK:(half + 1) * SWA_BLOCK, :] = jnp.where(
            in_window, band[:, :2 * SWA_BLOCK], NEG)

    for fetch in fetches:
        fetch.wait()
    w = w_ref[...].astype(BF16)
    for b in range(mem_ref.shape[0]):
        kv = _dot(_rms(mem_ref[b], g_ref[...]).astype(BF16), w)
        k_ref[b] = kv[:, :MEM_WIDTH].astype(BF16)
        v_ref[b] = kv[:, MEM_WIDTH:].astype(BF16)


def _bias_memkv(rel_table, bucket, mem, g, w):
    B, M, _ = mem.shape
    smem = pl.BlockSpec(memory_space=pltpu.SMEM)
    vmem = pl.BlockSpec(memory_space=pltpu.VMEM)
    hbm = pl.BlockSpec(memory_space=pl.ANY)
    return pl.pallas_call(
        _bias_memkv_kernel,
        in_specs=[smem, smem, hbm, vmem, hbm],
        out_specs=[vmem] * 3,
        out_shape=[jax.ShapeDtypeStruct((SWA_KV_HEADS, 2, 2 * SWA_BLOCK, 2 * SWA_BLOCK), F32),
                   jax.ShapeDtypeStruct((B, M, MEM_WIDTH), BF16),
                   jax.ShapeDtypeStruct((B, M, MEM_WIDTH), BF16)],
        scratch_shapes=[
            pltpu.VMEM((N_BUCKETS, SWA_HEADS, 128), F32),
            pltpu.VMEM(mem.shape, F32),
            pltpu.VMEM(w.shape, F32),
            pltpu.SemaphoreType.DMA((2,)),
        ],
        compiler_params=pltpu.CompilerParams(vmem_limit_bytes=_VMEM_LIMIT),
        name="t5bias_memkv",
    )(rel_table, bucket, mem, g, w)


def _swa_into(y_ref, sink_ref, q_ref, kp_ref, kc_ref, vp_ref, vc_ref, bias_ref, kb_ref, vb_ref, nsub,
              beside_unit):
    t = pl.program_id(1)
    scale = SWA_HEAD_DIM ** -0.5
    sb = SWA_BLOCK
    kb_ref[:sb] = kp_ref[0]
    kb_ref[sb:] = kc_ref[0]
    vb_ref[:sb] = vp_ref[0]
    vb_ref[sb:] = vc_ref[0]

    b_chunk = _chunk_of(lax.broadcasted_iota(jnp.int32, (2 * sb, 2 * sb), 1))
    has_key = b_chunk >= jnp.where(t > 0, 0, 2)
    upper = lax.broadcasted_iota(jnp.int32, (2 * sb, 1), 0) < sb

    def geometry(u):
        r, kvh = divmod(u, SWA_KV_HEADS)
        rows = slice(r * sb, (r + 1) * sb)
        band = slice(r * sb, (r + 2) * sb)
        pair0 = slice(2 * kvh * 128, (2 * kvh + 1) * 128)
        pair1 = slice((2 * kvh + 1) * 128, (2 * kvh + 2) * 128)
        return r, kvh, rows, band, pair0, pair1

    def scores(u):
        r, kvh, rows, band, pair0, pair1 = geometry(u)
        q = jnp.concatenate([q_ref[0, rows, pair0], q_ref[0, rows, pair1]], axis=0)
        return [_dot_nt(q, kb_ref[band, (2 * kvh + par) * 128:(2 * kvh + par + 1) * 128])
                for par in range(2)]

    def attend(u, unit_scores):
        r, kvh, rows, band, pair0, pair1 = geometry(u)
        o = None
        for par, s in enumerate(unit_scores):
            kcols = slice((2 * kvh + par) * 128, (2 * kvh + par + 1) * 128)
            hd = SWA_GROUP * kvh + par
            s = s * scale + bias_ref[kvh, par]
            if r == 0:
                s = jnp.where(has_key, s, NEG)
            sink = jnp.where(upper, sink_ref[hd], sink_ref[hd + 2])
            m = jnp.maximum(jnp.max(s, axis=-1, keepdims=True), sink)
            p = jnp.exp(s - m)
            den = jnp.sum(p, axis=-1, keepdims=True) + jnp.exp(sink - m)
            pv = _dot((p * (1.0 / den)).astype(BF16), vb_ref[band, kcols])
            o = pv if o is None else o + pv
        y_ref[rows, pair0] = o[:sb]
        y_ref[rows, pair1] = o[sb:]

    n_units = nsub * SWA_KV_HEADS
    pending = scores(0)
    for u in range(n_units):
        upcoming = scores(u + 1) if u + 1 < n_units else None
        beside_unit(u)
        attend(u, pending)
        pending = upcoming


def _mem_into(y_ref, q_ref, k_ref, v_ref, after_head):
    scale = MEM_HEAD_DIM ** -0.5

    def head_cols(hd):
        return slice(hd * MEM_HEAD_DIM, (hd + 1) * MEM_HEAD_DIM)

    def scores(hd):
        return _dot_nt(q_ref[0, :, head_cols(hd)], k_ref[0, :, head_cols(hd)])

    pending = scores(0)
    for hd in range(MEM_HEADS):
        upcoming = scores(hd + 1) if hd + 1 < MEM_HEADS else None
        s = pending * scale
        m = jnp.max(s, axis=-1, keepdims=True)
        p = jnp.exp(s - m)
        inv = 1.0 / jnp.sum(p, axis=-1, keepdims=True)
        y_ref[:, head_cols(hd)] = _dot((p * inv).astype(BF16), v_ref[0, :, head_cols(hd)])
        after_head(hd)
        pending = upcoming


def _tail_kernel(sink_ref, x_ref, ya_ref, za_ref, zb_ref, zc_ref, qs_ref, qm_ref,
                 kp_ref, kc_ref, vp_ref, vc_ref, bias_ref, km_ref, vm_ref, w_hbm, g_ref,
                 o_ref, kb_ref, vb_ref, yb_ref, yc_ref, yacc_ref, w_ref, stage_ref, sem_ref, *, nsub):
    @pl.when(_first_grid_step())
    def _():
        _load_weight_bf16(w_hbm, w_ref, stage_ref, sem_ref)

    def gated(y, z_ref):
        h = 0.5 * z_ref[0].astype(F32)
        return (y * (h + h * jnp.tanh(h))).astype(BF16)

    swa_lo, mem_lo = MLA_WIDTH, MLA_WIDTH + SWA_WIDTH

    g_mla = gated(ya_ref[0].astype(F32), za_ref)
    mla_chunk = D_MODEL // (nsub * SWA_KV_HEADS)

    def project_mla(u):
        cols = slice(u * mla_chunk, (u + 1) * mla_chunk)
        yacc_ref[:, cols] = _dot(g_mla, w_ref[:swa_lo, cols])

    _swa_into(yb_ref, sink_ref, qs_ref, kp_ref, kc_ref, vp_ref, vc_ref, bias_ref, kb_ref, vb_ref, nsub,
              project_mla)

    g_swa = gated(yb_ref[...], zb_ref)
    swa_chunk = D_MODEL // MEM_HEADS

    def project_swa(u):
        cols = slice(u * swa_chunk, (u + 1) * swa_chunk)
        yacc_ref[:, cols] += _dot(g_swa, w_ref[swa_lo:mem_lo, cols])

    _mem_into(yc_ref, qm_ref, km_ref, vm_ref, project_swa)

    g_mem = gated(yc_ref[...], zc_ref)
    sum_sq = jnp.zeros((g_mem.shape[0], 1), F32)
    chunks = [slice(j * swa_chunk, (j + 1) * swa_chunk) for j in range(D_MODEL // swa_chunk)]
    for cols in chunks:
        r = x_ref[0, :, cols] + yacc_ref[:, cols] + _dot(g_mem, w_ref[mem_lo:, cols])
        o_ref[0, :, cols] = r
        sum_sq = sum_sq + jnp.sum(r * r, axis=-1, keepdims=True)
    inv = lax.rsqrt(sum_sq * (1.0 / D_MODEL) + EPS)
    for cols in chunks:
        o_ref[0, :, cols] = (o_ref[0, :, cols] * inv) * g_ref[:, cols]


def _tail(sinks, x, y_mla, z, qs, ks, vs, bias, kmem, vmem, w, g, nsub):
    B, S, _ = x.shape
    tm = nsub * SWA_BLOCK
    M = kmem.shape[1]
    row = lambda b, i: (b, i, 0)
    prev = lambda b, i: (b, jnp.maximum(i * nsub - 1, 0), 0)
    col = lambda c: (lambda b, i: (b, i, c))
    return pl.pallas_call(
        functools.partial(_tail_kernel, nsub=nsub),
        grid=(B, S // tm),
        in_specs=[
            pl.BlockSpec(memory_space=pltpu.SMEM),
            pl.BlockSpec((1, tm, D_MODEL), row),
            pl.BlockSpec((1, tm, MLA_WIDTH), row),
            pl.BlockSpec((1, tm, MLA_WIDTH), col(0)),
            pl.BlockSpec((1, tm, SWA_WIDTH), col(MLA_WIDTH // SWA_WIDTH)),
            pl.BlockSpec((1, tm, MEM_WIDTH), col((MLA_WIDTH + SWA_WIDTH) // MEM_WIDTH)),
            pl.BlockSpec((1, tm, SWA_WIDTH), col(0)),
            pl.BlockSpec((1, tm, MEM_WIDTH), col(1)),
            pl.BlockSpec((1, SWA_BLOCK, _SWA_KV_PAD), prev),
            pl.BlockSpec((1, tm, _SWA_KV_PAD), row),
            pl.BlockSpec((1, SWA_BLOCK, _SWA_KV_PAD), prev),
            pl.BlockSpec((1, tm, _SWA_KV_PAD), row),
            pl.BlockSpec((SWA_KV_HEADS, 2, 2 * SWA_BLOCK, 2 * SWA_BLOCK), lambda b, i: (0, 0, 0, 0)),
            pl.BlockSpec((1, M, MEM_WIDTH), lambda b, i: (b, 0, 0)),
            pl.BlockSpec((1, M, MEM_WIDTH), lambda b, i: (b, 0, 0)),
            pl.BlockSpec(memory_space=pl.ANY),
            pl.BlockSpec((1, D_MODEL), lambda b, i: (0, 0)),
        ],
        out_specs=pl.BlockSpec((1, tm, D_MODEL), row),
        out_shape=jax.ShapeDtypeStruct((B, S, D_MODEL), F32),
        scratch_shapes=[
            pltpu.VMEM((tm + SWA_BLOCK, _SWA_KV_PAD), BF16),
            pltpu.VMEM((tm + SWA_BLOCK, _SWA_KV_PAD), BF16),
            pltpu.VMEM((tm, SWA_WIDTH), F32),
            pltpu.VMEM((tm, MEM_WIDTH), F32),
            pltpu.VMEM((tm, D_MODEL), F32),
            pltpu.VMEM((MIX_WIDTH, D_MODEL), BF16),
            pltpu.VMEM((_STAGE_SLOTS, _W_OUT_CHUNK, D_MODEL), F32),
            pltpu.SemaphoreType.DMA((_STAGE_SLOTS,)),
        ],
        compiler_params=_params(("arbitrary", "arbitrary")),
        name="tail",
    )(sinks, x, y_mla, z, z, z, qs, qs, ks, ks, vs, vs, bias, kmem, vmem, w, g)


def _rope_tables(seq):
    inv = 1.0 / (ROPE_THETA ** (jnp.arange(0, MLA_ROPE, 2, dtype=F32) / MLA_ROPE))
    step = 1 << ((seq - 1).bit_length() // 2)
    assert seq % step == 0
    fine = jnp.arange(step, dtype=F32)[:, None] * inv[None, :]
    coarse = (jnp.arange(seq // step, dtype=F32) * step)[:, None] * inv[None, :]
    cf, sf = jnp.cos(fine)[None], jnp.sin(fine)[None]
    cc, sc = jnp.cos(coarse)[:, None], jnp.sin(coarse)[:, None]
    cos = (cc * cf - sc * sf).reshape(seq, -1)
    sin = (sc * cf + cc * sf).reshape(seq, -1)
    zero = jnp.zeros_like(cos)
    return (jnp.concatenate([cos, cos, zero, zero], axis=-1),
            jnp.concatenate([-sin, sin, zero, zero], axis=-1))


def _t5_bucket(rel):
    nb = N_BUCKETS // 2
    max_exact = nb // 2
    bucket = jnp.where(rel > 0, nb, 0)
    n = jnp.abs(rel)
    nf = jnp.maximum(n, 1).astype(F32)
    large = max_exact + (jnp.log(nf / max_exact) / math.log(MAX_DISTANCE / max_exact)
                         * (nb - max_exact)).astype(jnp.int32)
    large = jnp.minimum(large, nb - 1)
    return bucket + jnp.where(n < max_exact, n, large)


def kernel(x, mem, norm_in, w_in, norm_q, norm_kv, w_uq, w_ukv, attn_sinks, rel_bias,
           norm_mem, w_mem_kv, w_out, norm_final):
    B, S, _ = x.shape
    assert norm_in.shape[0] == 1, "single-layer trunk"
    assert x.shape[2] == D_MODEL and w_in.shape[1:] == (D_MODEL, IN_WIDTH)
    assert S % _PROJ_ROWS == 0 and S % (_TAIL_BLOCKS * SWA_BLOCK) == 0

    wt = w_in[0].T
    cos_t, sin_t = _rope_tables(S)
    q, k, v, qs, z, ks, vs = _proj(x, norm_in, wt, norm_q, norm_kv, w_uq[0].astype(BF16),
                                   w_ukv[0].astype(BF16), cos_t, sin_t, tm=_PROJ_ROWS)

    y_mla = _mla(q, k, v, t=_MLA_TILE, g=_MLA_GROUP, nq=_MLA_TILES_PER_STEP)

    bucket = _t5_bucket(jnp.arange(_REL_SPAN) - (2 * SWA_BLOCK - 1)).astype(jnp.int32)
    bias, kmem, vmem = _bias_memkv(rel_bias, bucket, mem, norm_mem, w_mem_kv[0])
    return _tail(attn_sinks[0], x, y_mla, z, qs, ks, vs, bias, kmem, vmem,
                 w_out[0], norm_final[None, :], nsub=_TAIL_BLOCKS)
```
